```python
import math
import jax
import jax.numpy as jnp
from jax import lax
import numpy as np

D_MODEL = 1024
BATCH = 8
SEQ = 2048
DEPTH = 2
DEC_BATCH = 128
DEC_SEQ = 8
PAST_LEN = 16384
PAGE_SIZE = 128

GROUP_WIDTH = D_MODEL // 4
HEAD_DIM = 64
N_MIX_HEADS = GROUP_WIDTH // HEAD_DIM
S5_CH = GROUP_WIDTH
S5_GROUP = 16
S5_NGROUPS = S5_CH // S5_GROUP
S5_STATE = 64
S5_STEP_MIN = 1e-3
S5_STEP_MAX = 1e-1
HG_CHUNK = 16
RET_CHUNK = 64
ROPE_BASE = 10000.0
RW_W_RANK = 64
RW_A_RANK = 64
RW_G_RANK = 128
RW_PROJ = 3 * GROUP_WIDTH + RW_W_RANK + RW_A_RANK + RW_G_RANK
HG_COLS = 4 * GROUP_WIDTH
RET_COLS = 4 * GROUP_WIDTH
IN_WIDTH = S5_CH + HG_COLS + RET_COLS + RW_PROJ
MEM_LEN = 256
MEM_HEADS = 4
MEM_HEAD_DIM = D_MODEL // MEM_HEADS
D_FF = 4 * D_MODEL
EPS = 1e-6
RW_LN_EPS = 64e-5

kernel_name = 'hymba_s5_hgrn2_retnet_rwkv7_step'


def rmsnorm(x, gain):
    xf = x.astype(jnp.float32)
    return xf * lax.rsqrt(jnp.mean(xf * xf, axis=-1, keepdims=True) + EPS) * gain.astype(jnp.float32)


def to_heads(t):
    b, l, _ = t.shape
    return t.reshape(b, l, N_MIX_HEADS, HEAD_DIM).transpose(0, 2, 1, 3)


def from_heads(t):
    b, h, l, d = t.shape
    return t.transpose(0, 2, 1, 3).reshape(b, l, h * d)


def head_rms(t):
    return t * lax.rsqrt(jnp.mean(t * t, axis=-1, keepdims=True) + EPS)


def rope(x, pos):
    half = HEAD_DIM // 2
    inv = ROPE_BASE ** (-jnp.arange(half, dtype=jnp.float32) / half)
    ang = pos.astype(jnp.float32)[:, None] * inv[None, :]
    cos, sin = jnp.cos(ang), jnp.sin(ang)
    x1, x2 = x[..., :half], x[..., half:]
    return jnp.concatenate([x1 * cos - x2 * sin, x1 * sin + x2 * cos], axis=-1)


def _diag_combine(e1, e2):
    a1, b1 = e1
    a2, b2 = e2
    return a1 * a2, a2 * b1 + b2


def s5_mixer(u, lam_re, lam_im, b_re, b_im, c_re, c_im, d_skip, log_step, w_glu, b_glu, gain, h0_re, h0_im):
    f32 = jnp.float32
    bsz, seq, _ = u.shape
    uf = u.astype(f32).reshape(bsz, seq, S5_NGROUPS, S5_GROUP)
    lam = lax.complex(lam_re.astype(f32), lam_im.astype(f32))
    step = jnp.exp(log_step.astype(f32))[:, None]
    lam_bar = jnp.exp(lam * step)
    b_mat = lax.complex(b_re.astype(f32), b_im.astype(f32))
    b_bar = ((lam_bar - 1.0) / lam)[..., None] * b_mat
    bu = jnp.einsum('gph,blgh->blgp', b_bar, uf)
    h0 = lax.complex(h0_re.astype(f32), h0_im.astype(f32))
    bu = bu.at[:, 0].add(lam_bar * h0)
    a = jnp.broadcast_to(lam_bar, bu.shape)
    _, h = lax.associative_scan(_diag_combine, (a, bu), axis=1)
    c_mat = lax.complex(c_re.astype(f32), c_im.astype(f32))
    y = jnp.real(jnp.einsum('ghp,blgp->blgh', c_mat, h)) + d_skip.astype(f32) * uf
    y = jax.nn.gelu(y.reshape(bsz, seq, S5_CH))
    y = y * jax.nn.sigmoid(y @ w_glu + b_glu)
    h_last = h[:, -1]
    return rmsnorm(y, gain), jnp.real(h_last), jnp.imag(h_last)


def chunked_gated_recurrence(q, k, v, log_f, s0, chunk):
    b, h, l, dk = q.shape
    dv = v.shape[-1]
    n = l // chunk
    q = q.reshape(b, h, n, chunk, dk)
    k = k.reshape(b, h, n, chunk, dk)
    v = v.reshape(b, h, n, chunk, dv)
    cum = jnp.cumsum(log_f.reshape(b, h, n, chunk, dk), axis=3)
    causal = jnp.tril(jnp.ones((chunk, chunk), dtype=bool))[:, :, None]
    diff = cum[:, :, :, :, None, :] - cum[:, :, :, None, :, :]
    decay = jnp.exp(jnp.where(causal, diff, -jnp.inf))
    scores = jnp.einsum('bhntd,bhnsd,bhntsd->bhnts', q, k, decay)
    o_intra = jnp.einsum('bhnts,bhnsv->bhntv', scores, v)
    last = cum[:, :, :, -1:, :]
    delta = jnp.einsum('bhncd,bhncv->bhndv', k * jnp.exp(last - cum), v)
    g_chunk = jnp.exp(last[:, :, :, 0, :])

    def step(s, inp):
        g, dlt = inp
        return g[..., None] * s + dlt, s

    s_fin, s_start = lax.scan(step, s0, (jnp.moveaxis(g_chunk, 2, 0), jnp.moveaxis(delta, 2, 0)))
    o_inter = jnp.einsum('bhntd,nbhdv->bhntv', q * jnp.exp(cum), s_start)
    return (o_intra + o_inter).reshape(b, h, l, dv), s_fin


def hgrn2_mixer(q_pre, f_pre, i_in, g_pre, lb, gain, s0):
    f32 = jnp.float32
    zf = f_pre.astype(f32)
    lbf = lb.astype(f32)
    log_f = jnp.logaddexp(jax.nn.log_sigmoid(zf), jnp.log(lbf) + jax.nn.log_sigmoid(-zf))
    key = (1.0 - lbf) * jax.nn.sigmoid(-zf)
    q = jax.nn.silu(q_pre.astype(f32))
    seq = q_pre.shape[1]
    o, s = chunked_gated_recurrence(to_heads(q), to_heads(key), to_heads(i_in.astype(f32)),
                                    to_heads(log_f), s0.astype(f32), math.gcd(seq, HG_CHUNK))
    o = from_heads(head_rms(o)) * gain.astype(f32)
    return o * jax.nn.silu(g_pre.astype(f32)), s


def retention_mixer(q_pre, k_pre, v_pre, g_pre, pos, s0):
    f32 = jnp.float32
    bsz, seq, _ = q_pre.shape
    chunk = math.gcd(seq, RET_CHUNK)
    n = seq // chunk
    q = rope(to_heads(q_pre.astype(f32)), pos)
    k = rope(to_heads(k_pre.astype(f32)), pos) * HEAD_DIM ** -0.5
    v = to_heads(v_pre.astype(f32))
    log_gamma = jnp.log(1.0 - 2.0 ** (-5.0 - jnp.arange(N_MIX_HEADS, dtype=f32)))
    q = q.reshape(bsz, N_MIX_HEADS, n, chunk, HEAD_DIM)
    k = k.reshape(bsz, N_MIX_HEADS, n, chunk, HEAD_DIM)
    v = v.reshape(bsz, N_MIX_HEADS, n, chunk, HEAD_DIM)
    idx = jnp.arange(chunk, dtype=f32)
    rel = idx[:, None] - idx[None, :]
    dmat = jnp.where(rel >= 0, jnp.exp(log_gamma[:, None, None] * jnp.maximum(rel, 0.0)), 0.0)
    scores = jnp.einsum('bhntd,bhnsd->bhnts', q, k) * dmat[None, :, None]
    o_intra = jnp.einsum('bhnts,bhnsv->bhntv', scores, v)
    k_tail = k * jnp.exp(log_gamma[:, None] * (chunk - 1.0 - idx))[None, :, None, :, None]
    delta = jnp.einsum('bhncd,bhncv->bhndv', k_tail, v)
    g_chunk = jnp.exp(log_gamma * chunk)[None, :, None, None]

    def step(s, dlt):
        return g_chunk * s + dlt, s

    s_fin, s_start = lax.scan(step, s0.astype(f32), jnp.moveaxis(delta, 2, 0))
    o_inter = jnp.einsum('bhntd,nbhdv->bhntv', q, s_start) * jnp.exp(log_gamma[:, None] * (idx + 1.0))[None, :, None, :, None]
    o = (o_intra + o_inter).reshape(bsz, N_MIX_HEADS, seq, HEAD_DIM)
    return from_heads(head_rms(o)) * jax.nn.silu(g_pre.astype(f32)), s_fin


def rwkv7_mixer(proj, shift0, mu, w0, w_w2, a0, w_a2, w_g2, k_k, k_a, r_k, ln_g, ln_b, s0):
    f32 = jnp.float32
    pf = proj.astype(f32)
    bsz, seq, _ = pf.shape
    prev = jnp.concatenate([shift0.astype(f32)[:, None], pf[:, :-1]], axis=1)
    pm = pf + (prev - pf) * mu.astype(f32)
    gw = GROUP_WIDTH
    r, k, v = pm[..., :gw], pm[..., gw:2 * gw], pm[..., 2 * gw:3 * gw]
    o = 3 * gw
    w_lr = pm[..., o:o + RW_W_RANK]
    a_lr = pm[..., o + RW_W_RANK:o + RW_W_RANK + RW_A_RANK]
    g_lr = pm[..., o + RW_W_RANK + RW_A_RANK:]
    log_w = -jax.nn.softplus(-(w0 + jnp.tanh(w_lr) @ w_w2)) - 0.5
    decay = jnp.exp(-jnp.exp(log_w))
    iclr = jax.nn.sigmoid(a0 + a_lr @ w_a2)
    gate = jax.nn.sigmoid(g_lr) @ w_g2

    def hd(t):
        return t.reshape(bsz, seq, N_MIX_HEADS, HEAD_DIM)

    kk = hd(k * k_k)
    kk = kk / jnp.maximum(jnp.linalg.norm(kk, axis=-1, keepdims=True), 1e-12)
    k = k * (1.0 + (iclr - 1.0) * k_a)
    r_h, k_h, v_h, w_h, a_h = hd(r), hd(k), hd(v), hd(decay), hd(iclr)

    def tm(t):
        return jnp.swapaxes(t, 0, 1)

    def step(s, inp):
        r_t, w_t, k_t, v_t, kk_t, a_t = inp
        sa = jnp.einsum('bhvk,bhk->bhv', s, -kk_t)
        s = s * w_t[:, :, None, :] + sa[..., None] * (kk_t * a_t)[:, :, None, :] + v_t[..., None] * k_t[:, :, None, :]
        return s, jnp.einsum('bhvk,bhk->bhv', s, r_t)

    s_fin, y = lax.scan(step, s0.astype(f32), (tm(r_h), tm(w_h), tm(k_h), tm(v_h), tm(kk), tm(a_h)))
    y = tm(y)
    mean = jnp.mean(y, axis=-1, keepdims=True)
    var = jnp.mean(jnp.square(y - mean), axis=-1, keepdims=True)
    y = ((y - mean) * lax.rsqrt(var + RW_LN_EPS)).reshape(bsz, seq, gw) * ln_g + ln_b
    bonus = jnp.sum(r_h * k_h * r_k.reshape(N_MIX_HEADS, HEAD_DIM), axis=-1, keepdims=True) * v_h
    y = (y + bonus.reshape(bsz, seq, gw)) * gate
    return y, s_fin, pf[:, -1]


def memory_kv(mem, w_k, w_v):
    b, m, _ = mem.shape
    k = (mem @ w_k).reshape(b, m, MEM_HEADS, MEM_HEAD_DIM)
    v = (mem @ w_v).reshape(b, m, MEM_HEADS, MEM_HEAD_DIM)
    return k, v


def cross_attend(h, mem_k, mem_v, w_q, w_o):
    b, l, _ = h.shape
    q = (h @ w_q).reshape(b, l, MEM_HEADS, MEM_HEAD_DIM)
    s = jnp.einsum('blhd,bmhd->bhlm', q, mem_k).astype(jnp.float32) * MEM_HEAD_DIM ** -0.5
    p = jax.nn.softmax(s, axis=-1)
    o = jnp.einsum('bhlm,bmhd->blhd', p, mem_v).reshape(b, l, D_MODEL)
    return o @ w_o


def trunk_layer(x, pos, mem_k, mem_v, s5_re0, s5_im0, hg0, ret0, rw0, shift0, p, lb):
    h = rmsnorm(x, p['norm_mix'])
    proj = h @ p['w_in']
    o0 = S5_CH
    o1 = o0 + HG_COLS
    o2 = o1 + RET_COLS
    hg_q, hg_f, hg_i, hg_g = jnp.split(proj[..., o0:o1], 4, axis=-1)
    rt_q, rt_k, rt_v, rt_g = jnp.split(proj[..., o1:o2], 4, axis=-1)
    y_s5, s5_re, s5_im = s5_mixer(proj[..., :o0], p['s5_lam_re'], p['s5_lam_im'], p['s5_b_re'], p['s5_b_im'],
                                  p['s5_c_re'], p['s5_c_im'], p['s5_d'], p['s5_log_step'], p['s5_w_glu'],
                                  p['s5_b_glu'], p['s5_norm'], s5_re0, s5_im0)
    y_hg, hg_s = hgrn2_mixer(hg_q, hg_f, hg_i, hg_g, lb, p['hg_norm'], hg0)
    y_rt, rt_s = retention_mixer(rt_q, rt_k, rt_v, rt_g, pos, ret0)
    y_rw, rw_s, shift = rwkv7_mixer(proj[..., o2:], shift0, p['rw_mu'], p['rw_w0'], p['rw_w_w2'], p['rw_a0'],
                                    p['rw_w_a2'], p['rw_w_g2'], p['rw_k_k'], p['rw_k_a'], p['rw_r_k'],
                                    p['rw_ln_g'], p['rw_ln_b'], rw0)
    mix = jnp.concatenate([y_s5, y_hg, y_rt, y_rw], axis=-1)
    x = x + mix @ p['w_out']
    x = x + cross_attend(rmsnorm(x, p['norm_mem']), mem_k, mem_v, p['mem_w_q'], p['mem_w_o'])
    hf = rmsnorm(x, p['norm_ffn']) @ p['ffn_w_up']
    x = x + jnp.square(jax.nn.relu(hf)) @ p['ffn_w_down']
    return x, s5_re, s5_im, hg_s, rt_s, rw_s, shift


def setup_inputs(seed: int = 0) -> dict:
    key = jax.random.key(seed)
    ks = iter(jax.random.split(key, 64))
    f32 = jnp.float32

    def nrm(shape, scale):
        return scale * jax.random.normal(next(ks), shape, f32)

    def uni(shape, lo, hi):
        return jax.random.uniform(next(ks), shape, f32, lo, hi)

    def gain(shape):
        return 1.0 + nrm(shape, 0.01)

    H, D, G, P = N_MIX_HEADS, HEAD_DIM, S5_NGROUPS, S5_STATE
    gw = GROUP_WIDTH
    inp = {}
    inp['x_prompt'] = nrm((BATCH, SEQ, D_MODEL), 1.0)
    inp['x_sample'] = nrm((DEC_BATCH, DEC_SEQ, D_MODEL), 1.0)
    inp['mem_prompt'] = nrm((BATCH, MEM_LEN, D_MODEL), 1.0)
    inp['state_s5_re'] = nrm((DEPTH, DEC_BATCH, G, P), 0.5)
    inp['state_s5_im'] = nrm((DEPTH, DEC_BATCH, G, P), 0.5)
    inp['state_hgrn'] = nrm((DEPTH, DEC_BATCH, H, D, D), 0.3)
    inp['state_ret'] = nrm((DEPTH, DEC_BATCH, H, D, D), 1.0)
    inp['state_rwkv'] = nrm((DEPTH, DEC_BATCH, H, D, D), 0.3)
    inp['state_rwkv_shift'] = nrm((DEPTH, DEC_BATCH, RW_PROJ), 1.0)
    inp['cache_mem_k'] = nrm((DEPTH, DEC_BATCH, MEM_LEN, MEM_HEADS, MEM_HEAD_DIM), 1.0)
    inp['cache_mem_v'] = nrm((DEPTH, DEC_BATCH, MEM_LEN, MEM_HEADS, MEM_HEAD_DIM), 1.0)
    inp['norm_mix'] = gain((DEPTH, D_MODEL))
    inp['w_in'] = nrm((DEPTH, D_MODEL, IN_WIDTH), D_MODEL ** -0.5)
    inp['w_out'] = nrm((DEPTH, D_MODEL, D_MODEL), D_MODEL ** -0.5)
    inp['s5_lam_re'] = -0.5 * jnp.exp(nrm((DEPTH, G, P), 0.1))
    inp['s5_lam_im'] = jnp.pi * jnp.arange(P, dtype=f32)[None, None, :] + nrm((DEPTH, G, P), 0.01)
    inp['s5_b_re'] = nrm((DEPTH, G, P, S5_GROUP), (2 * S5_GROUP) ** -0.5)
    inp['s5_b_im'] = nrm((DEPTH, G, P, S5_GROUP), (2 * S5_GROUP) ** -0.5)
    inp['s5_c_re'] = nrm((DEPTH, G, S5_GROUP, P), (2 * P) ** -0.5)
    inp['s5_c_im'] = nrm((DEPTH, G, S5_GROUP, P), (2 * P) ** -0.5)
    inp['s5_d'] = nrm((DEPTH, G, S5_GROUP), 1.0)
    inp['s5_log_step'] = uni((DEPTH, G), math.log(S5_STEP_MIN), math.log(S5_STEP_MAX))
    inp['s5_w_glu'] = nrm((DEPTH, S5_CH, S5_CH), S5_CH ** -0.5)
    inp['s5_b_glu'] = nrm((DEPTH, S5_CH), 0.01)
    inp['s5_norm'] = gain((DEPTH, S5_CH))
    inp['hg_lb_logits'] = nrm((DEPTH, gw), 0.5)
    inp['hg_norm'] = gain((DEPTH, gw))
    inp['rw_mu'] = uni((DEPTH, RW_PROJ), 0.0, 1.0)
    inp['rw_w0'] = uni((DEPTH, gw), -6.0, 1.0)
    inp['rw_w_w2'] = nrm((DEPTH, RW_W_RANK, gw), 0.1)
    inp['rw_a0'] = nrm((DEPTH, gw), 0.5)
    inp['rw_w_a2'] = nrm((DEPTH, RW_A_RANK, gw), 0.1)
    inp['rw_w_g2'] = nrm((DEPTH, RW_G_RANK, gw), RW_G_RANK ** -0.5)
    inp['rw_k_k'] = 0.85 + nrm((DEPTH, gw), 0.05)
    inp['rw_k_a'] = 1.0 + nrm((DEPTH, gw), 0.05)
    inp['rw_r_k'] = nrm((DEPTH, gw), 0.1)
    inp['rw_ln_g'] = gain((DEPTH, gw))
    inp['rw_ln_b'] = nrm((DEPTH, gw), 0.01)
    inp['norm_mem'] = gain((DEPTH, D_MODEL))
    inp['mem_w_q'] = nrm((DEPTH, D_MODEL, D_MODEL), D_MODEL ** -0.5)
    inp['mem_w_k'] = nrm((DEPTH, D_MODEL, D_MODEL), D_MODEL ** -0.5)
    inp['mem_w_v'] = nrm((DEPTH, D_MODEL, D_MODEL), D_MODEL ** -0.5)
    inp['mem_w_o'] = nrm((DEPTH, D_MODEL, D_MODEL), D_MODEL ** -0.5)
    inp['norm_ffn'] = gain((DEPTH, D_MODEL))
    inp['ffn_w_up'] = nrm((DEPTH, D_MODEL, D_FF), D_MODEL ** -0.5)
    inp['ffn_w_down'] = nrm((DEPTH, D_FF, D_MODEL), D_FF ** -0.5)
    inp['norm_final'] = gain((D_MODEL,))
    return inp


def reference(x_prompt, x_sample, mem_prompt, state_s5_re, state_s5_im, state_hgrn, state_ret, state_rwkv,
              state_rwkv_shift, cache_mem_k, cache_mem_v, norm_mix, w_in, w_out, s5_lam_re, s5_lam_im,
              s5_b_re, s5_b_im, s5_c_re, s5_c_im, s5_d, s5_log_step, s5_w_glu, s5_b_glu, s5_norm,
              hg_lb_logits, hg_norm, rw_mu, rw_w0, rw_w_w2, rw_a0, rw_w_a2, rw_w_g2, rw_k_k, rw_k_a, rw_r_k,
              rw_ln_g, rw_ln_b, norm_mem, mem_w_q, mem_w_k, mem_w_v, mem_w_o, norm_ffn, ffn_w_up, ffn_w_down,
              norm_final):
    f32 = jnp.float32
    lb_all = jnp.cumsum(jax.nn.softmax(hg_lb_logits.astype(f32), axis=0), axis=0)
    lb_all = lb_all - lb_all[0:1]
    bp, lp = x_prompt.shape[0], x_prompt.shape[1]
    ls = x_sample.shape[1]
    pos_p = jnp.arange(lp, dtype=jnp.int32)
    pos_s = PAST_LEN + jnp.arange(ls, dtype=jnp.int32)
    z_s5 = jnp.zeros((bp, S5_NGROUPS, S5_STATE), f32)
    z_mat = jnp.zeros((bp, N_MIX_HEADS, HEAD_DIM, HEAD_DIM), f32)
    z_shift = jnp.zeros((bp, RW_PROJ), f32)
    yp, ys = x_prompt, x_sample
    p_re, p_im, p_hg, p_rt, p_rw, p_sh, p_mk, p_mv = [], [], [], [], [], [], [], []
    s_re, s_im, s_hg, s_rt, s_rw, s_sh = [], [], [], [], [], []
    for l in range(DEPTH):
        prm = {'norm_mix': norm_mix[l], 'w_in': w_in[l], 'w_out': w_out[l],
               's5_lam_re': s5_lam_re[l], 's5_lam_im': s5_lam_im[l], 's5_b_re': s5_b_re[l],
               's5_b_im': s5_b_im[l], 's5_c_re': s5_c_re[l], 's5_c_im': s5_c_im[l], 's5_d': s5_d[l],
               's5_log_step': s5_log_step[l], 's5_w_glu': s5_w_glu[l], 's5_b_glu': s5_b_glu[l],
               's5_norm': s5_norm[l], 'hg_norm': hg_norm[l], 'rw_mu': rw_mu[l], 'rw_w0': rw_w0[l],
               'rw_w_w2': rw_w_w2[l], 'rw_a0': rw_a0[l], 'rw_w_a2': rw_w_a2[l], 'rw_w_g2': rw_w_g2[l],
               'rw_k_k': rw_k_k[l], 'rw_k_a': rw_k_a[l], 'rw_r_k': rw_r_k[l], 'rw_ln_g': rw_ln_g[l],
               'rw_ln_b': rw_ln_b[l], 'norm_mem': norm_mem[l], 'mem_w_q': mem_w_q[l], 'mem_w_o': mem_w_o[l],
               'norm_ffn': norm_ffn[l], 'ffn_w_up': ffn_w_up[l], 'ffn_w_down': ffn_w_down[l]}
        mk, mv = memory_kv(mem_prompt, mem_w_k[l], mem_w_v[l])
        yp, a1, a2, a3, a4, a5, a6 = trunk_layer(yp, pos_p, mk, mv, z_s5, z_s5, z_mat, z_mat, z_mat, z_shift,
                                                 prm, lb_all[l])
        p_re.append(a1); p_im.append(a2); p_hg.append(a3); p_rt.append(a4); p_rw.append(a5); p_sh.append(a6)
        p_mk.append(mk); p_mv.append(mv)
        ys, b1, b2, b3, b4, b5, b6 = trunk_layer(ys, pos_s, cache_mem_k[l], cache_mem_v[l], state_s5_re[l],
                                                 state_s5_im[l], state_hgrn[l], state_ret[l], state_rwkv[l],
                                                 state_rwkv_shift[l], prm, lb_all[l])
        s_re.append(b1); s_im.append(b2); s_hg.append(b3); s_rt.append(b4); s_rw.append(b5); s_sh.append(b6)
    y_prompt = rmsnorm(yp, norm_final)
    y_sample = rmsnorm(ys, norm_final)
    return (y_prompt, y_sample,
            jnp.stack(p_re), jnp.stack(p_im), jnp.stack(p_hg), jnp.stack(p_rt), jnp.stack(p_rw), jnp.stack(p_sh),
            jnp.stack(p_mk), jnp.stack(p_mv),
            jnp.stack(s_re), jnp.stack(s_im), jnp.stack(s_hg), jnp.stack(s_rt), jnp.stack(s_rw), jnp.stack(s_sh))
```

```python
import functools
import math

import jax
import jax.numpy as jnp
from jax import lax
from jax.experimental import pallas as pl
from jax.experimental.pallas import tpu as pltpu

F32 = jnp.float32
BF16 = jnp.bfloat16

D_MODEL = 1024
DEPTH = 2
PAST_LEN = 16384
GW = 256
HD = 64
NH = GW // HD
S5_GROUP = 16
S5_NG = GW // S5_GROUP
S5_P = 64
S5_W = S5_NG * S5_P
RW_PROJ = 4 * GW
IN_WIDTH = 13 * GW
MEM_LEN = 256
MEM_HEADS = 4
MEM_HD = D_MODEL // MEM_HEADS
D_FF = 4 * D_MODEL
EPS = 1e-6
RW_LN_EPS = 64e-5
ROPE_BASE = 10000.0
CHUNK = 16
RET_CHUNK = 64

VMEM_LIMIT = 56 * 1024 * 1024


def _cparams(sem):
    return pltpu.CompilerParams(dimension_semantics=sem, vmem_limit_bytes=VMEM_LIMIT)


def _mm(a, b):
    return jnp.dot(a.astype(BF16), b.astype(BF16), preferred_element_type=F32)


def _mm_nt(a, b):
    return lax.dot_general(a.astype(BF16), b.astype(BF16), (((1,), (1,)), ((), ())),
                           preferred_element_type=F32)


def _split3(x):
    hi = x.astype(BF16)
    r1 = x - hi.astype(F32)
    mid = r1.astype(BF16)
    lo = (r1 - mid.astype(F32)).astype(BF16)
    return hi, mid, lo


def _mm_exact_lhs(sel, x):
    s = sel.astype(BF16)
    hi, mid, lo = _split3(x)
    return (jnp.dot(s, hi, preferred_element_type=F32) + jnp.dot(s, mid, preferred_element_type=F32)
            + jnp.dot(s, lo, preferred_element_type=F32))


def _mm_exact_rhs(x, sel):
    s = sel.astype(BF16)
    hi, mid, lo = _split3(x)
    return (jnp.dot(hi, s, preferred_element_type=F32) + jnp.dot(mid, s, preferred_element_type=F32)
            + jnp.dot(lo, s, preferred_element_type=F32))


def _mm3(a, b):
    ah = a.astype(BF16)
    al = (a - ah.astype(F32)).astype(BF16)
    bh = b.astype(BF16)
    bl = (b - bh.astype(F32)).astype(BF16)
    return (jnp.dot(ah, bh, preferred_element_type=F32) + jnp.dot(ah, bl, preferred_element_type=F32)
            + jnp.dot(al, bh, preferred_element_type=F32))


def _rms(x, gain):
    return x * lax.rsqrt(jnp.mean(x * x, axis=-1, keepdims=True) + EPS) * gain


def _sigmoid(x):
    return 1.0 / (1.0 + jnp.exp(-x))


def _iota2(shape, axis):
    return lax.broadcasted_iota(jnp.int32, shape, axis)


def _head_ones():
    return (_iota2((GW, GW), 0) // HD == _iota2((GW, GW), 1) // HD).astype(F32)


def _eye(n):
    return (_iota2((n, n), 0) == _iota2((n, n), 1)).astype(F32)


def _head_sum(x, ones_bd):
    return _mm_exact_rhs(x, ones_bd)


def _transpose_rows(x, eye_gw):
    return _mm_nt(eye_gw, x)


def _norm_mm_kernel(x_ref, g_ref, w_ref, o_ref, xn_ref):
    @pl.when(pl.program_id(1) == 0)
    def _():
        xn_ref[...] = _rms(x_ref[...], g_ref[...]).astype(BF16)

    o_ref[...] = jnp.dot(xn_ref[...], w_ref[...], preferred_element_type=F32)


def norm_matmul(x, gain, w, *, tm, tn):
    t, d = x.shape
    n = w.shape[1]
    return pl.pallas_call(
        _norm_mm_kernel,
        out_shape=jax.ShapeDtypeStruct((t, n), F32),
        grid=(t // tm, n // tn),
        in_specs=[pl.BlockSpec((tm, d), lambda i, j: (i, 0)),
                  pl.BlockSpec((1, d), lambda i, j: (0, 0)),
                  pl.BlockSpec((d, tn), lambda i, j: (0, j))],
        out_specs=pl.BlockSpec((tm, tn), lambda i, j: (i, j)),
        scratch_shapes=[pltpu.VMEM((tm, d), BF16)],
        compiler_params=_cparams(("parallel", "arbitrary")),
        name="norm_matmul",
    )(x, gain, w)


def _mm_kernel(a_ref, w_ref, o_ref):
    o_ref[...] = _mm(a_ref[...], w_ref[...])


def matmul(a, w, *, tm, tn):
    t, k = a.shape
    n = w.shape[1]
    return pl.pallas_call(
        _mm_kernel,
        out_shape=jax.ShapeDtypeStruct((t, n), F32),
        grid=(t // tm, n // tn),
        in_specs=[pl.BlockSpec((tm, k), lambda i, j: (i, 0)),
                  pl.BlockSpec((k, tn), lambda i, j: (0, j))],
        out_specs=pl.BlockSpec((tm, tn), lambda i, j: (i, j)),
        compiler_params=_cparams(("parallel", "parallel")),
        name="matmul",
    )(a, w)


def _mm_res_kernel(a_ref, w_ref, r_ref, o_ref):
    o_ref[...] = r_ref[...] + _mm(a_ref[...], w_ref[...])


def matmul_residual(a, w, res, *, tm):
    t, k = a.shape
    n = w.shape[1]
    return pl.pallas_call(
        _mm_res_kernel,
        out_shape=jax.ShapeDtypeStruct((t, n), F32),
        grid=(t // tm,),
        in_specs=[pl.BlockSpec((tm, k), lambda i: (i, 0)),
                  pl.BlockSpec((k, n), lambda i: (0, 0)),
                  pl.BlockSpec((tm, n), lambda i: (i, 0))],
        out_specs=pl.BlockSpec((tm, n), lambda i: (i, 0)),
        compiler_params=_cparams(("parallel",)),
        name="matmul_residual",
    )(a, w, res)


def _mix_out_kernel(a0_ref, a1_ref, a2_ref, a3_ref, w_ref, r_ref, o_ref):
    a = jnp.concatenate([a0_ref[...], a1_ref[...], a2_ref[...], a3_ref[...]], axis=-1)
    o_ref[...] = r_ref[...] + _mm(a, w_ref[...])


def mix_out(parts, w, res, *, tm):
    t = res.shape[0]
    n = w.shape[1]
    part = pl.BlockSpec((tm, GW), lambda i: (i, 0))
    return pl.pallas_call(
        _mix_out_kernel,
        out_shape=jax.ShapeDtypeStruct((t, n), F32),
        grid=(t // tm,),
        in_specs=[part, part, part, part,
                  pl.BlockSpec((NH * GW, n), lambda i: (0, 0)),
                  pl.BlockSpec((tm, n), lambda i: (i, 0))],
        out_specs=pl.BlockSpec((tm, n), lambda i: (i, 0)),
        compiler_params=_cparams(("parallel",)),
        name="mix_out",
    )(*parts, w, res)


def _ffn_kernel(x_ref, g_ref, wu_ref, wd_ref, gf_ref, o_ref, xn_ref, acc_ref, *, final_norm):
    j = pl.program_id(1)

    @pl.when(j == 0)
    def _():
        x = x_ref[...]
        xn_ref[...] = _rms(x, g_ref[...]).astype(BF16)
        acc_ref[...] = x

    h = jnp.dot(xn_ref[...], wu_ref[...], preferred_element_type=F32)
    h = jnp.square(jnp.maximum(h, 0.0))
    acc_ref[...] += jnp.dot(h.astype(BF16), wd_ref[...], preferred_element_type=F32)

    @pl.when(j == pl.num_programs(1) - 1)
    def _():
        y = acc_ref[...]
        if final_norm:
            y = _rms(y, gf_ref[...])
        o_ref[...] = y


def ffn(x, gain, w_up, w_down, gain_final, *, tm, tf, final_norm):
    t, d = x.shape
    ff = w_up.shape[1]
    return pl.pallas_call(
        functools.partial(_ffn_kernel, final_norm=final_norm),
        out_shape=jax.ShapeDtypeStruct((t, d), F32),
        grid=(t // tm, ff // tf),
        in_specs=[pl.BlockSpec((tm, d), lambda i, j: (i, 0)),
                  pl.BlockSpec((1, d), lambda i, j: (0, 0)),
                  pl.BlockSpec((d, tf), lambda i, j: (0, j)),
                  pl.BlockSpec((tf, d), lambda i, j: (j, 0)),
                  pl.BlockSpec((1, d), lambda i, j: (0, 0))],
        out_specs=pl.BlockSpec((tm, d), lambda i, j: (i, 0)),
        scratch_shapes=[pltpu.VMEM((tm, d), BF16), pltpu.VMEM((tm, d), F32)],
        compiler_params=_cparams(("parallel", "arbitrary")),
        name="ffn",
    )(x, gain, w_up, w_down, gain_final)


def _attn_kernel(q_ref, k_ref, v_ref, o_ref, *, nb, rows):
    for b in range(nb):
        q = q_ref[b * rows:(b + 1) * rows, :] * (MEM_HD ** -0.5)
        for h in range(MEM_HEADS):
            sl = slice(h * MEM_HD, (h + 1) * MEM_HD)
            s = _mm_nt(q[:, sl], k_ref[b, :, sl])
            s = s - jnp.max(s, axis=-1, keepdims=True)
            p = jnp.exp(s)
            p = p / jnp.sum(p, axis=-1, keepdims=True)
            o_ref[b * rows:(b + 1) * rows, sl] = _mm(p, v_ref[b, :, sl])


def cross_attention(q, mem_k, mem_v, *, bsz, mem_off, nb, rows):
    t = q.shape[0]
    seq = t // bsz
    lt = seq // rows if nb == 1 else 1
    assert nb == 1 or rows == seq
    assert mem_off % nb == 0
    off = mem_off // nb
    return pl.pallas_call(
        functools.partial(_attn_kernel, nb=nb, rows=rows),
        out_shape=jax.ShapeDtypeStruct((t, D_MODEL), F32),
        grid=(bsz // nb, lt),
        in_specs=[pl.BlockSpec((nb * rows, D_MODEL), lambda b, l: (b * lt + l, 0)),
                  pl.BlockSpec((nb, MEM_LEN, D_MODEL), lambda b, l: (b + off, 0, 0)),
                  pl.BlockSpec((nb, MEM_LEN, D_MODEL), lambda b, l: (b + off, 0, 0))],
        out_specs=pl.BlockSpec((nb * rows, D_MODEL), lambda b, l: (b * lt + l, 0)),
        compiler_params=_cparams(("parallel", "arbitrary")),
        name="cross_attention",
    )(q, mem_k, mem_v)


def _gelu_tanh(x):
    return 0.5 * x * (1.0 + jnp.tanh(math.sqrt(2.0 / math.pi) * (x + 0.044715 * (x * x * x))))


def _s5_kernel(u_ref, h0_ref, lam_ref, bblk_ref, cblk_ref, d_ref, wglu_ref, bglu_ref, gain_ref,
               y_ref, hfin_ref, scr_ref, *, steps, nb):
    @pl.when(pl.program_id(1) == 0)
    def _():
        scr_ref[0:nb, :] = h0_ref[...]

    u = u_ref[...].reshape(steps * nb, GW)
    scr_ref[nb:, :] = _mm(u, bblk_ref[...])
    lam_re = lam_ref[0:1, :]
    lam_im = lam_ref[1:2, :]

    def step(t, carry):
        p0 = pl.multiple_of(t * nb, nb)
        c0 = pl.multiple_of((t + 1) * nb, nb)
        h_re = scr_ref[pl.ds(p0, nb), 0:S5_W]
        h_im = scr_ref[pl.ds(p0, nb), S5_W:]
        scr_ref[pl.ds(c0, nb), 0:S5_W] = scr_ref[pl.ds(c0, nb), 0:S5_W] + lam_re * h_re - lam_im * h_im
        scr_ref[pl.ds(c0, nb), S5_W:] = scr_ref[pl.ds(c0, nb), S5_W:] + lam_re * h_im + lam_im * h_re
        return carry

    lax.fori_loop(0, steps, step, 0)
    h_last = scr_ref[steps * nb:, :]
    hfin_ref[...] = h_last
    y = _mm(scr_ref[nb:, :], cblk_ref[...]) + d_ref[...] * u
    y = _gelu_tanh(y)
    y = y * _sigmoid(_mm(y, wglu_ref[...]) + bglu_ref[...])
    y_ref[...] = _rms(y, gain_ref[...]).reshape(steps, nb, GW)
    scr_ref[0:nb, :] = h_last


def s5_mixer(u_tm, h0, lam, bblk, cblk, dvec, wglu, bglu, gain, *, steps, nb):
    seq, bsz, _ = u_tm.shape
    const = lambda shape: pl.BlockSpec(shape, lambda b, t: (0, 0))
    return pl.pallas_call(
        functools.partial(_s5_kernel, steps=steps, nb=nb),
        out_shape=(jax.ShapeDtypeStruct((seq, bsz, GW), F32),
                   jax.ShapeDtypeStruct((bsz, 2 * S5_W), F32)),
        grid=(bsz // nb, seq // steps),
        in_specs=[pl.BlockSpec((steps, nb, GW), lambda b, t: (t, b, 0)),
                  pl.BlockSpec((nb, 2 * S5_W), lambda b, t: (b, 0)),
                  const((2, S5_W)), const((GW, 2 * S5_W)), const((2 * S5_W, GW)), const((1, GW)),
                  const((GW, GW)), const((1, GW)), const((1, GW))],
        out_specs=(pl.BlockSpec((steps, nb, GW), lambda b, t: (t, b, 0)),
                   pl.BlockSpec((nb, 2 * S5_W), lambda b, t: (b, 0))),
        scratch_shapes=[pltpu.VMEM(((steps + 1) * nb, 2 * S5_W), F32)],
        compiler_params=_cparams(("parallel", "arbitrary")),
        name="s5_mixer",
    )(u_tm, h0, lam, bblk, cblk, dvec, wglu, bglu, gain)


def s5_params(lam_re, lam_im, b_re, b_im, c_re, c_im, log_step):
    lam = lax.complex(lam_re, lam_im)
    step = jnp.exp(log_step)[:, None]
    lam_bar = jnp.exp(lam * step)
    b_bar = ((lam_bar - 1.0) / lam)[..., None] * lax.complex(b_re, b_im)
    eye_g = jnp.eye(S5_NG, dtype=F32)

    def in_blk(m):
        return jnp.einsum('gph,gk->ghkp', m, eye_g).reshape(GW, S5_W)

    def out_blk(m):
        return jnp.einsum('ghp,gk->gpkh', m, eye_g).reshape(S5_W, GW)

    bblk = jnp.concatenate([in_blk(jnp.real(b_bar)), in_blk(jnp.imag(b_bar))], axis=1)
    cblk = jnp.concatenate([out_blk(c_re), -out_blk(c_im)], axis=0)
    lam2 = jnp.stack([jnp.real(lam_bar).reshape(S5_W), jnp.imag(lam_bar).reshape(S5_W)])
    return lam2, bblk.astype(BF16), cblk.astype(BF16)


def s5_apply(u, h0_re, h0_im, p, l, *, steps, nb):
    bsz = u.shape[0]
    lam2, bblk, cblk = s5_params(p['s5_lam_re'][l], p['s5_lam_im'][l], p['s5_b_re'][l], p['s5_b_im'][l],
                                 p['s5_c_re'][l], p['s5_c_im'][l], p['s5_log_step'][l])
    h0 = jnp.concatenate([h0_re.reshape(bsz, S5_W), h0_im.reshape(bsz, S5_W)], axis=1)
    y, h = s5_mixer(jnp.swapaxes(u, 0, 1), h0, lam2, bblk, cblk, p['s5_d'][l].reshape(1, GW),
                    p['s5_w_glu'][l].astype(BF16), p['s5_b_glu'][l].reshape(1, GW),
                    p['s5_norm'][l].reshape(1, GW), steps=steps, nb=nb)
    return (jnp.swapaxes(y, 0, 1), h[:, :S5_W].reshape(bsz, S5_NG, S5_P),
            h[:, S5_W:].reshape(bsz, S5_NG, S5_P))


def _pack_state(s):
    bsz = s.shape[0]
    return jnp.einsum('bhij,hk->bhikj', s, jnp.eye(NH, dtype=F32)).reshape(bsz, GW, GW)


def _unpack_state(sbd):
    bsz = sbd.shape[0]
    s5 = sbd.reshape(bsz, NH, HD, NH, HD)
    return jnp.stack([s5[:, h, :, h, :] for h in range(NH)], axis=1)


def _head_masks():
    lane_head = _iota2((1, GW), 1) // HD
    return [(lane_head == h).astype(F32) for h in range(NH)]


def _stack_heads(x, masks):
    return jnp.concatenate([x * m for m in masks], axis=0)


def _pad_rows(x, rows):
    if x.shape[0] == rows:
        return x
    return jnp.concatenate([x, jnp.zeros((rows - x.shape[0], x.shape[1]), x.dtype)], axis=0)


def _head_rms(o, ones_bd):
    return o * lax.rsqrt(_head_sum(o * o, ones_bd) * (1.0 / HD) + EPS)


def _silu(x):
    return x * _sigmoid(x)


def _chunk_rows(ref, b, r0, rows, c):
    return _pad_rows(ref[b, pl.ds(r0, rows), :], c)


def _ret_kernel(q_ref, k_ref, v_ref, g_ref, cos_ref, sin_ref, s0_ref, o_ref, sfin_ref, st_ref,
                *, nb, tb, c, c_real):
    rows = min(tb, c)

    @pl.when(pl.program_id(1) == 0)
    def _():
        st_ref[...] = s0_ref[...]

    masks = _head_masks()
    ones_bd = _head_ones()
    eye_gw = _eye(GW)
    lane_head = _iota2((1, GW), 1) // HD
    log_gamma = jnp.zeros((1, GW), F32)
    for h in range(NH):
        log_gamma = jnp.where(lane_head == h, math.log(1.0 - 2.0 ** (-5.0 - h)), log_gamma)
    tt = _iota2((c, GW), 0).astype(F32)
    scale = HD ** -0.5
    g_q = jnp.exp(log_gamma * (tt + 1.0))
    g_k = jnp.exp(-log_gamma * (tt + 1.0)) * scale
    g_tail = jnp.exp(log_gamma * (c_real - 1.0 - tt)) * scale
    g_chunk = jnp.exp(log_gamma * float(c_real))
    causal = _iota2((c, NH * c), 0) >= _iota2((c, NH * c), 1) % c
    first_half = _iota2((c, GW), 1) % HD < HD // 2

    def rope(x, cos, sin):
        swapped = jnp.where(first_half, pltpu.roll(x, GW - HD // 2, axis=1), pltpu.roll(x, HD // 2, axis=1))
        return x * cos + swapped * sin

    def chunk(ci, carry):
        r0 = pl.multiple_of(ci * rows, rows)
        cos = _pad_rows(cos_ref[pl.ds(r0, rows), :], c)
        sin = _pad_rows(sin_ref[pl.ds(r0, rows), :], c)
        for b in range(nb):
            q = rope(_chunk_rows(q_ref, b, r0, rows, c), cos, sin)
            k = rope(_chunk_rows(k_ref, b, r0, rows, c), cos, sin)
            v = _chunk_rows(v_ref, b, r0, rows, c)
            st = st_ref[b]
            qt = q * g_q
            sc = _mm_nt(qt, _stack_heads(k * g_k, masks))
            sc = jnp.where(causal, sc, 0.0)
            o = _mm(sc, _stack_heads(v, masks)) + _mm_nt(qt, st)
            vt = _transpose_rows(v, eye_gw)
            st_ref[b] = st * g_chunk + ones_bd * _mm(vt, k * g_tail)
            o = _head_rms(o, ones_bd) * _silu(_chunk_rows(g_ref, b, r0, rows, c))
            o_ref[b, pl.ds(r0, rows), :] = o[:rows]
        return carry

    lax.fori_loop(0, tb // rows, chunk, 0)
    sfin_ref[...] = st_ref[...]


def _rope_tables(pos0, seq):
    half = HD // 2
    inv = ROPE_BASE ** (-jnp.arange(half, dtype=F32) / half)
    pos = pos0 + jnp.arange(seq, dtype=jnp.int32)
    ang = pos.astype(F32)[:, None] * inv[None, :]
    cos, sin = jnp.cos(ang), jnp.sin(ang)
    return (jnp.tile(jnp.concatenate([cos, cos], axis=-1), (1, NH)),
            jnp.tile(jnp.concatenate([-sin, sin], axis=-1), (1, NH)))


def _proj_spec(nb, tb, col):
    return pl.BlockSpec((nb, tb, GW), lambda b, t: (b, t, col))


def _state_spec(nb):
    return pl.BlockSpec((nb, GW, GW), lambda b, t: (b, 0, 0))


def retention_apply(proj, s0, pos0, *, tb, nb):
    bsz, seq, _ = proj.shape
    c = RET_CHUNK if tb >= RET_CHUNK else CHUNK
    cos, sin = _rope_tables(pos0, seq)
    st0 = _pack_state(jnp.swapaxes(s0, -1, -2))
    y, st = pl.pallas_call(
        functools.partial(_ret_kernel, nb=nb, tb=tb, c=c, c_real=min(tb, c)),
        out_shape=(jax.ShapeDtypeStruct((bsz, seq, GW), F32), jax.ShapeDtypeStruct((bsz, GW, GW), F32)),
        grid=(bsz // nb, seq // tb),
        in_specs=[_proj_spec(nb, tb, 5), _proj_spec(nb, tb, 6), _proj_spec(nb, tb, 7), _proj_spec(nb, tb, 8),
                  pl.BlockSpec((tb, GW), lambda b, t: (t, 0)), pl.BlockSpec((tb, GW), lambda b, t: (t, 0)),
                  _state_spec(nb)],
        out_specs=(pl.BlockSpec((nb, tb, GW), lambda b, t: (b, t, 0)), _state_spec(nb)),
        scratch_shapes=[pltpu.VMEM((nb, GW, GW), F32)],
        compiler_params=_cparams(("parallel", "arbitrary")),
        name="retention",
    )(proj, proj, proj, proj, cos, sin, st0)
    return y, jnp.swapaxes(_unpack_state(st), -1, -2)


def _log_sigmoid(z):
    return jnp.minimum(z, 0.0) - jnp.log(1.0 + jnp.exp(-jnp.abs(z)))


def _hgrn_kernel(q_ref, f_ref, i_ref, g_ref, lb_ref, gain_ref, s0_ref, o_ref, sfin_ref, st_ref,
                 *, nb, tb, c):
    rows = min(tb, c)

    @pl.when(pl.program_id(1) == 0)
    def _():
        st_ref[...] = s0_ref[...]

    ones_bd = _head_ones()
    eye_gw = _eye(GW)
    tri = (_iota2((c, c), 0) >= _iota2((c, c), 1)).astype(F32)
    t_idx = _iota2((c, GW), 0)
    lb = lb_ref[...]
    log_lb = jnp.log(lb)
    gain = gain_ref[...]

    def chunk(ci, carry):
        r0 = pl.multiple_of(ci * rows, rows)
        for b in range(nb):
            z = _chunk_rows(f_ref, b, r0, rows, c)
            valid = t_idx < rows
            ls_pos = _log_sigmoid(z)
            ls_neg = ls_pos - z
            b2 = log_lb + ls_neg
            log_f = jnp.maximum(ls_pos, b2) + jnp.log(1.0 + jnp.exp(-jnp.abs(ls_pos - b2)))
            log_f = jnp.where(valid, log_f, 0.0)
            key = jnp.where(valid, (1.0 - lb) * jnp.exp(ls_neg), 0.0)
            q = _silu(_chunk_rows(q_ref, b, r0, rows, c))
            v = _chunk_rows(i_ref, b, r0, rows, c)
            st = st_ref[b]
            cum = _mm_exact_lhs(tri, log_f)
            last = cum[c - 1:c, :]
            pair = [jnp.where(t_idx >= s, jnp.exp(cum - cum[s:s + 1, :]), 0.0) * q * key[s:s + 1, :]
                    for s in range(c)]
            sc = _mm(jnp.concatenate(pair, axis=0), ones_bd)
            o = _mm_nt(q * jnp.exp(cum), st)
            for s in range(c):
                o = o + sc[s * c:(s + 1) * c, :] * v[s:s + 1, :]
            vt = _transpose_rows(v, eye_gw)
            st_ref[b] = st * jnp.exp(last) + ones_bd * _mm(vt, key * jnp.exp(last - cum))
            o = _head_rms(o, ones_bd) * gain * _silu(_chunk_rows(g_ref, b, r0, rows, c))
            o_ref[b, pl.ds(r0, rows), :] = o[:rows]
        return carry

    lax.fori_loop(0, tb // rows, chunk, 0)
    sfin_ref[...] = st_ref[...]


def hgrn_apply(proj, s0, lb, gain, *, tb, nb):
    bsz, seq, _ = proj.shape
    row = pl.BlockSpec((1, GW), lambda b, t: (0, 0))
    st0 = _pack_state(jnp.swapaxes(s0, -1, -2))
    y, st = pl.pallas_call(
        functools.partial(_hgrn_kernel, nb=nb, tb=tb, c=CHUNK),
        out_shape=(jax.ShapeDtypeStruct((bsz, seq, GW), F32), jax.ShapeDtypeStruct((bsz, GW, GW), F32)),
        grid=(bsz // nb, seq // tb),
        in_specs=[_proj_spec(nb, tb, 1), _proj_spec(nb, tb, 2), _proj_spec(nb, tb, 3), _proj_spec(nb, tb, 4),
                  row, row, _state_spec(nb)],
        out_specs=(pl.BlockSpec((nb, tb, GW), lambda b, t: (b, t, 0)), _state_spec(nb)),
        scratch_shapes=[pltpu.VMEM((nb, GW, GW), F32)],
        compiler_params=_cparams(("parallel", "arbitrary")),
        name="hgrn2",
    )(proj, proj, proj, proj, lb.reshape(1, GW), gain.reshape(1, GW), st0)
    return y, jnp.swapaxes(_unpack_state(st), -1, -2)


def _softplus(z):
    return jnp.maximum(z, 0.0) + jnp.log(1.0 + jnp.exp(-jnp.abs(z)))


def _rwkv_chunk(r, lw, k, v, kk, a, st, consts):
    masks, ones_bd, eye_gw, eye_s, tri, strict, incl = consts
    c = r.shape[0]
    cum = _mm_exact_lhs(tri, lw)
    last = cum[c - 1:c, :]
    p_in = jnp.exp(cum)
    p_inv = jnp.exp(-cum)
    p_tail = jnp.exp(last - cum)
    ka = kk * a
    x = jnp.concatenate([_stack_heads(kk * jnp.exp(cum - lw), masks), _stack_heads(r * p_in, masks)], axis=0)
    ga = _mm_nt(x, _stack_heads(ka * p_inv, masks))
    gk = _mm_nt(x, _stack_heads(k * p_inv, masks))
    n = NH * c
    m_a = jnp.where(strict, ga[:n], 0.0)
    m_k = jnp.where(strict, gk[:n], 0.0)
    n_a = jnp.where(incl, ga[n:], 0.0)
    n_k = jnp.where(incl, gk[n:], 0.0)
    t_inv = eye_s - m_a
    pw = m_a
    for _ in range(int(math.log2(c)) - 1):
        pw = _mm3(pw, pw)
        t_inv = t_inv + _mm3(t_inv, pw)
    xa = _mm_nt(x, st)
    v_stk = _stack_heads(v, masks)
    u_stk = _mm3(t_inv, -xa[:n] - _mm(m_k, v_stk))
    y_stk = xa[n:] + _mm(n_a, u_stk) + _mm(n_k, v_stk)
    y = y_stk[0:c]
    u = u_stk[0:c]
    for h in range(1, NH):
        y = y + y_stk[h * c:(h + 1) * c]
        u = u + u_stk[h * c:(h + 1) * c]
    st_new = st * jnp.exp(last) + ones_bd * (_mm(_transpose_rows(u, eye_gw), ka * p_tail)
                                             + _mm(_transpose_rows(v, eye_gw), k * p_tail))
    return y, st_new


def _rwkv_kernel(x_r_ref, x_k_ref, x_v_ref, x_l_ref, sh0_ref, s0_ref, mu_ref, w0_ref, ww_ref, a0_ref,
                 wa_ref, wg_ref, kk_ref, ka_ref, rk_ref, lng_ref, lnb_ref,
                 o_ref, sfin_ref, shfin_ref,
                 st_ref, sh_ref, r_s, lw_s, k_s, v_s, kkn_s, a_s, y_s, bonus_s, gate_s, *, nb, tb, c):
    tbp = max(tb, c)

    @pl.when(pl.program_id(1) == 0)
    def _():
        st_ref[...] = s0_ref[...]
        sh_ref[...] = sh0_ref[...]

    masks = _head_masks()
    ones_bd = _head_ones()
    n = NH * c
    row_t = _iota2((n, n), 0) % c
    col_t = _iota2((n, n), 1) % c
    consts = (masks, ones_bd, _eye(GW), _eye(n), (_iota2((c, c), 0) >= _iota2((c, c), 1)).astype(F32),
              col_t < row_t, col_t <= row_t)
    first_row = _iota2((tb, GW), 0) == 0

    for b in range(nb):
        def mixed(x_ref, j):
            x = x_ref[b]
            prev = jnp.where(first_row, sh_ref[b, j:j + 1, :], pltpu.roll(x, 1, axis=0))
            sh_ref[b, j:j + 1, :] = x[tb - 1:tb, :]
            return x + (prev - x) * mu_ref[j:j + 1, :]

        r = mixed(x_r_ref, 0)
        k = mixed(x_k_ref, 1)
        v = mixed(x_v_ref, 2)
        xl = mixed(x_l_ref, 3)
        log_w = -_softplus(-(w0_ref[...] + _mm(jnp.tanh(xl), ww_ref[...]))) - 0.5
        a = _sigmoid(a0_ref[...] + _mm(xl, wa_ref[...]))
        gate = _mm(_sigmoid(xl), wg_ref[...])
        kk = k * kk_ref[...]
        kk = kk * lax.rsqrt(jnp.maximum(_head_sum(kk * kk, ones_bd), 1e-24))
        k = k * (1.0 + (a - 1.0) * ka_ref[...])
        r_s[b, 0:tb, :] = r
        lw_s[b, 0:tb, :] = -jnp.exp(log_w)
        k_s[b, 0:tb, :] = k
        v_s[b, 0:tb, :] = v
        kkn_s[b, 0:tb, :] = kk
        a_s[b, 0:tb, :] = a
        bonus_s[b] = _head_sum(r * k * rk_ref[...], ones_bd) * v
        gate_s[b] = gate
        if tbp > tb:
            zeros = jnp.zeros((tbp - tb, GW), F32)
            for s in (r_s, lw_s, k_s, v_s, kkn_s, a_s):
                s[b, tb:tbp, :] = zeros

    def chunk(ci, carry):
        r0 = pl.multiple_of(ci * c, c)
        for b in range(nb):
            y, st_new = _rwkv_chunk(r_s[b, pl.ds(r0, c), :], lw_s[b, pl.ds(r0, c), :], k_s[b, pl.ds(r0, c), :],
                                    v_s[b, pl.ds(r0, c), :], kkn_s[b, pl.ds(r0, c), :], a_s[b, pl.ds(r0, c), :],
                                    st_ref[b], consts)
            y_s[b, pl.ds(r0, c), :] = y
            st_ref[b] = st_new
        return carry

    lax.fori_loop(0, tbp // c, chunk, 0)

    for b in range(nb):
        y = y_s[b, 0:tb, :]
        mean = _head_sum(y, ones_bd) * (1.0 / HD)
        d = y - mean
        var = _head_sum(d * d, ones_bd) * (1.0 / HD)
        y = d * lax.rsqrt(var + RW_LN_EPS) * lng_ref[...] + lnb_ref[...]
        o_ref[b] = (y + bonus_s[b]) * gate_s[b]
    sfin_ref[...] = st_ref[...]
    shfin_ref[...] = sh_ref[...]


def rwkv_apply(proj, s0, shift0, p, l, *, tb, nb):
    bsz, seq, _ = proj.shape
    c = CHUNK
    tbp = max(tb, c)
    row = pl.BlockSpec((1, GW), lambda b, t: (0, 0))
    mat = pl.BlockSpec((GW, GW), lambda b, t: (0, 0))
    sh_spec = pl.BlockSpec((nb, 4, GW), lambda b, t: (b, 0, 0))
    zeros = lambda r: jnp.zeros((r, GW), F32)
    ww = jnp.concatenate([p['rw_w_w2'][l], zeros(192)], axis=0).astype(BF16)
    wa = jnp.concatenate([zeros(64), p['rw_w_a2'][l], zeros(128)], axis=0).astype(BF16)
    wg = jnp.concatenate([zeros(128), p['rw_w_g2'][l]], axis=0).astype(BF16)
    r1 = lambda name: p[name][l].reshape(1, GW)
    seq_buf = pltpu.VMEM((nb, tbp, GW), F32)
    blk_buf = pltpu.VMEM((nb, tb, GW), F32)
    y, st, sh = pl.pallas_call(
        functools.partial(_rwkv_kernel, nb=nb, tb=tb, c=c),
        out_shape=(jax.ShapeDtypeStruct((bsz, seq, GW), F32), jax.ShapeDtypeStruct((bsz, GW, GW), F32),
                   jax.ShapeDtypeStruct((bsz, 4, GW), F32)),
        grid=(bsz // nb, seq // tb),
        in_specs=[_proj_spec(nb, tb, 9), _proj_spec(nb, tb, 10), _proj_spec(nb, tb, 11), _proj_spec(nb, tb, 12),
                  sh_spec, _state_spec(nb), pl.BlockSpec((4, GW), lambda b, t: (0, 0)),
                  row, mat, row, mat, mat, row, row, row, row, row],
        out_specs=(pl.BlockSpec((nb, tb, GW), lambda b, t: (b, t, 0)), _state_spec(nb), sh_spec),
        scratch_shapes=[pltpu.VMEM((nb, GW, GW), F32), pltpu.VMEM((nb, 4, GW), F32),
                        seq_buf, seq_buf, seq_buf, seq_buf, seq_buf, seq_buf, seq_buf, blk_buf, blk_buf],
        compiler_params=_cparams(("parallel", "arbitrary")),
        name="rwkv7",
    )(proj, proj, proj, proj, shift0.reshape(bsz, 4, GW), _pack_state(s0), p['rw_mu'][l].reshape(4, GW),
      r1('rw_w0'), ww, r1('rw_a0'), wa, wg, r1('rw_k_k'), r1('rw_k_a'), r1('rw_r_k'), r1('rw_ln_g'), r1('rw_ln_b'))
    return y, _unpack_state(st), sh.reshape(bsz, RW_PROJ)


def _tiles(bsz, seq):
    if seq >= 128:
        return dict(tm=512, s5_steps=128, s5_nb=bsz, mix_tb=128, mix_nb=bsz, attn_nb=1, attn_rows=512)
    return dict(tm=512, s5_steps=seq, s5_nb=bsz, mix_tb=seq, mix_nb=8, attn_nb=4, attn_rows=seq)


def _trunk_layer(x, bsz, seq, pos0, mem_k, mem_v, mem_off, st, p, wb, l, lb, final_norm):
    cfg = _tiles(bsz, seq)
    tm = cfg['tm']
    row = lambda name: p[name][l].reshape(1, -1)
    proj = norm_matmul(x, row('norm_mix'), wb['w_in'][l], tm=tm, tn=IN_WIDTH // 2)
    proj = proj.reshape(bsz, seq, IN_WIDTH)
    y_s5, s5_re, s5_im = s5_apply(proj[..., :GW], st['s5_re'], st['s5_im'], p, l,
                                  steps=cfg['s5_steps'], nb=cfg['s5_nb'])
    y_hg, hg_s = hgrn_apply(proj, st['hgrn'], lb, p['hg_norm'][l], tb=cfg['mix_tb'], nb=cfg['mix_nb'])
    y_rt, rt_s = retention_apply(proj, st['ret'], pos0, tb=cfg['mix_tb'], nb=cfg['mix_nb'])
    y_rw, rw_s, shift = rwkv_apply(proj, st['rwkv'], st['shift'], p, l, tb=cfg['mix_tb'], nb=cfg['mix_nb'])
    parts = [y.reshape(bsz * seq, GW) for y in (y_s5, y_hg, y_rt, y_rw)]
    x = mix_out(parts, wb['w_out'][l], x, tm=tm)
    q = norm_matmul(x, row('norm_mem'), wb['mem_w_q'][l], tm=tm, tn=D_MODEL)
    o = cross_attention(q, mem_k, mem_v, bsz=bsz, mem_off=mem_off, nb=cfg['attn_nb'], rows=cfg['attn_rows'])
    x = matmul_residual(o, wb['mem_w_o'][l], x, tm=tm)
    x = ffn(x, row('norm_ffn'), wb['ffn_w_up'][l], wb['ffn_w_down'][l], p['norm_final'].reshape(1, -1),
            tm=1024, tf=512, final_norm=final_norm)
    return x, (s5_re, s5_im, hg_s, rt_s, rw_s, shift)


def kernel(x_prompt, x_sample, mem_prompt, state_s5_re, state_s5_im, state_hgrn, state_ret, state_rwkv,
           state_rwkv_shift, cache_mem_k, cache_mem_v, norm_mix, w_in, w_out, s5_lam_re, s5_lam_im,
           s5_b_re, s5_b_im, s5_c_re, s5_c_im, s5_d, s5_log_step, s5_w_glu, s5_b_glu, s5_norm,
           hg_lb_logits, hg_norm, rw_mu, rw_w0, rw_w_w2, rw_a0, rw_w_a2, rw_w_g2, rw_k_k, rw_k_a, rw_r_k,
           rw_ln_g, rw_ln_b, norm_mem, mem_w_q, mem_w_k, mem_w_v, mem_w_o, norm_ffn, ffn_w_up, ffn_w_down,
           norm_final):
    p = dict(norm_mix=norm_mix, s5_lam_re=s5_lam_re, s5_lam_im=s5_lam_im, s5_b_re=s5_b_re, s5_b_im=s5_b_im,
             s5_c_re=s5_c_re, s5_c_im=s5_c_im, s5_d=s5_d, s5_log_step=s5_log_step, s5_w_glu=s5_w_glu,
             s5_b_glu=s5_b_glu, s5_norm=s5_norm, hg_norm=hg_norm, rw_mu=rw_mu, rw_w0=rw_w0, rw_w_w2=rw_w_w2,
             rw_a0=rw_a0, rw_w_a2=rw_w_a2, rw_w_g2=rw_w_g2, rw_k_k=rw_k_k, rw_k_a=rw_k_a, rw_r_k=rw_r_k,
             rw_ln_g=rw_ln_g, rw_ln_b=rw_ln_b, norm_mem=norm_mem, norm_ffn=norm_ffn, norm_final=norm_final)
    wb = {name: w.astype(BF16) for name, w in dict(
        w_in=w_in, w_out=w_out, mem_w_q=mem_w_q, mem_w_k=mem_w_k, mem_w_v=mem_w_v, mem_w_o=mem_w_o,
        ffn_w_up=ffn_w_up, ffn_w_down=ffn_w_down).items()}
    lb_all = jnp.cumsum(jax.nn.softmax(hg_lb_logits.astype(F32), axis=0), axis=0)
    lb_all = lb_all - lb_all[0:1]

    bp, lp, _ = x_prompt.shape
    bs, ls, _ = x_sample.shape
    yp = x_prompt.reshape(bp * lp, D_MODEL)
    ys = x_sample.reshape(bs * ls, D_MODEL)
    mem2d = mem_prompt.reshape(bp * MEM_LEN, D_MODEL)
    zero_state = dict(s5_re=jnp.zeros((bp, S5_NG, S5_P), F32), s5_im=jnp.zeros((bp, S5_NG, S5_P), F32),
                      hgrn=jnp.zeros((bp, NH, HD, HD), F32), ret=jnp.zeros((bp, NH, HD, HD), F32),
                      rwkv=jnp.zeros((bp, NH, HD, HD), F32), shift=jnp.zeros((bp, RW_PROJ), F32))
    p_states, s_states, p_mk, p_mv = [], [], [], []
    for l in range(DEPTH):
        final = l == DEPTH - 1
        mk = matmul(mem2d, wb['mem_w_k'][l], tm=512, tn=D_MODEL)
        mv = matmul(mem2d, wb['mem_w_v'][l], tm=512, tn=D_MODEL)
        yp, stp = _trunk_layer(yp, bp, lp, 0, mk.reshape(bp, MEM_LEN, D_MODEL), mv.reshape(bp, MEM_LEN, D_MODEL),
                               0, zero_state, p, wb, l, lb_all[l], final)
        p_states.append(stp)
        p_mk.append(mk.reshape(bp, MEM_LEN, MEM_HEADS, MEM_HD))
        p_mv.append(mv.reshape(bp, MEM_LEN, MEM_HEADS, MEM_HD))
        sst = dict(s5_re=state_s5_re[l], s5_im=state_s5_im[l], hgrn=state_hgrn[l], ret=state_ret[l],
                   rwkv=state_rwkv[l], shift=state_rwkv_shift[l])
        ys, sts = _trunk_layer(ys, bs, ls, PAST_LEN, cache_mem_k.reshape(DEPTH * bs, MEM_LEN, D_MODEL),
                               cache_mem_v.reshape(DEPTH * bs, MEM_LEN, D_MODEL), l * bs, sst, p, wb, l,
                               lb_all[l], final)
        s_states.append(sts)
    stack = lambda states, i: jnp.stack([s[i] for s in states])
    return (yp.reshape(bp, lp, D_MODEL), ys.reshape(bs, ls, D_MODEL),
            *[stack(p_states, i) for i in range(6)], jnp.stack(p_mk), jnp.stack(p_mv),
            *[stack(s_states, i) for i in range(6)])
```

```python
import functools
import math

import jax
import jax.numpy as jnp
from jax import lax
from jax.experimental import pallas as pl
from jax.experimental.pallas import tpu as pltpu

F32 = jnp.float32
BF16 = jnp.bfloat16

D_MODEL = 1024
DEPTH = 2
PAST_LEN = 16384
GW = 256
HD = 64
NH = GW // HD
S5_GROUP = 16
S5_NG = GW // S5_GROUP
S5_P = 64
S5_W = S5_NG * S5_P
RW_PROJ = 4 * GW
IN_WIDTH = 13 * GW
MEM_LEN = 256
MEM_HEADS = 4
MEM_HD = D_MODEL // MEM_HEADS
D_FF = 4 * D_MODEL
EPS = 1e-6
RW_LN_EPS = 64e-5
ROPE_BASE = 10000.0
CHUNK = 16
RET_CHUNK = 64

VMEM_LIMIT = 56 * 1024 * 1024


def _cparams(sem):
    return pltpu.CompilerParams(dimension_semantics=sem, vmem_limit_bytes=VMEM_LIMIT)


def _mm(a, b):
    return jnp.dot(a.astype(BF16), b.astype(BF16), preferred_element_type=F32)


def _mm_nt(a, b):
    return lax.dot_general(a.astype(BF16), b.astype(BF16), (((1,), (1,)), ((), ())),
                           preferred_element_type=F32)


def _mm_tn(a, b):
    return lax.dot_general(a.astype(BF16), b.astype(BF16), (((0,), (0,)), ((), ())),
                           preferred_element_type=F32)


def _split3(x):
    hi = x.astype(BF16)
    r1 = x - hi.astype(F32)
    mid = r1.astype(BF16)
    lo = (r1 - mid.astype(F32)).astype(BF16)
    return hi, mid, lo


def _mm_exact_lhs(sel, x):
    s = sel.astype(BF16)
    hi, mid, lo = _split3(x)
    return (jnp.dot(s, hi, preferred_element_type=F32) + jnp.dot(s, mid, preferred_element_type=F32)
            + jnp.dot(s, lo, preferred_element_type=F32))


def _rms(x, gain):
    return x * lax.rsqrt(jnp.mean(x * x, axis=-1, keepdims=True) + EPS) * gain


def _sigmoid(x):
    return 1.0 / (1.0 + jnp.exp(-x))


def _iota2(shape, axis):
    return lax.broadcasted_iota(jnp.int32, shape, axis)


def _head_ones():
    return (_iota2((GW, GW), 0) // HD == _iota2((GW, GW), 1) // HD).astype(F32)


def _eye(n):
    return (_iota2((n, n), 0) == _iota2((n, n), 1)).astype(F32)


def _head_sum(x, ones_bd):
    s = ones_bd.astype(BF16)
    hi = x.astype(BF16)
    lo = (x - hi.astype(F32)).astype(BF16)
    return jnp.dot(hi, s, preferred_element_type=F32) + jnp.dot(lo, s, preferred_element_type=F32)


def _norm_mm_kernel(x_ref, g_ref, w_ref, o_ref, xn_ref):
    @pl.when(pl.program_id(1) == 0)
    def _():
        xn_ref[...] = _rms(x_ref[...], g_ref[...]).astype(BF16)

    o_ref[...] = jnp.dot(xn_ref[...], w_ref[...], preferred_element_type=F32)


def norm_matmul(x, gain, w, *, tm, tn):
    t, d = x.shape
    n = w.shape[1]
    return pl.pallas_call(
        _norm_mm_kernel,
        out_shape=jax.ShapeDtypeStruct((t, n), F32),
        grid=(t // tm, n // tn),
        in_specs=[pl.BlockSpec((tm, d), lambda i, j: (i, 0)),
                  pl.BlockSpec((1, d), lambda i, j: (0, 0)),
                  pl.BlockSpec((d, tn), lambda i, j: (0, j))],
        out_specs=pl.BlockSpec((tm, tn), lambda i, j: (i, j)),
        scratch_shapes=[pltpu.VMEM((tm, d), BF16)],
        compiler_params=_cparams(("parallel", "arbitrary")),
        name="norm_matmul",
    )(x, gain, w)


def _mm_kernel(a_ref, w_ref, o_ref):
    o_ref[...] = _mm(a_ref[...], w_ref[...])


def matmul(a, w, *, tm, tn):
    t, k = a.shape
    n = w.shape[1]
    return pl.pallas_call(
        _mm_kernel,
        out_shape=jax.ShapeDtypeStruct((t, n), F32),
        grid=(t // tm, n // tn),
        in_specs=[pl.BlockSpec((tm, k), lambda i, j: (i, 0)),
                  pl.BlockSpec((k, tn), lambda i, j: (0, j))],
        out_specs=pl.BlockSpec((tm, tn), lambda i, j: (i, j)),
        compiler_params=_cparams(("parallel", "parallel")),
        name="matmul",
    )(a, w)


def _mm_res_kernel(a_ref, w_ref, r_ref, o_ref):
    o_ref[...] = r_ref[...] + _mm(a_ref[...], w_ref[...])


def matmul_residual(a, w, res, *, tm):
    t, k = a.shape
    n = w.shape[1]
    return pl.pallas_call(
        _mm_res_kernel,
        out_shape=jax.ShapeDtypeStruct((t, n), F32),
        grid=(t // tm,),
        in_specs=[pl.BlockSpec((tm, k), lambda i: (i, 0)),
                  pl.BlockSpec((k, n), lambda i: (0, 0)),
                  pl.BlockSpec((tm, n), lambda i: (i, 0))],
        out_specs=pl.BlockSpec((tm, n), lambda i: (i, 0)),
        compiler_params=_cparams(("parallel",)),
        name="matmul_residual",
    )(a, w, res)


def _mix_out_kernel(a0_ref, a1_ref, a2_ref, a3_ref, w_ref, r_ref, o_ref):
    a = jnp.concatenate([a0_ref[...], a1_ref[...], a2_ref[...], a3_ref[...]], axis=-1)
    o_ref[...] = r_ref[...] + _mm(a, w_ref[...])


def mix_out(parts, w, res, *, tm):
    t = res.shape[0]
    n = w.shape[1]
    part = pl.BlockSpec((tm, GW), lambda i: (i, 0))
    return pl.pallas_call(
        _mix_out_kernel,
        out_shape=jax.ShapeDtypeStruct((t, n), F32),
        grid=(t // tm,),
        in_specs=[part, part, part, part,
                  pl.BlockSpec((NH * GW, n), lambda i: (0, 0)),
                  pl.BlockSpec((tm, n), lambda i: (i, 0))],
        out_specs=pl.BlockSpec((tm, n), lambda i: (i, 0)),
        compiler_params=_cparams(("parallel",)),
        name="mix_out",
    )(*parts, w, res)


def _ffn_kernel(x_ref, g_ref, wu_ref, wd_ref, gf_ref, o_ref, xn_ref, acc_ref, *, final_norm):
    j = pl.program_id(1)

    @pl.when(j == 0)
    def _():
        x = x_ref[...]
        xn_ref[...] = _rms(x, g_ref[...]).astype(BF16)
        acc_ref[...] = x

    h = jnp.dot(xn_ref[...], wu_ref[...], preferred_element_type=F32)
    h = jnp.square(jnp.maximum(h, 0.0))
    acc_ref[...] += jnp.dot(h.astype(BF16), wd_ref[...], preferred_element_type=F32)

    @pl.when(j == pl.num_programs(1) - 1)
    def _():
        y = acc_ref[...]
        if final_norm:
            y = _rms(y, gf_ref[...])
        o_ref[...] = y


def ffn(x, gain, w_up, w_down, gain_final, *, tm, tf, final_norm):
    t, d = x.shape
    ff = w_up.shape[1]
    return pl.pallas_call(
        functools.partial(_ffn_kernel, final_norm=final_norm),
        out_shape=jax.ShapeDtypeStruct((t, d), F32),
        grid=(t // tm, ff // tf),
        in_specs=[pl.BlockSpec((tm, d), lambda i, j: (i, 0)),
                  pl.BlockSpec((1, d), lambda i, j: (0, 0)),
                  pl.BlockSpec((d, tf), lambda i, j: (0, j)),
                  pl.BlockSpec((tf, d), lambda i, j: (j, 0)),
                  pl.BlockSpec((1, d), lambda i, j: (0, 0))],
        out_specs=pl.BlockSpec((tm, d), lambda i, j: (i, 0)),
        scratch_shapes=[pltpu.VMEM((tm, d), BF16), pltpu.VMEM((tm, d), F32)],
        compiler_params=_cparams(("parallel", "arbitrary")),
        name="ffn",
    )(x, gain, w_up, w_down, gain_final)


def _attn_kernel(q_ref, k_ref, v_ref, o_ref, *, nb, rows):
    for b in range(nb):
        q = q_ref[b * rows:(b + 1) * rows, :] * (MEM_HD ** -0.5)
        for h in range(MEM_HEADS):
            sl = slice(h * MEM_HD, (h + 1) * MEM_HD)
            s = _mm_nt(q[:, sl], k_ref[b, :, sl])
            s = s - jnp.max(s, axis=-1, keepdims=True)
            p = jnp.exp(s)
            p = p / jnp.sum(p, axis=-1, keepdims=True)
            o_ref[b * rows:(b + 1) * rows, sl] = _mm(p, v_ref[b, :, sl])


def cross_attention(q, mem_k, mem_v, *, nb, rows):
    t = q.shape[0]
    bsz = mem_k.shape[0]
    seq = t // bsz
    lt = seq // rows if nb == 1 else 1
    assert nb == 1 or rows == seq
    return pl.pallas_call(
        functools.partial(_attn_kernel, nb=nb, rows=rows),
        out_shape=jax.ShapeDtypeStruct((t, D_MODEL), F32),
        grid=(bsz // nb, lt),
        in_specs=[pl.BlockSpec((nb * rows, D_MODEL), lambda b, l: (b * lt + l, 0)),
                  pl.BlockSpec((nb, MEM_LEN, D_MODEL), lambda b, l: (b, 0, 0)),
                  pl.BlockSpec((nb, MEM_LEN, D_MODEL), lambda b, l: (b, 0, 0))],
        out_specs=pl.BlockSpec((nb * rows, D_MODEL), lambda b, l: (b * lt + l, 0)),
        compiler_params=_cparams(("parallel", "arbitrary")),
        name="cross_attention",
    )(q, mem_k, mem_v)


def _attn_cache_kernel(q_ref, k_ref, v_ref, o_ref, *, nb, rows):
    nr = MEM_HEADS * rows
    same_head = _iota2((nr, MEM_HEADS * MEM_LEN), 1) % MEM_HEADS == _iota2((nr, MEM_HEADS * MEM_LEN), 0) // rows
    for b in range(nb):
        q = q_ref[b * rows:(b + 1) * rows, :] * (MEM_HD ** -0.5)
        qs = jnp.concatenate([q[:, h * MEM_HD:(h + 1) * MEM_HD] for h in range(MEM_HEADS)], axis=0)
        s = _mm_nt(qs, k_ref[0, b].reshape(MEM_HEADS * MEM_LEN, MEM_HD))
        s = jnp.where(same_head, s, -1e30)
        p = jnp.exp(s - jnp.max(s, axis=-1, keepdims=True))
        p = p / jnp.sum(p, axis=-1, keepdims=True)
        o = _mm(p, v_ref[0, b].reshape(MEM_HEADS * MEM_LEN, MEM_HD))
        for h in range(MEM_HEADS):
            o_ref[b * rows:(b + 1) * rows, h * MEM_HD:(h + 1) * MEM_HD] = o[h * rows:(h + 1) * rows]


def cross_attention_cache(q, cache_k, cache_v, layer, *, nb, rows):
    t = q.shape[0]
    bsz = cache_k.shape[1]
    mem = pl.BlockSpec((1, nb, MEM_LEN, MEM_HEADS, MEM_HD), lambda b: (layer, b, 0, 0, 0))
    return pl.pallas_call(
        functools.partial(_attn_cache_kernel, nb=nb, rows=rows),
        out_shape=jax.ShapeDtypeStruct((t, D_MODEL), F32),
        grid=(bsz // nb,),
        in_specs=[pl.BlockSpec((nb * rows, D_MODEL), lambda b: (b, 0)), mem, mem],
        out_specs=pl.BlockSpec((nb * rows, D_MODEL), lambda b: (b, 0)),
        compiler_params=_cparams(("parallel",)),
        name="cross_attention_cache",
    )(q, cache_k, cache_v)


def _gelu_tanh(x):
    return 0.5 * x * (1.0 + jnp.tanh(math.sqrt(2.0 / math.pi) * (x + 0.044715 * (x * x * x))))


def _s5_kernel(u_ref, h0_ref, lam_ref, bblk_ref, cblk_ref, d_ref, wglu_ref, bglu_ref, gain_ref,
               y_ref, hfin_ref, scr_ref, *, steps, nb):
    @pl.when(pl.program_id(1) == 0)
    def _():
        scr_ref[0:nb, :] = h0_ref[...]

    u = u_ref[...].reshape(steps * nb, GW)
    scr_ref[nb:, :] = _mm(u, bblk_ref[...])
    lam_re = lam_ref[0:1, :]
    lam_im = lam_ref[1:2, :]

    def step(t, carry):
        p0 = pl.multiple_of(t * nb, nb)
        c0 = pl.multiple_of((t + 1) * nb, nb)
        h_re = scr_ref[pl.ds(p0, nb), 0:S5_W]
        h_im = scr_ref[pl.ds(p0, nb), S5_W:]
        scr_ref[pl.ds(c0, nb), 0:S5_W] = scr_ref[pl.ds(c0, nb), 0:S5_W] + lam_re * h_re - lam_im * h_im
        scr_ref[pl.ds(c0, nb), S5_W:] = scr_ref[pl.ds(c0, nb), S5_W:] + lam_re * h_im + lam_im * h_re
        return carry

    lax.fori_loop(0, steps, step, 0)
    h_last = scr_ref[steps * nb:, :]
    hfin_ref[...] = h_last
    y = _mm(scr_ref[nb:, :], cblk_ref[...]) + d_ref[...] * u
    y = _gelu_tanh(y)
    y = y * _sigmoid(_mm(y, wglu_ref[...]) + bglu_ref[...])
    y_ref[...] = _rms(y, gain_ref[...]).reshape(steps, nb, GW)
    scr_ref[0:nb, :] = h_last


def s5_mixer(u_tm, h0, lam, bblk, cblk, dvec, wglu, bglu, gain, *, steps, nb):
    seq, bsz, _ = u_tm.shape
    const = lambda shape: pl.BlockSpec(shape, lambda b, t: (0, 0))
    return pl.pallas_call(
        functools.partial(_s5_kernel, steps=steps, nb=nb),
        out_shape=(jax.ShapeDtypeStruct((seq, bsz, GW), F32),
                   jax.ShapeDtypeStruct((bsz, 2 * S5_W), F32)),
        grid=(bsz // nb, seq // steps),
        in_specs=[pl.BlockSpec((steps, nb, GW), lambda b, t: (t, b, 0)),
                  pl.BlockSpec((nb, 2 * S5_W), lambda b, t: (b, 0)),
                  const((2, S5_W)), const((GW, 2 * S5_W)), const((2 * S5_W, GW)), const((1, GW)),
                  const((GW, GW)), const((1, GW)), const((1, GW))],
        out_specs=(pl.BlockSpec((steps, nb, GW), lambda b, t: (t, b, 0)),
                   pl.BlockSpec((nb, 2 * S5_W), lambda b, t: (b, 0))),
        scratch_shapes=[pltpu.VMEM(((steps + 1) * nb, 2 * S5_W), F32)],
        compiler_params=_cparams(("parallel", "arbitrary")),
        name="s5_mixer",
    )(u_tm, h0, lam, bblk, cblk, dvec, wglu, bglu, gain)


def s5_params(lam_re, lam_im, b_re, b_im, c_re, c_im, log_step):
    step = jnp.exp(log_step)[:, None]
    mag = jnp.exp(lam_re * step)
    lbar_re = mag * jnp.cos(lam_im * step)
    lbar_im = mag * jnp.sin(lam_im * step)
    den = lam_re * lam_re + lam_im * lam_im
    f_re = ((lbar_re - 1.0) * lam_re + lbar_im * lam_im) / den
    f_im = (lbar_im * lam_re - (lbar_re - 1.0) * lam_im) / den
    bbar_re = f_re[..., None] * b_re - f_im[..., None] * b_im
    bbar_im = f_re[..., None] * b_im + f_im[..., None] * b_re
    eye_g = jnp.eye(S5_NG, dtype=F32)

    def in_blk(m):
        return jnp.einsum('gph,gk->ghkp', m, eye_g).reshape(GW, S5_W)

    def out_blk(m):
        return jnp.einsum('ghp,gk->gpkh', m, eye_g).reshape(S5_W, GW)

    bblk = jnp.concatenate([in_blk(bbar_re), in_blk(bbar_im)], axis=1)
    cblk = jnp.concatenate([out_blk(c_re), -out_blk(c_im)], axis=0)
    lam2 = jnp.stack([lbar_re.reshape(S5_W), lbar_im.reshape(S5_W)])
    return lam2, bblk.astype(BF16), cblk.astype(BF16)


def s5_apply(u, h0_re, h0_im, p, l, *, steps, nb):
    bsz = u.shape[0]
    lam2, bblk, cblk = s5_params(p['s5_lam_re'][l], p['s5_lam_im'][l], p['s5_b_re'][l], p['s5_b_im'][l],
                                 p['s5_c_re'][l], p['s5_c_im'][l], p['s5_log_step'][l])
    h0 = jnp.concatenate([h0_re.reshape(bsz, S5_W), h0_im.reshape(bsz, S5_W)], axis=1)
    y, h = s5_mixer(jnp.swapaxes(u, 0, 1), h0, lam2, bblk, cblk, p['s5_d'][l].reshape(1, GW),
                    p['s5_w_glu'][l].astype(BF16), p['s5_b_glu'][l].reshape(1, GW),
                    p['s5_norm'][l].reshape(1, GW), steps=steps, nb=nb)
    return (jnp.swapaxes(y, 0, 1), h[:, :S5_W].reshape(bsz, S5_NG, S5_P),
            h[:, S5_W:].reshape(bsz, S5_NG, S5_P))


def _pack_state(s):
    bsz = s.shape[0]
    return jnp.einsum('bhij,hk->bhikj', s, jnp.eye(NH, dtype=F32)).reshape(bsz, GW, GW)


def _unpack_state(sbd):
    bsz = sbd.shape[0]
    s5 = sbd.reshape(bsz, NH, HD, NH, HD)
    return jnp.stack([s5[:, h, :, h, :] for h in range(NH)], axis=1)


def _head_masks():
    lane_head = _iota2((1, GW), 1) // HD
    return [(lane_head == h).astype(F32) for h in range(NH)]


def _stack_heads(x, masks):
    return jnp.concatenate([x * m for m in masks], axis=0)


def _pad_rows(x, rows):
    if x.shape[0] == rows:
        return x
    return jnp.concatenate([x, jnp.zeros((rows - x.shape[0], x.shape[1]), x.dtype)], axis=0)


def _silu(x):
    return x * _sigmoid(x)


def _chunk_rows(ref, b, r0, rows, c):
    return _pad_rows(ref[b, pl.ds(r0, rows), :], c)


def _ret_kernel(q_ref, k_ref, v_ref, g_ref, cos_ref, sin_ref, s0_ref, o_ref, sfin_ref, st_ref,
                *, nb, tb, c, c_real):
    rows = min(tb, c)

    @pl.when(pl.program_id(1) == 0)
    def _():
        st_ref[...] = s0_ref[...]

    masks = _head_masks()
    ones_bd = _head_ones()
    lane_head = _iota2((1, GW), 1) // HD
    log_gamma = jnp.zeros((1, GW), F32)
    for h in range(NH):
        log_gamma = jnp.where(lane_head == h, math.log(1.0 - 2.0 ** (-5.0 - h)), log_gamma)
    tt = _iota2((c, GW), 0).astype(F32)
    scale = HD ** -0.5
    g_q = jnp.exp(log_gamma * (tt + 1.0))
    g_k = jnp.exp(-log_gamma * (tt + 1.0)) * scale
    g_tail = jnp.exp(log_gamma * (c_real - 1.0 - tt)) * scale
    g_chunk = jnp.exp(log_gamma * float(c_real))
    causal = _iota2((c, NH * c), 0) >= _iota2((c, NH * c), 1) % c
    first_half = _iota2((c, GW), 1) % HD < HD // 2

    def rope(x, cos, sin):
        swapped = jnp.where(first_half, pltpu.roll(x, GW - HD // 2, axis=1), pltpu.roll(x, HD // 2, axis=1))
        return x * cos + swapped * sin

    def chunk(ci, carry):
        r0 = pl.multiple_of(ci * rows, rows)
        cos = _pad_rows(cos_ref[pl.ds(r0, rows), :], c)
        sin = _pad_rows(sin_ref[pl.ds(r0, rows), :], c)
        rng = range(nb)
        qt = [rope(_chunk_rows(q_ref, b, r0, rows, c), cos, sin) * g_q for b in rng]
        k = [rope(_chunk_rows(k_ref, b, r0, rows, c), cos, sin) for b in rng]
        v = [_chunk_rows(v_ref, b, r0, rows, c) for b in rng]
        st = [st_ref[b] for b in rng]
        sc = [_mm_nt(qt[b], _stack_heads(k[b] * g_k, masks)) for b in rng]
        o_in = [_mm(jnp.where(causal, sc[b], 0.0), _stack_heads(v[b], masks)) for b in rng]
        o_st = [_mm_nt(qt[b], st[b]) for b in rng]
        d_st = [_mm_tn(v[b], k[b] * g_tail) for b in rng]
        o = [o_in[b] + o_st[b] for b in rng]
        ms = [_head_sum(o[b] * o[b], ones_bd) for b in rng]
        for b in rng:
            st_ref[b] = st[b] * g_chunk + ones_bd * d_st[b]
            ob = o[b] * lax.rsqrt(ms[b] * (1.0 / HD) + EPS) * _silu(_chunk_rows(g_ref, b, r0, rows, c))
            o_ref[b, pl.ds(r0, rows), :] = ob[:rows]
        return carry

    lax.fori_loop(0, tb // rows, chunk, 0)
    sfin_ref[...] = st_ref[...]


def _rope_tables(pos0, seq):
    half = HD // 2
    inv = ROPE_BASE ** (-jnp.arange(half, dtype=F32) / half)
    pos = pos0 + jnp.arange(seq, dtype=jnp.int32)
    ang = pos.astype(F32)[:, None] * inv[None, :]
    cos, sin = jnp.cos(ang), jnp.sin(ang)
    return (jnp.tile(jnp.concatenate([cos, cos], axis=-1), (1, NH)),
            jnp.tile(jnp.concatenate([-sin, sin], axis=-1), (1, NH)))


def _proj_spec(nb, tb, col):
    return pl.BlockSpec((nb, tb, GW), lambda b, t: (b, t, col))


def _state_spec(nb):
    return pl.BlockSpec((nb, GW, GW), lambda b, t: (b, 0, 0))


def retention_apply(proj, s0, pos0, *, tb, nb):
    bsz, seq, _ = proj.shape
    c = RET_CHUNK if tb >= RET_CHUNK else CHUNK
    cos, sin = _rope_tables(pos0, seq)
    st0 = _pack_state(jnp.swapaxes(s0, -1, -2))
    y, st = pl.pallas_call(
        functools.partial(_ret_kernel, nb=nb, tb=tb, c=c, c_real=min(tb, c)),
        out_shape=(jax.ShapeDtypeStruct((bsz, seq, GW), F32), jax.ShapeDtypeStruct((bsz, GW, GW), F32)),
        grid=(bsz // nb, seq // tb),
        in_specs=[_proj_spec(nb, tb, 5), _proj_spec(nb, tb, 6), _proj_spec(nb, tb, 7), _proj_spec(nb, tb, 8),
                  pl.BlockSpec((tb, GW), lambda b, t: (t, 0)), pl.BlockSpec((tb, GW), lambda b, t: (t, 0)),
                  _state_spec(nb)],
        out_specs=(pl.BlockSpec((nb, tb, GW), lambda b, t: (b, t, 0)), _state_spec(nb)),
        scratch_shapes=[pltpu.VMEM((nb, GW, GW), F32)],
        compiler_params=_cparams(("parallel", "arbitrary")),
        name="retention",
    )(proj, proj, proj, proj, cos, sin, st0)
    return y, jnp.swapaxes(_unpack_state(st), -1, -2)


def _log_sigmoid(z):
    return jnp.minimum(z, 0.0) - jnp.log(1.0 + jnp.exp(-jnp.abs(z)))


def _hgrn_kernel(q_ref, f_ref, i_ref, g_ref, lb_ref, gain_ref, s0_ref, o_ref, sfin_ref, st_ref,
                 *, nb, tb, c):
    rows = min(tb, c)

    @pl.when(pl.program_id(1) == 0)
    def _():
        st_ref[...] = s0_ref[...]

    ones_bd = _head_ones()
    tri = (_iota2((c, c), 0) >= _iota2((c, c), 1)).astype(F32)
    t_idx = _iota2((c, GW), 0)
    lb = lb_ref[...]
    log_lb = jnp.log(lb)
    gain = gain_ref[...]

    def chunk(ci, carry):
        r0 = pl.multiple_of(ci * rows, rows)
        rng = range(nb)
        valid = t_idx < rows

        def gates(b):
            z = _chunk_rows(f_ref, b, r0, rows, c)
            ls_pos = _log_sigmoid(z)
            ls_neg = ls_pos - z
            b2 = log_lb + ls_neg
            log_f = jnp.maximum(ls_pos, b2) + jnp.log(1.0 + jnp.exp(-jnp.abs(ls_pos - b2)))
            return jnp.where(valid, log_f, 0.0), jnp.where(valid, (1.0 - lb) * jnp.exp(ls_neg), 0.0)

        log_f, key = zip(*[gates(b) for b in rng])
        q = [_silu(_chunk_rows(q_ref, b, r0, rows, c)) for b in rng]
        v = [_chunk_rows(i_ref, b, r0, rows, c) for b in rng]
        st = [st_ref[b] for b in rng]
        cum = [_mm_exact_lhs(tri, log_f[b]) for b in rng]
        last = [cum[b][c - 1:c, :] for b in rng]
        pair = [jnp.concatenate([jnp.where(t_idx >= s, jnp.exp(cum[b] - cum[b][s:s + 1, :]), 0.0)
                                 * q[b] * key[b][s:s + 1, :] for s in range(c)], axis=0) for b in rng]
        sc = [_mm(pair[b], ones_bd) for b in rng]
        o_st = [_mm_nt(q[b] * jnp.exp(cum[b]), st[b]) for b in rng]
        d_st = [_mm_tn(v[b], key[b] * jnp.exp(last[b] - cum[b])) for b in rng]
        o = [o_st[b] + sum(sc[b][s * c:(s + 1) * c, :] * v[b][s:s + 1, :] for s in range(c)) for b in rng]
        ms = [_head_sum(o[b] * o[b], ones_bd) for b in rng]
        for b in rng:
            st_ref[b] = st[b] * jnp.exp(last[b]) + ones_bd * d_st[b]
            ob = o[b] * lax.rsqrt(ms[b] * (1.0 / HD) + EPS) * gain * _silu(_chunk_rows(g_ref, b, r0, rows, c))
            o_ref[b, pl.ds(r0, rows), :] = ob[:rows]
        return carry

    lax.fori_loop(0, tb // rows, chunk, 0)
    sfin_ref[...] = st_ref[...]


def hgrn_apply(proj, s0, lb, gain, *, tb, nb):
    bsz, seq, _ = proj.shape
    row = pl.BlockSpec((1, GW), lambda b, t: (0, 0))
    st0 = _pack_state(jnp.swapaxes(s0, -1, -2))
    y, st = pl.pallas_call(
        functools.partial(_hgrn_kernel, nb=nb, tb=tb, c=CHUNK),
        out_shape=(jax.ShapeDtypeStruct((bsz, seq, GW), F32), jax.ShapeDtypeStruct((bsz, GW, GW), F32)),
        grid=(bsz // nb, seq // tb),
        in_specs=[_proj_spec(nb, tb, 1), _proj_spec(nb, tb, 2), _proj_spec(nb, tb, 3), _proj_spec(nb, tb, 4),
                  row, row, _state_spec(nb)],
        out_specs=(pl.BlockSpec((nb, tb, GW), lambda b, t: (b, t, 0)), _state_spec(nb)),
        scratch_shapes=[pltpu.VMEM((nb, GW, GW), F32)],
        compiler_params=_cparams(("parallel", "arbitrary")),
        name="hgrn2",
    )(proj, proj, proj, proj, lb.reshape(1, GW), gain.reshape(1, GW), st0)
    return y, jnp.swapaxes(_unpack_state(st), -1, -2)


def _softplus(z):
    return jnp.maximum(z, 0.0) + jnp.log(1.0 + jnp.exp(-jnp.abs(z)))


def _rwkv_chunk(ins, consts):
    masks, ones_bd, eye_s, tri, strict, incl = consts
    rng = range(len(ins))
    r, lw, k, v, kk, a, st = [[i[j] for i in ins] for j in range(7)]
    c = r[0].shape[0]
    n = NH * c
    cum = [_mm_exact_lhs(tri, lw[i]) for i in rng]
    last = [cum[i][c - 1:c, :] for i in rng]
    p_inv = [jnp.exp(-cum[i]) for i in rng]
    p_tail = [jnp.exp(last[i] - cum[i]) for i in rng]
    ka = [kk[i] * a[i] for i in rng]
    x = [jnp.concatenate([_stack_heads(kk[i] * jnp.exp(cum[i] - lw[i]), masks),
                          _stack_heads(r[i] * jnp.exp(cum[i]), masks)], axis=0) for i in rng]
    g = [_mm_nt(x[i], jnp.concatenate([_stack_heads(ka[i] * p_inv[i], masks),
                                       _stack_heads(k[i] * p_inv[i], masks)], axis=0)) for i in rng]
    m_ak = [jnp.where(strict, g[i][:n], 0.0) for i in rng]
    n_ak = [jnp.where(incl, g[i][n:], 0.0) for i in rng]
    pw = [m_ak[i][:, :n] for i in rng]
    t_inv = [eye_s - pw[i] for i in rng]
    for _ in range(int(math.log2(c)) - 1):
        pw = [_mm(pw[i], pw[i]) for i in rng]
        t_inv = [t_inv[i] + _mm(t_inv[i], pw[i]) for i in rng]
    xa = [_mm_nt(x[i], st[i]) for i in rng]
    v_stk = [_stack_heads(v[i], masks) for i in rng]
    zeros = jnp.zeros((n, GW), F32)
    mkv = [_mm(m_ak[i], jnp.concatenate([zeros, v_stk[i]], axis=0)) for i in rng]
    u_stk = [_mm(t_inv[i], -xa[i][:n] - mkv[i]) for i in rng]
    nuv = [_mm(n_ak[i], jnp.concatenate([u_stk[i], v_stk[i]], axis=0)) for i in rng]
    u = [sum(u_stk[i][h * c:(h + 1) * c] for h in range(NH)) for i in rng]
    d_uv = [_mm_tn(jnp.concatenate([u[i], v[i]], axis=0),
                   jnp.concatenate([ka[i] * p_tail[i], k[i] * p_tail[i]], axis=0)) for i in rng]
    outs = []
    for i in rng:
        y_stk = xa[i][n:] + nuv[i]
        y = sum(y_stk[h * c:(h + 1) * c] for h in range(NH))
        outs.append((y, st[i] * jnp.exp(last[i]) + ones_bd * d_uv[i]))
    return outs


def _rwkv_kernel(x_r_ref, x_k_ref, x_v_ref, x_l_ref, sh0_ref, s0_ref, mu_ref, w0_ref, ww_ref, a0_ref,
                 wa_ref, wg_ref, kk_ref, ka_ref, rk_ref, lng_ref, lnb_ref,
                 o_ref, sfin_ref, shfin_ref,
                 st_ref, sh_ref, r_s, lw_s, k_s, v_s, kkn_s, a_s, y_s, bonus_s, gate_s, *, nb, tb, c):
    tbp = max(tb, c)

    @pl.when(pl.program_id(1) == 0)
    def _():
        st_ref[...] = s0_ref[...]
        sh_ref[...] = sh0_ref[...]

    masks = _head_masks()
    ones_bd = _head_ones()
    n = NH * c
    row_t = _iota2((n, 2 * n), 0) % c
    col_t = _iota2((n, 2 * n), 1) % c
    consts = (masks, ones_bd, _eye(n), (_iota2((c, c), 0) >= _iota2((c, c), 1)).astype(F32),
              col_t < row_t, col_t <= row_t)
    first_row = _iota2((tb, GW), 0) == 0

    for b in range(nb):
        def mixed(x_ref, j):
            x = x_ref[b]
            prev = jnp.where(first_row, sh_ref[b, j:j + 1, :], pltpu.roll(x, 1, axis=0))
            sh_ref[b, j:j + 1, :] = x[tb - 1:tb, :]
            return x + (prev - x) * mu_ref[j:j + 1, :]

        r = mixed(x_r_ref, 0)
        k = mixed(x_k_ref, 1)
        v = mixed(x_v_ref, 2)
        xl = mixed(x_l_ref, 3)
        log_w = -_softplus(-(w0_ref[...] + _mm(jnp.tanh(xl), ww_ref[...]))) - 0.5
        a = _sigmoid(a0_ref[...] + _mm(xl, wa_ref[...]))
        gate = _mm(_sigmoid(xl), wg_ref[...])
        kk = k * kk_ref[...]
        kk = kk * lax.rsqrt(jnp.maximum(_head_sum(kk * kk, ones_bd), 1e-24))
        k = k * (1.0 + (a - 1.0) * ka_ref[...])
        r_s[b, 0:tb, :] = r
        lw_s[b, 0:tb, :] = -jnp.exp(log_w)
        k_s[b, 0:tb, :] = k
        v_s[b, 0:tb, :] = v
        kkn_s[b, 0:tb, :] = kk
        a_s[b, 0:tb, :] = a
        bonus_s[b] = _head_sum(r * k * rk_ref[...], ones_bd) * v
        gate_s[b] = gate
        if tbp > tb:
            zeros = jnp.zeros((tbp - tb, GW), F32)
            for s in (r_s, lw_s, k_s, v_s, kkn_s, a_s):
                s[b, tb:tbp, :] = zeros

    def chunk(ci, carry):
        r0 = pl.multiple_of(ci * c, c)
        ins = [tuple(s[b, pl.ds(r0, c), :] for s in (r_s, lw_s, k_s, v_s, kkn_s, a_s)) + (st_ref[b],)
               for b in range(nb)]
        outs = _rwkv_chunk(ins, consts)
        for b, (y, st_new) in enumerate(outs):
            y_s[b, pl.ds(r0, c), :] = y
            st_ref[b] = st_new
        return carry

    lax.fori_loop(0, tbp // c, chunk, 0)

    for b in range(nb):
        y = y_s[b, 0:tb, :]
        mean = _head_sum(y, ones_bd) * (1.0 / HD)
        d = y - mean
        var = _head_sum(d * d, ones_bd) * (1.0 / HD)
        y = d * lax.rsqrt(var + RW_LN_EPS) * lng_ref[...] + lnb_ref[...]
        o_ref[b] = (y + bonus_s[b]) * gate_s[b]
    sfin_ref[...] = st_ref[...]
    shfin_ref[...] = sh_ref[...]


def rwkv_apply(proj, s0, shift0, p, l, *, tb, nb):
    bsz, seq, _ = proj.shape
    c = CHUNK
    tbp = max(tb, c)
    row = pl.BlockSpec((1, GW), lambda b, t: (0, 0))
    mat = pl.BlockSpec((GW, GW), lambda b, t: (0, 0))
    sh_spec = pl.BlockSpec((nb, 4, GW), lambda b, t: (b, 0, 0))
    zeros = lambda r: jnp.zeros((r, GW), F32)
    ww = jnp.concatenate([p['rw_w_w2'][l], zeros(192)], axis=0).astype(BF16)
    wa = jnp.concatenate([zeros(64), p['rw_w_a2'][l], zeros(128)], axis=0).astype(BF16)
    wg = jnp.concatenate([zeros(128), p['rw_w_g2'][l]], axis=0).astype(BF16)
    r1 = lambda name: p[name][l].reshape(1, GW)
    seq_buf = pltpu.VMEM((nb, tbp, GW), F32)
    blk_buf = pltpu.VMEM((nb, tb, GW), F32)
    y, st, sh = pl.pallas_call(
        functools.partial(_rwkv_kernel, nb=nb, tb=tb, c=c),
        out_shape=(jax.ShapeDtypeStruct((bsz, seq, GW), F32), jax.ShapeDtypeStruct((bsz, GW, GW), F32),
                   jax.ShapeDtypeStruct((bsz, 4, GW), F32)),
        grid=(bsz // nb, seq // tb),
        in_specs=[_proj_spec(nb, tb, 9), _proj_spec(nb, tb, 10), _proj_spec(nb, tb, 11), _proj_spec(nb, tb, 12),
                  sh_spec, _state_spec(nb), pl.BlockSpec((4, GW), lambda b, t: (0, 0)),
                  row, mat, row, mat, mat, row, row, row, row, row],
        out_specs=(pl.BlockSpec((nb, tb, GW), lambda b, t: (b, t, 0)), _state_spec(nb), sh_spec),
        scratch_shapes=[pltpu.VMEM((nb, GW, GW), F32), pltpu.VMEM((nb, 4, GW), F32),
                        seq_buf, seq_buf, seq_buf, seq_buf, seq_buf, seq_buf, seq_buf, blk_buf, blk_buf],
        compiler_params=_cparams(("parallel", "arbitrary")),
        name="rwkv7",
    )(proj, proj, proj, proj, shift0.reshape(bsz, 4, GW), _pack_state(s0), p['rw_mu'][l].reshape(4, GW),
      r1('rw_w0'), ww, r1('rw_a0'), wa, wg, r1('rw_k_k'), r1('rw_k_a'), r1('rw_r_k'), r1('rw_ln_g'), r1('rw_ln_b'))
    return y, _unpack_state(st), sh.reshape(bsz, RW_PROJ)


def _tiles(bsz, seq):
    if seq >= 128:
        return dict(tm=512, s5_steps=128, s5_nb=bsz, mix_tb=128, mix_nb=bsz)
    return dict(tm=512, s5_steps=seq, s5_nb=bsz, mix_tb=seq, mix_nb=8)


def _trunk_layer(x, bsz, seq, pos0, attend, st, p, wb, l, lb, final_norm):
    cfg = _tiles(bsz, seq)
    tm = cfg['tm']
    row = lambda name: p[name][l].reshape(1, -1)
    proj = norm_matmul(x, row('norm_mix'), wb['w_in'][l], tm=tm, tn=IN_WIDTH // 2)
    proj = proj.reshape(bsz, seq, IN_WIDTH)
    y_s5, s5_re, s5_im = s5_apply(proj[..., :GW], st['s5_re'], st['s5_im'], p, l,
                                  steps=cfg['s5_steps'], nb=cfg['s5_nb'])
    y_hg, hg_s = hgrn_apply(proj, st['hgrn'], lb, p['hg_norm'][l], tb=cfg['mix_tb'], nb=cfg['mix_nb'])
    y_rt, rt_s = retention_apply(proj, st['ret'], pos0, tb=cfg['mix_tb'], nb=cfg['mix_nb'])
    y_rw, rw_s, shift = rwkv_apply(proj, st['rwkv'], st['shift'], p, l, tb=cfg['mix_tb'], nb=cfg['mix_nb'])
    parts = [y.reshape(bsz * seq, GW) for y in (y_s5, y_hg, y_rt, y_rw)]
    x = mix_out(parts, wb['w_out'][l], x, tm=tm)
    q = norm_matmul(x, row('norm_mem'), wb['mem_w_q'][l], tm=tm, tn=D_MODEL)
    x = matmul_residual(attend(q), wb['mem_w_o'][l], x, tm=tm)
    x = ffn(x, row('norm_ffn'), wb['ffn_w_up'][l], wb['ffn_w_down'][l], p['norm_final'].reshape(1, -1),
            tm=1024, tf=512, final_norm=final_norm)
    return x, (s5_re, s5_im, hg_s, rt_s, rw_s, shift)


def kernel(x_prompt, x_sample, mem_prompt, state_s5_re, state_s5_im, state_hgrn, state_ret, state_rwkv,
           state_rwkv_shift, cache_mem_k, cache_mem_v, norm_mix, w_in, w_out, s5_lam_re, s5_lam_im,
           s5_b_re, s5_b_im, s5_c_re, s5_c_im, s5_d, s5_log_step, s5_w_glu, s5_b_glu, s5_norm,
           hg_lb_logits, hg_norm, rw_mu, rw_w0, rw_w_w2, rw_a0, rw_w_a2, rw_w_g2, rw_k_k, rw_k_a, rw_r_k,
           rw_ln_g, rw_ln_b, norm_mem, mem_w_q, mem_w_k, mem_w_v, mem_w_o, norm_ffn, ffn_w_up, ffn_w_down,
           norm_final):
    p = dict(norm_mix=norm_mix, s5_lam_re=s5_lam_re, s5_lam_im=s5_lam_im, s5_b_re=s5_b_re, s5_b_im=s5_b_im,
             s5_c_re=s5_c_re, s5_c_im=s5_c_im, s5_d=s5_d, s5_log_step=s5_log_step, s5_w_glu=s5_w_glu,
             s5_b_glu=s5_b_glu, s5_norm=s5_norm, hg_norm=hg_norm, rw_mu=rw_mu, rw_w0=rw_w0, rw_w_w2=rw_w_w2,
             rw_a0=rw_a0, rw_w_a2=rw_w_a2, rw_w_g2=rw_w_g2, rw_k_k=rw_k_k, rw_k_a=rw_k_a, rw_r_k=rw_r_k,
             rw_ln_g=rw_ln_g, rw_ln_b=rw_ln_b, norm_mem=norm_mem, norm_ffn=norm_ffn, norm_final=norm_final)
    wb = {name: w.astype(BF16) for name, w in dict(
        w_in=w_in, w_out=w_out, mem_w_q=mem_w_q, mem_w_k=mem_w_k, mem_w_v=mem_w_v, mem_w_o=mem_w_o,
        ffn_w_up=ffn_w_up, ffn_w_down=ffn_w_down).items()}
    lb_all = jnp.cumsum(jax.nn.softmax(hg_lb_logits.astype(F32), axis=0), axis=0)
    lb_all = lb_all - lb_all[0:1]

    bp, lp, _ = x_prompt.shape
    bs, ls, _ = x_sample.shape
    yp = x_prompt.reshape(bp * lp, D_MODEL)
    ys = x_sample.reshape(bs * ls, D_MODEL)
    mem2d = mem_prompt.reshape(bp * MEM_LEN, D_MODEL)
    zero_state = dict(s5_re=jnp.zeros((bp, S5_NG, S5_P), F32), s5_im=jnp.zeros((bp, S5_NG, S5_P), F32),
                      hgrn=jnp.zeros((bp, NH, HD, HD), F32), ret=jnp.zeros((bp, NH, HD, HD), F32),
                      rwkv=jnp.zeros((bp, NH, HD, HD), F32), shift=jnp.zeros((bp, RW_PROJ), F32))
    p_states, s_states, p_mk, p_mv = [], [], [], []
    for l in range(DEPTH):
        final = l == DEPTH - 1
        mk = matmul(mem2d, wb['mem_w_k'][l], tm=512, tn=D_MODEL)
        mv = matmul(mem2d, wb['mem_w_v'][l], tm=512, tn=D_MODEL)
        attend_p = functools.partial(cross_attention, mem_k=mk.reshape(bp, MEM_LEN, D_MODEL),
                                     mem_v=mv.reshape(bp, MEM_LEN, D_MODEL), nb=1, rows=512)
        yp, stp = _trunk_layer(yp, bp, lp, 0, attend_p, zero_state, p, wb, l, lb_all[l], final)
        p_states.append(stp)
        p_mk.append(mk.reshape(bp, MEM_LEN, MEM_HEADS, MEM_HD))
        p_mv.append(mv.reshape(bp, MEM_LEN, MEM_HEADS, MEM_HD))
        sst = dict(s5_re=state_s5_re[l], s5_im=state_s5_im[l], hgrn=state_hgrn[l], ret=state_ret[l],
                   rwkv=state_rwkv[l], shift=state_rwkv_shift[l])
        attend_s = functools.partial(cross_attention_cache, cache_k=cache_mem_k, cache_v=cache_mem_v, layer=l,
                                     nb=4, rows=ls)
        ys, sts = _trunk_layer(ys, bs, ls, PAST_LEN, attend_s, sst, p, wb, l, lb_all[l], final)
        s_states.append(sts)
    stack = lambda states, i: jnp.stack([s[i] for s in states])
    return (yp.reshape(bp, lp, D_MODEL), ys.reshape(bs, ls, D_MODEL),
            *[stack(p_states, i) for i in range(6)], jnp.stack(p_mk), jnp.stack(p_mv),
            *[stack(s_states, i) for i in range(6)])
```

```python
import functools
import math

import jax
import jax.numpy as jnp
from jax import lax
from jax.experimental import pallas as pl
from jax.experimental.pallas import tpu as pltpu

F32 = jnp.float32
BF16 = jnp.bfloat16

D_MODEL = 1024
DEPTH = 2
PAST_LEN = 16384
GW = 256
HD = 64
NH = GW // HD
S5_GROUP = 16
S5_NG = GW // S5_GROUP
S5_P = 64
S5_W = S5_NG * S5_P
RW_PROJ = 4 * GW
IN_WIDTH = 13 * GW
MEM_LEN = 256
MEM_HEADS = 4
MEM_HD = D_MODEL // MEM_HEADS
D_FF = 4 * D_MODEL
EPS = 1e-6
RW_LN_EPS = 64e-5
ROPE_BASE = 10000.0
CHUNK = 16
RET_CHUNK = 64

VMEM_LIMIT = 56 * 1024 * 1024
SUBLANES = 8


def _cparams(sem):
    return pltpu.CompilerParams(dimension_semantics=sem, vmem_limit_bytes=VMEM_LIMIT)


def _mm(a, b):
    return jnp.dot(a.astype(BF16), b.astype(BF16), preferred_element_type=F32)


def _mm_nt(a, b):
    return lax.dot_general(a.astype(BF16), b.astype(BF16), (((1,), (1,)), ((), ())),
                           preferred_element_type=F32)


def _mm_tn(a, b):
    return lax.dot_general(a.astype(BF16), b.astype(BF16), (((0,), (0,)), ((), ())),
                           preferred_element_type=F32)


def _split3(x):
    hi = x.astype(BF16)
    r1 = x - hi.astype(F32)
    mid = r1.astype(BF16)
    lo = (r1 - mid.astype(F32)).astype(BF16)
    return hi, mid, lo


def _mm_exact_lhs(sel, x):
    s = sel.astype(BF16)
    hi, mid, lo = _split3(x)
    return (jnp.dot(s, hi, preferred_element_type=F32) + jnp.dot(s, mid, preferred_element_type=F32)
            + jnp.dot(s, lo, preferred_element_type=F32))


def _rms(x, gain):
    return x * lax.rsqrt(jnp.mean(x * x, axis=-1, keepdims=True) + EPS) * gain


def _sigmoid(x):
    return 1.0 / (1.0 + jnp.exp(-x))


def _iota2(shape, axis):
    return lax.broadcasted_iota(jnp.int32, shape, axis)


def _head_ones():
    return (_iota2((GW, GW), 0) // HD == _iota2((GW, GW), 1) // HD).astype(F32)


def _eye(n):
    return (_iota2((n, n), 0) == _iota2((n, n), 1)).astype(F32)


def _head_sum(x, ones_bd):
    s = ones_bd.astype(BF16)
    hi = x.astype(BF16)
    lo = (x - hi.astype(F32)).astype(BF16)
    return jnp.dot(hi, s, preferred_element_type=F32) + jnp.dot(lo, s, preferred_element_type=F32)


def _norm_mm_kernel(x_ref, g_ref, w_ref, o_ref, xn_ref):
    @pl.when(pl.program_id(1) == 0)
    def _():
        xn_ref[...] = _rms(x_ref[...], g_ref[...]).astype(BF16)

    o_ref[...] = jnp.dot(xn_ref[...], w_ref[...], preferred_element_type=F32)


def norm_matmul(x, gain, w, *, tm, tn):
    t, d = x.shape
    n = w.shape[1]
    return pl.pallas_call(
        _norm_mm_kernel,
        out_shape=jax.ShapeDtypeStruct((t, n), F32),
        grid=(t // tm, n // tn),
        in_specs=[pl.BlockSpec((tm, d), lambda i, j: (i, 0)),
                  pl.BlockSpec((1, d), lambda i, j: (0, 0)),
                  pl.BlockSpec((d, tn), lambda i, j: (0, j))],
        out_specs=pl.BlockSpec((tm, tn), lambda i, j: (i, j)),
        scratch_shapes=[pltpu.VMEM((tm, d), BF16)],
        compiler_params=_cparams(("parallel", "arbitrary")),
        name="norm_matmul",
    )(x, gain, w)


def _mm_kernel(a_ref, w_ref, o_ref):
    o_ref[...] = _mm(a_ref[...], w_ref[...])


def matmul(a, w, *, tm, tn):
    t, k = a.shape
    n = w.shape[1]
    return pl.pallas_call(
        _mm_kernel,
        out_shape=jax.ShapeDtypeStruct((t, n), F32),
        grid=(t // tm, n // tn),
        in_specs=[pl.BlockSpec((tm, k), lambda i, j: (i, 0)),
                  pl.BlockSpec((k, tn), lambda i, j: (0, j))],
        out_specs=pl.BlockSpec((tm, tn), lambda i, j: (i, j)),
        compiler_params=_cparams(("parallel", "parallel")),
        name="matmul",
    )(a, w)


def _mm_res_kernel(a_ref, w_ref, r_ref, o_ref):
    o_ref[...] = r_ref[...] + _mm(a_ref[...], w_ref[...])


def matmul_residual(a, w, res, *, tm):
    t, k = a.shape
    n = w.shape[1]
    return pl.pallas_call(
        _mm_res_kernel,
        out_shape=jax.ShapeDtypeStruct((t, n), F32),
        grid=(t // tm,),
        in_specs=[pl.BlockSpec((tm, k), lambda i: (i, 0)),
                  pl.BlockSpec((k, n), lambda i: (0, 0)),
                  pl.BlockSpec((tm, n), lambda i: (i, 0))],
        out_specs=pl.BlockSpec((tm, n), lambda i: (i, 0)),
        compiler_params=_cparams(("parallel",)),
        name="matmul_residual",
    )(a, w, res)


def _mix_out_q_kernel(a0_ref, a1_ref, a2_ref, a3_ref, w_ref, r_ref, g_ref, wq_ref, x_ref, q_ref):
    a = jnp.concatenate([a0_ref[...], a1_ref[...], a2_ref[...], a3_ref[...]], axis=-1)
    x = r_ref[...] + _mm(a, w_ref[...])
    x_ref[...] = x
    q_ref[...] = jnp.dot(_rms(x, g_ref[...]).astype(BF16), wq_ref[...], preferred_element_type=F32).astype(BF16)


def mix_out_q(parts, w, res, gain, w_q, *, tm):
    t, d = res.shape
    part = pl.BlockSpec((tm, GW), lambda i: (i, 0))
    mat = pl.BlockSpec((d, d), lambda i: (0, 0))
    tile = pl.BlockSpec((tm, d), lambda i: (i, 0))
    return pl.pallas_call(
        _mix_out_q_kernel,
        out_shape=(jax.ShapeDtypeStruct((t, d), F32), jax.ShapeDtypeStruct((t, d), BF16)),
        grid=(t // tm,),
        in_specs=[part, part, part, part, mat, tile, pl.BlockSpec((1, d), lambda i: (0, 0)), mat],
        out_specs=(tile, tile),
        compiler_params=_cparams(("parallel",)),
        name="mix_out_q",
    )(*parts, w, res, gain, w_q)


def _ffn_kernel(x_ref, g_ref, wu_ref, wd_ref, gf_ref, o_ref, *, final_norm):
    x = x_ref[...]
    h = jnp.dot(_rms(x, g_ref[...]).astype(BF16), wu_ref[...], preferred_element_type=F32)
    h = jnp.square(jnp.maximum(h, 0.0))
    y = x + jnp.dot(h.astype(BF16), wd_ref[...], preferred_element_type=F32)
    if final_norm:
        y = _rms(y, gf_ref[...])
    o_ref[...] = y


def ffn(x, gain, w_up, w_down, gain_final, *, tm, final_norm):
    t, d = x.shape
    ff = w_up.shape[1]
    resident = lambda shape: pl.BlockSpec(shape, lambda i: (0, 0), pipeline_mode=pl.Buffered(1))
    return pl.pallas_call(
        functools.partial(_ffn_kernel, final_norm=final_norm),
        out_shape=jax.ShapeDtypeStruct((t, d), F32),
        grid=(t // tm,),
        in_specs=[pl.BlockSpec((tm, d), lambda i: (i, 0)),
                  pl.BlockSpec((1, d), lambda i: (0, 0)),
                  resident((d, ff)), resident((ff, d)),
                  pl.BlockSpec((1, d), lambda i: (0, 0))],
        out_specs=pl.BlockSpec((tm, d), lambda i: (i, 0)),
        compiler_params=_cparams(("parallel",)),
        name="ffn",
    )(x, gain, w_up, w_down, gain_final)


def _attn_out_kernel(q_ref, k_ref, v_ref, wo_ref, r_ref, x_ref):
    heads = []
    for h in range(MEM_HEADS):
        sl = slice(h * MEM_HD, (h + 1) * MEM_HD)
        s = _mm_nt(q_ref[:, sl], k_ref[:, sl]) * (MEM_HD ** -0.5)
        p = jnp.exp(s - jnp.max(s, axis=-1, keepdims=True))
        p = p / jnp.sum(p, axis=-1, keepdims=True)
        heads.append(_mm(p, v_ref[:, sl]))
    x_ref[...] = r_ref[...] + _mm(jnp.concatenate(heads, axis=-1), wo_ref[...])


def cross_attention_out(q, mem_k, mem_v, w_o, res, *, rows):
    t, d = res.shape
    bsz = mem_k.shape[0]
    lt = t // bsz // rows
    tile = pl.BlockSpec((rows, d), lambda b, l: (b * lt + l, 0))
    mem = pl.BlockSpec((None, MEM_LEN, d), lambda b, l: (b, 0, 0))
    return pl.pallas_call(
        _attn_out_kernel,
        out_shape=jax.ShapeDtypeStruct((t, d), F32),
        grid=(bsz, lt),
        in_specs=[tile, mem, mem, pl.BlockSpec((d, d), lambda b, l: (0, 0)), tile],
        out_specs=tile,
        compiler_params=_cparams(("parallel", "arbitrary")),
        name="cross_attention_out",
    )(q, mem_k, mem_v, w_o, res)


def _attn_cache_kernel(q_ref, k_ref, v_ref, o_ref, *, nb, rows):
    nr = MEM_HEADS * rows
    same_head = _iota2((nr, MEM_HEADS * MEM_LEN), 1) % MEM_HEADS == _iota2((nr, MEM_HEADS * MEM_LEN), 0) // rows
    q_all = q_ref[...].astype(F32)
    for b in range(nb):
        q = q_all[b * rows:(b + 1) * rows, :]
        qs = jnp.concatenate([q[:, h * MEM_HD:(h + 1) * MEM_HD] for h in range(MEM_HEADS)], axis=0)
        s = _mm_nt(qs, k_ref[0, b].reshape(MEM_HEADS * MEM_LEN, MEM_HD)) * (MEM_HD ** -0.5)
        s = jnp.where(same_head, s, -1e30)
        p = jnp.exp(s - jnp.max(s, axis=-1, keepdims=True))
        p = p / jnp.sum(p, axis=-1, keepdims=True)
        o = _mm(p, v_ref[0, b].reshape(MEM_HEADS * MEM_LEN, MEM_HD))
        for h in range(MEM_HEADS):
            o_ref[b * rows:(b + 1) * rows, h * MEM_HD:(h + 1) * MEM_HD] = o[h * rows:(h + 1) * rows]


def cross_attention_cache(q, cache_k, cache_v, layer, *, nb, rows):
    t = q.shape[0]
    bsz = cache_k.shape[1]
    mem = pl.BlockSpec((1, nb, MEM_LEN, MEM_HEADS, MEM_HD), lambda b: (layer, b, 0, 0, 0))
    return pl.pallas_call(
        functools.partial(_attn_cache_kernel, nb=nb, rows=rows),
        out_shape=jax.ShapeDtypeStruct((t, D_MODEL), F32),
        grid=(bsz // nb,),
        in_specs=[pl.BlockSpec((nb * rows, D_MODEL), lambda b: (b, 0)), mem, mem],
        out_specs=pl.BlockSpec((nb * rows, D_MODEL), lambda b: (b, 0)),
        compiler_params=_cparams(("parallel",)),
        name="cross_attention_cache",
    )(q, cache_k, cache_v)


def _gelu_tanh(x):
    return 0.5 * x * (1.0 + jnp.tanh(math.sqrt(2.0 / math.pi) * (x + 0.044715 * (x * x * x))))


def _s5_kernel(u_ref, h0_ref, lam_ref, bblk_ref, cblk_ref, d_ref, wglu_ref, bglu_ref, gain_ref,
               y_ref, hfin_ref, scr_ref, *, steps, nb):
    @pl.when(pl.program_id(1) == 0)
    def _():
        scr_ref[0:nb, :] = h0_ref[...]

    u = u_ref[...].reshape(steps * nb, GW)
    scr_ref[nb:, :] = _mm(u, bblk_ref[...])
    lam_re = lam_ref[0:1, :]
    lam_im = lam_ref[1:2, :]

    def step(t, carry):
        p0 = pl.multiple_of(t * nb, nb)
        c0 = pl.multiple_of((t + 1) * nb, nb)
        h_re = scr_ref[pl.ds(p0, nb), 0:S5_W]
        h_im = scr_ref[pl.ds(p0, nb), S5_W:]
        scr_ref[pl.ds(c0, nb), 0:S5_W] = scr_ref[pl.ds(c0, nb), 0:S5_W] + lam_re * h_re - lam_im * h_im
        scr_ref[pl.ds(c0, nb), S5_W:] = scr_ref[pl.ds(c0, nb), S5_W:] + lam_re * h_im + lam_im * h_re
        return carry

    lax.fori_loop(0, steps, step, 0)
    h_last = scr_ref[steps * nb:, :]
    hfin_ref[...] = h_last
    y = _mm(scr_ref[nb:, :], cblk_ref[...]) + d_ref[...] * u
    y = _gelu_tanh(y)
    y = y * _sigmoid(_mm(y, wglu_ref[...]) + bglu_ref[...])
    y_ref[...] = _rms(y, gain_ref[...]).reshape(steps, nb, GW)
    scr_ref[0:nb, :] = h_last


def s5_mixer(u_tm, h0, lam, bblk, cblk, dvec, wglu, bglu, gain, *, steps, nb):
    seq, bsz, _ = u_tm.shape
    const = lambda shape: pl.BlockSpec(shape, lambda b, t: (0, 0))
    return pl.pallas_call(
        functools.partial(_s5_kernel, steps=steps, nb=nb),
        out_shape=(jax.ShapeDtypeStruct((seq, bsz, GW), F32),
                   jax.ShapeDtypeStruct((bsz, 2 * S5_W), F32)),
        grid=(bsz // nb, seq // steps),
        in_specs=[pl.BlockSpec((steps, nb, GW), lambda b, t: (t, b, 0)),
                  pl.BlockSpec((nb, 2 * S5_W), lambda b, t: (b, 0)),
                  const((2, S5_W)), const((GW, 2 * S5_W)), const((2 * S5_W, GW)), const((1, GW)),
                  const((GW, GW)), const((1, GW)), const((1, GW))],
        out_specs=(pl.BlockSpec((steps, nb, GW), lambda b, t: (t, b, 0)),
                   pl.BlockSpec((nb, 2 * S5_W), lambda b, t: (b, 0))),
        scratch_shapes=[pltpu.VMEM(((steps + 1) * nb, 2 * S5_W), F32)],
        compiler_params=_cparams(("parallel", "arbitrary")),
        name="s5_mixer",
    )(u_tm, h0, lam, bblk, cblk, dvec, wglu, bglu, gain)


def s5_params(lam_re, lam_im, b_re, b_im, c_re, c_im, log_step):
    step = jnp.exp(log_step)[:, None]
    mag = jnp.exp(lam_re * step)
    lbar_re = mag * jnp.cos(lam_im * step)
    lbar_im = mag * jnp.sin(lam_im * step)
    den = lam_re * lam_re + lam_im * lam_im
    f_re = ((lbar_re - 1.0) * lam_re + lbar_im * lam_im) / den
    f_im = (lbar_im * lam_re - (lbar_re - 1.0) * lam_im) / den
    bbar_re = f_re[..., None] * b_re - f_im[..., None] * b_im
    bbar_im = f_re[..., None] * b_im + f_im[..., None] * b_re
    eye_g = jnp.eye(S5_NG, dtype=F32)

    def in_blk(m):
        return jnp.einsum('gph,gk->ghkp', m, eye_g).reshape(GW, S5_W)

    def out_blk(m):
        return jnp.einsum('ghp,gk->gpkh', m, eye_g).reshape(S5_W, GW)

    bblk = jnp.concatenate([in_blk(bbar_re), in_blk(bbar_im)], axis=1)
    cblk = jnp.concatenate([out_blk(c_re), -out_blk(c_im)], axis=0)
    lam2 = jnp.stack([lbar_re.reshape(S5_W), lbar_im.reshape(S5_W)])
    return lam2, bblk.astype(BF16), cblk.astype(BF16)


def s5_apply(u, h0_re, h0_im, p, l, *, steps, nb):
    bsz = u.shape[0]
    lam2, bblk, cblk = s5_params(p['s5_lam_re'][l], p['s5_lam_im'][l], p['s5_b_re'][l], p['s5_b_im'][l],
                                 p['s5_c_re'][l], p['s5_c_im'][l], p['s5_log_step'][l])
    h0 = jnp.concatenate([h0_re.reshape(bsz, S5_W), h0_im.reshape(bsz, S5_W)], axis=1)
    y, h = s5_mixer(jnp.swapaxes(u, 0, 1), h0, lam2, bblk, cblk, p['s5_d'][l].reshape(1, GW),
                    p['s5_w_glu'][l].astype(BF16), p['s5_b_glu'][l].reshape(1, GW),
                    p['s5_norm'][l].reshape(1, GW), steps=steps, nb=nb)
    return (jnp.swapaxes(y, 0, 1), h[:, :S5_W].reshape(bsz, S5_NG, S5_P),
            h[:, S5_W:].reshape(bsz, S5_NG, S5_P))


def _load_state(s0_ref, st_ref, nb, transpose):
    for b in range(nb):
        rows = []
        for h in range(NH):
            pieces = [s0_ref[b, h]]
            if h:
                pieces.insert(0, jnp.zeros((HD, h * HD), F32))
            if h < NH - 1:
                pieces.append(jnp.zeros((HD, (NH - 1 - h) * HD), F32))
            rows.append(jnp.concatenate(pieces, axis=1))
        st = jnp.concatenate(rows, axis=0)
        st_ref[b] = st.T if transpose else st


def _store_state(st_ref, sfin_ref, nb, transpose):
    for b in range(nb):
        st = st_ref[b].T if transpose else st_ref[b]
        for h in range(NH):
            sfin_ref[b, h] = st[h * HD:(h + 1) * HD, h * HD:(h + 1) * HD]


def _head_masks():
    lane_head = _iota2((1, GW), 1) // HD
    return [(lane_head == h).astype(F32) for h in range(NH)]


def _stack_heads(x, masks):
    return jnp.concatenate([x * m for m in masks], axis=0)


def _pad_rows(x, rows):
    if x.shape[0] == rows:
        return x
    return jnp.concatenate([x, jnp.zeros((rows - x.shape[0], x.shape[1]), x.dtype)], axis=0)


def _silu(x):
    return x * _sigmoid(x)


def _chunk_rows(ref, b, r0, rows, c):
    return _pad_rows(ref[b, pl.ds(r0, rows), :], c)


def _ret_kernel(q_ref, k_ref, v_ref, g_ref, cos_ref, sin_ref, s0_ref, o_ref, sfin_ref, st_ref,
                *, nb, tb, c, c_real):
    rows = min(tb, c)

    @pl.when(pl.program_id(1) == 0)
    def _():
        _load_state(s0_ref, st_ref, nb, transpose=True)

    masks = _head_masks()
    ones_bd = _head_ones()
    lane_head = _iota2((1, GW), 1) // HD
    log_gamma = jnp.zeros((1, GW), F32)
    for h in range(NH):
        log_gamma = jnp.where(lane_head == h, math.log(1.0 - 2.0 ** (-5.0 - h)), log_gamma)
    tt = _iota2((c, GW), 0).astype(F32)
    scale = HD ** -0.5
    g_q = jnp.exp(log_gamma * (tt + 1.0))
    g_k = jnp.exp(-log_gamma * (tt + 1.0)) * scale
    g_tail = jnp.exp(log_gamma * (c_real - 1.0 - tt)) * scale
    g_chunk = jnp.exp(log_gamma * float(c_real))
    causal = _iota2((c, NH * c), 0) >= _iota2((c, NH * c), 1) % c
    first_half = _iota2((c, GW), 1) % HD < HD // 2

    def rope(x, cos, sin):
        swapped = jnp.where(first_half, pltpu.roll(x, GW - HD // 2, axis=1), pltpu.roll(x, HD // 2, axis=1))
        return x * cos + swapped * sin

    def chunk(ci, carry):
        r0 = pl.multiple_of(ci * rows, rows)
        cos = _pad_rows(cos_ref[pl.ds(r0, rows), :], c)
        sin = _pad_rows(sin_ref[pl.ds(r0, rows), :], c)
        rng = range(nb)
        qt = [rope(_chunk_rows(q_ref, b, r0, rows, c), cos, sin) * g_q for b in rng]
        k = [rope(_chunk_rows(k_ref, b, r0, rows, c), cos, sin) for b in rng]
        v = [_chunk_rows(v_ref, b, r0, rows, c) for b in rng]
        st = [st_ref[b] for b in rng]
        sc = [_mm_nt(qt[b], _stack_heads(k[b] * g_k, masks)) for b in rng]
        o_in = [_mm(jnp.where(causal, sc[b], 0.0), _stack_heads(v[b], masks)) for b in rng]
        o_st = [_mm_nt(qt[b], st[b]) for b in rng]
        d_st = [_mm_tn(v[b], k[b] * g_tail) for b in rng]
        o = [o_in[b] + o_st[b] for b in rng]
        ms = [_head_sum(o[b] * o[b], ones_bd) for b in rng]
        for b in rng:
            st_ref[b] = st[b] * g_chunk + ones_bd * d_st[b]
            ob = o[b] * lax.rsqrt(ms[b] * (1.0 / HD) + EPS) * _silu(_chunk_rows(g_ref, b, r0, rows, c))
            o_ref[b, pl.ds(r0, rows), :] = ob[:rows]
        return carry

    lax.fori_loop(0, tb // rows, chunk, 0)

    @pl.when(pl.program_id(1) == pl.num_programs(1) - 1)
    def _():
        _store_state(st_ref, sfin_ref, nb, transpose=True)


def _rope_tables(pos0, seq):
    half = HD // 2
    inv = ROPE_BASE ** (-jnp.arange(half, dtype=F32) / half)
    pos = pos0 + jnp.arange(seq, dtype=jnp.int32)
    ang = pos.astype(F32)[:, None] * inv[None, :]
    cos, sin = jnp.cos(ang), jnp.sin(ang)
    return (jnp.tile(jnp.concatenate([cos, cos], axis=-1), (1, NH)),
            jnp.tile(jnp.concatenate([-sin, sin], axis=-1), (1, NH)))


def _proj_spec(nb, tb, col):
    return pl.BlockSpec((nb, tb, GW), lambda b, t: (b, t, col))


def _state_spec(nb):
    return pl.BlockSpec((nb, NH, HD, HD), lambda b, t: (b, 0, 0, 0))


def _state_shape(bsz):
    return jax.ShapeDtypeStruct((bsz, NH, HD, HD), F32)


def retention_apply(proj, s0, pos0, *, tb, nb):
    bsz, seq, _ = proj.shape
    c = RET_CHUNK if tb >= RET_CHUNK else CHUNK
    cos, sin = _rope_tables(pos0, seq)
    y, st = pl.pallas_call(
        functools.partial(_ret_kernel, nb=nb, tb=tb, c=c, c_real=min(tb, c)),
        out_shape=(jax.ShapeDtypeStruct((bsz, seq, GW), F32), _state_shape(bsz)),
        grid=(bsz // nb, seq // tb),
        in_specs=[_proj_spec(nb, tb, 5), _proj_spec(nb, tb, 6), _proj_spec(nb, tb, 7), _proj_spec(nb, tb, 8),
                  pl.BlockSpec((tb, GW), lambda b, t: (t, 0)), pl.BlockSpec((tb, GW), lambda b, t: (t, 0)),
                  _state_spec(nb)],
        out_specs=(pl.BlockSpec((nb, tb, GW), lambda b, t: (b, t, 0)), _state_spec(nb)),
        scratch_shapes=[pltpu.VMEM((nb, GW, GW), F32)],
        compiler_params=_cparams(("parallel", "arbitrary")),
        name="retention",
    )(proj, proj, proj, proj, cos, sin, s0)
    return y, st


def _log_sigmoid(z):
    return jnp.minimum(z, 0.0) - jnp.log(1.0 + jnp.exp(-jnp.abs(z)))


def _hgrn_kernel(q_ref, f_ref, i_ref, g_ref, lb_ref, gain_ref, s0_ref, o_ref, sfin_ref, st_ref,
                 *, nb, tb, c):
    rows = min(tb, c)

    @pl.when(pl.program_id(1) == 0)
    def _():
        _load_state(s0_ref, st_ref, nb, transpose=True)

    t_hi = -(-rows // SUBLANES) * SUBLANES
    ones_bd = _head_ones()
    tri = (_iota2((c, c), 0) >= _iota2((c, c), 1)).astype(F32)
    t_idx = _iota2((c, GW), 0)
    lb = lb_ref[...]
    log_lb = jnp.log(lb)
    gain = gain_ref[...]

    def chunk(ci, carry):
        r0 = pl.multiple_of(ci * rows, rows)
        rng = range(nb)
        valid = t_idx < rows

        def gates(b):
            z = _chunk_rows(f_ref, b, r0, rows, c)
            ls_pos = _log_sigmoid(z)
            ls_neg = ls_pos - z
            b2 = log_lb + ls_neg
            log_f = jnp.maximum(ls_pos, b2) + jnp.log(1.0 + jnp.exp(-jnp.abs(ls_pos - b2)))
            return jnp.where(valid, log_f, 0.0), jnp.where(valid, (1.0 - lb) * jnp.exp(ls_neg), 0.0)

        log_f, key = zip(*[gates(b) for b in rng])
        q = [_silu(_chunk_rows(q_ref, b, r0, rows, c)) for b in rng]
        v = [_chunk_rows(i_ref, b, r0, rows, c) for b in rng]
        st = [st_ref[b] for b in rng]
        cum = [_mm_exact_lhs(tri, log_f[b]) for b in rng]
        last = [cum[b][c - 1:c, :] for b in rng]
        def pair_rows(b, s):
            lo = SUBLANES * (s // SUBLANES)
            t_rows = _iota2((t_hi - lo, GW), 0) + lo
            decay = jnp.where(t_rows >= s, jnp.exp(cum[b][lo:t_hi] - cum[b][s:s + 1, :]), 0.0)
            return decay * q[b][lo:t_hi] * key[b][s:s + 1, :]

        sc = [_mm(jnp.concatenate([pair_rows(b, s) for s in range(rows)], axis=0), ones_bd) for b in rng]
        o_st = [_mm_nt(q[b] * jnp.exp(cum[b]), st[b]) for b in rng]
        d_st = [_mm_tn(v[b], key[b] * jnp.exp(last[b] - cum[b])) for b in rng]

        def intra(b):
            tiles = [None] * (c // SUBLANES)
            off = 0
            for s in range(rows):
                for ti in range(s // SUBLANES, t_hi // SUBLANES):
                    term = sc[b][off:off + SUBLANES, :] * v[b][s:s + 1, :]
                    tiles[ti] = term if tiles[ti] is None else tiles[ti] + term
                    off += SUBLANES
            zero = jnp.zeros((SUBLANES, GW), F32)
            return jnp.concatenate([zero if t is None else t for t in tiles], axis=0)

        o = [o_st[b] + intra(b) for b in rng]
        ms = [_head_sum(o[b] * o[b], ones_bd) for b in rng]
        for b in rng:
            st_ref[b] = st[b] * jnp.exp(last[b]) + ones_bd * d_st[b]
            ob = o[b] * lax.rsqrt(ms[b] * (1.0 / HD) + EPS) * gain * _silu(_chunk_rows(g_ref, b, r0, rows, c))
            o_ref[b, pl.ds(r0, rows), :] = ob[:rows]
        return carry

    lax.fori_loop(0, tb // rows, chunk, 0)

    @pl.when(pl.program_id(1) == pl.num_programs(1) - 1)
    def _():
        _store_state(st_ref, sfin_ref, nb, transpose=True)


def hgrn_apply(proj, s0, lb, gain, *, tb, nb):
    bsz, seq, _ = proj.shape
    row = pl.BlockSpec((1, GW), lambda b, t: (0, 0))
    y, st = pl.pallas_call(
        functools.partial(_hgrn_kernel, nb=nb, tb=tb, c=CHUNK),
        out_shape=(jax.ShapeDtypeStruct((bsz, seq, GW), F32), _state_shape(bsz)),
        grid=(bsz // nb, seq // tb),
        in_specs=[_proj_spec(nb, tb, 1), _proj_spec(nb, tb, 2), _proj_spec(nb, tb, 3), _proj_spec(nb, tb, 4),
                  row, row, _state_spec(nb)],
        out_specs=(pl.BlockSpec((nb, tb, GW), lambda b, t: (b, t, 0)), _state_spec(nb)),
        scratch_shapes=[pltpu.VMEM((nb, GW, GW), F32)],
        compiler_params=_cparams(("parallel", "arbitrary")),
        name="hgrn2",
    )(proj, proj, proj, proj, lb.reshape(1, GW), gain.reshape(1, GW), s0)
    return y, st


def _softplus(z):
    return jnp.maximum(z, 0.0) + jnp.log(1.0 + jnp.exp(-jnp.abs(z)))


def _rwkv_prepare(ins, consts):
    masks, ones_bd, eye_s, tri, strict, incl = consts
    rng = range(len(ins))
    r, lw, k, v, kk, a = [[i[j] for i in ins] for j in range(6)]
    c = r[0].shape[0]
    n = NH * c
    cum = [_mm_exact_lhs(tri, lw[i]) for i in rng]
    last = [cum[i][c - 1:c, :] for i in rng]
    p_inv = [jnp.exp(-cum[i]) for i in rng]
    p_tail = [jnp.exp(last[i] - cum[i]) for i in rng]
    ka = [kk[i] * a[i] for i in rng]
    x = [jnp.concatenate([_stack_heads(kk[i] * jnp.exp(cum[i] - lw[i]), masks),
                          _stack_heads(r[i] * jnp.exp(cum[i]), masks)], axis=0) for i in rng]
    g = [_mm_nt(x[i], jnp.concatenate([_stack_heads(ka[i] * p_inv[i], masks),
                                       _stack_heads(k[i] * p_inv[i], masks)], axis=0)) for i in rng]
    m_ak = [jnp.where(strict, g[i][:n], 0.0) for i in rng]
    n_ak = [jnp.where(incl, g[i][n:], 0.0) for i in rng]
    pw = [m_ak[i][:, :n] for i in rng]
    t_inv = [eye_s - pw[i] for i in rng]
    for _ in range(int(math.log2(c)) - 1):
        pw = [_mm(pw[i], pw[i]) for i in rng]
        t_inv = [t_inv[i] + _mm(t_inv[i], pw[i]) for i in rng]
    v_stk = [_stack_heads(v[i], masks) for i in rng]
    zeros = jnp.zeros((n, GW), F32)
    mkv = [_mm(m_ak[i], jnp.concatenate([zeros, v_stk[i]], axis=0)) for i in rng]
    tmkv = [_mm(t_inv[i], mkv[i]) for i in rng]
    return [dict(x=x[i], t_inv=t_inv[i], tmkv=tmkv[i], n_ak=n_ak[i], v=v[i], v_stk=v_stk[i],
                 decay=jnp.exp(last[i]),
                 k_tail=jnp.concatenate([ka[i] * p_tail[i], k[i] * p_tail[i]], axis=0)) for i in rng]


def _rwkv_advance(prep, st, consts):
    ones_bd = consts[1]
    rng = range(len(prep))
    n = prep[0]['t_inv'].shape[0]
    c = n // NH
    xa = [_mm_nt(prep[i]['x'], st[i]) for i in rng]
    u_stk = [-_mm(prep[i]['t_inv'], xa[i][:n]) - prep[i]['tmkv'] for i in rng]
    nuv = [_mm(prep[i]['n_ak'], jnp.concatenate([u_stk[i], prep[i]['v_stk']], axis=0)) for i in rng]
    u = [sum(u_stk[i][h * c:(h + 1) * c] for h in range(NH)) for i in rng]
    d_uv = [_mm_tn(jnp.concatenate([u[i], prep[i]['v']], axis=0), prep[i]['k_tail']) for i in rng]
    outs = []
    for i in rng:
        y_stk = xa[i][n:] + nuv[i]
        y = sum(y_stk[h * c:(h + 1) * c] for h in range(NH))
        outs.append((y, st[i] * prep[i]['decay'] + ones_bd * d_uv[i]))
    return outs


def _rwkv_kernel(x_r_ref, x_k_ref, x_v_ref, x_l_ref, sh0_ref, s0_ref, mu_ref, w0_ref, ww_ref, a0_ref,
                 wa_ref, wg_ref, kk_ref, ka_ref, rk_ref, lng_ref, lnb_ref,
                 o_ref, sfin_ref, shfin_ref,
                 st_ref, sh_ref, r_s, lw_s, k_s, v_s, kkn_s, a_s, y_s, bonus_s, gate_s, *, nb, tb, c):
    tbp = max(tb, c)

    @pl.when(pl.program_id(1) == 0)
    def _():
        _load_state(s0_ref, st_ref, nb, transpose=False)
        sh_ref[...] = sh0_ref[...]

    masks = _head_masks()
    ones_bd = _head_ones()
    n = NH * c
    row_t = _iota2((n, 2 * n), 0) % c
    col_t = _iota2((n, 2 * n), 1) % c
    consts = (masks, ones_bd, _eye(n), (_iota2((c, c), 0) >= _iota2((c, c), 1)).astype(F32),
              col_t < row_t, col_t <= row_t)
    first_row = _iota2((tb, GW), 0) == 0

    for b in range(nb):
        def mixed(x_ref, j):
            x = x_ref[b]
            prev = jnp.where(first_row, sh_ref[b, j:j + 1, :], pltpu.roll(x, 1, axis=0))
            sh_ref[b, j:j + 1, :] = x[tb - 1:tb, :]
            return x + (prev - x) * mu_ref[j:j + 1, :]

        r = mixed(x_r_ref, 0)
        k = mixed(x_k_ref, 1)
        v = mixed(x_v_ref, 2)
        xl = mixed(x_l_ref, 3)
        log_w = -_softplus(-(w0_ref[...] + _mm(jnp.tanh(xl), ww_ref[...]))) - 0.5
        a = _sigmoid(a0_ref[...] + _mm(xl, wa_ref[...]))
        gate = _mm(_sigmoid(xl), wg_ref[...])
        kk = k * kk_ref[...]
        kk = kk * lax.rsqrt(jnp.maximum(_head_sum(kk * kk, ones_bd), 1e-24))
        k = k * (1.0 + (a - 1.0) * ka_ref[...])
        r_s[b, 0:tb, :] = r
        lw_s[b, 0:tb, :] = -jnp.exp(log_w)
        k_s[b, 0:tb, :] = k
        v_s[b, 0:tb, :] = v
        kkn_s[b, 0:tb, :] = kk
        a_s[b, 0:tb, :] = a
        bonus_s[b] = _head_sum(r * k * rk_ref[...], ones_bd) * v
        gate_s[b] = gate
        if tbp > tb:
            zeros = jnp.zeros((tbp - tb, GW), F32)
            for s in (r_s, lw_s, k_s, v_s, kkn_s, a_s):
                s[b, tb:tbp, :] = zeros

    n_chunks = tbp // c
    group = 2 if n_chunks % 2 == 0 else 1

    def chunks(gi, carry):
        r0 = [pl.multiple_of((gi * group + j) * c, c) for j in range(group)]
        prep = _rwkv_prepare([tuple(s[b, pl.ds(r0[j], c), :] for s in (r_s, lw_s, k_s, v_s, kkn_s, a_s))
                              for j in range(group) for b in range(nb)], consts)
        st = [st_ref[b] for b in range(nb)]
        for j in range(group):
            outs = _rwkv_advance(prep[j * nb:(j + 1) * nb], st, consts)
            st = [o[1] for o in outs]
            for b in range(nb):
                y_s[b, pl.ds(r0[j], c), :] = outs[b][0]
        for b in range(nb):
            st_ref[b] = st[b]
        return carry

    lax.fori_loop(0, n_chunks // group, chunks, 0)

    for b in range(nb):
        y = y_s[b, 0:tb, :]
        mean = _head_sum(y, ones_bd) * (1.0 / HD)
        d = y - mean
        var = _head_sum(d * d, ones_bd) * (1.0 / HD)
        y = d * lax.rsqrt(var + RW_LN_EPS) * lng_ref[...] + lnb_ref[...]
        o_ref[b] = (y + bonus_s[b]) * gate_s[b]
    shfin_ref[...] = sh_ref[...]

    @pl.when(pl.program_id(1) == pl.num_programs(1) - 1)
    def _():
        _store_state(st_ref, sfin_ref, nb, transpose=False)


def rwkv_apply(proj, s0, shift0, p, l, *, tb, nb):
    bsz, seq, _ = proj.shape
    c = CHUNK
    tbp = max(tb, c)
    row = pl.BlockSpec((1, GW), lambda b, t: (0, 0))
    mat = pl.BlockSpec((GW, GW), lambda b, t: (0, 0))
    sh_spec = pl.BlockSpec((nb, 4, GW), lambda b, t: (b, 0, 0))
    zeros = lambda r: jnp.zeros((r, GW), F32)
    ww = jnp.concatenate([p['rw_w_w2'][l], zeros(192)], axis=0).astype(BF16)
    wa = jnp.concatenate([zeros(64), p['rw_w_a2'][l], zeros(128)], axis=0).astype(BF16)
    wg = jnp.concatenate([zeros(128), p['rw_w_g2'][l]], axis=0).astype(BF16)
    r1 = lambda name: p[name][l].reshape(1, GW)
    seq_buf = pltpu.VMEM((nb, tbp, GW), F32)
    blk_buf = pltpu.VMEM((nb, tb, GW), F32)
    y, st, sh = pl.pallas_call(
        functools.partial(_rwkv_kernel, nb=nb, tb=tb, c=c),
        out_shape=(jax.ShapeDtypeStruct((bsz, seq, GW), F32), _state_shape(bsz),
                   jax.ShapeDtypeStruct((bsz, 4, GW), F32)),
        grid=(bsz // nb, seq // tb),
        in_specs=[_proj_spec(nb, tb, 9), _proj_spec(nb, tb, 10), _proj_spec(nb, tb, 11), _proj_spec(nb, tb, 12),
                  sh_spec, _state_spec(nb), pl.BlockSpec((4, GW), lambda b, t: (0, 0)),
                  row, mat, row, mat, mat, row, row, row, row, row],
        out_specs=(pl.BlockSpec((nb, tb, GW), lambda b, t: (b, t, 0)), _state_spec(nb), sh_spec),
        scratch_shapes=[pltpu.VMEM((nb, GW, GW), F32), pltpu.VMEM((nb, 4, GW), F32),
                        seq_buf, seq_buf, seq_buf, seq_buf, seq_buf, seq_buf, seq_buf, blk_buf, blk_buf],
        compiler_params=_cparams(("parallel", "arbitrary")),
        name="rwkv7",
    )(proj, proj, proj, proj, shift0.reshape(bsz, 4, GW), s0, p['rw_mu'][l].reshape(4, GW),
      r1('rw_w0'), ww, r1('rw_a0'), wa, wg, r1('rw_k_k'), r1('rw_k_a'), r1('rw_r_k'), r1('rw_ln_g'), r1('rw_ln_b'))
    return y, st, sh.reshape(bsz, RW_PROJ)


def _tiles(bsz, seq):
    if seq >= 128:
        return dict(tm=512, s5_steps=128, s5_nb=bsz, mix_tb=128, mix_nb=bsz)
    return dict(tm=512, s5_steps=seq, s5_nb=bsz, mix_tb=seq, mix_nb=8)


def _trunk_layer(x, bsz, seq, pos0, attend, st, p, wb, l, lb, final_norm):
    cfg = _tiles(bsz, seq)
    tm = cfg['tm']
    row = lambda name: p[name][l].reshape(1, -1)
    proj = norm_matmul(x, row('norm_mix'), wb['w_in'][l], tm=tm, tn=IN_WIDTH)
    proj = proj.reshape(bsz, seq, IN_WIDTH)
    y_s5, s5_re, s5_im = s5_apply(proj[..., :GW], st['s5_re'], st['s5_im'], p, l,
                                  steps=cfg['s5_steps'], nb=cfg['s5_nb'])
    y_hg, hg_s = hgrn_apply(proj, st['hgrn'], lb, p['hg_norm'][l], tb=cfg['mix_tb'], nb=cfg['mix_nb'])
    y_rt, rt_s = retention_apply(proj, st['ret'], pos0, tb=cfg['mix_tb'], nb=cfg['mix_nb'])
    y_rw, rw_s, shift = rwkv_apply(proj, st['rwkv'], st['shift'], p, l, tb=cfg['mix_tb'], nb=cfg['mix_nb'])
    parts = [y.reshape(bsz * seq, GW) for y in (y_s5, y_hg, y_rt, y_rw)]
    x, q = mix_out_q(parts, wb['w_out'][l], x, row('norm_mem'), wb['mem_w_q'][l], tm=tm)
    x = attend(q, wb['mem_w_o'][l], x)
    x = ffn(x, row('norm_ffn'), wb['ffn_w_up'][l], wb['ffn_w_down'][l], p['norm_final'].reshape(1, -1),
            tm=256, final_norm=final_norm)
    return x, (s5_re, s5_im, hg_s, rt_s, rw_s, shift)


def kernel(x_prompt, x_sample, mem_prompt, state_s5_re, state_s5_im, state_hgrn, state_ret, state_rwkv,
           state_rwkv_shift, cache_mem_k, cache_mem_v, norm_mix, w_in, w_out, s5_lam_re, s5_lam_im,
           s5_b_re, s5_b_im, s5_c_re, s5_c_im, s5_d, s5_log_step, s5_w_glu, s5_b_glu, s5_norm,
           hg_lb_logits, hg_norm, rw_mu, rw_w0, rw_w_w2, rw_a0, rw_w_a2, rw_w_g2, rw_k_k, rw_k_a, rw_r_k,
           rw_ln_g, rw_ln_b, norm_mem, mem_w_q, mem_w_k, mem_w_v, mem_w_o, norm_ffn, ffn_w_up, ffn_w_down,
           norm_final):
    p = dict(norm_mix=norm_mix, s5_lam_re=s5_lam_re, s5_lam_im=s5_lam_im, s5_b_re=s5_b_re, s5_b_im=s5_b_im,
             s5_c_re=s5_c_re, s5_c_im=s5_c_im, s5_d=s5_d, s5_log_step=s5_log_step, s5_w_glu=s5_w_glu,
             s5_b_glu=s5_b_glu, s5_norm=s5_norm, hg_norm=hg_norm, rw_mu=rw_mu, rw_w0=rw_w0, rw_w_w2=rw_w_w2,
             rw_a0=rw_a0, rw_w_a2=rw_w_a2, rw_w_g2=rw_w_g2, rw_k_k=rw_k_k, rw_k_a=rw_k_a, rw_r_k=rw_r_k,
             rw_ln_g=rw_ln_g, rw_ln_b=rw_ln_b, norm_mem=norm_mem, norm_ffn=norm_ffn, norm_final=norm_final)
    wb = {name: w.astype(BF16) for name, w in dict(
        w_in=w_in, w_out=w_out, mem_w_q=mem_w_q, mem_w_k=mem_w_k, mem_w_v=mem_w_v, mem_w_o=mem_w_o,
        ffn_w_up=ffn_w_up, ffn_w_down=ffn_w_down).items()}
    lb_all = jnp.cumsum(jax.nn.softmax(hg_lb_logits.astype(F32), axis=0), axis=0)
    lb_all = lb_all - lb_all[0:1]

    bp, lp, _ = x_prompt.shape
    bs, ls, _ = x_sample.shape
    yp = x_prompt.reshape(bp * lp, D_MODEL)
    ys = x_sample.reshape(bs * ls, D_MODEL)
    mem2d = mem_prompt.reshape(bp * MEM_LEN, D_MODEL)
    zero_state = dict(s5_re=jnp.zeros((bp, S5_NG, S5_P), F32), s5_im=jnp.zeros((bp, S5_NG, S5_P), F32),
                      hgrn=jnp.zeros((bp, NH, HD, HD), F32), ret=jnp.zeros((bp, NH, HD, HD), F32),
                      rwkv=jnp.zeros((bp, NH, HD, HD), F32), shift=jnp.zeros((bp, RW_PROJ), F32))
    p_states, s_states, p_mk, p_mv = [], [], [], []
    for l in range(DEPTH):
        final = l == DEPTH - 1
        mk = matmul(mem2d, wb['mem_w_k'][l], tm=512, tn=D_MODEL)
        mv = matmul(mem2d, wb['mem_w_v'][l], tm=512, tn=D_MODEL)
        def attend_p(q, w_o, x, mk=mk, mv=mv):
            return cross_attention_out(q, mk.reshape(bp, MEM_LEN, D_MODEL), mv.reshape(bp, MEM_LEN, D_MODEL),
                                       w_o, x, rows=512)

        yp, stp = _trunk_layer(yp, bp, lp, 0, attend_p, zero_state, p, wb, l, lb_all[l], final)
        p_states.append(stp)
        p_mk.append(mk.reshape(bp, MEM_LEN, MEM_HEADS, MEM_HD))
        p_mv.append(mv.reshape(bp, MEM_LEN, MEM_HEADS, MEM_HD))
        sst = dict(s5_re=state_s5_re[l], s5_im=state_s5_im[l], hgrn=state_hgrn[l], ret=state_ret[l],
                   rwkv=state_rwkv[l], shift=state_rwkv_shift[l])
        def attend_s(q, w_o, x, l=l):
            o = cross_attention_cache(q, cache_mem_k, cache_mem_v, l, nb=4, rows=ls)
            return matmul_residual(o, w_o, x, tm=512)

        ys, sts = _trunk_layer(ys, bs, ls, PAST_LEN, attend_s, sst, p, wb, l, lb_all[l], final)
        s_states.append(sts)
    stack = lambda states, i: jnp.stack([s[i] for s in states])
    return (yp.reshape(bp, lp, D_MODEL), ys.reshape(bs, ls, D_MODEL),
            *[stack(p_states, i) for i in range(6)], jnp.stack(p_mk), jnp.stack(p_mv),
            *[stack(s_states, i) for i in range(6)])
```

```python
import functools
import math

import jax
import jax.numpy as jnp
from jax import lax
from jax.experimental import pallas as pl
from jax.experimental.pallas import tpu as pltpu

F32 = jnp.float32
BF16 = jnp.bfloat16

D_MODEL = 1024
DEPTH = 2
PAST_LEN = 16384
GW = 256
HD = 64
NH = GW // HD
S5_GROUP = 16
S5_NG = GW // S5_GROUP
S5_P = 64
S5_W = S5_NG * S5_P
RW_PROJ = 4 * GW
IN_WIDTH = 13 * GW
MEM_LEN = 256
MEM_HEADS = 4
MEM_HD = D_MODEL // MEM_HEADS
D_FF = 4 * D_MODEL
EPS = 1e-6
RW_LN_EPS = 64e-5
ROPE_BASE = 10000.0
CHUNK = 16
RET_CHUNK = 64

VMEM_LIMIT = 56 * 1024 * 1024
SUBLANES = 8


def _cparams(sem):
    return pltpu.CompilerParams(dimension_semantics=sem, vmem_limit_bytes=VMEM_LIMIT)


def _mm(a, b):
    return jnp.dot(a.astype(BF16), b.astype(BF16), preferred_element_type=F32)


def _mm_nt(a, b):
    return lax.dot_general(a.astype(BF16), b.astype(BF16), (((1,), (1,)), ((), ())),
                           preferred_element_type=F32)


def _mm_tn(a, b):
    return lax.dot_general(a.astype(BF16), b.astype(BF16), (((0,), (0,)), ((), ())),
                           preferred_element_type=F32)


def _split3(x):
    hi = x.astype(BF16)
    r1 = x - hi.astype(F32)
    mid = r1.astype(BF16)
    lo = (r1 - mid.astype(F32)).astype(BF16)
    return hi, mid, lo


def _mm_exact_lhs(sel, x):
    s = sel.astype(BF16)
    hi, mid, lo = _split3(x)
    return (jnp.dot(s, hi, preferred_element_type=F32) + jnp.dot(s, mid, preferred_element_type=F32)
            + jnp.dot(s, lo, preferred_element_type=F32))


def _rms(x, gain):
    return x * lax.rsqrt(jnp.mean(x * x, axis=-1, keepdims=True) + EPS) * gain


def _sigmoid(x):
    return 1.0 / (1.0 + jnp.exp(-x))


def _iota2(shape, axis):
    return lax.broadcasted_iota(jnp.int32, shape, axis)


def _head_ones():
    return (_iota2((GW, GW), 0) // HD == _iota2((GW, GW), 1) // HD).astype(F32)


def _eye(n):
    return (_iota2((n, n), 0) == _iota2((n, n), 1)).astype(F32)


def _head_sum(x, ones_bd):
    s = ones_bd.astype(BF16)
    hi = x.astype(BF16)
    lo = (x - hi.astype(F32)).astype(BF16)
    return jnp.dot(hi, s, preferred_element_type=F32) + jnp.dot(lo, s, preferred_element_type=F32)


def _norm_mm_kernel(x_ref, g_ref, w_ref, o_ref, xn_ref):
    @pl.when(pl.program_id(1) == 0)
    def _():
        xn_ref[...] = _rms(x_ref[...], g_ref[...]).astype(BF16)

    o_ref[...] = jnp.dot(xn_ref[...], w_ref[...], preferred_element_type=F32)


def norm_matmul(x, gain, w, *, tm, tn):
    t, d = x.shape
    n = w.shape[1]
    return pl.pallas_call(
        _norm_mm_kernel,
        out_shape=jax.ShapeDtypeStruct((t, n), F32),
        grid=(t // tm, n // tn),
        in_specs=[pl.BlockSpec((tm, d), lambda i, j: (i, 0)),
                  pl.BlockSpec((1, d), lambda i, j: (0, 0)),
                  pl.BlockSpec((d, tn), lambda i, j: (0, j))],
        out_specs=pl.BlockSpec((tm, tn), lambda i, j: (i, j)),
        scratch_shapes=[pltpu.VMEM((tm, d), BF16)],
        compiler_params=_cparams(("parallel", "arbitrary")),
        name="norm_matmul",
    )(x, gain, w)


def _mm_kernel(a_ref, w_ref, o_ref):
    o_ref[...] = _mm(a_ref[...], w_ref[...])


def matmul(a, w, *, tm, tn):
    t, k = a.shape
    n = w.shape[1]
    return pl.pallas_call(
        _mm_kernel,
        out_shape=jax.ShapeDtypeStruct((t, n), F32),
        grid=(t // tm, n // tn),
        in_specs=[pl.BlockSpec((tm, k), lambda i, j: (i, 0)),
                  pl.BlockSpec((k, tn), lambda i, j: (0, j))],
        out_specs=pl.BlockSpec((tm, tn), lambda i, j: (i, j)),
        compiler_params=_cparams(("parallel", "parallel")),
        name="matmul",
    )(a, w)


def _mm_res_kernel(a_ref, w_ref, r_ref, o_ref):
    o_ref[...] = r_ref[...] + _mm(a_ref[...], w_ref[...])


def matmul_residual(a, w, res, *, tm):
    t, k = a.shape
    n = w.shape[1]
    return pl.pallas_call(
        _mm_res_kernel,
        out_shape=jax.ShapeDtypeStruct((t, n), F32),
        grid=(t // tm,),
        in_specs=[pl.BlockSpec((tm, k), lambda i: (i, 0)),
                  pl.BlockSpec((k, n), lambda i: (0, 0)),
                  pl.BlockSpec((tm, n), lambda i: (i, 0))],
        out_specs=pl.BlockSpec((tm, n), lambda i: (i, 0)),
        compiler_params=_cparams(("parallel",)),
        name="matmul_residual",
    )(a, w, res)


def _mix_out_q_kernel(a0_ref, a1_ref, a2_ref, a3_ref, w_ref, r_ref, g_ref, wq_ref, x_ref, q_ref):
    a = jnp.concatenate([a0_ref[...], a1_ref[...], a2_ref[...], a3_ref[...]], axis=-1)
    x = r_ref[...] + _mm(a, w_ref[...])
    x_ref[...] = x
    q_ref[...] = jnp.dot(_rms(x, g_ref[...]).astype(BF16), wq_ref[...], preferred_element_type=F32).astype(BF16)


def mix_out_q(parts, w, res, gain, w_q, *, tm):
    t, d = res.shape
    part = pl.BlockSpec((tm, GW), lambda i: (i, 0))
    mat = pl.BlockSpec((d, d), lambda i: (0, 0))
    tile = pl.BlockSpec((tm, d), lambda i: (i, 0))
    return pl.pallas_call(
        _mix_out_q_kernel,
        out_shape=(jax.ShapeDtypeStruct((t, d), F32), jax.ShapeDtypeStruct((t, d), BF16)),
        grid=(t // tm,),
        in_specs=[part, part, part, part, mat, tile, pl.BlockSpec((1, d), lambda i: (0, 0)), mat],
        out_specs=(tile, tile),
        compiler_params=_cparams(("parallel",)),
        name="mix_out_q",
    )(*parts, w, res, gain, w_q)


def _ffn_kernel(x_ref, g_ref, wu_ref, wd_ref, gf_ref, o_ref, *, final_norm):
    x = x_ref[...]
    h = jnp.dot(_rms(x, g_ref[...]).astype(BF16), wu_ref[...], preferred_element_type=F32)
    h = jnp.square(jnp.maximum(h, 0.0))
    y = x + jnp.dot(h.astype(BF16), wd_ref[...], preferred_element_type=F32)
    if final_norm:
        y = _rms(y, gf_ref[...])
    o_ref[...] = y


def ffn(x, gain, w_up, w_down, gain_final, *, tm, final_norm):
    t, d = x.shape
    ff = w_up.shape[1]
    resident = lambda shape: pl.BlockSpec(shape, lambda i: (0, 0), pipeline_mode=pl.Buffered(1))
    return pl.pallas_call(
        functools.partial(_ffn_kernel, final_norm=final_norm),
        out_shape=jax.ShapeDtypeStruct((t, d), F32),
        grid=(t // tm,),
        in_specs=[pl.BlockSpec((tm, d), lambda i: (i, 0)),
                  pl.BlockSpec((1, d), lambda i: (0, 0)),
                  resident((d, ff)), resident((ff, d)),
                  pl.BlockSpec((1, d), lambda i: (0, 0))],
        out_specs=pl.BlockSpec((tm, d), lambda i: (i, 0)),
        compiler_params=_cparams(("parallel",)),
        name="ffn",
    )(x, gain, w_up, w_down, gain_final)


def _attn_out_kernel(q_ref, k_ref, v_ref, wo_ref, r_ref, x_ref):
    heads = []
    for h in range(MEM_HEADS):
        sl = slice(h * MEM_HD, (h + 1) * MEM_HD)
        s = _mm_nt(q_ref[:, sl], k_ref[:, sl]) * (MEM_HD ** -0.5)
        p = jnp.exp(s - jnp.max(s, axis=-1, keepdims=True))
        p = p * (1.0 / jnp.sum(p, axis=-1, keepdims=True))
        heads.append(_mm(p, v_ref[:, sl]))
    x_ref[...] = r_ref[...] + _mm(jnp.concatenate(heads, axis=-1), wo_ref[...])


def cross_attention_out(q, mem_k, mem_v, w_o, res, *, rows):
    t, d = res.shape
    bsz = mem_k.shape[0]
    lt = t // bsz // rows
    tile = pl.BlockSpec((rows, d), lambda b, l: (b * lt + l, 0))
    mem = pl.BlockSpec((None, MEM_LEN, d), lambda b, l: (b, 0, 0))
    return pl.pallas_call(
        _attn_out_kernel,
        out_shape=jax.ShapeDtypeStruct((t, d), F32),
        grid=(bsz, lt),
        in_specs=[tile, mem, mem, pl.BlockSpec((d, d), lambda b, l: (0, 0)), tile],
        out_specs=tile,
        compiler_params=_cparams(("parallel", "arbitrary")),
        name="cross_attention_out",
    )(q, mem_k, mem_v, w_o, res)


def _attn_cache_kernel(q_ref, k_ref, v_ref, o_ref, *, nb, rows):
    nr = MEM_HEADS * rows
    same_head = _iota2((nr, MEM_HEADS * MEM_LEN), 1) % MEM_HEADS == _iota2((nr, MEM_HEADS * MEM_LEN), 0) // rows
    q_all = q_ref[...].astype(F32)
    for b in range(nb):
        q = q_all[b * rows:(b + 1) * rows, :]
        qs = jnp.concatenate([q[:, h * MEM_HD:(h + 1) * MEM_HD] for h in range(MEM_HEADS)], axis=0)
        s = _mm_nt(qs, k_ref[0, b].reshape(MEM_HEADS * MEM_LEN, MEM_HD)) * (MEM_HD ** -0.5)
        s = jnp.where(same_head, s, -1e30)
        p = jnp.exp(s - jnp.max(s, axis=-1, keepdims=True))
        p = p * (1.0 / jnp.sum(p, axis=-1, keepdims=True))
        o = _mm(p, v_ref[0, b].reshape(MEM_HEADS * MEM_LEN, MEM_HD))
        for h in range(MEM_HEADS):
            o_ref[b * rows:(b + 1) * rows, h * MEM_HD:(h + 1) * MEM_HD] = o[h * rows:(h + 1) * rows]


def cross_attention_cache(q, cache_k, cache_v, layer, *, nb, rows):
    t = q.shape[0]
    bsz = cache_k.shape[1]
    mem = pl.BlockSpec((1, nb, MEM_LEN, MEM_HEADS, MEM_HD), lambda b: (layer, b, 0, 0, 0))
    return pl.pallas_call(
        functools.partial(_attn_cache_kernel, nb=nb, rows=rows),
        out_shape=jax.ShapeDtypeStruct((t, D_MODEL), F32),
        grid=(bsz // nb,),
        in_specs=[pl.BlockSpec((nb * rows, D_MODEL), lambda b: (b, 0)), mem, mem],
        out_specs=pl.BlockSpec((nb * rows, D_MODEL), lambda b: (b, 0)),
        compiler_params=_cparams(("parallel",)),
        name="cross_attention_cache",
    )(q, cache_k, cache_v)


def _gelu_tanh(x):
    return 0.5 * x * (1.0 + jnp.tanh(math.sqrt(2.0 / math.pi) * (x + 0.044715 * (x * x * x))))


def _s5_kernel(u_ref, h0_ref, lam_ref, bblk_ref, cblk_ref, d_ref, wglu_ref, bglu_ref, gain_ref,
               y_ref, hfin_ref, scr_ref, *, steps, nb):
    @pl.when(pl.program_id(1) == 0)
    def _():
        scr_ref[0:nb, :] = h0_ref[...]

    u = u_ref[...].reshape(steps * nb, GW)
    scr_ref[nb:, :] = _mm(u, bblk_ref[...])
    lam_re = lam_ref[0:1, :]
    lam_im = lam_ref[1:2, :]

    def step(t, carry):
        p0 = pl.multiple_of(t * nb, nb)
        c0 = pl.multiple_of((t + 1) * nb, nb)
        h_re = scr_ref[pl.ds(p0, nb), 0:S5_W]
        h_im = scr_ref[pl.ds(p0, nb), S5_W:]
        scr_ref[pl.ds(c0, nb), 0:S5_W] = scr_ref[pl.ds(c0, nb), 0:S5_W] + lam_re * h_re - lam_im * h_im
        scr_ref[pl.ds(c0, nb), S5_W:] = scr_ref[pl.ds(c0, nb), S5_W:] + lam_re * h_im + lam_im * h_re
        return carry

    lax.fori_loop(0, steps, step, 0)
    h_last = scr_ref[steps * nb:, :]
    hfin_ref[...] = h_last
    y = _mm(scr_ref[nb:, :], cblk_ref[...]) + d_ref[...] * u
    y = _gelu_tanh(y)
    y = y * _sigmoid(_mm(y, wglu_ref[...]) + bglu_ref[...])
    y_ref[...] = _rms(y, gain_ref[...]).reshape(steps, nb, GW)
    scr_ref[0:nb, :] = h_last


def s5_mixer(u_tm, h0, lam, bblk, cblk, dvec, wglu, bglu, gain, *, steps, nb):
    seq, bsz, _ = u_tm.shape
    const = lambda shape: pl.BlockSpec(shape, lambda b, t: (0, 0))
    return pl.pallas_call(
        functools.partial(_s5_kernel, steps=steps, nb=nb),
        out_shape=(jax.ShapeDtypeStruct((seq, bsz, GW), F32),
                   jax.ShapeDtypeStruct((bsz, 2 * S5_W), F32)),
        grid=(bsz // nb, seq // steps),
        in_specs=[pl.BlockSpec((steps, nb, GW), lambda b, t: (t, b, 0)),
                  pl.BlockSpec((nb, 2 * S5_W), lambda b, t: (b, 0)),
                  const((2, S5_W)), const((GW, 2 * S5_W)), const((2 * S5_W, GW)), const((1, GW)),
                  const((GW, GW)), const((1, GW)), const((1, GW))],
        out_specs=(pl.BlockSpec((steps, nb, GW), lambda b, t: (t, b, 0)),
                   pl.BlockSpec((nb, 2 * S5_W), lambda b, t: (b, 0))),
        scratch_shapes=[pltpu.VMEM(((steps + 1) * nb, 2 * S5_W), F32)],
        compiler_params=_cparams(("parallel", "arbitrary")),
        name="s5_mixer",
    )(u_tm, h0, lam, bblk, cblk, dvec, wglu, bglu, gain)


def s5_params(lam_re, lam_im, b_re, b_im, c_re, c_im, log_step):
    step = jnp.exp(log_step)[:, None]
    mag = jnp.exp(lam_re * step)
    lbar_re = mag * jnp.cos(lam_im * step)
    lbar_im = mag * jnp.sin(lam_im * step)
    den = lam_re * lam_re + lam_im * lam_im
    f_re = ((lbar_re - 1.0) * lam_re + lbar_im * lam_im) / den
    f_im = (lbar_im * lam_re - (lbar_re - 1.0) * lam_im) / den
    bbar_re = f_re[..., None] * b_re - f_im[..., None] * b_im
    bbar_im = f_re[..., None] * b_im + f_im[..., None] * b_re
    eye_g = jnp.eye(S5_NG, dtype=F32)

    def in_blk(m):
        return jnp.einsum('gph,gk->ghkp', m, eye_g).reshape(GW, S5_W)

    def out_blk(m):
        return jnp.einsum('ghp,gk->gpkh', m, eye_g).reshape(S5_W, GW)

    bblk = jnp.concatenate([in_blk(bbar_re), in_blk(bbar_im)], axis=1)
    cblk = jnp.concatenate([out_blk(c_re), -out_blk(c_im)], axis=0)
    lam2 = jnp.stack([lbar_re.reshape(S5_W), lbar_im.reshape(S5_W)])
    return lam2, bblk.astype(BF16), cblk.astype(BF16)


def s5_apply(u, h0_re, h0_im, p, l, *, steps, nb):
    bsz = u.shape[0]
    lam2, bblk, cblk = s5_params(p['s5_lam_re'][l], p['s5_lam_im'][l], p['s5_b_re'][l], p['s5_b_im'][l],
                                 p['s5_c_re'][l], p['s5_c_im'][l], p['s5_log_step'][l])
    h0 = jnp.concatenate([h0_re.reshape(bsz, S5_W), h0_im.reshape(bsz, S5_W)], axis=1)
    y, h = s5_mixer(jnp.swapaxes(u, 0, 1), h0, lam2, bblk, cblk, p['s5_d'][l].reshape(1, GW),
                    p['s5_w_glu'][l].astype(BF16), p['s5_b_glu'][l].reshape(1, GW),
                    p['s5_norm'][l].reshape(1, GW), steps=steps, nb=nb)
    return (jnp.swapaxes(y, 0, 1), h[:, :S5_W].reshape(bsz, S5_NG, S5_P),
            h[:, S5_W:].reshape(bsz, S5_NG, S5_P))


def _load_state(s0_ref, st_ref, nb, transpose):
    for b in range(nb):
        rows = []
        for h in range(NH):
            pieces = [s0_ref[b, h]]
            if h:
                pieces.insert(0, jnp.zeros((HD, h * HD), F32))
            if h < NH - 1:
                pieces.append(jnp.zeros((HD, (NH - 1 - h) * HD), F32))
            rows.append(jnp.concatenate(pieces, axis=1))
        st = jnp.concatenate(rows, axis=0)
        st_ref[b] = st.T if transpose else st


def _store_state(st_ref, sfin_ref, nb, transpose):
    for b in range(nb):
        st = st_ref[b].T if transpose else st_ref[b]
        for h in range(NH):
            sfin_ref[b, h] = st[h * HD:(h + 1) * HD, h * HD:(h + 1) * HD]


def _head_masks():
    lane_head = _iota2((1, GW), 1) // HD
    return [(lane_head == h).astype(F32) for h in range(NH)]


def _stack_heads(x, masks):
    return jnp.concatenate([x * m for m in masks], axis=0)


def _pad_rows(x, rows):
    if x.shape[0] == rows:
        return x
    return jnp.concatenate([x, jnp.zeros((rows - x.shape[0], x.shape[1]), x.dtype)], axis=0)


def _silu(x):
    return x * _sigmoid(x)


def _chunk_rows(ref, b, r0, rows, c):
    return _pad_rows(ref[b, pl.ds(r0, rows), :], c)


def _ret_kernel(q_ref, k_ref, v_ref, g_ref, cos_ref, sin_ref, s0_ref, _states_in, o_ref, sfin_ref, st_ref,
                *, nb, tb, c, c_real):
    rows = min(tb, c)

    @pl.when(pl.program_id(1) == 0)
    def _():
        _load_state(s0_ref, st_ref, nb, transpose=True)

    masks = _head_masks()
    ones_bd = _head_ones()
    lane_head = _iota2((1, GW), 1) // HD
    log_gamma = jnp.zeros((1, GW), F32)
    for h in range(NH):
        log_gamma = jnp.where(lane_head == h, math.log(1.0 - 2.0 ** (-5.0 - h)), log_gamma)
    tt = _iota2((c, GW), 0).astype(F32)
    scale = HD ** -0.5
    g_q = jnp.exp(log_gamma * (tt + 1.0))
    g_k = jnp.exp(-log_gamma * (tt + 1.0)) * scale
    g_tail = jnp.exp(log_gamma * (c_real - 1.0 - tt)) * scale
    g_chunk = jnp.exp(log_gamma * float(c_real))
    causal = _iota2((c, NH * c), 0) >= _iota2((c, NH * c), 1) % c
    first_half = _iota2((c, GW), 1) % HD < HD // 2

    def rope(x, cos, sin):
        swapped = jnp.where(first_half, pltpu.roll(x, GW - HD // 2, axis=1), pltpu.roll(x, HD // 2, axis=1))
        return x * cos + swapped * sin

    def chunk(ci, carry):
        r0 = pl.multiple_of(ci * rows, rows)
        cos = _pad_rows(cos_ref[pl.ds(r0, rows), :], c)
        sin = _pad_rows(sin_ref[pl.ds(r0, rows), :], c)
        rng = range(nb)
        qt = [rope(_chunk_rows(q_ref, b, r0, rows, c), cos, sin) * g_q for b in rng]
        k = [rope(_chunk_rows(k_ref, b, r0, rows, c), cos, sin) for b in rng]
        v = [_chunk_rows(v_ref, b, r0, rows, c) for b in rng]
        st = [st_ref[b] for b in rng]
        sc = [_mm_nt(qt[b], _stack_heads(k[b] * g_k, masks)) for b in rng]
        o_in = [_mm(jnp.where(causal, sc[b], 0.0), _stack_heads(v[b], masks)) for b in rng]
        o_st = [_mm_nt(qt[b], st[b]) for b in rng]
        d_st = [_mm_tn(v[b], k[b] * g_tail) for b in rng]
        o = [o_in[b] + o_st[b] for b in rng]
        ms = [_head_sum(o[b] * o[b], ones_bd) for b in rng]
        for b in rng:
            st_ref[b] = st[b] * g_chunk + ones_bd * d_st[b]
            ob = o[b] * lax.rsqrt(ms[b] * (1.0 / HD) + EPS) * _silu(_chunk_rows(g_ref, b, r0, rows, c))
            o_ref[b, pl.ds(r0, rows), :] = ob[:rows]
        return carry

    lax.fori_loop(0, tb // rows, chunk, 0)

    @pl.when(pl.program_id(1) == pl.num_programs(1) - 1)
    def _():
        _store_state(st_ref, sfin_ref, nb, transpose=True)


def _rope_tables(pos0, seq):
    half = HD // 2
    inv = ROPE_BASE ** (-jnp.arange(half, dtype=F32) / half)
    pos = pos0 + jnp.arange(seq, dtype=jnp.int32)
    ang = pos.astype(F32)[:, None] * inv[None, :]
    cos, sin = jnp.cos(ang), jnp.sin(ang)
    return (jnp.tile(jnp.concatenate([cos, cos], axis=-1), (1, NH)),
            jnp.tile(jnp.concatenate([-sin, sin], axis=-1), (1, NH)))


def _proj_spec(nb, tb, col):
    return pl.BlockSpec((nb, tb, GW), lambda b, t: (b, t, col))


def _state_spec(nb, layer):
    return pl.BlockSpec((None, nb, NH, HD, HD), lambda b, t: (layer, b, 0, 0, 0))


_ALIASED = pl.BlockSpec(memory_space=pl.ANY)


def retention_apply(proj, s0, s0_layer, states, layer, pos0, *, tb, nb):
    bsz, seq, _ = proj.shape
    c = RET_CHUNK if tb >= RET_CHUNK else CHUNK
    cos, sin = _rope_tables(pos0, seq)
    return pl.pallas_call(
        functools.partial(_ret_kernel, nb=nb, tb=tb, c=c, c_real=min(tb, c)),
        out_shape=(jax.ShapeDtypeStruct((bsz, seq, GW), F32), jax.ShapeDtypeStruct(states.shape, F32)),
        grid=(bsz // nb, seq // tb),
        in_specs=[_proj_spec(nb, tb, 5), _proj_spec(nb, tb, 6), _proj_spec(nb, tb, 7), _proj_spec(nb, tb, 8),
                  pl.BlockSpec((tb, GW), lambda b, t: (t, 0)), pl.BlockSpec((tb, GW), lambda b, t: (t, 0)),
                  _state_spec(nb, s0_layer), _ALIASED],
        out_specs=(pl.BlockSpec((nb, tb, GW), lambda b, t: (b, t, 0)), _state_spec(nb, layer)),
        scratch_shapes=[pltpu.VMEM((nb, GW, GW), F32)],
        input_output_aliases={7: 1},
        compiler_params=_cparams(("parallel", "arbitrary")),
        name="retention",
    )(proj, proj, proj, proj, cos, sin, s0, states)


def _log_sigmoid(z):
    return jnp.minimum(z, 0.0) - jnp.log(1.0 + jnp.exp(-jnp.abs(z)))


def _hgrn_kernel(q_ref, f_ref, i_ref, g_ref, lb_ref, gain_ref, s0_ref, _states_in, o_ref, sfin_ref, st_ref,
                 *, nb, tb, c):
    rows = min(tb, c)

    @pl.when(pl.program_id(1) == 0)
    def _():
        _load_state(s0_ref, st_ref, nb, transpose=True)

    t_hi = -(-rows // SUBLANES) * SUBLANES
    ones_bd = _head_ones()
    tri = (_iota2((c, c), 0) >= _iota2((c, c), 1)).astype(F32)
    t_idx = _iota2((c, GW), 0)
    lb = lb_ref[...]
    log_lb = jnp.log(lb)
    gain = gain_ref[...]

    def chunk(ci, carry):
        r0 = pl.multiple_of(ci * rows, rows)
        rng = range(nb)
        valid = t_idx < rows

        def gates(b):
            z = _chunk_rows(f_ref, b, r0, rows, c)
            ls_pos = _log_sigmoid(z)
            ls_neg = ls_pos - z
            b2 = log_lb + ls_neg
            log_f = jnp.maximum(ls_pos, b2) + jnp.log(1.0 + jnp.exp(-jnp.abs(ls_pos - b2)))
            return jnp.where(valid, log_f, 0.0), jnp.where(valid, (1.0 - lb) * jnp.exp(ls_neg), 0.0)

        log_f, key = zip(*[gates(b) for b in rng])
        q = [_silu(_chunk_rows(q_ref, b, r0, rows, c)) for b in rng]
        v = [_chunk_rows(i_ref, b, r0, rows, c) for b in rng]
        st = [st_ref[b] for b in rng]
        cum = [_mm_exact_lhs(tri, log_f[b]) for b in rng]
        last = [cum[b][c - 1:c, :] for b in rng]
        def pair_rows(b, s):
            lo = SUBLANES * (s // SUBLANES)
            t_rows = _iota2((t_hi - lo, GW), 0) + lo
            decay = jnp.where(t_rows >= s, jnp.exp(cum[b][lo:t_hi] - cum[b][s:s + 1, :]), 0.0)
            return decay * q[b][lo:t_hi] * key[b][s:s + 1, :]

        sc = [_mm(jnp.concatenate([pair_rows(b, s) for s in range(rows)], axis=0), ones_bd) for b in rng]
        o_st = [_mm_nt(q[b] * jnp.exp(cum[b]), st[b]) for b in rng]
        d_st = [_mm_tn(v[b], key[b] * jnp.exp(last[b] - cum[b])) for b in rng]

        def intra(b):
            tiles = [None] * (c // SUBLANES)
            off = 0
            for s in range(rows):
                for ti in range(s // SUBLANES, t_hi // SUBLANES):
                    term = sc[b][off:off + SUBLANES, :] * v[b][s:s + 1, :]
                    tiles[ti] = term if tiles[ti] is None else tiles[ti] + term
                    off += SUBLANES
            zero = jnp.zeros((SUBLANES, GW), F32)
            return jnp.concatenate([zero if t is None else t for t in tiles], axis=0)

        o = [o_st[b] + intra(b) for b in rng]
        ms = [_head_sum(o[b] * o[b], ones_bd) for b in rng]
        for b in rng:
            st_ref[b] = st[b] * jnp.exp(last[b]) + ones_bd * d_st[b]
            ob = o[b] * lax.rsqrt(ms[b] * (1.0 / HD) + EPS) * gain * _silu(_chunk_rows(g_ref, b, r0, rows, c))
            o_ref[b, pl.ds(r0, rows), :] = ob[:rows]
        return carry

    lax.fori_loop(0, tb // rows, chunk, 0)

    @pl.when(pl.program_id(1) == pl.num_programs(1) - 1)
    def _():
        _store_state(st_ref, sfin_ref, nb, transpose=True)


def hgrn_apply(proj, s0, s0_layer, states, layer, lb, gain, *, tb, nb):
    bsz, seq, _ = proj.shape
    row = pl.BlockSpec((1, GW), lambda b, t: (0, 0))
    return pl.pallas_call(
        functools.partial(_hgrn_kernel, nb=nb, tb=tb, c=CHUNK),
        out_shape=(jax.ShapeDtypeStruct((bsz, seq, GW), F32), jax.ShapeDtypeStruct(states.shape, F32)),
        grid=(bsz // nb, seq // tb),
        in_specs=[_proj_spec(nb, tb, 1), _proj_spec(nb, tb, 2), _proj_spec(nb, tb, 3), _proj_spec(nb, tb, 4),
                  row, row, _state_spec(nb, s0_layer), _ALIASED],
        out_specs=(pl.BlockSpec((nb, tb, GW), lambda b, t: (b, t, 0)), _state_spec(nb, layer)),
        scratch_shapes=[pltpu.VMEM((nb, GW, GW), F32)],
        input_output_aliases={7: 1},
        compiler_params=_cparams(("parallel", "arbitrary")),
        name="hgrn2",
    )(proj, proj, proj, proj, lb.reshape(1, GW), gain.reshape(1, GW), s0, states)


def _softplus(z):
    return jnp.maximum(z, 0.0) + jnp.log(1.0 + jnp.exp(-jnp.abs(z)))


def _rwkv_prepare(ins, consts):
    masks, ones_bd, eye_s, tri, strict, incl = consts
    rng = range(len(ins))
    r, lw, k, v, kk, a = [[i[j] for i in ins] for j in range(6)]
    c = r[0].shape[0]
    n = NH * c
    cum = [_mm_exact_lhs(tri, lw[i]) for i in rng]
    last = [cum[i][c - 1:c, :] for i in rng]
    p_inv = [jnp.exp(-cum[i]) for i in rng]
    p_tail = [jnp.exp(last[i] - cum[i]) for i in rng]
    ka = [kk[i] * a[i] for i in rng]
    x = [jnp.concatenate([_stack_heads(kk[i] * jnp.exp(cum[i] - lw[i]), masks),
                          _stack_heads(r[i] * jnp.exp(cum[i]), masks)], axis=0) for i in rng]
    g = [_mm_nt(x[i], jnp.concatenate([_stack_heads(ka[i] * p_inv[i], masks),
                                       _stack_heads(k[i] * p_inv[i], masks)], axis=0)) for i in rng]
    m_ak = [jnp.where(strict, g[i][:n], 0.0) for i in rng]
    n_ak = [jnp.where(incl, g[i][n:], 0.0) for i in rng]
    pw = [m_ak[i][:, :n] for i in rng]
    t_inv = [eye_s - pw[i] for i in rng]
    for _ in range(int(math.log2(c)) - 1):
        pw = [_mm(pw[i], pw[i]) for i in rng]
        t_inv = [t_inv[i] + _mm(t_inv[i], pw[i]) for i in rng]
    v_stk = [_stack_heads(v[i], masks) for i in rng]
    zeros = jnp.zeros((n, GW), F32)
    mkv = [_mm(m_ak[i], jnp.concatenate([zeros, v_stk[i]], axis=0)) for i in rng]
    tmkv = [_mm(t_inv[i], mkv[i]) for i in rng]
    return [dict(x=x[i], t_inv=t_inv[i], tmkv=tmkv[i], n_ak=n_ak[i], v=v[i], v_stk=v_stk[i],
                 decay=jnp.exp(last[i]),
                 k_tail=jnp.concatenate([ka[i] * p_tail[i], k[i] * p_tail[i]], axis=0)) for i in rng]


def _rwkv_advance(prep, st, consts):
    ones_bd = consts[1]
    rng = range(len(prep))
    n = prep[0]['t_inv'].shape[0]
    c = n // NH
    xa = [_mm_nt(prep[i]['x'], st[i]) for i in rng]
    u_stk = [-_mm(prep[i]['t_inv'], xa[i][:n]) - prep[i]['tmkv'] for i in rng]
    nuv = [_mm(prep[i]['n_ak'], jnp.concatenate([u_stk[i], prep[i]['v_stk']], axis=0)) for i in rng]
    u = [sum(u_stk[i][h * c:(h + 1) * c] for h in range(NH)) for i in rng]
    d_uv = [_mm_tn(jnp.concatenate([u[i], prep[i]['v']], axis=0), prep[i]['k_tail']) for i in rng]
    outs = []
    for i in rng:
        y_stk = xa[i][n:] + nuv[i]
        y = sum(y_stk[h * c:(h + 1) * c] for h in range(NH))
        outs.append((y, st[i] * prep[i]['decay'] + ones_bd * d_uv[i]))
    return outs


def _rwkv_kernel(x_r_ref, x_k_ref, x_v_ref, x_l_ref, sh0_ref, s0_ref, mu_ref, w0_ref, ww_ref, a0_ref,
                 wa_ref, wg_ref, kk_ref, ka_ref, rk_ref, lng_ref, lnb_ref, _states_in,
                 o_ref, sfin_ref, shfin_ref,
                 st_ref, sh_ref, r_s, lw_s, k_s, v_s, kkn_s, a_s, y_s, bonus_s, gate_s, *, nb, tb, c):
    tbp = max(tb, c)

    @pl.when(pl.program_id(1) == 0)
    def _():
        _load_state(s0_ref, st_ref, nb, transpose=False)
        sh_ref[...] = sh0_ref[...]

    masks = _head_masks()
    ones_bd = _head_ones()
    n = NH * c
    row_t = _iota2((n, 2 * n), 0) % c
    col_t = _iota2((n, 2 * n), 1) % c
    consts = (masks, ones_bd, _eye(n), (_iota2((c, c), 0) >= _iota2((c, c), 1)).astype(F32),
              col_t < row_t, col_t <= row_t)
    first_row = _iota2((tb, GW), 0) == 0

    for b in range(nb):
        def mixed(x_ref, j):
            x = x_ref[b]
            prev = jnp.where(first_row, sh_ref[b, j:j + 1, :], pltpu.roll(x, 1, axis=0))
            sh_ref[b, j:j + 1, :] = x[tb - 1:tb, :]
            return x + (prev - x) * mu_ref[j:j + 1, :]

        r = mixed(x_r_ref, 0)
        k = mixed(x_k_ref, 1)
        v = mixed(x_v_ref, 2)
        xl = mixed(x_l_ref, 3)
        log_w = -_softplus(-(w0_ref[...] + _mm(jnp.tanh(xl), ww_ref[...]))) - 0.5
        a = _sigmoid(a0_ref[...] + _mm(xl, wa_ref[...]))
        gate = _mm(_sigmoid(xl), wg_ref[...])
        kk = k * kk_ref[...]
        kk = kk * lax.rsqrt(jnp.maximum(_head_sum(kk * kk, ones_bd), 1e-24))
        k = k * (1.0 + (a - 1.0) * ka_ref[...])
        r_s[b, 0:tb, :] = r
        lw_s[b, 0:tb, :] = -jnp.exp(log_w)
        k_s[b, 0:tb, :] = k
        v_s[b, 0:tb, :] = v
        kkn_s[b, 0:tb, :] = kk
        a_s[b, 0:tb, :] = a
        bonus_s[b] = _head_sum(r * k * rk_ref[...], ones_bd) * v
        gate_s[b] = gate
        if tbp > tb:
            zeros = jnp.zeros((tbp - tb, GW), F32)
            for s in (r_s, lw_s, k_s, v_s, kkn_s, a_s):
                s[b, tb:tbp, :] = zeros

    n_chunks = tbp // c
    group = 2 if n_chunks % 2 == 0 else 1

    def chunks(gi, carry):
        r0 = [pl.multiple_of((gi * group + j) * c, c) for j in range(group)]
        prep = _rwkv_prepare([tuple(s[b, pl.ds(r0[j], c), :] for s in (r_s, lw_s, k_s, v_s, kkn_s, a_s))
                              for j in range(group) for b in range(nb)], consts)
        st = [st_ref[b] for b in range(nb)]
        for j in range(group):
            outs = _rwkv_advance(prep[j * nb:(j + 1) * nb], st, consts)
            st = [o[1] for o in outs]
            for b in range(nb):
                y_s[b, pl.ds(r0[j], c), :] = outs[b][0]
        for b in range(nb):
            st_ref[b] = st[b]
        return carry

    lax.fori_loop(0, n_chunks // group, chunks, 0)

    for b in range(nb):
        y = y_s[b, 0:tb, :]
        mean = _head_sum(y, ones_bd) * (1.0 / HD)
        d = y - mean
        var = _head_sum(d * d, ones_bd) * (1.0 / HD)
        y = d * lax.rsqrt(var + RW_LN_EPS) * lng_ref[...] + lnb_ref[...]
        o_ref[b] = (y + bonus_s[b]) * gate_s[b]
    shfin_ref[...] = sh_ref[...]

    @pl.when(pl.program_id(1) == pl.num_programs(1) - 1)
    def _():
        _store_state(st_ref, sfin_ref, nb, transpose=False)


def rwkv_apply(proj, s0, s0_layer, states, shift0, p, l, *, tb, nb):
    bsz, seq, _ = proj.shape
    c = CHUNK
    tbp = max(tb, c)
    row = pl.BlockSpec((1, GW), lambda b, t: (0, 0))
    mat = pl.BlockSpec((GW, GW), lambda b, t: (0, 0))
    sh_spec = pl.BlockSpec((nb, 4, GW), lambda b, t: (b, 0, 0))
    zeros = lambda r: jnp.zeros((r, GW), F32)
    ww = jnp.concatenate([p['rw_w_w2'][l], zeros(192)], axis=0).astype(BF16)
    wa = jnp.concatenate([zeros(64), p['rw_w_a2'][l], zeros(128)], axis=0).astype(BF16)
    wg = jnp.concatenate([zeros(128), p['rw_w_g2'][l]], axis=0).astype(BF16)
    r1 = lambda name: p[name][l].reshape(1, GW)
    seq_buf = pltpu.VMEM((nb, tbp, GW), F32)
    blk_buf = pltpu.VMEM((nb, tb, GW), F32)
    y, st, sh = pl.pallas_call(
        functools.partial(_rwkv_kernel, nb=nb, tb=tb, c=c),
        out_shape=(jax.ShapeDtypeStruct((bsz, seq, GW), F32), jax.ShapeDtypeStruct(states.shape, F32),
                   jax.ShapeDtypeStruct((bsz, 4, GW), F32)),
        grid=(bsz // nb, seq // tb),
        in_specs=[_proj_spec(nb, tb, 9), _proj_spec(nb, tb, 10), _proj_spec(nb, tb, 11), _proj_spec(nb, tb, 12),
                  sh_spec, _state_spec(nb, s0_layer), pl.BlockSpec((4, GW), lambda b, t: (0, 0)),
                  row, mat, row, mat, mat, row, row, row, row, row, _ALIASED],
        out_specs=(pl.BlockSpec((nb, tb, GW), lambda b, t: (b, t, 0)), _state_spec(nb, l), sh_spec),
        scratch_shapes=[pltpu.VMEM((nb, GW, GW), F32), pltpu.VMEM((nb, 4, GW), F32),
                        seq_buf, seq_buf, seq_buf, seq_buf, seq_buf, seq_buf, seq_buf, blk_buf, blk_buf],
        input_output_aliases={17: 1},
        compiler_params=_cparams(("parallel", "arbitrary")),
        name="rwkv7",
    )(proj, proj, proj, proj, shift0.reshape(bsz, 4, GW), s0, p['rw_mu'][l].reshape(4, GW),
      r1('rw_w0'), ww, r1('rw_a0'), wa, wg, r1('rw_k_k'), r1('rw_k_a'), r1('rw_r_k'), r1('rw_ln_g'), r1('rw_ln_b'),
      states)
    return y, st, sh.reshape(bsz, RW_PROJ)


def _tiles(bsz, seq):
    if seq >= 128:
        return dict(tm=512, s5_steps=128, s5_nb=bsz, mix_tb=128, mix_nb=bsz, rw_nb=bsz)
    return dict(tm=512, s5_steps=seq, s5_nb=bsz, mix_tb=seq, mix_nb=8, rw_nb=16)


def _trunk_layer(x, bsz, seq, pos0, attend, st, mats, p, wb, l, lb, final_norm):
    cfg = _tiles(bsz, seq)
    tm = cfg['tm']
    row = lambda name: p[name][l].reshape(1, -1)
    proj = norm_matmul(x, row('norm_mix'), wb['w_in'][l], tm=tm, tn=IN_WIDTH)
    proj = proj.reshape(bsz, seq, IN_WIDTH)
    y_s5, s5_re, s5_im = s5_apply(proj[..., :GW], st['s5_re'], st['s5_im'], p, l,
                                  steps=cfg['s5_steps'], nb=cfg['s5_nb'])
    y_hg, hg_s = hgrn_apply(proj, st['hgrn'], st['layer'], mats['hgrn'], l, lb, p['hg_norm'][l],
                            tb=cfg['mix_tb'], nb=cfg['mix_nb'])
    y_rt, rt_s = retention_apply(proj, st['ret'], st['layer'], mats['ret'], l, pos0,
                                 tb=cfg['mix_tb'], nb=cfg['mix_nb'])
    y_rw, rw_s, shift = rwkv_apply(proj, st['rwkv'], st['layer'], mats['rwkv'], st['shift'], p, l,
                                   tb=cfg['mix_tb'], nb=cfg['rw_nb'])
    parts = [y.reshape(bsz * seq, GW) for y in (y_s5, y_hg, y_rt, y_rw)]
    x, q = mix_out_q(parts, wb['w_out'][l], x, row('norm_mem'), wb['mem_w_q'][l], tm=tm)
    x = attend(q, wb['mem_w_o'][l], x)
    x = ffn(x, row('norm_ffn'), wb['ffn_w_up'][l], wb['ffn_w_down'][l], p['norm_final'].reshape(1, -1),
            tm=512, final_norm=final_norm)
    return x, (s5_re, s5_im, shift), dict(hgrn=hg_s, ret=rt_s, rwkv=rw_s)


def kernel(x_prompt, x_sample, mem_prompt, state_s5_re, state_s5_im, state_hgrn, state_ret, state_rwkv,
           state_rwkv_shift, cache_mem_k, cache_mem_v, norm_mix, w_in, w_out, s5_lam_re, s5_lam_im,
           s5_b_re, s5_b_im, s5_c_re, s5_c_im, s5_d, s5_log_step, s5_w_glu, s5_b_glu, s5_norm,
           hg_lb_logits, hg_norm, rw_mu, rw_w0, rw_w_w2, rw_a0, rw_w_a2, rw_w_g2, rw_k_k, rw_k_a, rw_r_k,
           rw_ln_g, rw_ln_b, norm_mem, mem_w_q, mem_w_k, mem_w_v, mem_w_o, norm_ffn, ffn_w_up, ffn_w_down,
           norm_final):
    p = dict(norm_mix=norm_mix, s5_lam_re=s5_lam_re, s5_lam_im=s5_lam_im, s5_b_re=s5_b_re, s5_b_im=s5_b_im,
             s5_c_re=s5_c_re, s5_c_im=s5_c_im, s5_d=s5_d, s5_log_step=s5_log_step, s5_w_glu=s5_w_glu,
             s5_b_glu=s5_b_glu, s5_norm=s5_norm, hg_norm=hg_norm, rw_mu=rw_mu, rw_w0=rw_w0, rw_w_w2=rw_w_w2,
             rw_a0=rw_a0, rw_w_a2=rw_w_a2, rw_w_g2=rw_w_g2, rw_k_k=rw_k_k, rw_k_a=rw_k_a, rw_r_k=rw_r_k,
             rw_ln_g=rw_ln_g, rw_ln_b=rw_ln_b, norm_mem=norm_mem, norm_ffn=norm_ffn, norm_final=norm_final)
    wb = {name: w.astype(BF16) for name, w in dict(
        w_in=w_in, w_out=w_out, mem_w_q=mem_w_q, mem_w_k=mem_w_k, mem_w_v=mem_w_v, mem_w_o=mem_w_o,
        ffn_w_up=ffn_w_up, ffn_w_down=ffn_w_down).items()}
    lb_all = jnp.cumsum(jax.nn.softmax(hg_lb_logits.astype(F32), axis=0), axis=0)
    lb_all = lb_all - lb_all[0:1]

    bp, lp, _ = x_prompt.shape
    bs, ls, _ = x_sample.shape
    yp = x_prompt.reshape(bp * lp, D_MODEL)
    ys = x_sample.reshape(bs * ls, D_MODEL)
    mem2d = mem_prompt.reshape(bp * MEM_LEN, D_MODEL)
    mat_zero = jnp.zeros((1, bp, NH, HD, HD), F32)
    p_small, s_small, p_mk, p_mv = [], [], [], []
    p_mats = {name: jnp.zeros((DEPTH, bp, NH, HD, HD), F32) for name in ('hgrn', 'ret', 'rwkv')}
    s_mats = {name: jnp.zeros((DEPTH, bs, NH, HD, HD), F32) for name in ('hgrn', 'ret', 'rwkv')}
    for l in range(DEPTH):
        final = l == DEPTH - 1
        mk = matmul(mem2d, wb['mem_w_k'][l], tm=512, tn=D_MODEL)
        mv = matmul(mem2d, wb['mem_w_v'][l], tm=512, tn=D_MODEL)
        def attend_p(q, w_o, x, mk=mk, mv=mv):
            return cross_attention_out(q, mk.reshape(bp, MEM_LEN, D_MODEL), mv.reshape(bp, MEM_LEN, D_MODEL),
                                       w_o, x, rows=512)

        zero_state = dict(s5_re=jnp.zeros((bp, S5_NG, S5_P), F32), s5_im=jnp.zeros((bp, S5_NG, S5_P), F32),
                          shift=jnp.zeros((bp, RW_PROJ), F32), hgrn=mat_zero, ret=mat_zero, rwkv=mat_zero, layer=0)
        yp, small, p_mats = _trunk_layer(yp, bp, lp, 0, attend_p, zero_state, p_mats, p, wb, l, lb_all[l], final)
        p_small.append(small)
        p_mk.append(mk.reshape(bp, MEM_LEN, MEM_HEADS, MEM_HD))
        p_mv.append(mv.reshape(bp, MEM_LEN, MEM_HEADS, MEM_HD))
        sst = dict(s5_re=state_s5_re[l], s5_im=state_s5_im[l], shift=state_rwkv_shift[l],
                   hgrn=state_hgrn, ret=state_ret, rwkv=state_rwkv, layer=l)
        def attend_s(q, w_o, x, l=l):
            o = cross_attention_cache(q, cache_mem_k, cache_mem_v, l, nb=4, rows=ls)
            return matmul_residual(o, w_o, x, tm=512)

        ys, small, s_mats = _trunk_layer(ys, bs, ls, PAST_LEN, attend_s, sst, s_mats, p, wb, l, lb_all[l], final)
        s_small.append(small)
    stack = lambda states, i: jnp.stack([s[i] for s in states])
    return (yp.reshape(bp, lp, D_MODEL), ys.reshape(bs, ls, D_MODEL),
            stack(p_small, 0), stack(p_small, 1), p_mats['hgrn'], p_mats['ret'], p_mats['rwkv'], stack(p_small, 2),
            jnp.stack(p_mk), jnp.stack(p_mv),
            stack(s_small, 0), stack(s_small, 1), s_mats['hgrn'], s_mats['ret'], s_mats['rwkv'], stack(s_small, 2))
```

```python
import functools
import math

import jax
import jax.numpy as jnp
from jax import lax
from jax.experimental import pallas as pl
from jax.experimental.pallas import tpu as pltpu

F32 = jnp.float32
BF16 = jnp.bfloat16

D_MODEL = 1024
DEPTH = 2
PAST_LEN = 16384
GW = 256
HD = 64
NH = GW // HD
S5_GROUP = 16
S5_NG = GW // S5_GROUP
S5_P = 64
S5_W = S5_NG * S5_P
RW_PROJ = 4 * GW
IN_WIDTH = 13 * GW
MEM_LEN = 256
MEM_HEADS = 4
MEM_HD = D_MODEL // MEM_HEADS
D_FF = 4 * D_MODEL
EPS = 1e-6
RW_LN_EPS = 64e-5
ROPE_BASE = 10000.0
CHUNK = 16
RET_CHUNK = 64

VMEM_LIMIT = 56 * 1024 * 1024
SUBLANES = 8
LOG2_E = 1.4426950408889634


def _cparams(sem):
    return pltpu.CompilerParams(dimension_semantics=sem, vmem_limit_bytes=VMEM_LIMIT)


def _mm(a, b):
    return jnp.dot(a.astype(BF16), b.astype(BF16), preferred_element_type=F32)


def _mm_nt(a, b):
    return lax.dot_general(a.astype(BF16), b.astype(BF16), (((1,), (1,)), ((), ())),
                           preferred_element_type=F32)


def _mm_tn(a, b):
    return lax.dot_general(a.astype(BF16), b.astype(BF16), (((0,), (0,)), ((), ())),
                           preferred_element_type=F32)


def _split3(x):
    hi = x.astype(BF16)
    r1 = x - hi.astype(F32)
    mid = r1.astype(BF16)
    lo = (r1 - mid.astype(F32)).astype(BF16)
    return hi, mid, lo


def _mm_exact_lhs(sel, x):
    s = sel.astype(BF16)
    hi, mid, lo = _split3(x)
    return (jnp.dot(s, hi, preferred_element_type=F32) + jnp.dot(s, mid, preferred_element_type=F32)
            + jnp.dot(s, lo, preferred_element_type=F32))


def _rms(x, gain):
    return x * lax.rsqrt(jnp.mean(x * x, axis=-1, keepdims=True) + EPS) * gain


def _sigmoid(x):
    return 1.0 / (1.0 + jnp.exp(-x))


def _iota2(shape, axis):
    return lax.broadcasted_iota(jnp.int32, shape, axis)


def _head_ones():
    return (_iota2((GW, GW), 0) // HD == _iota2((GW, GW), 1) // HD).astype(F32)


def _eye(n):
    return (_iota2((n, n), 0) == _iota2((n, n), 1)).astype(F32)


def _head_sum(x, ones_bd):
    s = ones_bd.astype(BF16)
    hi = x.astype(BF16)
    lo = (x - hi.astype(F32)).astype(BF16)
    return jnp.dot(hi, s, preferred_element_type=F32) + jnp.dot(lo, s, preferred_element_type=F32)


def _norm_mm_kernel(x_ref, g_ref, w_ref, o_ref, xn_ref):
    @pl.when(pl.program_id(1) == 0)
    def _():
        xn_ref[...] = _rms(x_ref[...], g_ref[...]).astype(BF16)

    o_ref[...] = jnp.dot(xn_ref[...], w_ref[...], preferred_element_type=F32)


def norm_matmul(x, gain, w, *, tm, tn):
    t, d = x.shape
    n = w.shape[1]
    return pl.pallas_call(
        _norm_mm_kernel,
        out_shape=jax.ShapeDtypeStruct((t, n), F32),
        grid=(t // tm, n // tn),
        in_specs=[pl.BlockSpec((tm, d), lambda i, j: (i, 0)),
                  pl.BlockSpec((1, d), lambda i, j: (0, 0)),
                  pl.BlockSpec((d, tn), lambda i, j: (0, j))],
        out_specs=pl.BlockSpec((tm, tn), lambda i, j: (i, j)),
        scratch_shapes=[pltpu.VMEM((tm, d), BF16)],
        compiler_params=_cparams(("parallel", "arbitrary")),
        name="norm_matmul",
    )(x, gain, w)


def _mm_kernel(a_ref, w_ref, o_ref):
    o_ref[...] = _mm(a_ref[...], w_ref[...])


def matmul(a, w, *, tm, tn):
    t, k = a.shape
    n = w.shape[1]
    return pl.pallas_call(
        _mm_kernel,
        out_shape=jax.ShapeDtypeStruct((t, n), F32),
        grid=(t // tm, n // tn),
        in_specs=[pl.BlockSpec((tm, k), lambda i, j: (i, 0)),
                  pl.BlockSpec((k, tn), lambda i, j: (0, j))],
        out_specs=pl.BlockSpec((tm, tn), lambda i, j: (i, j)),
        compiler_params=_cparams(("parallel", "parallel")),
        name="matmul",
    )(a, w)


def _mm_res_kernel(a_ref, w_ref, r_ref, o_ref):
    o_ref[...] = r_ref[...] + _mm(a_ref[...], w_ref[...])


def matmul_residual(a, w, res, *, tm):
    t, k = a.shape
    n = w.shape[1]
    return pl.pallas_call(
        _mm_res_kernel,
        out_shape=jax.ShapeDtypeStruct((t, n), F32),
        grid=(t // tm,),
        in_specs=[pl.BlockSpec((tm, k), lambda i: (i, 0)),
                  pl.BlockSpec((k, n), lambda i: (0, 0)),
                  pl.BlockSpec((tm, n), lambda i: (i, 0))],
        out_specs=pl.BlockSpec((tm, n), lambda i: (i, 0)),
        compiler_params=_cparams(("parallel",)),
        name="matmul_residual",
    )(a, w, res)


def _mix_out_q_kernel(a0_ref, a1_ref, a2_ref, a3_ref, w_ref, r_ref, g_ref, wq_ref, x_ref, q_ref):
    a = jnp.concatenate([a0_ref[...], a1_ref[...], a2_ref[...], a3_ref[...]], axis=-1)
    x = r_ref[...] + _mm(a, w_ref[...])
    x_ref[...] = x
    q_ref[...] = jnp.dot(_rms(x, g_ref[...]).astype(BF16), wq_ref[...], preferred_element_type=F32).astype(BF16)


def mix_out_q(parts, w, res, gain, w_q, *, tm):
    t, d = res.shape
    part = pl.BlockSpec((tm, GW), lambda i: (i, 0))
    mat = pl.BlockSpec((d, d), lambda i: (0, 0))
    tile = pl.BlockSpec((tm, d), lambda i: (i, 0))
    return pl.pallas_call(
        _mix_out_q_kernel,
        out_shape=(jax.ShapeDtypeStruct((t, d), F32), jax.ShapeDtypeStruct((t, d), BF16)),
        grid=(t // tm,),
        in_specs=[part, part, part, part, mat, tile, pl.BlockSpec((1, d), lambda i: (0, 0)), mat],
        out_specs=(tile, tile),
        compiler_params=_cparams(("parallel",)),
        name="mix_out_q",
    )(*parts, w, res, gain, w_q)


def _ffn_kernel(x_ref, g_ref, wu_ref, wd_ref, gf_ref, o_ref, *, final_norm):
    x = x_ref[...]
    h = jnp.dot(_rms(x, g_ref[...]).astype(BF16), wu_ref[...], preferred_element_type=F32)
    h = jnp.square(jnp.maximum(h, 0.0))
    y = x + jnp.dot(h.astype(BF16), wd_ref[...], preferred_element_type=F32)
    if final_norm:
        y = _rms(y, gf_ref[...])
    o_ref[...] = y


def ffn(x, gain, w_up, w_down, gain_final, *, tm, final_norm):
    t, d = x.shape
    ff = w_up.shape[1]
    resident = lambda shape: pl.BlockSpec(shape, lambda i: (0, 0), pipeline_mode=pl.Buffered(1))
    return pl.pallas_call(
        functools.partial(_ffn_kernel, final_norm=final_norm),
        out_shape=jax.ShapeDtypeStruct((t, d), F32),
        grid=(t // tm,),
        in_specs=[pl.BlockSpec((tm, d), lambda i: (i, 0)),
                  pl.BlockSpec((1, d), lambda i: (0, 0)),
                  resident((d, ff)), resident((ff, d)),
                  pl.BlockSpec((1, d), lambda i: (0, 0))],
        out_specs=pl.BlockSpec((tm, d), lambda i: (i, 0)),
        compiler_params=_cparams(("parallel",)),
        name="ffn",
    )(x, gain, w_up, w_down, gain_final)


def _attn_out_kernel(q_ref, k_ref, v_ref, wo_ref, r_ref, x_ref):
    heads = []
    for h in range(MEM_HEADS):
        sl = slice(h * MEM_HD, (h + 1) * MEM_HD)
        s = _mm_nt(q_ref[:, sl], k_ref[:, sl]) * (MEM_HD ** -0.5)
        p = jnp.exp(s - jnp.max(s, axis=-1, keepdims=True))
        p = p * (1.0 / jnp.sum(p, axis=-1, keepdims=True))
        heads.append(_mm(p, v_ref[:, sl]))
    x_ref[...] = r_ref[...] + _mm(jnp.concatenate(heads, axis=-1), wo_ref[...])


def cross_attention_out(q, mem_k, mem_v, w_o, res, *, rows):
    t, d = res.shape
    bsz = mem_k.shape[0]
    lt = t // bsz // rows
    tile = pl.BlockSpec((rows, d), lambda b, l: (b * lt + l, 0))
    mem = pl.BlockSpec((None, MEM_LEN, d), lambda b, l: (b, 0, 0))
    return pl.pallas_call(
        _attn_out_kernel,
        out_shape=jax.ShapeDtypeStruct((t, d), F32),
        grid=(bsz, lt),
        in_specs=[tile, mem, mem, pl.BlockSpec((d, d), lambda b, l: (0, 0)), tile],
        out_specs=tile,
        compiler_params=_cparams(("parallel", "arbitrary")),
        name="cross_attention_out",
    )(q, mem_k, mem_v, w_o, res)


def _attn_cache_kernel(q_ref, k_ref, v_ref, o_ref, *, nb, rows):
    nr = MEM_HEADS * rows
    same_head = _iota2((nr, MEM_HEADS * MEM_LEN), 1) % MEM_HEADS == _iota2((nr, MEM_HEADS * MEM_LEN), 0) // rows
    q_all = q_ref[...].astype(F32)
    for b in range(nb):
        q = q_all[b * rows:(b + 1) * rows, :]
        qs = jnp.concatenate([q[:, h * MEM_HD:(h + 1) * MEM_HD] for h in range(MEM_HEADS)], axis=0)
        s = _mm_nt(qs, k_ref[0, b].reshape(MEM_HEADS * MEM_LEN, MEM_HD)) * (MEM_HD ** -0.5)
        s = jnp.where(same_head, s, -1e30)
        p = jnp.exp(s - jnp.max(s, axis=-1, keepdims=True))
        p = p * (1.0 / jnp.sum(p, axis=-1, keepdims=True))
        o = _mm(p, v_ref[0, b].reshape(MEM_HEADS * MEM_LEN, MEM_HD))
        for h in range(MEM_HEADS):
            o_ref[b * rows:(b + 1) * rows, h * MEM_HD:(h + 1) * MEM_HD] = o[h * rows:(h + 1) * rows]


def cross_attention_cache(q, cache_k, cache_v, layer, *, nb, rows):
    t = q.shape[0]
    bsz = cache_k.shape[1]
    mem = pl.BlockSpec((1, nb, MEM_LEN, MEM_HEADS, MEM_HD), lambda b: (layer, b, 0, 0, 0))
    return pl.pallas_call(
        functools.partial(_attn_cache_kernel, nb=nb, rows=rows),
        out_shape=jax.ShapeDtypeStruct((t, D_MODEL), F32),
        grid=(bsz // nb,),
        in_specs=[pl.BlockSpec((nb * rows, D_MODEL), lambda b: (b, 0)), mem, mem],
        out_specs=pl.BlockSpec((nb * rows, D_MODEL), lambda b: (b, 0)),
        compiler_params=_cparams(("parallel",)),
        name="cross_attention_cache",
    )(q, cache_k, cache_v)


def _gelu_tanh(x):
    return 0.5 * x * (1.0 + jnp.tanh(math.sqrt(2.0 / math.pi) * (x + 0.044715 * (x * x * x))))


def _s5_kernel(u_ref, h0_ref, lam_ref, bblk_ref, cblk_ref, d_ref, wglu_ref, bglu_ref, gain_ref,
               y_ref, hfin_ref, scr_ref, *, steps, nb):
    @pl.when(pl.program_id(1) == 0)
    def _():
        scr_ref[0:nb, :] = h0_ref[...]

    u = u_ref[...].reshape(steps * nb, GW)
    scr_ref[nb:, :] = _mm(u, bblk_ref[...])
    lam_re = lam_ref[0:1, :]
    lam_im = lam_ref[1:2, :]

    def step(t, carry):
        p0 = pl.multiple_of(t * nb, nb)
        c0 = pl.multiple_of((t + 1) * nb, nb)
        h_re = scr_ref[pl.ds(p0, nb), 0:S5_W]
        h_im = scr_ref[pl.ds(p0, nb), S5_W:]
        scr_ref[pl.ds(c0, nb), 0:S5_W] = scr_ref[pl.ds(c0, nb), 0:S5_W] + lam_re * h_re - lam_im * h_im
        scr_ref[pl.ds(c0, nb), S5_W:] = scr_ref[pl.ds(c0, nb), S5_W:] + lam_re * h_im + lam_im * h_re
        return carry

    lax.fori_loop(0, steps, step, 0)
    h_last = scr_ref[steps * nb:, :]
    hfin_ref[...] = h_last
    y = _mm(scr_ref[nb:, :], cblk_ref[...]) + d_ref[...] * u
    y = _gelu_tanh(y)
    y = y * _sigmoid(_mm(y, wglu_ref[...]) + bglu_ref[...])
    y_ref[...] = _rms(y, gain_ref[...]).reshape(steps, nb, GW)
    scr_ref[0:nb, :] = h_last


def s5_mixer(u_tm, h0, lam, bblk, cblk, dvec, wglu, bglu, gain, *, steps, nb):
    seq, bsz, _ = u_tm.shape
    const = lambda shape: pl.BlockSpec(shape, lambda b, t: (0, 0))
    return pl.pallas_call(
        functools.partial(_s5_kernel, steps=steps, nb=nb),
        out_shape=(jax.ShapeDtypeStruct((seq, bsz, GW), F32),
                   jax.ShapeDtypeStruct((bsz, 2 * S5_W), F32)),
        grid=(bsz // nb, seq // steps),
        in_specs=[pl.BlockSpec((steps, nb, GW), lambda b, t: (t, b, 0)),
                  pl.BlockSpec((nb, 2 * S5_W), lambda b, t: (b, 0)),
                  const((2, S5_W)), const((GW, 2 * S5_W)), const((2 * S5_W, GW)), const((1, GW)),
                  const((GW, GW)), const((1, GW)), const((1, GW))],
        out_specs=(pl.BlockSpec((steps, nb, GW), lambda b, t: (t, b, 0)),
                   pl.BlockSpec((nb, 2 * S5_W), lambda b, t: (b, 0))),
        scratch_shapes=[pltpu.VMEM(((steps + 1) * nb, 2 * S5_W), F32)],
        compiler_params=_cparams(("parallel", "arbitrary")),
        name="s5_mixer",
    )(u_tm, h0, lam, bblk, cblk, dvec, wglu, bglu, gain)


def s5_params(lam_re, lam_im, b_re, b_im, c_re, c_im, log_step):
    step = jnp.exp(log_step)[:, None]
    mag = jnp.exp(lam_re * step)
    lbar_re = mag * jnp.cos(lam_im * step)
    lbar_im = mag * jnp.sin(lam_im * step)
    den = lam_re * lam_re + lam_im * lam_im
    f_re = ((lbar_re - 1.0) * lam_re + lbar_im * lam_im) / den
    f_im = (lbar_im * lam_re - (lbar_re - 1.0) * lam_im) / den
    bbar_re = f_re[..., None] * b_re - f_im[..., None] * b_im
    bbar_im = f_re[..., None] * b_im + f_im[..., None] * b_re
    eye_g = jnp.eye(S5_NG, dtype=F32)

    def in_blk(m):
        return jnp.einsum('gph,gk->ghkp', m, eye_g).reshape(GW, S5_W)

    def out_blk(m):
        return jnp.einsum('ghp,gk->gpkh', m, eye_g).reshape(S5_W, GW)

    bblk = jnp.concatenate([in_blk(bbar_re), in_blk(bbar_im)], axis=1)
    cblk = jnp.concatenate([out_blk(c_re), -out_blk(c_im)], axis=0)
    lam2 = jnp.stack([lbar_re.reshape(S5_W), lbar_im.reshape(S5_W)])
    return lam2, bblk.astype(BF16), cblk.astype(BF16)


def s5_apply(u, h0_re, h0_im, p, l, *, steps, nb):
    bsz = u.shape[0]
    lam2, bblk, cblk = s5_params(p['s5_lam_re'][l], p['s5_lam_im'][l], p['s5_b_re'][l], p['s5_b_im'][l],
                                 p['s5_c_re'][l], p['s5_c_im'][l], p['s5_log_step'][l])
    h0 = jnp.concatenate([h0_re.reshape(bsz, S5_W), h0_im.reshape(bsz, S5_W)], axis=1)
    y, h = s5_mixer(jnp.swapaxes(u, 0, 1), h0, lam2, bblk, cblk, p['s5_d'][l].reshape(1, GW),
                    p['s5_w_glu'][l].astype(BF16), p['s5_b_glu'][l].reshape(1, GW),
                    p['s5_norm'][l].reshape(1, GW), steps=steps, nb=nb)
    return (jnp.swapaxes(y, 0, 1), h[:, :S5_W].reshape(bsz, S5_NG, S5_P),
            h[:, S5_W:].reshape(bsz, S5_NG, S5_P))


def _load_state(s0_ref, st_ref, nb, transpose):
    for b in range(nb):
        rows = []
        for h in range(NH):
            pieces = [s0_ref[b, h]]
            if h:
                pieces.insert(0, jnp.zeros((HD, h * HD), F32))
            if h < NH - 1:
                pieces.append(jnp.zeros((HD, (NH - 1 - h) * HD), F32))
            rows.append(jnp.concatenate(pieces, axis=1))
        st = jnp.concatenate(rows, axis=0)
        st_ref[b] = st.T if transpose else st


def _store_state(st_ref, sfin_ref, nb, transpose):
    for b in range(nb):
        st = st_ref[b].T if transpose else st_ref[b]
        for h in range(NH):
            sfin_ref[b, h] = st[h * HD:(h + 1) * HD, h * HD:(h + 1) * HD]


def _head_masks():
    lane_head = _iota2((1, GW), 1) // HD
    return [(lane_head == h).astype(F32) for h in range(NH)]


def _stack_heads(x, masks):
    return jnp.concatenate([x * m for m in masks], axis=0)


def _pad_rows(x, rows):
    if x.shape[0] == rows:
        return x
    return jnp.concatenate([x, jnp.zeros((rows - x.shape[0], x.shape[1]), x.dtype)], axis=0)


def _silu(x):
    return x * _sigmoid(x)


def _chunk_rows(ref, b, r0, rows, c):
    return _pad_rows(ref[b, pl.ds(r0, rows), :], c)


def _ret_kernel(q_ref, k_ref, v_ref, g_ref, cos_ref, sin_ref, s0_ref, _states_in, o_ref, sfin_ref, st_ref,
                *, nb, tb, c, c_real):
    rows = min(tb, c)

    @pl.when(pl.program_id(1) == 0)
    def _():
        _load_state(s0_ref, st_ref, nb, transpose=True)

    masks = _head_masks()
    ones_bd = _head_ones()
    lane_head = _iota2((1, GW), 1) // HD
    log_gamma = jnp.zeros((1, GW), F32)
    for h in range(NH):
        log_gamma = jnp.where(lane_head == h, math.log(1.0 - 2.0 ** (-5.0 - h)), log_gamma)
    tt = _iota2((c, GW), 0).astype(F32)
    scale = HD ** -0.5
    g_q = jnp.exp(log_gamma * (tt + 1.0))
    g_k = jnp.exp(-log_gamma * (tt + 1.0)) * scale
    g_tail = jnp.exp(log_gamma * (c_real - 1.0 - tt)) * scale
    g_chunk = jnp.exp(log_gamma * float(c_real))
    causal = _iota2((c, NH * c), 0) >= _iota2((c, NH * c), 1) % c
    first_half = _iota2((c, GW), 1) % HD < HD // 2

    def rope(x, cos, sin):
        swapped = jnp.where(first_half, pltpu.roll(x, GW - HD // 2, axis=1), pltpu.roll(x, HD // 2, axis=1))
        return x * cos + swapped * sin

    def chunk(ci, carry):
        r0 = pl.multiple_of(ci * rows, rows)
        cos = _pad_rows(cos_ref[pl.ds(r0, rows), :], c)
        sin = _pad_rows(sin_ref[pl.ds(r0, rows), :], c)
        rng = range(nb)
        qt = [rope(_chunk_rows(q_ref, b, r0, rows, c), cos, sin) * g_q for b in rng]
        k = [rope(_chunk_rows(k_ref, b, r0, rows, c), cos, sin) for b in rng]
        v = [_chunk_rows(v_ref, b, r0, rows, c) for b in rng]
        st = [st_ref[b] for b in rng]
        sc = [_mm_nt(qt[b], _stack_heads(k[b] * g_k, masks)) for b in rng]
        o_in = [_mm(jnp.where(causal, sc[b], 0.0), _stack_heads(v[b], masks)) for b in rng]
        o_st = [_mm_nt(qt[b], st[b]) for b in rng]
        d_st = [_mm_tn(v[b], k[b] * g_tail) for b in rng]
        o = [o_in[b] + o_st[b] for b in rng]
        ms = [_head_sum(o[b] * o[b], ones_bd) for b in rng]
        for b in rng:
            st_ref[b] = st[b] * g_chunk + ones_bd * d_st[b]
            ob = o[b] * lax.rsqrt(ms[b] * (1.0 / HD) + EPS) * _silu(_chunk_rows(g_ref, b, r0, rows, c))
            o_ref[b, pl.ds(r0, rows), :] = ob[:rows]
        return carry

    lax.fori_loop(0, tb // rows, chunk, 0)

    @pl.when(pl.program_id(1) == pl.num_programs(1) - 1)
    def _():
        _store_state(st_ref, sfin_ref, nb, transpose=True)


def _rope_tables(pos0, seq):
    half = HD // 2
    inv = ROPE_BASE ** (-jnp.arange(half, dtype=F32) / half)
    pos = pos0 + jnp.arange(seq, dtype=jnp.int32)
    ang = pos.astype(F32)[:, None] * inv[None, :]
    cos, sin = jnp.cos(ang), jnp.sin(ang)
    return (jnp.tile(jnp.concatenate([cos, cos], axis=-1), (1, NH)),
            jnp.tile(jnp.concatenate([-sin, sin], axis=-1), (1, NH)))


def _proj_spec(nb, tb, col):
    return pl.BlockSpec((nb, tb, GW), lambda b, t: (b, t, col))


def _state_spec(nb, layer):
    return pl.BlockSpec((None, nb, NH, HD, HD), lambda b, t: (layer, b, 0, 0, 0))


_ALIASED = pl.BlockSpec(memory_space=pl.ANY)


def retention_apply(proj, s0, s0_layer, states, layer, pos0, *, tb, nb):
    bsz, seq, _ = proj.shape
    c = RET_CHUNK if tb >= RET_CHUNK else CHUNK
    cos, sin = _rope_tables(pos0, seq)
    return pl.pallas_call(
        functools.partial(_ret_kernel, nb=nb, tb=tb, c=c, c_real=min(tb, c)),
        out_shape=(jax.ShapeDtypeStruct((bsz, seq, GW), F32), jax.ShapeDtypeStruct(states.shape, F32)),
        grid=(bsz // nb, seq // tb),
        in_specs=[_proj_spec(nb, tb, 5), _proj_spec(nb, tb, 6), _proj_spec(nb, tb, 7), _proj_spec(nb, tb, 8),
                  pl.BlockSpec((tb, GW), lambda b, t: (t, 0)), pl.BlockSpec((tb, GW), lambda b, t: (t, 0)),
                  _state_spec(nb, s0_layer), _ALIASED],
        out_specs=(pl.BlockSpec((nb, tb, GW), lambda b, t: (b, t, 0)), _state_spec(nb, layer)),
        scratch_shapes=[pltpu.VMEM((nb, GW, GW), F32)],
        input_output_aliases={7: 1},
        compiler_params=_cparams(("parallel", "arbitrary")),
        name="retention",
    )(proj, proj, proj, proj, cos, sin, s0, states)


def _log_sigmoid(z):
    return jnp.minimum(z, 0.0) - jnp.log(1.0 + jnp.exp(-jnp.abs(z)))


def _hgrn_kernel(q_ref, f_ref, i_ref, g_ref, lb_ref, gain_ref, s0_ref, _states_in, o_ref, sfin_ref, st_ref,
                 *, nb, tb, c):
    rows = min(tb, c)

    @pl.when(pl.program_id(1) == 0)
    def _():
        _load_state(s0_ref, st_ref, nb, transpose=True)

    t_hi = -(-rows // SUBLANES) * SUBLANES
    ones_bd = _head_ones()
    tri = (_iota2((c, c), 0) >= _iota2((c, c), 1)).astype(F32)
    t_idx = _iota2((c, GW), 0)
    lb = lb_ref[...]
    log_lb = jnp.log(lb)
    log_1m_lb = jnp.log(1.0 - lb)
    gain = gain_ref[...]

    def chunk(ci, carry):
        r0 = pl.multiple_of(ci * rows, rows)
        rng = range(nb)
        valid = t_idx < rows

        def gates(b):
            z = _chunk_rows(f_ref, b, r0, rows, c)
            ls_pos = _log_sigmoid(z)
            ls_neg = ls_pos - z
            b2 = log_lb + ls_neg
            log_f = jnp.maximum(ls_pos, b2) + jnp.log(1.0 + jnp.exp(-jnp.abs(ls_pos - b2)))
            return (jnp.where(valid, log_f, 0.0), jnp.where(valid, (1.0 - lb) * jnp.exp(ls_neg), 0.0),
                    ls_neg + log_1m_lb)

        log_f, key, log_key = zip(*[gates(b) for b in rng])
        q = [_silu(_chunk_rows(q_ref, b, r0, rows, c)) for b in rng]
        v = [_chunk_rows(i_ref, b, r0, rows, c) for b in rng]
        st = [st_ref[b] for b in rng]
        cum = [_mm_exact_lhs(tri, log_f[b]) for b in rng]
        last = [cum[b][c - 1:c, :] for b in rng]
        cum2 = [cum[b] * LOG2_E for b in rng]
        kd2 = [(log_key[b] - cum[b]) * LOG2_E for b in rng]

        def pair_rows(b, s):
            lo = SUBLANES * (s // SUBLANES)
            t_rows = _iota2((t_hi - lo, GW), 0) + lo
            key_decay = jnp.where(t_rows >= s, jnp.exp2(cum2[b][lo:t_hi] + kd2[b][s:s + 1, :]), 0.0)
            return key_decay * q[b][lo:t_hi]

        sc = [_mm(jnp.concatenate([pair_rows(b, s) for s in range(rows)], axis=0), ones_bd) for b in rng]
        o_st = [_mm_nt(q[b] * jnp.exp(cum[b]), st[b]) for b in rng]
        d_st = [_mm_tn(v[b], key[b] * jnp.exp(last[b] - cum[b])) for b in rng]

        def intra(b):
            tiles = [None] * (c // SUBLANES)
            off = 0
            for s in range(rows):
                for ti in range(s // SUBLANES, t_hi // SUBLANES):
                    term = sc[b][off:off + SUBLANES, :] * v[b][s:s + 1, :]
                    tiles[ti] = term if tiles[ti] is None else tiles[ti] + term
                    off += SUBLANES
            zero = jnp.zeros((SUBLANES, GW), F32)
            return jnp.concatenate([zero if t is None else t for t in tiles], axis=0)

        o = [o_st[b] + intra(b) for b in rng]
        ms = [_head_sum(o[b] * o[b], ones_bd) for b in rng]
        for b in rng:
            st_ref[b] = st[b] * jnp.exp(last[b]) + ones_bd * d_st[b]
            ob = o[b] * lax.rsqrt(ms[b] * (1.0 / HD) + EPS) * gain * _silu(_chunk_rows(g_ref, b, r0, rows, c))
            o_ref[b, pl.ds(r0, rows), :] = ob[:rows]
        return carry

    lax.fori_loop(0, tb // rows, chunk, 0)

    @pl.when(pl.program_id(1) == pl.num_programs(1) - 1)
    def _():
        _store_state(st_ref, sfin_ref, nb, transpose=True)


def hgrn_apply(proj, s0, s0_layer, states, layer, lb, gain, *, tb, nb):
    bsz, seq, _ = proj.shape
    row = pl.BlockSpec((1, GW), lambda b, t: (0, 0))
    return pl.pallas_call(
        functools.partial(_hgrn_kernel, nb=nb, tb=tb, c=CHUNK),
        out_shape=(jax.ShapeDtypeStruct((bsz, seq, GW), F32), jax.ShapeDtypeStruct(states.shape, F32)),
        grid=(bsz // nb, seq // tb),
        in_specs=[_proj_spec(nb, tb, 1), _proj_spec(nb, tb, 2), _proj_spec(nb, tb, 3), _proj_spec(nb, tb, 4),
                  row, row, _state_spec(nb, s0_layer), _ALIASED],
        out_specs=(pl.BlockSpec((nb, tb, GW), lambda b, t: (b, t, 0)), _state_spec(nb, layer)),
        scratch_shapes=[pltpu.VMEM((nb, GW, GW), F32)],
        input_output_aliases={7: 1},
        compiler_params=_cparams(("parallel", "arbitrary")),
        name="hgrn2",
    )(proj, proj, proj, proj, lb.reshape(1, GW), gain.reshape(1, GW), s0, states)


def _softplus(z):
    return jnp.maximum(z, 0.0) + jnp.log(1.0 + jnp.exp(-jnp.abs(z)))


def _rwkv_prepare(ins, consts):
    masks, ones_bd, bd_mask, eye_flat, tri, strict, incl = consts
    rng = range(len(ins))
    r, lw, k, v, kk, a = [[i[j] for i in ins] for j in range(6)]
    c = r[0].shape[0]
    n = NH * c

    def to_bd(flat):
        return jnp.concatenate([flat] * NH, axis=0) * bd_mask

    cum = [_mm_exact_lhs(tri, lw[i]) for i in rng]
    last = [cum[i][c - 1:c, :] for i in rng]
    p_inv = [jnp.exp(-cum[i]) for i in rng]
    p_tail = [jnp.exp(last[i] - cum[i]) for i in rng]
    ka = [kk[i] * a[i] for i in rng]
    x = [jnp.concatenate([kk[i] * jnp.exp(cum[i] - lw[i]), r[i] * jnp.exp(cum[i])], axis=0) for i in rng]
    g = [_mm_nt(x[i], jnp.concatenate([_stack_heads(ka[i] * p_inv[i], masks),
                                       _stack_heads(k[i] * p_inv[i], masks)], axis=0)) for i in rng]
    m_ak = [jnp.where(strict, g[i][:c], 0.0) for i in rng]
    n_ak = [jnp.where(incl, g[i][c:], 0.0) for i in rng]
    pw = [m_ak[i][:, :n] for i in rng]
    t_inv = [eye_flat - pw[i] for i in rng]
    pw_bd = [to_bd(pw[i]) for i in rng]
    for _ in range(int(math.log2(c)) - 1):
        pw = [_mm(pw[i], pw_bd[i]) for i in rng]
        pw_bd = [to_bd(pw[i]) for i in rng]
        t_inv = [t_inv[i] + _mm(t_inv[i], pw_bd[i]) for i in rng]
    v_stk = [_stack_heads(v[i], masks) for i in rng]
    zeros = jnp.zeros((n, GW), F32)
    mkv = [_mm(m_ak[i], jnp.concatenate([zeros, v_stk[i]], axis=0)) for i in rng]
    tmkv = [_mm(t_inv[i], _stack_heads(mkv[i], masks)) for i in rng]
    return [dict(x=x[i], t_inv=t_inv[i], tmkv=tmkv[i], n_ak=n_ak[i], v=v[i], v_stk=v_stk[i],
                 decay=jnp.exp(last[i]),
                 k_tail=jnp.concatenate([ka[i] * p_tail[i], k[i] * p_tail[i]], axis=0)) for i in rng]


def _rwkv_advance(prep, st, consts):
    masks, ones_bd = consts[0], consts[1]
    rng = range(len(prep))
    c = prep[0]['t_inv'].shape[0]
    xa = [_mm_nt(prep[i]['x'], st[i]) for i in rng]
    u = [-_mm(prep[i]['t_inv'], _stack_heads(xa[i][:c], masks)) - prep[i]['tmkv'] for i in rng]
    nuv = [_mm(prep[i]['n_ak'], jnp.concatenate([_stack_heads(u[i], masks), prep[i]['v_stk']], axis=0))
           for i in rng]
    d_uv = [_mm_tn(jnp.concatenate([u[i], prep[i]['v']], axis=0), prep[i]['k_tail']) for i in rng]
    return [(xa[i][c:] + nuv[i], st[i] * prep[i]['decay'] + ones_bd * d_uv[i]) for i in rng]


def _rwkv_kernel(x_r_ref, x_k_ref, x_v_ref, x_l_ref, sh0_ref, s0_ref, mu_ref, w0_ref, ww_ref, a0_ref,
                 wa_ref, wg_ref, kk_ref, ka_ref, rk_ref, lng_ref, lnb_ref, _states_in,
                 o_ref, sfin_ref, shfin_ref,
                 st_ref, sh_ref, r_s, lw_s, k_s, v_s, kkn_s, a_s, y_s, bonus_s, gate_s, *, nb, tb, c):
    tbp = max(tb, c)

    @pl.when(pl.program_id(1) == 0)
    def _():
        _load_state(s0_ref, st_ref, nb, transpose=False)
        sh_ref[...] = sh0_ref[...]

    masks = _head_masks()
    ones_bd = _head_ones()
    n = NH * c
    row_t = _iota2((c, 2 * n), 0)
    col_t = _iota2((c, 2 * n), 1) % c
    bd_mask = (_iota2((n, n), 0) // c == _iota2((n, n), 1) // c).astype(F32)
    eye_flat = (_iota2((c, n), 1) % c == _iota2((c, n), 0)).astype(F32)
    consts = (masks, ones_bd, bd_mask, eye_flat, (_iota2((c, c), 0) >= _iota2((c, c), 1)).astype(F32),
              col_t < row_t, col_t <= row_t)
    first_row = _iota2((tb, GW), 0) == 0

    for b in range(nb):
        def mixed(x_ref, j):
            x = x_ref[b]
            prev = jnp.where(first_row, sh_ref[b, j:j + 1, :], pltpu.roll(x, 1, axis=0))
            sh_ref[b, j:j + 1, :] = x[tb - 1:tb, :]
            return x + (prev - x) * mu_ref[j:j + 1, :]

        r = mixed(x_r_ref, 0)
        k = mixed(x_k_ref, 1)
        v = mixed(x_v_ref, 2)
        xl = mixed(x_l_ref, 3)
        log_w = -_softplus(-(w0_ref[...] + _mm(jnp.tanh(xl), ww_ref[...]))) - 0.5
        a = _sigmoid(a0_ref[...] + _mm(xl, wa_ref[...]))
        gate = _mm(_sigmoid(xl), wg_ref[...])
        kk = k * kk_ref[...]
        kk = kk * lax.rsqrt(jnp.maximum(_head_sum(kk * kk, ones_bd), 1e-24))
        k = k * (1.0 + (a - 1.0) * ka_ref[...])
        r_s[b, 0:tb, :] = r
        lw_s[b, 0:tb, :] = -jnp.exp(log_w)
        k_s[b, 0:tb, :] = k
        v_s[b, 0:tb, :] = v
        kkn_s[b, 0:tb, :] = kk
        a_s[b, 0:tb, :] = a
        bonus_s[b] = _head_sum(r * k * rk_ref[...], ones_bd) * v
        gate_s[b] = gate
        if tbp > tb:
            zeros = jnp.zeros((tbp - tb, GW), F32)
            for s in (r_s, lw_s, k_s, v_s, kkn_s, a_s):
                s[b, tb:tbp, :] = zeros

    n_chunks = tbp // c
    group = 4 if n_chunks % 4 == 0 else 1

    def chunks(gi, carry):
        r0 = [pl.multiple_of((gi * group + j) * c, c) for j in range(group)]
        prep = _rwkv_prepare([tuple(s[b, pl.ds(r0[j], c), :] for s in (r_s, lw_s, k_s, v_s, kkn_s, a_s))
                              for j in range(group) for b in range(nb)], consts)
        st = [st_ref[b] for b in range(nb)]
        for j in range(group):
            outs = _rwkv_advance(prep[j * nb:(j + 1) * nb], st, consts)
            st = [o[1] for o in outs]
            for b in range(nb):
                y_s[b, pl.ds(r0[j], c), :] = outs[b][0]
        for b in range(nb):
            st_ref[b] = st[b]
        return carry

    lax.fori_loop(0, n_chunks // group, chunks, 0)

    for b in range(nb):
        y = y_s[b, 0:tb, :]
        mean = _head_sum(y, ones_bd) * (1.0 / HD)
        d = y - mean
        var = _head_sum(d * d, ones_bd) * (1.0 / HD)
        y = d * lax.rsqrt(var + RW_LN_EPS) * lng_ref[...] + lnb_ref[...]
        o_ref[b] = (y + bonus_s[b]) * gate_s[b]
    shfin_ref[...] = sh_ref[...]

    @pl.when(pl.program_id(1) == pl.num_programs(1) - 1)
    def _():
        _store_state(st_ref, sfin_ref, nb, transpose=False)


def rwkv_apply(proj, s0, s0_layer, states, shift0, p, l, *, tb, nb):
    bsz, seq, _ = proj.shape
    c = CHUNK
    tbp = max(tb, c)
    row = pl.BlockSpec((1, GW), lambda b, t: (0, 0))
    mat = pl.BlockSpec((GW, GW), lambda b, t: (0, 0))
    sh_spec = pl.BlockSpec((nb, 4, GW), lambda b, t: (b, 0, 0))
    zeros = lambda r: jnp.zeros((r, GW), F32)
    ww = jnp.concatenate([p['rw_w_w2'][l], zeros(192)], axis=0).astype(BF16)
    wa = jnp.concatenate([zeros(64), p['rw_w_a2'][l], zeros(128)], axis=0).astype(BF16)
    wg = jnp.concatenate([zeros(128), p['rw_w_g2'][l]], axis=0).astype(BF16)
    r1 = lambda name: p[name][l].reshape(1, GW)
    seq_buf = pltpu.VMEM((nb, tbp, GW), F32)
    blk_buf = pltpu.VMEM((nb, tb, GW), F32)
    y, st, sh = pl.pallas_call(
        functools.partial(_rwkv_kernel, nb=nb, tb=tb, c=c),
        out_shape=(jax.ShapeDtypeStruct((bsz, seq, GW), F32), jax.ShapeDtypeStruct(states.shape, F32),
                   jax.ShapeDtypeStruct((bsz, 4, GW), F32)),
        grid=(bsz // nb, seq // tb),
        in_specs=[_proj_spec(nb, tb, 9), _proj_spec(nb, tb, 10), _proj_spec(nb, tb, 11), _proj_spec(nb, tb, 12),
                  sh_spec, _state_spec(nb, s0_layer), pl.BlockSpec((4, GW), lambda b, t: (0, 0)),
                  row, mat, row, mat, mat, row, row, row, row, row, _ALIASED],
        out_specs=(pl.BlockSpec((nb, tb, GW), lambda b, t: (b, t, 0)), _state_spec(nb, l), sh_spec),
        scratch_shapes=[pltpu.VMEM((nb, GW, GW), F32), pltpu.VMEM((nb, 4, GW), F32),
                        seq_buf, seq_buf, seq_buf, seq_buf, seq_buf, seq_buf, seq_buf, blk_buf, blk_buf],
        input_output_aliases={17: 1},
        compiler_params=_cparams(("parallel", "arbitrary")),
        name="rwkv7",
    )(proj, proj, proj, proj, shift0.reshape(bsz, 4, GW), s0, p['rw_mu'][l].reshape(4, GW),
      r1('rw_w0'), ww, r1('rw_a0'), wa, wg, r1('rw_k_k'), r1('rw_k_a'), r1('rw_r_k'), r1('rw_ln_g'), r1('rw_ln_b'),
      states)
    return y, st, sh.reshape(bsz, RW_PROJ)


def _tiles(bsz, seq):
    if seq >= 128:
        return dict(tm=512, s5_steps=128, s5_nb=bsz, mix_tb=128, mix_nb=bsz, rw_nb=bsz)
    return dict(tm=512, s5_steps=seq, s5_nb=bsz, mix_tb=seq, mix_nb=8, rw_nb=16)


def _trunk_layer(x, bsz, seq, pos0, attend, st, mats, p, wb, l, lb, final_norm):
    cfg = _tiles(bsz, seq)
    tm = cfg['tm']
    row = lambda name: p[name][l].reshape(1, -1)
    proj = norm_matmul(x, row('norm_mix'), wb['w_in'][l], tm=tm, tn=IN_WIDTH)
    proj = proj.reshape(bsz, seq, IN_WIDTH)
    y_s5, s5_re, s5_im = s5_apply(proj[..., :GW], st['s5_re'], st['s5_im'], p, l,
                                  steps=cfg['s5_steps'], nb=cfg['s5_nb'])
    y_hg, hg_s = hgrn_apply(proj, st['hgrn'], st['layer'], mats['hgrn'], l, lb, p['hg_norm'][l],
                            tb=cfg['mix_tb'], nb=cfg['mix_nb'])
    y_rt, rt_s = retention_apply(proj, st['ret'], st['layer'], mats['ret'], l, pos0,
                                 tb=cfg['mix_tb'], nb=cfg['mix_nb'])
    y_rw, rw_s, shift = rwkv_apply(proj, st['rwkv'], st['layer'], mats['rwkv'], st['shift'], p, l,
                                   tb=cfg['mix_tb'], nb=cfg['rw_nb'])
    parts = [y.reshape(bsz * seq, GW) for y in (y_s5, y_hg, y_rt, y_rw)]
    x, q = mix_out_q(parts, wb['w_out'][l], x, row('norm_mem'), wb['mem_w_q'][l], tm=tm)
    x = attend(q, wb['mem_w_o'][l], x)
    x = ffn(x, row('norm_ffn'), wb['ffn_w_up'][l], wb['ffn_w_down'][l], p['norm_final'].reshape(1, -1),
            tm=512, final_norm=final_norm)
    return x, (s5_re, s5_im, shift), dict(hgrn=hg_s, ret=rt_s, rwkv=rw_s)


def kernel(x_prompt, x_sample, mem_prompt, state_s5_re, state_s5_im, state_hgrn, state_ret, state_rwkv,
           state_rwkv_shift, cache_mem_k, cache_mem_v, norm_mix, w_in, w_out, s5_lam_re, s5_lam_im,
           s5_b_re, s5_b_im, s5_c_re, s5_c_im, s5_d, s5_log_step, s5_w_glu, s5_b_glu, s5_norm,
           hg_lb_logits, hg_norm, rw_mu, rw_w0, rw_w_w2, rw_a0, rw_w_a2, rw_w_g2, rw_k_k, rw_k_a, rw_r_k,
           rw_ln_g, rw_ln_b, norm_mem, mem_w_q, mem_w_k, mem_w_v, mem_w_o, norm_ffn, ffn_w_up, ffn_w_down,
           norm_final):
    p = dict(norm_mix=norm_mix, s5_lam_re=s5_lam_re, s5_lam_im=s5_lam_im, s5_b_re=s5_b_re, s5_b_im=s5_b_im,
             s5_c_re=s5_c_re, s5_c_im=s5_c_im, s5_d=s5_d, s5_log_step=s5_log_step, s5_w_glu=s5_w_glu,
             s5_b_glu=s5_b_glu, s5_norm=s5_norm, hg_norm=hg_norm, rw_mu=rw_mu, rw_w0=rw_w0, rw_w_w2=rw_w_w2,
             rw_a0=rw_a0, rw_w_a2=rw_w_a2, rw_w_g2=rw_w_g2, rw_k_k=rw_k_k, rw_k_a=rw_k_a, rw_r_k=rw_r_k,
             rw_ln_g=rw_ln_g, rw_ln_b=rw_ln_b, norm_mem=norm_mem, norm_ffn=norm_ffn, norm_final=norm_final)
    wb = {name: w.astype(BF16) for name, w in dict(
        w_in=w_in, w_out=w_out, mem_w_q=mem_w_q, mem_w_k=mem_w_k, mem_w_v=mem_w_v, mem_w_o=mem_w_o,
        ffn_w_up=ffn_w_up, ffn_w_down=ffn_w_down).items()}
    lb_all = jnp.cumsum(jax.nn.softmax(hg_lb_logits.astype(F32), axis=0), axis=0)
    lb_all = lb_all - lb_all[0:1]

    bp, lp, _ = x_prompt.shape
    bs, ls, _ = x_sample.shape
    yp = x_prompt.reshape(bp * lp, D_MODEL)
    ys = x_sample.reshape(bs * ls, D_MODEL)
    mem2d = mem_prompt.reshape(bp * MEM_LEN, D_MODEL)
    mat_zero = jnp.zeros((1, bp, NH, HD, HD), F32)
    p_small, s_small, p_mk, p_mv = [], [], [], []
    p_mats = {name: jnp.full((DEPTH, bp, NH, HD, HD), float(i), F32) for i, name in enumerate(('hgrn', 'ret', 'rwkv'))}
    s_mats = {name: jnp.full((DEPTH, bs, NH, HD, HD), float(i), F32) for i, name in enumerate(('hgrn', 'ret', 'rwkv'))}
    for l in range(DEPTH):
        final = l == DEPTH - 1
        mk = matmul(mem2d, wb['mem_w_k'][l], tm=512, tn=D_MODEL)
        mv = matmul(mem2d, wb['mem_w_v'][l], tm=512, tn=D_MODEL)
        def attend_p(q, w_o, x, mk=mk, mv=mv):
            return cross_attention_out(q, mk.reshape(bp, MEM_LEN, D_MODEL), mv.reshape(bp, MEM_LEN, D_MODEL),
                                       w_o, x, rows=512)

        zero_state = dict(s5_re=jnp.zeros((bp, S5_NG, S5_P), F32), s5_im=jnp.zeros((bp, S5_NG, S5_P), F32),
                          shift=jnp.zeros((bp, RW_PROJ), F32), hgrn=mat_zero, ret=mat_zero, rwkv=mat_zero, layer=0)
        yp, small, p_mats = _trunk_layer(yp, bp, lp, 0, attend_p, zero_state, p_mats, p, wb, l, lb_all[l], final)
        p_small.append(small)
        p_mk.append(mk.reshape(bp, MEM_LEN, MEM_HEADS, MEM_HD))
        p_mv.append(mv.reshape(bp, MEM_LEN, MEM_HEADS, MEM_HD))
        sst = dict(s5_re=state_s5_re[l], s5_im=state_s5_im[l], shift=state_rwkv_shift[l],
                   hgrn=state_hgrn, ret=state_ret, rwkv=state_rwkv, layer=l)
        def attend_s(q, w_o, x, l=l):
            o = cross_attention_cache(q, cache_mem_k, cache_mem_v, l, nb=4, rows=ls)
            return matmul_residual(o, w_o, x, tm=512)

        ys, small, s_mats = _trunk_layer(ys, bs, ls, PAST_LEN, attend_s, sst, s_mats, p, wb, l, lb_all[l], final)
        s_small.append(small)
    stack = lambda states, i: jnp.stack([s[i] for s in states])
    return (yp.reshape(bp, lp, D_MODEL), ys.reshape(bs, ls, D_MODEL),
            stack(p_small, 0), stack(p_small, 1), p_mats['hgrn'], p_mats['ret'], p_mats['rwkv'], stack(p_small, 2),
            jnp.stack(p_mk), jnp.stack(p_mv),
            stack(s_small, 0), stack(s_small, 1), s_mats['hgrn'], s_mats['ret'], s_mats['rwkv'], stack(s_small, 2))
```

```python
import functools
import math

import jax
import jax.numpy as jnp
from jax import lax
from jax.experimental import pallas as pl
from jax.experimental.pallas import tpu as pltpu

F32 = jnp.float32
BF16 = jnp.bfloat16

D_MODEL = 1024
DEPTH = 2
PAST_LEN = 16384
GW = 256
HD = 64
NH = GW // HD
S5_GROUP = 16
S5_NG = GW // S5_GROUP
S5_P = 64
S5_W = S5_NG * S5_P
RW_PROJ = 4 * GW
IN_WIDTH = 13 * GW
MEM_LEN = 256
MEM_HEADS = 4
MEM_HD = D_MODEL // MEM_HEADS
D_FF = 4 * D_MODEL
EPS = 1e-6
RW_LN_EPS = 64e-5
ROPE_BASE = 10000.0
CHUNK = 16
RET_CHUNK = 64

VMEM_LIMIT = 56 * 1024 * 1024
SUBLANES = 8
LANES = 128
LOG2_E = 1.4426950408889634


def _cparams(sem):
    return pltpu.CompilerParams(dimension_semantics=sem, vmem_limit_bytes=VMEM_LIMIT)


def _mm(a, b):
    return jnp.dot(a.astype(BF16), b.astype(BF16), preferred_element_type=F32)


def _mm_nt(a, b):
    return lax.dot_general(a.astype(BF16), b.astype(BF16), (((1,), (1,)), ((), ())),
                           preferred_element_type=F32)


def _mm_tn(a, b):
    return lax.dot_general(a.astype(BF16), b.astype(BF16), (((0,), (0,)), ((), ())),
                           preferred_element_type=F32)


def _split3(x):
    hi = x.astype(BF16)
    r1 = x - hi.astype(F32)
    mid = r1.astype(BF16)
    lo = (r1 - mid.astype(F32)).astype(BF16)
    return hi, mid, lo


def _mm_exact_lhs(sel, x):
    s = sel.astype(BF16)
    hi, mid, lo = _split3(x)
    return (jnp.dot(s, hi, preferred_element_type=F32) + jnp.dot(s, mid, preferred_element_type=F32)
            + jnp.dot(s, lo, preferred_element_type=F32))


def _rms(x, gain):
    return x * lax.rsqrt(jnp.mean(x * x, axis=-1, keepdims=True) + EPS) * gain


def _sigmoid(x):
    return 1.0 / (1.0 + jnp.exp(-x))


def _iota2(shape, axis):
    return lax.broadcasted_iota(jnp.int32, shape, axis)


def _head_ones():
    return (_iota2((GW, GW), 0) // HD == _iota2((GW, GW), 1) // HD).astype(F32)


def _eye(n):
    return (_iota2((n, n), 0) == _iota2((n, n), 1)).astype(F32)


def _head_sum(x, ones_bd):
    s = ones_bd.astype(BF16)
    hi = x.astype(BF16)
    lo = (x - hi.astype(F32)).astype(BF16)
    return jnp.dot(hi, s, preferred_element_type=F32) + jnp.dot(lo, s, preferred_element_type=F32)


def _norm_mm_kernel(x_ref, g_ref, w_ref, o_ref, xn_ref):
    @pl.when(pl.program_id(1) == 0)
    def _():
        xn_ref[...] = _rms(x_ref[...], g_ref[...]).astype(BF16)

    o_ref[...] = jnp.dot(xn_ref[...], w_ref[...], preferred_element_type=F32)


def norm_matmul(x, gain, w, *, tm, tn):
    t, d = x.shape
    n = w.shape[1]
    return pl.pallas_call(
        _norm_mm_kernel,
        out_shape=jax.ShapeDtypeStruct((t, n), F32),
        grid=(t // tm, n // tn),
        in_specs=[pl.BlockSpec((tm, d), lambda i, j: (i, 0)),
                  pl.BlockSpec((1, d), lambda i, j: (0, 0)),
                  pl.BlockSpec((d, tn), lambda i, j: (0, j))],
        out_specs=pl.BlockSpec((tm, tn), lambda i, j: (i, j)),
        scratch_shapes=[pltpu.VMEM((tm, d), BF16)],
        compiler_params=_cparams(("parallel", "arbitrary")),
        name="norm_matmul",
    )(x, gain, w)


def _mm_kernel(a_ref, w_ref, o_ref):
    o_ref[...] = _mm(a_ref[...], w_ref[...])


def matmul_layers(a, w, *, tm):
    t, k = a.shape
    layers, _, n = w.shape
    return pl.pallas_call(
        _mm_kernel,
        out_shape=jax.ShapeDtypeStruct((layers, t, n), F32),
        grid=(layers, t // tm),
        in_specs=[pl.BlockSpec((tm, k), lambda l, i: (i, 0)),
                  pl.BlockSpec((None, k, n), lambda l, i: (l, 0, 0))],
        out_specs=pl.BlockSpec((None, tm, n), lambda l, i: (l, i, 0)),
        compiler_params=_cparams(("parallel", "parallel")),
        name="matmul_layers",
    )(a, w)


def _mm_res_kernel(a_ref, w_ref, r_ref, o_ref):
    o_ref[...] = r_ref[...] + _mm(a_ref[...], w_ref[...])


def matmul_residual(a, w, res, *, tm):
    t, k = a.shape
    n = w.shape[1]
    return pl.pallas_call(
        _mm_res_kernel,
        out_shape=jax.ShapeDtypeStruct((t, n), F32),
        grid=(t // tm,),
        in_specs=[pl.BlockSpec((tm, k), lambda i: (i, 0)),
                  pl.BlockSpec((k, n), lambda i: (0, 0)),
                  pl.BlockSpec((tm, n), lambda i: (i, 0))],
        out_specs=pl.BlockSpec((tm, n), lambda i: (i, 0)),
        compiler_params=_cparams(("parallel",)),
        name="matmul_residual",
    )(a, w, res)


def _mix_out_q_kernel(a0_ref, a1_ref, a2_ref, a3_ref, w_ref, r_ref, g_ref, wq_ref, x_ref, q_ref):
    a = jnp.concatenate([a0_ref[...], a1_ref[...], a2_ref[...], a3_ref[...]], axis=-1)
    x = r_ref[...] + _mm(a, w_ref[...])
    x_ref[...] = x
    q_ref[...] = jnp.dot(_rms(x, g_ref[...]).astype(BF16), wq_ref[...], preferred_element_type=F32).astype(BF16)


def mix_out_q(parts, w, res, gain, w_q, *, tm):
    t, d = res.shape
    part = pl.BlockSpec((tm, GW), lambda i: (i, 0))
    mat = pl.BlockSpec((d, d), lambda i: (0, 0))
    tile = pl.BlockSpec((tm, d), lambda i: (i, 0))
    return pl.pallas_call(
        _mix_out_q_kernel,
        out_shape=(jax.ShapeDtypeStruct((t, d), F32), jax.ShapeDtypeStruct((t, d), BF16)),
        grid=(t // tm,),
        in_specs=[part, part, part, part, mat, tile, pl.BlockSpec((1, d), lambda i: (0, 0)), mat],
        out_specs=(tile, tile),
        compiler_params=_cparams(("parallel",)),
        name="mix_out_q",
    )(*parts, w, res, gain, w_q)


def _ffn_kernel(x_ref, g_ref, wu_ref, wd_ref, gf_ref, o_ref, *, final_norm):
    x = x_ref[...]
    h = jnp.dot(_rms(x, g_ref[...]).astype(BF16), wu_ref[...], preferred_element_type=F32)
    h = jnp.square(jnp.maximum(h, 0.0))
    y = x + jnp.dot(h.astype(BF16), wd_ref[...], preferred_element_type=F32)
    if final_norm:
        y = _rms(y, gf_ref[...])
    o_ref[...] = y


def ffn(x, gain, w_up, w_down, gain_final, *, tm, final_norm):
    t, d = x.shape
    ff = w_up.shape[1]
    resident = lambda shape: pl.BlockSpec(shape, lambda i: (0, 0), pipeline_mode=pl.Buffered(1))
    return pl.pallas_call(
        functools.partial(_ffn_kernel, final_norm=final_norm),
        out_shape=jax.ShapeDtypeStruct((t, d), F32),
        grid=(t // tm,),
        in_specs=[pl.BlockSpec((tm, d), lambda i: (i, 0)),
                  pl.BlockSpec((1, d), lambda i: (0, 0)),
                  resident((d, ff)), resident((ff, d)),
                  pl.BlockSpec((1, d), lambda i: (0, 0))],
        out_specs=pl.BlockSpec((tm, d), lambda i: (i, 0)),
        compiler_params=_cparams(("parallel",)),
        name="ffn",
    )(x, gain, w_up, w_down, gain_final)


def _attn_out_kernel(q_ref, k_ref, v_ref, wo_ref, r_ref, x_ref):
    heads = []
    for h in range(MEM_HEADS):
        sl = slice(h * MEM_HD, (h + 1) * MEM_HD)
        s = _mm_nt(q_ref[:, sl], k_ref[:, sl]) * (MEM_HD ** -0.5)
        p = jnp.exp(s - jnp.max(s, axis=-1, keepdims=True))
        p = p * (1.0 / jnp.sum(p, axis=-1, keepdims=True))
        heads.append(_mm(p, v_ref[:, sl]))
    x_ref[...] = r_ref[...] + _mm(jnp.concatenate(heads, axis=-1), wo_ref[...])


def cross_attention_out(q, mem_k, mem_v, layer, w_o, res, *, rows):
    t, d = res.shape
    bsz = mem_k.shape[1] // MEM_LEN
    lt = t // bsz // rows
    tile = pl.BlockSpec((rows, d), lambda b, l: (b * lt + l, 0))
    mem = pl.BlockSpec((None, MEM_LEN, d), lambda b, l: (layer, b, 0))
    return pl.pallas_call(
        _attn_out_kernel,
        out_shape=jax.ShapeDtypeStruct((t, d), F32),
        grid=(bsz, lt),
        in_specs=[tile, mem, mem, pl.BlockSpec((d, d), lambda b, l: (0, 0)), tile],
        out_specs=tile,
        compiler_params=_cparams(("parallel", "arbitrary")),
        name="cross_attention_out",
    )(q, mem_k, mem_v, w_o, res)


def _attn_cache_kernel(q_ref, k_ref, v_ref, o_ref, *, nb, rows):
    nr = MEM_HEADS * rows
    same_head = _iota2((nr, MEM_HEADS * MEM_LEN), 1) % MEM_HEADS == _iota2((nr, MEM_HEADS * MEM_LEN), 0) // rows
    q_all = q_ref[...].astype(F32)
    for b in range(nb):
        q = q_all[b * rows:(b + 1) * rows, :]
        qs = jnp.concatenate([q[:, h * MEM_HD:(h + 1) * MEM_HD] for h in range(MEM_HEADS)], axis=0)
        s = _mm_nt(qs, k_ref[0, b].reshape(MEM_HEADS * MEM_LEN, MEM_HD)) * (MEM_HD ** -0.5)
        s = jnp.where(same_head, s, -1e30)
        p = jnp.exp(s - jnp.max(s, axis=-1, keepdims=True))
        p = p * (1.0 / jnp.sum(p, axis=-1, keepdims=True))
        o = _mm(p, v_ref[0, b].reshape(MEM_HEADS * MEM_LEN, MEM_HD))
        for h in range(MEM_HEADS):
            o_ref[b * rows:(b + 1) * rows, h * MEM_HD:(h + 1) * MEM_HD] = o[h * rows:(h + 1) * rows]


def cross_attention_cache(q, cache_k, cache_v, layer, *, nb, rows):
    t = q.shape[0]
    bsz = cache_k.shape[1]
    mem = pl.BlockSpec((1, nb, MEM_LEN, MEM_HEADS, MEM_HD), lambda b: (layer, b, 0, 0, 0))
    return pl.pallas_call(
        functools.partial(_attn_cache_kernel, nb=nb, rows=rows),
        out_shape=jax.ShapeDtypeStruct((t, D_MODEL), F32),
        grid=(bsz // nb,),
        in_specs=[pl.BlockSpec((nb * rows, D_MODEL), lambda b: (b, 0)), mem, mem],
        out_specs=pl.BlockSpec((nb * rows, D_MODEL), lambda b: (b, 0)),
        compiler_params=_cparams(("parallel",)),
        name="cross_attention_cache",
    )(q, cache_k, cache_v)


def _gelu_tanh(x):
    return 0.5 * x * (1.0 + jnp.tanh(math.sqrt(2.0 / math.pi) * (x + 0.044715 * (x * x * x))))


def _s5_kernel(u_ref, h0_ref, lam_ref, bblk_ref, cblk_ref, d_ref, wglu_ref, bglu_ref, gain_ref,
               y_ref, hfin_ref, scr_ref, tm_ref, *, steps, nb):
    @pl.when(pl.program_id(1) == 0)
    def _():
        scr_ref[0:nb, :] = h0_ref[...]

    def to_time_major(b, carry):
        for j in range(GW // LANES):
            tm_ref[j, pl.ds(b, steps, stride=nb), :] = u_ref[b, :, j * LANES:(j + 1) * LANES]
        return carry

    lax.fori_loop(0, nb, to_time_major, 0)
    u = jnp.concatenate([tm_ref[j] for j in range(GW // LANES)], axis=1)
    scr_ref[nb:, :] = _mm(u, bblk_ref[...])
    lam_re = lam_ref[0:1, :]
    lam_im = lam_ref[1:2, :]

    def step(t, carry):
        p0 = pl.multiple_of(t * nb, nb)
        c0 = pl.multiple_of((t + 1) * nb, nb)
        h_re = scr_ref[pl.ds(p0, nb), 0:S5_W]
        h_im = scr_ref[pl.ds(p0, nb), S5_W:]
        scr_ref[pl.ds(c0, nb), 0:S5_W] = scr_ref[pl.ds(c0, nb), 0:S5_W] + lam_re * h_re - lam_im * h_im
        scr_ref[pl.ds(c0, nb), S5_W:] = scr_ref[pl.ds(c0, nb), S5_W:] + lam_re * h_im + lam_im * h_re
        return carry

    lax.fori_loop(0, steps, step, 0)
    h_last = scr_ref[steps * nb:, :]
    hfin_ref[...] = h_last
    y = _mm(scr_ref[nb:, :], cblk_ref[...]) + d_ref[...] * u
    y = _gelu_tanh(y)
    y = y * _sigmoid(_mm(y, wglu_ref[...]) + bglu_ref[...])
    y = _rms(y, gain_ref[...])
    for j in range(GW // LANES):
        tm_ref[j] = y[:, j * LANES:(j + 1) * LANES]

    def to_batch_major(b, carry):
        for j in range(GW // LANES):
            y_ref[b, :, j * LANES:(j + 1) * LANES] = tm_ref[j, pl.ds(b, steps, stride=nb), :]
        return carry

    lax.fori_loop(0, nb, to_batch_major, 0)
    scr_ref[0:nb, :] = h_last


def s5_mixer(proj, h0, lam, bblk, cblk, dvec, wglu, bglu, gain, *, steps, nb):
    bsz, seq, _ = proj.shape
    const = lambda shape: pl.BlockSpec(shape, lambda b, t: (0, 0))
    return pl.pallas_call(
        functools.partial(_s5_kernel, steps=steps, nb=nb),
        out_shape=(jax.ShapeDtypeStruct((bsz, seq, GW), F32),
                   jax.ShapeDtypeStruct((bsz, 2 * S5_W), F32)),
        grid=(bsz // nb, seq // steps),
        in_specs=[pl.BlockSpec((nb, steps, GW), lambda b, t: (b, t, 0)),
                  pl.BlockSpec((nb, 2 * S5_W), lambda b, t: (b, 0)),
                  const((2, S5_W)), const((GW, 2 * S5_W)), const((2 * S5_W, GW)), const((1, GW)),
                  const((GW, GW)), const((1, GW)), const((1, GW))],
        out_specs=(pl.BlockSpec((nb, steps, GW), lambda b, t: (b, t, 0)),
                   pl.BlockSpec((nb, 2 * S5_W), lambda b, t: (b, 0))),
        scratch_shapes=[pltpu.VMEM(((steps + 1) * nb, 2 * S5_W), F32),
                        pltpu.VMEM((GW // LANES, steps * nb, LANES), F32)],
        compiler_params=_cparams(("parallel", "arbitrary")),
        name="s5_mixer",
    )(proj, h0, lam, bblk, cblk, dvec, wglu, bglu, gain)


def s5_params(lam_re, lam_im, b_re, b_im, c_re, c_im, log_step):
    step = jnp.exp(log_step)[:, None]
    mag = jnp.exp(lam_re * step)
    lbar_re = mag * jnp.cos(lam_im * step)
    lbar_im = mag * jnp.sin(lam_im * step)
    den = lam_re * lam_re + lam_im * lam_im
    f_re = ((lbar_re - 1.0) * lam_re + lbar_im * lam_im) / den
    f_im = (lbar_im * lam_re - (lbar_re - 1.0) * lam_im) / den
    bbar_re = f_re[..., None] * b_re - f_im[..., None] * b_im
    bbar_im = f_re[..., None] * b_im + f_im[..., None] * b_re
    eye_g = jnp.eye(S5_NG, dtype=F32)

    def in_blk(m):
        return jnp.einsum('gph,gk->ghkp', m, eye_g).reshape(GW, S5_W)

    def out_blk(m):
        return jnp.einsum('ghp,gk->gpkh', m, eye_g).reshape(S5_W, GW)

    bblk = jnp.concatenate([in_blk(bbar_re), in_blk(bbar_im)], axis=1)
    cblk = jnp.concatenate([out_blk(c_re), -out_blk(c_im)], axis=0)
    lam2 = jnp.stack([lbar_re.reshape(S5_W), lbar_im.reshape(S5_W)])
    return lam2, bblk.astype(BF16), cblk.astype(BF16)


def s5_apply(proj, h0_re, h0_im, p, l, *, steps, nb):
    bsz = proj.shape[0]
    lam2, bblk, cblk = s5_params(p['s5_lam_re'][l], p['s5_lam_im'][l], p['s5_b_re'][l], p['s5_b_im'][l],
                                 p['s5_c_re'][l], p['s5_c_im'][l], p['s5_log_step'][l])
    h0 = jnp.concatenate([h0_re.reshape(bsz, S5_W), h0_im.reshape(bsz, S5_W)], axis=1)
    y, h = s5_mixer(proj, h0, lam2, bblk, cblk, p['s5_d'][l].reshape(1, GW),
                    p['s5_w_glu'][l].astype(BF16), p['s5_b_glu'][l].reshape(1, GW),
                    p['s5_norm'][l].reshape(1, GW), steps=steps, nb=nb)
    return y, h[:, :S5_W].reshape(bsz, S5_NG, S5_P), h[:, S5_W:].reshape(bsz, S5_NG, S5_P)


def _load_state(s0_ref, st_ref, nb, transpose):
    for b in range(nb):
        rows = []
        for h in range(NH):
            pieces = [s0_ref[b, h]]
            if h:
                pieces.insert(0, jnp.zeros((HD, h * HD), F32))
            if h < NH - 1:
                pieces.append(jnp.zeros((HD, (NH - 1 - h) * HD), F32))
            rows.append(jnp.concatenate(pieces, axis=1))
        st = jnp.concatenate(rows, axis=0)
        st_ref[b] = st.T if transpose else st


def _store_state(st_ref, sfin_ref, nb, transpose, own_layer):
    if own_layer is not None:
        for other in range(DEPTH):
            if other != own_layer:
                sfin_ref[other] = jnp.zeros(sfin_ref.shape[1:], F32)
        sfin_ref = sfin_ref.at[own_layer]
    for b in range(nb):
        st = st_ref[b].T if transpose else st_ref[b]
        for h in range(NH):
            sfin_ref[b, h] = st[h * HD:(h + 1) * HD, h * HD:(h + 1) * HD]


def _head_masks():
    lane_head = _iota2((1, GW), 1) // HD
    return [(lane_head == h).astype(F32) for h in range(NH)]


def _stack_heads(x, masks):
    return jnp.concatenate([x * m for m in masks], axis=0)


def _pad_rows(x, rows):
    if x.shape[0] == rows:
        return x
    return jnp.concatenate([x, jnp.zeros((rows - x.shape[0], x.shape[1]), x.dtype)], axis=0)


def _silu(x):
    return x * _sigmoid(x)


def _chunk_rows(ref, b, r0, rows, c):
    return _pad_rows(ref[b, pl.ds(r0, rows), :], c)


def _ret_kernel(q_ref, k_ref, v_ref, g_ref, cos_ref, sin_ref, s0_ref, _states_in, o_ref, sfin_ref, st_ref,
                *, nb, tb, c, c_real, own_layer):
    rows = min(tb, c)

    @pl.when(pl.program_id(1) == 0)
    def _():
        _load_state(s0_ref, st_ref, nb, transpose=True)

    masks = _head_masks()
    ones_bd = _head_ones()
    lane_head = _iota2((1, GW), 1) // HD
    log_gamma = jnp.zeros((1, GW), F32)
    for h in range(NH):
        log_gamma = jnp.where(lane_head == h, math.log(1.0 - 2.0 ** (-5.0 - h)), log_gamma)
    tt = _iota2((c, GW), 0).astype(F32)
    scale = HD ** -0.5
    g_q = jnp.exp(log_gamma * (tt + 1.0))
    g_k = jnp.exp(-log_gamma * (tt + 1.0)) * scale
    g_tail = jnp.exp(log_gamma * (c_real - 1.0 - tt)) * scale
    g_chunk = jnp.exp(log_gamma * float(c_real))
    causal = _iota2((c, NH * c), 0) >= _iota2((c, NH * c), 1) % c
    first_half = _iota2((c, GW), 1) % HD < HD // 2

    def rope(x, cos, sin):
        swapped = jnp.where(first_half, pltpu.roll(x, GW - HD // 2, axis=1), pltpu.roll(x, HD // 2, axis=1))
        return x * cos + swapped * sin

    def chunk(ci, carry):
        r0 = pl.multiple_of(ci * rows, rows)
        cos = _pad_rows(cos_ref[pl.ds(r0, rows), :], c)
        sin = _pad_rows(sin_ref[pl.ds(r0, rows), :], c)
        rng = range(nb)
        qt = [rope(_chunk_rows(q_ref, b, r0, rows, c), cos, sin) * g_q for b in rng]
        k = [rope(_chunk_rows(k_ref, b, r0, rows, c), cos, sin) for b in rng]
        v = [_chunk_rows(v_ref, b, r0, rows, c) for b in rng]
        st = [st_ref[b] for b in rng]
        sc = [_mm_nt(qt[b], _stack_heads(k[b] * g_k, masks)) for b in rng]
        o_in = [_mm(jnp.where(causal, sc[b], 0.0), _stack_heads(v[b], masks)) for b in rng]
        o_st = [_mm_nt(qt[b], st[b]) for b in rng]
        d_st = [_mm_tn(v[b], k[b] * g_tail) for b in rng]
        o = [o_in[b] + o_st[b] for b in rng]
        ms = [_head_sum(o[b] * o[b], ones_bd) for b in rng]
        for b in rng:
            st_ref[b] = st[b] * g_chunk + ones_bd * d_st[b]
            ob = o[b] * lax.rsqrt(ms[b] * (1.0 / HD) + EPS) * _silu(_chunk_rows(g_ref, b, r0, rows, c))
            o_ref[b, pl.ds(r0, rows), :] = ob[:rows]
        return carry

    lax.fori_loop(0, tb // rows, chunk, 0)

    @pl.when(pl.program_id(1) == pl.num_programs(1) - 1)
    def _():
        _store_state(st_ref, sfin_ref, nb, transpose=True, own_layer=own_layer)


def _rope_tables(pos0, seq):
    half = HD // 2
    inv = ROPE_BASE ** (-jnp.arange(half, dtype=F32) / half)
    pos = pos0 + jnp.arange(seq, dtype=jnp.int32)
    ang = pos.astype(F32)[:, None] * inv[None, :]
    cos, sin = jnp.cos(ang), jnp.sin(ang)
    return (jnp.tile(jnp.concatenate([cos, cos], axis=-1), (1, NH)),
            jnp.tile(jnp.concatenate([-sin, sin], axis=-1), (1, NH)))


def _proj_spec(nb, tb, col):
    return pl.BlockSpec((nb, tb, GW), lambda b, t: (b, t, col))


def _state_spec(nb, layer):
    return pl.BlockSpec((None, nb, NH, HD, HD), lambda b, t: (layer, b, 0, 0, 0))


_ALIASED = pl.BlockSpec(memory_space=pl.ANY)


def _collector(states, bsz, nb, layer, operand_index):
    shape = jax.ShapeDtypeStruct((DEPTH, bsz, NH, HD, HD), F32)
    if states is None:
        spec = pl.BlockSpec((DEPTH, nb, NH, HD, HD), lambda b, t: (0, b, 0, 0, 0))
        return jnp.zeros((1, 1, NH, HD, HD), F32), spec, shape, {}, layer
    return states, _state_spec(nb, layer), shape, {operand_index: 1}, None


def retention_apply(proj, s0, s0_layer, states, layer, pos0, *, tb, nb):
    bsz, seq, _ = proj.shape
    c = RET_CHUNK if tb >= RET_CHUNK else CHUNK
    cos, sin = _rope_tables(pos0, seq)
    states, st_spec, st_shape, aliases, own_layer = _collector(states, bsz, nb, layer, 7)
    return pl.pallas_call(
        functools.partial(_ret_kernel, nb=nb, tb=tb, c=c, c_real=min(tb, c), own_layer=own_layer),
        out_shape=(jax.ShapeDtypeStruct((bsz, seq, GW), F32), st_shape),
        grid=(bsz // nb, seq // tb),
        in_specs=[_proj_spec(nb, tb, 5), _proj_spec(nb, tb, 6), _proj_spec(nb, tb, 7), _proj_spec(nb, tb, 8),
                  pl.BlockSpec((tb, GW), lambda b, t: (t, 0)), pl.BlockSpec((tb, GW), lambda b, t: (t, 0)),
                  _state_spec(nb, s0_layer), _ALIASED],
        out_specs=(pl.BlockSpec((nb, tb, GW), lambda b, t: (b, t, 0)), st_spec),
        scratch_shapes=[pltpu.VMEM((nb, GW, GW), F32)],
        input_output_aliases=aliases,
        compiler_params=_cparams(("parallel", "arbitrary")),
        name="retention",
    )(proj, proj, proj, proj, cos, sin, s0, states)


def _log_sigmoid(z):
    return jnp.minimum(z, 0.0) - jnp.log(1.0 + jnp.exp(-jnp.abs(z)))


def _hgrn_kernel(q_ref, f_ref, i_ref, g_ref, lb_ref, gain_ref, s0_ref, _states_in, o_ref, sfin_ref, st_ref,
                 *, nb, tb, c, own_layer):
    rows = min(tb, c)

    @pl.when(pl.program_id(1) == 0)
    def _():
        _load_state(s0_ref, st_ref, nb, transpose=True)

    t_hi = -(-rows // SUBLANES) * SUBLANES
    ones_bd = _head_ones()
    tri = (_iota2((c, c), 0) >= _iota2((c, c), 1)).astype(F32)
    t_idx = _iota2((c, GW), 0)
    lb = lb_ref[...]
    log_lb = jnp.log(lb)
    log_1m_lb = jnp.log(1.0 - lb)
    gain = gain_ref[...]

    def chunk(ci, carry):
        r0 = pl.multiple_of(ci * rows, rows)
        rng = range(nb)
        valid = t_idx < rows

        def gates(b):
            z = _chunk_rows(f_ref, b, r0, rows, c)
            ls_pos = _log_sigmoid(z)
            ls_neg = ls_pos - z
            b2 = log_lb + ls_neg
            log_f = jnp.maximum(ls_pos, b2) + jnp.log(1.0 + jnp.exp(-jnp.abs(ls_pos - b2)))
            return (jnp.where(valid, log_f, 0.0), jnp.where(valid, (1.0 - lb) * jnp.exp(ls_neg), 0.0),
                    ls_neg + log_1m_lb)

        log_f, key, log_key = zip(*[gates(b) for b in rng])
        q = [_silu(_chunk_rows(q_ref, b, r0, rows, c)) for b in rng]
        v = [_chunk_rows(i_ref, b, r0, rows, c) for b in rng]
        st = [st_ref[b] for b in rng]
        cum = [_mm_exact_lhs(tri, log_f[b]) for b in rng]
        last = [cum[b][c - 1:c, :] for b in rng]
        cum2 = [cum[b] * LOG2_E for b in rng]
        kd2 = [(log_key[b] - cum[b]) * LOG2_E for b in rng]

        def pair_rows(b, s):
            lo = SUBLANES * (s // SUBLANES)
            t_rows = _iota2((t_hi - lo, GW), 0) + lo
            key_decay = jnp.where(t_rows >= s, jnp.exp2(cum2[b][lo:t_hi] + kd2[b][s:s + 1, :]), 0.0)
            return key_decay * q[b][lo:t_hi]

        sc = [_mm(jnp.concatenate([pair_rows(b, s) for s in range(rows)], axis=0), ones_bd) for b in rng]
        o_st = [_mm_nt(q[b] * jnp.exp(cum[b]), st[b]) for b in rng]
        d_st = [_mm_tn(v[b], key[b] * jnp.exp(last[b] - cum[b])) for b in rng]

        def intra(b):
            tiles = [None] * (c // SUBLANES)
            off = 0
            for s in range(rows):
                for ti in range(s // SUBLANES, t_hi // SUBLANES):
                    term = sc[b][off:off + SUBLANES, :] * v[b][s:s + 1, :]
                    tiles[ti] = term if tiles[ti] is None else tiles[ti] + term
                    off += SUBLANES
            zero = jnp.zeros((SUBLANES, GW), F32)
            return jnp.concatenate([zero if t is None else t for t in tiles], axis=0)

        o = [o_st[b] + intra(b) for b in rng]
        ms = [_head_sum(o[b] * o[b], ones_bd) for b in rng]
        for b in rng:
            st_ref[b] = st[b] * jnp.exp(last[b]) + ones_bd * d_st[b]
            ob = o[b] * lax.rsqrt(ms[b] * (1.0 / HD) + EPS) * gain * _silu(_chunk_rows(g_ref, b, r0, rows, c))
            o_ref[b, pl.ds(r0, rows), :] = ob[:rows]
        return carry

    lax.fori_loop(0, tb // rows, chunk, 0)

    @pl.when(pl.program_id(1) == pl.num_programs(1) - 1)
    def _():
        _store_state(st_ref, sfin_ref, nb, transpose=True, own_layer=own_layer)


def hgrn_apply(proj, s0, s0_layer, states, layer, lb, gain, *, tb, nb):
    bsz, seq, _ = proj.shape
    row = pl.BlockSpec((1, GW), lambda b, t: (0, 0))
    states, st_spec, st_shape, aliases, own_layer = _collector(states, bsz, nb, layer, 7)
    return pl.pallas_call(
        functools.partial(_hgrn_kernel, nb=nb, tb=tb, c=CHUNK, own_layer=own_layer),
        out_shape=(jax.ShapeDtypeStruct((bsz, seq, GW), F32), st_shape),
        grid=(bsz // nb, seq // tb),
        in_specs=[_proj_spec(nb, tb, 1), _proj_spec(nb, tb, 2), _proj_spec(nb, tb, 3), _proj_spec(nb, tb, 4),
                  row, row, _state_spec(nb, s0_layer), _ALIASED],
        out_specs=(pl.BlockSpec((nb, tb, GW), lambda b, t: (b, t, 0)), st_spec),
        scratch_shapes=[pltpu.VMEM((nb, GW, GW), F32)],
        input_output_aliases=aliases,
        compiler_params=_cparams(("parallel", "arbitrary")),
        name="hgrn2",
    )(proj, proj, proj, proj, lb.reshape(1, GW), gain.reshape(1, GW), s0, states)


def _softplus(z):
    return jnp.maximum(z, 0.0) + jnp.log(1.0 + jnp.exp(-jnp.abs(z)))


def _rwkv_prepare(ins, consts):
    masks, ones_bd, bd_mask, eye_flat, tri, strict, incl = consts
    rng = range(len(ins))
    r, lw, k, v, kk, a = [[i[j] for i in ins] for j in range(6)]
    c = r[0].shape[0]
    n = NH * c

    def to_bd(flat):
        return jnp.concatenate([flat] * NH, axis=0) * bd_mask

    cum = [_mm_exact_lhs(tri, lw[i]) for i in rng]
    last = [cum[i][c - 1:c, :] for i in rng]
    p_inv = [jnp.exp(-cum[i]) for i in rng]
    p_tail = [jnp.exp(last[i] - cum[i]) for i in rng]
    ka = [kk[i] * a[i] for i in rng]
    x = [jnp.concatenate([kk[i] * jnp.exp(cum[i] - lw[i]), r[i] * jnp.exp(cum[i])], axis=0) for i in rng]
    g = [_mm_nt(x[i], jnp.concatenate([_stack_heads(ka[i] * p_inv[i], masks),
                                       _stack_heads(k[i] * p_inv[i], masks)], axis=0)) for i in rng]
    m_ak = [jnp.where(strict, g[i][:c], 0.0) for i in rng]
    n_ak = [jnp.where(incl, g[i][c:], 0.0) for i in rng]
    pw = [m_ak[i][:, :n] for i in rng]
    t_inv = [eye_flat - pw[i] for i in rng]
    pw_bd = [to_bd(pw[i]) for i in rng]
    for _ in range(int(math.log2(c)) - 1):
        pw = [_mm(pw[i], pw_bd[i]) for i in rng]
        pw_bd = [to_bd(pw[i]) for i in rng]
        t_inv = [t_inv[i] + _mm(t_inv[i], pw_bd[i]) for i in rng]
    v_stk = [_stack_heads(v[i], masks) for i in rng]
    zeros = jnp.zeros((n, GW), F32)
    mkv = [_mm(m_ak[i], jnp.concatenate([zeros, v_stk[i]], axis=0)) for i in rng]
    tmkv = [_mm(t_inv[i], _stack_heads(mkv[i], masks)) for i in rng]
    return [dict(x=x[i], t_inv=t_inv[i], tmkv=tmkv[i], n_ak=n_ak[i], v=v[i], v_stk=v_stk[i],
                 decay=jnp.exp(last[i]),
                 k_tail=jnp.concatenate([ka[i] * p_tail[i], k[i] * p_tail[i]], axis=0)) for i in rng]


def _rwkv_advance(prep, st, consts):
    masks, ones_bd = consts[0], consts[1]
    rng = range(len(prep))
    c = prep[0]['t_inv'].shape[0]
    xa = [_mm_nt(prep[i]['x'], st[i]) for i in rng]
    u = [-_mm(prep[i]['t_inv'], _stack_heads(xa[i][:c], masks)) - prep[i]['tmkv'] for i in rng]
    nuv = [_mm(prep[i]['n_ak'], jnp.concatenate([_stack_heads(u[i], masks), prep[i]['v_stk']], axis=0))
           for i in rng]
    d_uv = [_mm_tn(jnp.concatenate([u[i], prep[i]['v']], axis=0), prep[i]['k_tail']) for i in rng]
    return [(xa[i][c:] + nuv[i], st[i] * prep[i]['decay'] + ones_bd * d_uv[i]) for i in rng]


def _rwkv_kernel(x_r_ref, x_k_ref, x_v_ref, x_l_ref, sh0_ref, s0_ref, mu_ref, w0_ref, ww_ref, a0_ref,
                 wa_ref, wg_ref, kk_ref, ka_ref, rk_ref, lng_ref, lnb_ref, _states_in,
                 o_ref, sfin_ref, shfin_ref,
                 st_ref, sh_ref, r_s, lw_s, k_s, v_s, kkn_s, a_s, y_s, bonus_s, gate_s, *, nb, tb, c, own_layer):
    tbp = max(tb, c)

    @pl.when(pl.program_id(1) == 0)
    def _():
        _load_state(s0_ref, st_ref, nb, transpose=False)
        sh_ref[...] = sh0_ref[...]

    masks = _head_masks()
    ones_bd = _head_ones()
    n = NH * c
    row_t = _iota2((c, 2 * n), 0)
    col_t = _iota2((c, 2 * n), 1) % c
    bd_mask = (_iota2((n, n), 0) // c == _iota2((n, n), 1) // c).astype(F32)
    eye_flat = (_iota2((c, n), 1) % c == _iota2((c, n), 0)).astype(F32)
    consts = (masks, ones_bd, bd_mask, eye_flat, (_iota2((c, c), 0) >= _iota2((c, c), 1)).astype(F32),
              col_t < row_t, col_t <= row_t)
    first_row = _iota2((tb, GW), 0) == 0

    for b in range(nb):
        def mixed(x_ref, j):
            x = x_ref[b]
            prev = jnp.where(first_row, sh_ref[b, j:j + 1, :], pltpu.roll(x, 1, axis=0))
            sh_ref[b, j:j + 1, :] = x[tb - 1:tb, :]
            return x + (prev - x) * mu_ref[j:j + 1, :]

        r = mixed(x_r_ref, 0)
        k = mixed(x_k_ref, 1)
        v = mixed(x_v_ref, 2)
        xl = mixed(x_l_ref, 3)
        log_w = -_softplus(-(w0_ref[...] + _mm(jnp.tanh(xl), ww_ref[...]))) - 0.5
        a = _sigmoid(a0_ref[...] + _mm(xl, wa_ref[...]))
        gate = _mm(_sigmoid(xl), wg_ref[...])
        kk = k * kk_ref[...]
        kk = kk * lax.rsqrt(jnp.maximum(_head_sum(kk * kk, ones_bd), 1e-24))
        k = k * (1.0 + (a - 1.0) * ka_ref[...])
        r_s[b, 0:tb, :] = r
        lw_s[b, 0:tb, :] = -jnp.exp(log_w)
        k_s[b, 0:tb, :] = k
        v_s[b, 0:tb, :] = v
        kkn_s[b, 0:tb, :] = kk
        a_s[b, 0:tb, :] = a
        bonus_s[b] = _head_sum(r * k * rk_ref[...], ones_bd) * v
        gate_s[b] = gate
        if tbp > tb:
            zeros = jnp.zeros((tbp - tb, GW), F32)
            for s in (r_s, lw_s, k_s, v_s, kkn_s, a_s):
                s[b, tb:tbp, :] = zeros

    n_chunks = tbp // c
    group = 4 if n_chunks % 4 == 0 else 1

    def chunks(gi, carry):
        r0 = [pl.multiple_of((gi * group + j) * c, c) for j in range(group)]
        prep = _rwkv_prepare([tuple(s[b, pl.ds(r0[j], c), :] for s in (r_s, lw_s, k_s, v_s, kkn_s, a_s))
                              for j in range(group) for b in range(nb)], consts)
        st = [st_ref[b] for b in range(nb)]
        for j in range(group):
            outs = _rwkv_advance(prep[j * nb:(j + 1) * nb], st, consts)
            st = [o[1] for o in outs]
            for b in range(nb):
                y_s[b, pl.ds(r0[j], c), :] = outs[b][0]
        for b in range(nb):
            st_ref[b] = st[b]
        return carry

    lax.fori_loop(0, n_chunks // group, chunks, 0)

    for b in range(nb):
        y = y_s[b, 0:tb, :]
        mean = _head_sum(y, ones_bd) * (1.0 / HD)
        d = y - mean
        var = _head_sum(d * d, ones_bd) * (1.0 / HD)
        y = d * lax.rsqrt(var + RW_LN_EPS) * lng_ref[...] + lnb_ref[...]
        o_ref[b] = (y + bonus_s[b]) * gate_s[b]
    shfin_ref[...] = sh_ref[...]

    @pl.when(pl.program_id(1) == pl.num_programs(1) - 1)
    def _():
        _store_state(st_ref, sfin_ref, nb, transpose=False, own_layer=own_layer)


def rwkv_apply(proj, s0, s0_layer, states, shift0, p, l, *, tb, nb):
    bsz, seq, _ = proj.shape
    states, st_spec, st_shape, aliases, own_layer = _collector(states, bsz, nb, l, 17)
    c = CHUNK
    tbp = max(tb, c)
    row = pl.BlockSpec((1, GW), lambda b, t: (0, 0))
    mat = pl.BlockSpec((GW, GW), lambda b, t: (0, 0))
    sh_spec = pl.BlockSpec((nb, 4, GW), lambda b, t: (b, 0, 0))
    zeros = lambda r: jnp.zeros((r, GW), F32)
    ww = jnp.concatenate([p['rw_w_w2'][l], zeros(192)], axis=0).astype(BF16)
    wa = jnp.concatenate([zeros(64), p['rw_w_a2'][l], zeros(128)], axis=0).astype(BF16)
    wg = jnp.concatenate([zeros(128), p['rw_w_g2'][l]], axis=0).astype(BF16)
    r1 = lambda name: p[name][l].reshape(1, GW)
    seq_buf = pltpu.VMEM((nb, tbp, GW), F32)
    blk_buf = pltpu.VMEM((nb, tb, GW), F32)
    y, st, sh = pl.pallas_call(
        functools.partial(_rwkv_kernel, nb=nb, tb=tb, c=c, own_layer=own_layer),
        out_shape=(jax.ShapeDtypeStruct((bsz, seq, GW), F32), st_shape,
                   jax.ShapeDtypeStruct((bsz, 4, GW), F32)),
        grid=(bsz // nb, seq // tb),
        in_specs=[_proj_spec(nb, tb, 9), _proj_spec(nb, tb, 10), _proj_spec(nb, tb, 11), _proj_spec(nb, tb, 12),
                  sh_spec, _state_spec(nb, s0_layer), pl.BlockSpec((4, GW), lambda b, t: (0, 0)),
                  row, mat, row, mat, mat, row, row, row, row, row, _ALIASED],
        out_specs=(pl.BlockSpec((nb, tb, GW), lambda b, t: (b, t, 0)), st_spec, sh_spec),
        scratch_shapes=[pltpu.VMEM((nb, GW, GW), F32), pltpu.VMEM((nb, 4, GW), F32),
                        seq_buf, seq_buf, seq_buf, seq_buf, seq_buf, seq_buf, seq_buf, blk_buf, blk_buf],
        input_output_aliases=aliases,
        compiler_params=_cparams(("parallel", "arbitrary")),
        name="rwkv7",
    )(proj, proj, proj, proj, shift0.reshape(bsz, 4, GW), s0, p['rw_mu'][l].reshape(4, GW),
      r1('rw_w0'), ww, r1('rw_a0'), wa, wg, r1('rw_k_k'), r1('rw_k_a'), r1('rw_r_k'), r1('rw_ln_g'), r1('rw_ln_b'),
      states)
    return y, st, sh.reshape(bsz, RW_PROJ)


def _tiles(bsz, seq):
    if seq >= 128:
        return dict(tm=512, s5_steps=128, s5_nb=bsz, mix_tb=128, mix_nb=bsz, rw_nb=bsz)
    return dict(tm=512, s5_steps=seq, s5_nb=bsz, mix_tb=seq, mix_nb=8, rw_nb=16)


def _trunk_layer(x, bsz, seq, pos0, attend, st, mats, p, wb, l, lb, final_norm):
    cfg = _tiles(bsz, seq)
    tm = cfg['tm']
    row = lambda name: p[name][l].reshape(1, -1)
    proj = norm_matmul(x, row('norm_mix'), wb['w_in'][l], tm=tm, tn=IN_WIDTH)
    proj = proj.reshape(bsz, seq, IN_WIDTH)
    y_s5, s5_re, s5_im = s5_apply(proj, st['s5_re'], st['s5_im'], p, l,
                                  steps=cfg['s5_steps'], nb=cfg['s5_nb'])
    y_hg, hg_s = hgrn_apply(proj, st['hgrn'], st['layer'], mats['hgrn'], l, lb, p['hg_norm'][l],
                            tb=cfg['mix_tb'], nb=cfg['mix_nb'])
    y_rt, rt_s = retention_apply(proj, st['ret'], st['layer'], mats['ret'], l, pos0,
                                 tb=cfg['mix_tb'], nb=cfg['mix_nb'])
    y_rw, rw_s, shift = rwkv_apply(proj, st['rwkv'], st['layer'], mats['rwkv'], st['shift'], p, l,
                                   tb=cfg['mix_tb'], nb=cfg['rw_nb'])
    parts = [y.reshape(bsz * seq, GW) for y in (y_s5, y_hg, y_rt, y_rw)]
    x, q = mix_out_q(parts, wb['w_out'][l], x, row('norm_mem'), wb['mem_w_q'][l], tm=tm)
    x = attend(q, wb['mem_w_o'][l], x)
    x = ffn(x, row('norm_ffn'), wb['ffn_w_up'][l], wb['ffn_w_down'][l], p['norm_final'].reshape(1, -1),
            tm=512, final_norm=final_norm)
    return x, (s5_re, s5_im, shift), dict(hgrn=hg_s, ret=rt_s, rwkv=rw_s)


def kernel(x_prompt, x_sample, mem_prompt, state_s5_re, state_s5_im, state_hgrn, state_ret, state_rwkv,
           state_rwkv_shift, cache_mem_k, cache_mem_v, norm_mix, w_in, w_out, s5_lam_re, s5_lam_im,
           s5_b_re, s5_b_im, s5_c_re, s5_c_im, s5_d, s5_log_step, s5_w_glu, s5_b_glu, s5_norm,
           hg_lb_logits, hg_norm, rw_mu, rw_w0, rw_w_w2, rw_a0, rw_w_a2, rw_w_g2, rw_k_k, rw_k_a, rw_r_k,
           rw_ln_g, rw_ln_b, norm_mem, mem_w_q, mem_w_k, mem_w_v, mem_w_o, norm_ffn, ffn_w_up, ffn_w_down,
           norm_final):
    p = dict(norm_mix=norm_mix, s5_lam_re=s5_lam_re, s5_lam_im=s5_lam_im, s5_b_re=s5_b_re, s5_b_im=s5_b_im,
             s5_c_re=s5_c_re, s5_c_im=s5_c_im, s5_d=s5_d, s5_log_step=s5_log_step, s5_w_glu=s5_w_glu,
             s5_b_glu=s5_b_glu, s5_norm=s5_norm, hg_norm=hg_norm, rw_mu=rw_mu, rw_w0=rw_w0, rw_w_w2=rw_w_w2,
             rw_a0=rw_a0, rw_w_a2=rw_w_a2, rw_w_g2=rw_w_g2, rw_k_k=rw_k_k, rw_k_a=rw_k_a, rw_r_k=rw_r_k,
             rw_ln_g=rw_ln_g, rw_ln_b=rw_ln_b, norm_mem=norm_mem, norm_ffn=norm_ffn, norm_final=norm_final)
    wb = {name: w.astype(BF16) for name, w in dict(
        w_in=w_in, w_out=w_out, mem_w_q=mem_w_q, mem_w_k=mem_w_k, mem_w_v=mem_w_v, mem_w_o=mem_w_o,
        ffn_w_up=ffn_w_up, ffn_w_down=ffn_w_down).items()}
    lb_all = jnp.cumsum(jax.nn.softmax(hg_lb_logits.astype(F32), axis=0), axis=0)
    lb_all = lb_all - lb_all[0:1]

    bp, lp, _ = x_prompt.shape
    bs, ls, _ = x_sample.shape
    yp = x_prompt.reshape(bp * lp, D_MODEL)
    ys = x_sample.reshape(bs * ls, D_MODEL)
    mem2d = mem_prompt.reshape(bp * MEM_LEN, D_MODEL)
    mat_zero = jnp.zeros((1, bp, NH, HD, HD), F32)
    p_small, s_small = [], []
    p_mats = dict(hgrn=None, ret=None, rwkv=None)
    s_mats = dict(hgrn=None, ret=None, rwkv=None)
    mk = matmul_layers(mem2d, wb['mem_w_k'], tm=512)
    mv = matmul_layers(mem2d, wb['mem_w_v'], tm=512)
    for l in range(DEPTH):
        final = l == DEPTH - 1
        def attend_p(q, w_o, x, l=l):
            return cross_attention_out(q, mk, mv, l, w_o, x, rows=512)

        zero_state = dict(s5_re=jnp.zeros((bp, S5_NG, S5_P), F32), s5_im=jnp.zeros((bp, S5_NG, S5_P), F32),
                          shift=jnp.zeros((bp, RW_PROJ), F32), hgrn=mat_zero, ret=mat_zero, rwkv=mat_zero, layer=0)
        yp, small, p_mats = _trunk_layer(yp, bp, lp, 0, attend_p, zero_state, p_mats, p, wb, l, lb_all[l], final)
        p_small.append(small)
        sst = dict(s5_re=state_s5_re[l], s5_im=state_s5_im[l], shift=state_rwkv_shift[l],
                   hgrn=state_hgrn, ret=state_ret, rwkv=state_rwkv, layer=l)
        def attend_s(q, w_o, x, l=l):
            o = cross_attention_cache(q, cache_mem_k, cache_mem_v, l, nb=4, rows=ls)
            return matmul_residual(o, w_o, x, tm=512)

        ys, small, s_mats = _trunk_layer(ys, bs, ls, PAST_LEN, attend_s, sst, s_mats, p, wb, l, lb_all[l], final)
        s_small.append(small)
    stack = lambda states, i: jnp.stack([s[i] for s in states])
    return (yp.reshape(bp, lp, D_MODEL), ys.reshape(bs, ls, D_MODEL),
            stack(p_small, 0), stack(p_small, 1), p_mats['hgrn'], p_mats['ret'], p_mats['rwkv'], stack(p_small, 2),
            mk.reshape(DEPTH, bp, MEM_LEN, MEM_HEADS, MEM_HD), mv.reshape(DEPTH, bp, MEM_LEN, MEM_HEADS, MEM_HD),
            stack(s_small, 0), stack(s_small, 1), s_mats['hgrn'], s_mats['ret'], s_mats['rwkv'], stack(s_small, 2))
```

```python
import functools
import math

import jax
import jax.numpy as jnp
from jax import lax
from jax.experimental import pallas as pl
from jax.experimental.pallas import tpu as pltpu

F32 = jnp.float32
BF16 = jnp.bfloat16

D_MODEL = 1024
DEPTH = 2
PAST_LEN = 16384
GW = 256
HD = 64
NH = GW // HD
S5_GROUP = 16
S5_NG = GW // S5_GROUP
S5_P = 64
S5_W = S5_NG * S5_P
RW_PROJ = 4 * GW
IN_WIDTH = 13 * GW
MEM_LEN = 256
MEM_HEADS = 4
MEM_HD = D_MODEL // MEM_HEADS
D_FF = 4 * D_MODEL
EPS = 1e-6
RW_LN_EPS = 64e-5
ROPE_BASE = 10000.0
CHUNK = 16
RET_CHUNK = 64

VMEM_LIMIT = 56 * 1024 * 1024
SUBLANES = 8
LANES = 128
LOG2_E = 1.4426950408889634


def _cparams(sem):
    return pltpu.CompilerParams(dimension_semantics=sem, vmem_limit_bytes=VMEM_LIMIT)


def _mm(a, b):
    return jnp.dot(a.astype(BF16), b.astype(BF16), preferred_element_type=F32)


def _mm_nt(a, b):
    return lax.dot_general(a.astype(BF16), b.astype(BF16), (((1,), (1,)), ((), ())),
                           preferred_element_type=F32)


def _mm_tn(a, b):
    return lax.dot_general(a.astype(BF16), b.astype(BF16), (((0,), (0,)), ((), ())),
                           preferred_element_type=F32)


def _split3(x):
    hi = x.astype(BF16)
    r1 = x - hi.astype(F32)
    mid = r1.astype(BF16)
    lo = (r1 - mid.astype(F32)).astype(BF16)
    return hi, mid, lo


def _mm_exact_lhs(sel, x):
    s = sel.astype(BF16)
    hi, mid, lo = _split3(x)
    return (jnp.dot(s, hi, preferred_element_type=F32) + jnp.dot(s, mid, preferred_element_type=F32)
            + jnp.dot(s, lo, preferred_element_type=F32))


def _rms(x, gain):
    return x * lax.rsqrt(jnp.mean(x * x, axis=-1, keepdims=True) + EPS) * gain


def _sigmoid(x):
    return 1.0 / (1.0 + jnp.exp(-x))


def _iota2(shape, axis):
    return lax.broadcasted_iota(jnp.int32, shape, axis)


def _head_ones():
    return (_iota2((GW, GW), 0) // HD == _iota2((GW, GW), 1) // HD).astype(F32)


def _head_sum(x, ones_bd):
    s = ones_bd.astype(BF16)
    hi = x.astype(BF16)
    lo = (x - hi.astype(F32)).astype(BF16)
    return jnp.dot(hi, s, preferred_element_type=F32) + jnp.dot(lo, s, preferred_element_type=F32)


def _norm_mm_kernel(x_ref, g_ref, w_ref, o_ref):
    o_ref[...] = _mm(_rms(x_ref[...], g_ref[...]), w_ref[...])


def norm_matmul(x, gain, w, *, tm):
    t, d = x.shape
    n = w.shape[1]
    return pl.pallas_call(
        _norm_mm_kernel,
        out_shape=jax.ShapeDtypeStruct((t, n), F32),
        grid=(t // tm,),
        in_specs=[pl.BlockSpec((tm, d), lambda i: (i, 0)),
                  pl.BlockSpec((1, d), lambda i: (0, 0)),
                  pl.BlockSpec((d, n), lambda i: (0, 0))],
        out_specs=pl.BlockSpec((tm, n), lambda i: (i, 0)),
        compiler_params=_cparams(("parallel",)),
        name="norm_matmul",
    )(x, gain, w)


def _mm_kernel(a_ref, w_ref, o_ref):
    o_ref[...] = _mm(a_ref[...], w_ref[...])


def matmul_layers(a, w, *, tm):
    t, k = a.shape
    layers, _, n = w.shape
    return pl.pallas_call(
        _mm_kernel,
        out_shape=jax.ShapeDtypeStruct((layers, t, n), F32),
        grid=(layers, t // tm),
        in_specs=[pl.BlockSpec((tm, k), lambda l, i: (i, 0)),
                  pl.BlockSpec((None, k, n), lambda l, i: (l, 0, 0))],
        out_specs=pl.BlockSpec((None, tm, n), lambda l, i: (l, i, 0)),
        compiler_params=_cparams(("parallel", "parallel")),
        name="matmul_layers",
    )(a, w)


def _mm_res_kernel(a_ref, w_ref, r_ref, o_ref):
    o_ref[...] = r_ref[...] + _mm(a_ref[...], w_ref[...])


def matmul_residual(a, w, res, *, tm):
    t, k = a.shape
    n = w.shape[1]
    return pl.pallas_call(
        _mm_res_kernel,
        out_shape=jax.ShapeDtypeStruct((t, n), F32),
        grid=(t // tm,),
        in_specs=[pl.BlockSpec((tm, k), lambda i: (i, 0)),
                  pl.BlockSpec((k, n), lambda i: (0, 0)),
                  pl.BlockSpec((tm, n), lambda i: (i, 0))],
        out_specs=pl.BlockSpec((tm, n), lambda i: (i, 0)),
        compiler_params=_cparams(("parallel",)),
        name="matmul_residual",
    )(a, w, res)


def _mix_out_q_kernel(a0_ref, a1_ref, a2_ref, a3_ref, w_ref, r_ref, g_ref, wq_ref, x_ref, q_ref):
    a = jnp.concatenate([a0_ref[...], a1_ref[...], a2_ref[...], a3_ref[...]], axis=-1)
    x = r_ref[...] + _mm(a, w_ref[...])
    x_ref[...] = x
    q_ref[...] = jnp.dot(_rms(x, g_ref[...]).astype(BF16), wq_ref[...], preferred_element_type=F32).astype(BF16)


def mix_out_q(parts, w, res, gain, w_q, *, tm):
    t, d = res.shape
    part = pl.BlockSpec((tm, GW), lambda i: (i, 0))
    mat = pl.BlockSpec((d, d), lambda i: (0, 0))
    tile = pl.BlockSpec((tm, d), lambda i: (i, 0))
    return pl.pallas_call(
        _mix_out_q_kernel,
        out_shape=(jax.ShapeDtypeStruct((t, d), F32), jax.ShapeDtypeStruct((t, d), BF16)),
        grid=(t // tm,),
        in_specs=[part, part, part, part, mat, tile, pl.BlockSpec((1, d), lambda i: (0, 0)), mat],
        out_specs=(tile, tile),
        compiler_params=_cparams(("parallel",)),
        name="mix_out_q",
    )(*parts, w, res, gain, w_q)


def _ffn_kernel(x_ref, g_ref, wu_ref, wd_ref, gf_ref, o_ref, *, final_norm):
    x = x_ref[...]
    h = jnp.dot(_rms(x, g_ref[...]).astype(BF16), wu_ref[...], preferred_element_type=F32)
    h = jnp.square(jnp.maximum(h, 0.0))
    y = x + jnp.dot(h.astype(BF16), wd_ref[...], preferred_element_type=F32)
    if final_norm:
        y = _rms(y, gf_ref[...])
    o_ref[...] = y


def ffn(x, gain, w_up, w_down, gain_final, *, tm, final_norm):
    t, d = x.shape
    ff = w_up.shape[1]
    resident = lambda shape: pl.BlockSpec(shape, lambda i: (0, 0), pipeline_mode=pl.Buffered(1))
    return pl.pallas_call(
        functools.partial(_ffn_kernel, final_norm=final_norm),
        out_shape=jax.ShapeDtypeStruct((t, d), F32),
        grid=(t // tm,),
        in_specs=[pl.BlockSpec((tm, d), lambda i: (i, 0)),
                  pl.BlockSpec((1, d), lambda i: (0, 0)),
                  resident((d, ff)), resident((ff, d)),
                  pl.BlockSpec((1, d), lambda i: (0, 0))],
        out_specs=pl.BlockSpec((tm, d), lambda i: (i, 0)),
        compiler_params=_cparams(("parallel",)),
        name="ffn",
    )(x, gain, w_up, w_down, gain_final)


def _attn_out_kernel(q_ref, k_ref, v_ref, wo_ref, r_ref, x_ref):
    sls = [slice(h * MEM_HD, (h + 1) * MEM_HD) for h in range(MEM_HEADS)]
    s = [_mm_nt(q_ref[:, sl], k_ref[:, sl]) * (MEM_HD ** -0.5) for sl in sls]
    p = [jnp.exp(sh - jnp.max(sh, axis=-1, keepdims=True)) for sh in s]
    p = [ph * (1.0 / jnp.sum(ph, axis=-1, keepdims=True)) for ph in p]
    heads = [_mm(ph, v_ref[:, sl]) for ph, sl in zip(p, sls)]
    x_ref[...] = r_ref[...] + _mm(jnp.concatenate(heads, axis=-1), wo_ref[...])


def cross_attention_out(q, mem_k, mem_v, layer, w_o, res, *, rows):
    t, d = res.shape
    bsz = mem_k.shape[1] // MEM_LEN
    lt = t // bsz // rows
    tile = pl.BlockSpec((rows, d), lambda b, l: (b * lt + l, 0))
    mem = pl.BlockSpec((None, MEM_LEN, d), lambda b, l: (layer, b, 0))
    return pl.pallas_call(
        _attn_out_kernel,
        out_shape=jax.ShapeDtypeStruct((t, d), F32),
        grid=(bsz, lt),
        in_specs=[tile, mem, mem, pl.BlockSpec((d, d), lambda b, l: (0, 0)), tile],
        out_specs=tile,
        compiler_params=_cparams(("parallel", "arbitrary")),
        name="cross_attention_out",
    )(q, mem_k, mem_v, w_o, res)


def _attn_cache_kernel(q_ref, k_ref, v_ref, o_ref, *, nb, rows):
    nr = MEM_HEADS * rows
    same_head = _iota2((nr, MEM_HEADS * MEM_LEN), 1) % MEM_HEADS == _iota2((nr, MEM_HEADS * MEM_LEN), 0) // rows
    q_all = q_ref[...].astype(F32)
    for b in range(nb):
        q = q_all[b * rows:(b + 1) * rows, :]
        qs = jnp.concatenate([q[:, h * MEM_HD:(h + 1) * MEM_HD] for h in range(MEM_HEADS)], axis=0)
        s = _mm_nt(qs, k_ref[0, b].reshape(MEM_HEADS * MEM_LEN, MEM_HD)) * (MEM_HD ** -0.5)
        s = jnp.where(same_head, s, -1e30)
        p = jnp.exp(s - jnp.max(s, axis=-1, keepdims=True))
        p = p * (1.0 / jnp.sum(p, axis=-1, keepdims=True))
        o = _mm(p, v_ref[0, b].reshape(MEM_HEADS * MEM_LEN, MEM_HD))
        for h in range(MEM_HEADS):
            o_ref[b * rows:(b + 1) * rows, h * MEM_HD:(h + 1) * MEM_HD] = o[h * rows:(h + 1) * rows]


def cross_attention_cache(q, cache_k, cache_v, layer, *, nb, rows):
    t = q.shape[0]
    bsz = cache_k.shape[1]
    mem = pl.BlockSpec((1, nb, MEM_LEN, MEM_HEADS, MEM_HD), lambda b: (layer, b, 0, 0, 0))
    return pl.pallas_call(
        functools.partial(_attn_cache_kernel, nb=nb, rows=rows),
        out_shape=jax.ShapeDtypeStruct((t, D_MODEL), F32),
        grid=(bsz // nb,),
        in_specs=[pl.BlockSpec((nb * rows, D_MODEL), lambda b: (b, 0)), mem, mem],
        out_specs=pl.BlockSpec((nb * rows, D_MODEL), lambda b: (b, 0)),
        compiler_params=_cparams(("parallel",)),
        name="cross_attention_cache",
    )(q, cache_k, cache_v)


def _gelu_tanh(x):
    return 0.5 * x * (1.0 + jnp.tanh(math.sqrt(2.0 / math.pi) * (x + 0.044715 * (x * x * x))))


def _s5_kernel(u_ref, h0_ref, lam_ref, bblk_ref, cblk_ref, d_ref, wglu_ref, bglu_ref, gain_ref,
               y_ref, hfin_ref, scr_ref, tm_ref, *, steps, nb):
    @pl.when(pl.program_id(1) == 0)
    def _():
        scr_ref[0:nb, :] = h0_ref[...]

    def to_time_major(b, carry):
        for j in range(GW // LANES):
            tm_ref[j, pl.ds(b, steps, stride=nb), :] = u_ref[b, :, j * LANES:(j + 1) * LANES]
        return carry

    lax.fori_loop(0, nb, to_time_major, 0)
    u = jnp.concatenate([tm_ref[j] for j in range(GW // LANES)], axis=1)
    scr_ref[nb:, :] = _mm(u, bblk_ref[...])
    lam_re = lam_ref[0:1, :]
    lam_im = lam_ref[1:2, :]

    def step(t, carry):
        p0 = pl.multiple_of(t * nb, nb)
        c0 = pl.multiple_of((t + 1) * nb, nb)
        h_re = scr_ref[pl.ds(p0, nb), 0:S5_W]
        h_im = scr_ref[pl.ds(p0, nb), S5_W:]
        scr_ref[pl.ds(c0, nb), 0:S5_W] = scr_ref[pl.ds(c0, nb), 0:S5_W] + lam_re * h_re - lam_im * h_im
        scr_ref[pl.ds(c0, nb), S5_W:] = scr_ref[pl.ds(c0, nb), S5_W:] + lam_re * h_im + lam_im * h_re
        return carry

    lax.fori_loop(0, steps, step, 0)
    h_last = scr_ref[steps * nb:, :]
    hfin_ref[...] = h_last
    y = _mm(scr_ref[nb:, :], cblk_ref[...]) + d_ref[...] * u
    y = _gelu_tanh(y)
    y = y * _sigmoid(_mm(y, wglu_ref[...]) + bglu_ref[...])
    y = _rms(y, gain_ref[...])
    for j in range(GW // LANES):
        tm_ref[j] = y[:, j * LANES:(j + 1) * LANES]

    def to_batch_major(b, carry):
        for j in range(GW // LANES):
            y_ref[b, :, j * LANES:(j + 1) * LANES] = tm_ref[j, pl.ds(b, steps, stride=nb), :]
        return carry

    lax.fori_loop(0, nb, to_batch_major, 0)
    scr_ref[0:nb, :] = h_last


def s5_mixer(proj, h0, lam, bblk, cblk, dvec, wglu, bglu, gain, *, steps, nb):
    bsz, seq, _ = proj.shape
    const = lambda shape: pl.BlockSpec(shape, lambda b, t: (0, 0))
    return pl.pallas_call(
        functools.partial(_s5_kernel, steps=steps, nb=nb),
        out_shape=(jax.ShapeDtypeStruct((bsz, seq, GW), F32),
                   jax.ShapeDtypeStruct((bsz, 2 * S5_W), F32)),
        grid=(bsz // nb, seq // steps),
        in_specs=[pl.BlockSpec((nb, steps, GW), lambda b, t: (b, t, 0)),
                  pl.BlockSpec((nb, 2 * S5_W), lambda b, t: (b, 0)),
                  const((2, S5_W)), const((GW, 2 * S5_W)), const((2 * S5_W, GW)), const((1, GW)),
                  const((GW, GW)), const((1, GW)), const((1, GW))],
        out_specs=(pl.BlockSpec((nb, steps, GW), lambda b, t: (b, t, 0)),
                   pl.BlockSpec((nb, 2 * S5_W), lambda b, t: (b, 0))),
        scratch_shapes=[pltpu.VMEM(((steps + 1) * nb, 2 * S5_W), F32),
                        pltpu.VMEM((GW // LANES, steps * nb, LANES), F32)],
        compiler_params=_cparams(("parallel", "arbitrary")),
        name="s5_mixer",
    )(proj, h0, lam, bblk, cblk, dvec, wglu, bglu, gain)


def s5_params(lam_re, lam_im, b_re, b_im, c_re, c_im, log_step):
    step = jnp.exp(log_step)[:, None]
    mag = jnp.exp(lam_re * step)
    lbar_re = mag * jnp.cos(lam_im * step)
    lbar_im = mag * jnp.sin(lam_im * step)
    den = lam_re * lam_re + lam_im * lam_im
    f_re = ((lbar_re - 1.0) * lam_re + lbar_im * lam_im) / den
    f_im = (lbar_im * lam_re - (lbar_re - 1.0) * lam_im) / den
    bbar_re = f_re[..., None] * b_re - f_im[..., None] * b_im
    bbar_im = f_re[..., None] * b_im + f_im[..., None] * b_re
    eye_g = jnp.eye(S5_NG, dtype=F32)

    def in_blk(m):
        return jnp.einsum('gph,gk->ghkp', m, eye_g).reshape(GW, S5_W)

    def out_blk(m):
        return jnp.einsum('ghp,gk->gpkh', m, eye_g).reshape(S5_W, GW)

    bblk = jnp.concatenate([in_blk(bbar_re), in_blk(bbar_im)], axis=1)
    cblk = jnp.concatenate([out_blk(c_re), -out_blk(c_im)], axis=0)
    lam2 = jnp.stack([lbar_re.reshape(S5_W), lbar_im.reshape(S5_W)])
    return lam2, bblk.astype(BF16), cblk.astype(BF16)


def s5_apply(proj, h0_re, h0_im, p, l, *, steps, nb):
    bsz = proj.shape[0]
    lam2, bblk, cblk = s5_params(p['s5_lam_re'][l], p['s5_lam_im'][l], p['s5_b_re'][l], p['s5_b_im'][l],
                                 p['s5_c_re'][l], p['s5_c_im'][l], p['s5_log_step'][l])
    h0 = jnp.concatenate([h0_re.reshape(bsz, S5_W), h0_im.reshape(bsz, S5_W)], axis=1)
    y, h = s5_mixer(proj, h0, lam2, bblk, cblk, p['s5_d'][l].reshape(1, GW),
                    p['s5_w_glu'][l].astype(BF16), p['s5_b_glu'][l].reshape(1, GW),
                    p['s5_norm'][l].reshape(1, GW), steps=steps, nb=nb)
    return y, h[:, :S5_W].reshape(bsz, S5_NG, S5_P), h[:, S5_W:].reshape(bsz, S5_NG, S5_P)


def _load_state(s0_ref, st_ref, nb, transpose):
    for b in range(nb):
        rows = []
        for h in range(NH):
            pieces = [s0_ref[b, h]]
            if h:
                pieces.insert(0, jnp.zeros((HD, h * HD), F32))
            if h < NH - 1:
                pieces.append(jnp.zeros((HD, (NH - 1 - h) * HD), F32))
            rows.append(jnp.concatenate(pieces, axis=1))
        st = jnp.concatenate(rows, axis=0)
        st_ref[b] = st.T if transpose else st


def _store_state(st_ref, sfin_ref, nb, transpose, own_layer):
    if own_layer is not None:
        for other in range(DEPTH):
            if other != own_layer:
                sfin_ref[other] = jnp.zeros(sfin_ref.shape[1:], F32)
        sfin_ref = sfin_ref.at[own_layer]
    for b in range(nb):
        st = st_ref[b].T if transpose else st_ref[b]
        for h in range(NH):
            sfin_ref[b, h] = st[h * HD:(h + 1) * HD, h * HD:(h + 1) * HD]


def _head_masks():
    lane_head = _iota2((1, GW), 1) // HD
    return [(lane_head == h).astype(F32) for h in range(NH)]


def _stack_heads(x, masks):
    return jnp.concatenate([x * m for m in masks], axis=0)


def _pad_rows(x, rows):
    if x.shape[0] == rows:
        return x
    return jnp.concatenate([x, jnp.zeros((rows - x.shape[0], x.shape[1]), x.dtype)], axis=0)


def _silu(x):
    return x * _sigmoid(x)


def _chunk_rows(ref, b, r0, rows, c):
    return _pad_rows(ref[b, pl.ds(r0, rows), :], c)


def _ret_kernel(q_ref, k_ref, v_ref, g_ref, cos_ref, sin_ref, s0_ref, _states_in, o_ref, sfin_ref, st_ref,
                *, nb, tb, c, c_real, own_layer):
    rows = min(tb, c)

    @pl.when(pl.program_id(1) == 0)
    def _():
        _load_state(s0_ref, st_ref, nb, transpose=False)

    masks = _head_masks()
    ones_bd = _head_ones()
    lane_head = _iota2((1, GW), 1) // HD
    log_gamma = jnp.zeros((1, GW), F32)
    for h in range(NH):
        log_gamma = jnp.where(lane_head == h, math.log(1.0 - 2.0 ** (-5.0 - h)), log_gamma)
    tt = _iota2((c, GW), 0).astype(F32)
    scale = HD ** -0.5
    g_q = jnp.exp(log_gamma * (tt + 1.0))
    g_k = jnp.exp(-log_gamma * (tt + 1.0)) * scale
    g_tail = jnp.exp(log_gamma * (c_real - 1.0 - tt)) * scale
    row_head = _iota2((GW, GW), 0) // HD
    g_chunk = jnp.zeros((GW, GW), F32)
    for h in range(NH):
        g_chunk = jnp.where(row_head == h, math.exp(math.log(1.0 - 2.0 ** (-5.0 - h)) * c_real), g_chunk)
    causal = _iota2((c, NH * c), 0) >= _iota2((c, NH * c), 1) % c
    first_half = _iota2((c, GW), 1) % HD < HD // 2

    def rope(x, cos, sin):
        swapped = jnp.where(first_half, pltpu.roll(x, GW - HD // 2, axis=1), pltpu.roll(x, HD // 2, axis=1))
        return x * cos + swapped * sin

    def chunk(ci, carry):
        r0 = pl.multiple_of(ci * rows, rows)
        cos = _pad_rows(cos_ref[pl.ds(r0, rows), :], c)
        sin = _pad_rows(sin_ref[pl.ds(r0, rows), :], c)
        rng = range(nb)
        qt = [rope(_chunk_rows(q_ref, b, r0, rows, c), cos, sin) * g_q for b in rng]
        k = [rope(_chunk_rows(k_ref, b, r0, rows, c), cos, sin) for b in rng]
        v = [_chunk_rows(v_ref, b, r0, rows, c) for b in rng]
        st = [st_ref[b] for b in rng]
        sc = [_mm_nt(qt[b], _stack_heads(k[b] * g_k, masks)) for b in rng]
        o_in = [_mm(jnp.where(causal, sc[b], 0.0), _stack_heads(v[b], masks)) for b in rng]
        o_st = [_mm(qt[b], st[b]) for b in rng]
        d_st = [_mm_tn(k[b] * g_tail, v[b]) for b in rng]
        o = [o_in[b] + o_st[b] for b in rng]
        ms = [_head_sum(o[b] * o[b], ones_bd) for b in rng]
        for b in rng:
            st_ref[b] = st[b] * g_chunk + ones_bd * d_st[b]
            ob = o[b] * lax.rsqrt(ms[b] * (1.0 / HD) + EPS) * _silu(_chunk_rows(g_ref, b, r0, rows, c))
            o_ref[b, pl.ds(r0, rows), :] = ob[:rows]
        return carry

    lax.fori_loop(0, tb // rows, chunk, 0)

    @pl.when(pl.program_id(1) == pl.num_programs(1) - 1)
    def _():
        _store_state(st_ref, sfin_ref, nb, transpose=False, own_layer=own_layer)


def _rope_tables(pos0, seq):
    half = HD // 2
    inv = ROPE_BASE ** (-jnp.arange(half, dtype=F32) / half)
    pos = pos0 + jnp.arange(seq, dtype=jnp.int32)
    ang = pos.astype(F32)[:, None] * inv[None, :]
    cos, sin = jnp.cos(ang), jnp.sin(ang)
    return (jnp.tile(jnp.concatenate([cos, cos], axis=-1), (1, NH)),
            jnp.tile(jnp.concatenate([-sin, sin], axis=-1), (1, NH)))


def _proj_spec(nb, tb, col):
    return pl.BlockSpec((nb, tb, GW), lambda b, t: (b, t, col))


def _state_spec(nb, layer):
    return pl.BlockSpec((None, nb, NH, HD, HD), lambda b, t: (layer, b, 0, 0, 0))


_ALIASED = pl.BlockSpec(memory_space=pl.ANY)


def _collector(states, bsz, nb, layer, operand_index):
    shape = jax.ShapeDtypeStruct((DEPTH, bsz, NH, HD, HD), F32)
    if states is None:
        spec = pl.BlockSpec((DEPTH, nb, NH, HD, HD), lambda b, t: (0, b, 0, 0, 0))
        return jnp.zeros((1, 1, NH, HD, HD), F32), spec, shape, {}, layer
    return states, _state_spec(nb, layer), shape, {operand_index: 1}, None


def retention_apply(proj, s0, s0_layer, states, layer, pos0, *, tb, nb):
    bsz, seq, _ = proj.shape
    c = RET_CHUNK if tb >= RET_CHUNK else CHUNK
    cos, sin = _rope_tables(pos0, seq)
    states, st_spec, st_shape, aliases, own_layer = _collector(states, bsz, nb, layer, 7)
    return pl.pallas_call(
        functools.partial(_ret_kernel, nb=nb, tb=tb, c=c, c_real=min(tb, c), own_layer=own_layer),
        out_shape=(jax.ShapeDtypeStruct((bsz, seq, GW), F32), st_shape),
        grid=(bsz // nb, seq // tb),
        in_specs=[_proj_spec(nb, tb, 5), _proj_spec(nb, tb, 6), _proj_spec(nb, tb, 7), _proj_spec(nb, tb, 8),
                  pl.BlockSpec((tb, GW), lambda b, t: (t, 0)), pl.BlockSpec((tb, GW), lambda b, t: (t, 0)),
                  _state_spec(nb, s0_layer), _ALIASED],
        out_specs=(pl.BlockSpec((nb, tb, GW), lambda b, t: (b, t, 0)), st_spec),
        scratch_shapes=[pltpu.VMEM((nb, GW, GW), F32)],
        input_output_aliases=aliases,
        compiler_params=_cparams(("parallel", "arbitrary")),
        name="retention",
    )(proj, proj, proj, proj, cos, sin, s0, states)


def _log_sigmoid(z):
    return jnp.minimum(z, 0.0) - jnp.log(1.0 + jnp.exp(-jnp.abs(z)))


def _hgrn_kernel(q_ref, f_ref, i_ref, g_ref, lb_ref, gain_ref, s0_ref, _states_in, o_ref, sfin_ref, st_ref,
                 *, nb, tb, c, own_layer):
    rows = min(tb, c)

    @pl.when(pl.program_id(1) == 0)
    def _():
        _load_state(s0_ref, st_ref, nb, transpose=True)

    t_hi = -(-rows // SUBLANES) * SUBLANES
    ones_bd = _head_ones()
    tri = (_iota2((c, c), 0) >= _iota2((c, c), 1)).astype(F32)
    t_idx = _iota2((c, GW), 0)
    lb = lb_ref[...]
    log_lb = jnp.log(lb)
    log_1m_lb = jnp.log(1.0 - lb)
    gain = gain_ref[...]

    def chunk(ci, carry):
        r0 = pl.multiple_of(ci * rows, rows)
        rng = range(nb)
        valid = t_idx < rows

        def gates(b):
            z = _chunk_rows(f_ref, b, r0, rows, c)
            ls_pos = _log_sigmoid(z)
            ls_neg = ls_pos - z
            b2 = log_lb + ls_neg
            log_f = jnp.maximum(ls_pos, b2) + jnp.log(1.0 + jnp.exp(-jnp.abs(ls_pos - b2)))
            return (jnp.where(valid, log_f, 0.0), jnp.where(valid, (1.0 - lb) * jnp.exp(ls_neg), 0.0),
                    ls_neg + log_1m_lb)

        log_f, key, log_key = zip(*[gates(b) for b in rng])
        q = [_silu(_chunk_rows(q_ref, b, r0, rows, c)) for b in rng]
        v = [_chunk_rows(i_ref, b, r0, rows, c) for b in rng]
        st = [st_ref[b] for b in rng]
        cum = [_mm_exact_lhs(tri, log_f[b]) for b in rng]
        last = [cum[b][c - 1:c, :] for b in rng]
        cum2 = [cum[b] * LOG2_E for b in rng]
        kd2 = [(log_key[b] - cum[b]) * LOG2_E for b in rng]

        def pair_rows(b, s):
            lo = SUBLANES * (s // SUBLANES)
            t_rows = _iota2((t_hi - lo, GW), 0) + lo
            key_decay = jnp.where(t_rows >= s, jnp.exp2(cum2[b][lo:t_hi] + kd2[b][s:s + 1, :]), 0.0)
            return key_decay * q[b][lo:t_hi]

        sc = [_mm(jnp.concatenate([pair_rows(b, s) for s in range(rows)], axis=0), ones_bd) for b in rng]
        o_st = [_mm_nt(q[b] * jnp.exp(cum[b]), st[b]) for b in rng]
        d_st = [_mm_tn(v[b], key[b] * jnp.exp(last[b] - cum[b])) for b in rng]

        def intra(b):
            tiles = [None] * (c // SUBLANES)
            off = 0
            for s in range(rows):
                for ti in range(s // SUBLANES, t_hi // SUBLANES):
                    term = sc[b][off:off + SUBLANES, :] * v[b][s:s + 1, :]
                    tiles[ti] = term if tiles[ti] is None else tiles[ti] + term
                    off += SUBLANES
            zero = jnp.zeros((SUBLANES, GW), F32)
            return jnp.concatenate([zero if t is None else t for t in tiles], axis=0)

        o = [o_st[b] + intra(b) for b in rng]
        ms = [_head_sum(o[b] * o[b], ones_bd) for b in rng]
        for b in rng:
            st_ref[b] = st[b] * jnp.exp(last[b]) + ones_bd * d_st[b]
            ob = o[b] * lax.rsqrt(ms[b] * (1.0 / HD) + EPS) * gain * _silu(_chunk_rows(g_ref, b, r0, rows, c))
            o_ref[b, pl.ds(r0, rows), :] = ob[:rows]
        return carry

    lax.fori_loop(0, tb // rows, chunk, 0)

    @pl.when(pl.program_id(1) == pl.num_programs(1) - 1)
    def _():
        _store_state(st_ref, sfin_ref, nb, transpose=True, own_layer=own_layer)


def hgrn_apply(proj, s0, s0_layer, states, layer, lb, gain, *, tb, nb):
    bsz, seq, _ = proj.shape
    row = pl.BlockSpec((1, GW), lambda b, t: (0, 0))
    states, st_spec, st_shape, aliases, own_layer = _collector(states, bsz, nb, layer, 7)
    return pl.pallas_call(
        functools.partial(_hgrn_kernel, nb=nb, tb=tb, c=CHUNK, own_layer=own_layer),
        out_shape=(jax.ShapeDtypeStruct((bsz, seq, GW), F32), st_shape),
        grid=(bsz // nb, seq // tb),
        in_specs=[_proj_spec(nb, tb, 1), _proj_spec(nb, tb, 2), _proj_spec(nb, tb, 3), _proj_spec(nb, tb, 4),
                  row, row, _state_spec(nb, s0_layer), _ALIASED],
        out_specs=(pl.BlockSpec((nb, tb, GW), lambda b, t: (b, t, 0)), st_spec),
        scratch_shapes=[pltpu.VMEM((nb, GW, GW), F32)],
        input_output_aliases=aliases,
        compiler_params=_cparams(("parallel", "arbitrary")),
        name="hgrn2",
    )(proj, proj, proj, proj, lb.reshape(1, GW), gain.reshape(1, GW), s0, states)


def _softplus(z):
    return jnp.maximum(z, 0.0) + jnp.log(1.0 + jnp.exp(-jnp.abs(z)))


def _rwkv_prepare(ins, consts):
    masks, ones_bd, bd_mask, eye_flat, tri, strict, incl = consts
    rng = range(len(ins))
    r, lw, k, v, kk, a = [[i[j] for i in ins] for j in range(6)]
    c = r[0].shape[0]
    n = NH * c

    def to_bd(flat):
        return jnp.concatenate([flat] * NH, axis=0) * bd_mask

    cum = [_mm_exact_lhs(tri, lw[i]) for i in rng]
    last = [cum[i][c - 1:c, :] for i in rng]
    p_inv = [jnp.exp(-cum[i]) for i in rng]
    p_tail = [jnp.exp(last[i] - cum[i]) for i in rng]
    ka = [kk[i] * a[i] for i in rng]
    x = [jnp.concatenate([kk[i] * jnp.exp(cum[i] - lw[i]), r[i] * jnp.exp(cum[i])], axis=0) for i in rng]
    g = [_mm_nt(x[i], jnp.concatenate([_stack_heads(ka[i] * p_inv[i], masks),
                                       _stack_heads(k[i] * p_inv[i], masks)], axis=0)) for i in rng]
    m_ak = [jnp.where(strict, g[i][:c], 0.0) for i in rng]
    n_ak = [jnp.where(incl, g[i][c:], 0.0) for i in rng]
    pw = [m_ak[i][:, :n] for i in rng]
    t_inv = [eye_flat - pw[i] for i in rng]
    pw_bd = [to_bd(pw[i]) for i in rng]
    for _ in range(int(math.log2(c)) - 1):
        pw = [_mm(pw[i], pw_bd[i]) for i in rng]
        pw_bd = [to_bd(pw[i]) for i in rng]
        t_inv = [t_inv[i] + _mm(t_inv[i], pw_bd[i]) for i in rng]
    v_stk = [_stack_heads(v[i], masks) for i in rng]
    zeros = jnp.zeros((n, GW), F32)
    mkv = [_mm(m_ak[i], jnp.concatenate([zeros, v_stk[i]], axis=0)) for i in rng]
    tmkv = [_mm(t_inv[i], _stack_heads(mkv[i], masks)) for i in rng]
    return [dict(x=x[i], t_inv=t_inv[i], tmkv=tmkv[i], n_ak=n_ak[i], v=v[i], v_stk=v_stk[i],
                 decay=jnp.exp(last[i]),
                 k_tail=jnp.concatenate([ka[i] * p_tail[i], k[i] * p_tail[i]], axis=0)) for i in rng]


def _rwkv_advance(prep, st, consts):
    masks, ones_bd = consts[0], consts[1]
    rng = range(len(prep))
    c = prep[0]['t_inv'].shape[0]
    xa = [_mm_nt(prep[i]['x'], st[i]) for i in rng]
    u = [-_mm(prep[i]['t_inv'], _stack_heads(xa[i][:c], masks)) - prep[i]['tmkv'] for i in rng]
    nuv = [_mm(prep[i]['n_ak'], jnp.concatenate([_stack_heads(u[i], masks), prep[i]['v_stk']], axis=0))
           for i in rng]
    d_uv = [_mm_tn(jnp.concatenate([u[i], prep[i]['v']], axis=0), prep[i]['k_tail']) for i in rng]
    return [(xa[i][c:] + nuv[i], st[i] * prep[i]['decay'] + ones_bd * d_uv[i]) for i in rng]


def _rwkv_kernel(x_r_ref, x_k_ref, x_v_ref, x_l_ref, sh0_ref, s0_ref, mu_ref, w0_ref, ww_ref, a0_ref,
                 wa_ref, wg_ref, kk_ref, ka_ref, rk_ref, lng_ref, lnb_ref, _states_in,
                 o_ref, sfin_ref, shfin_ref,
                 st_ref, sh_ref, r_s, lw_s, k_s, v_s, kkn_s, a_s, y_s, bonus_s, gate_s, *, nb, tb, c, own_layer):
    tbp = max(tb, c)

    @pl.when(pl.program_id(1) == 0)
    def _():
        _load_state(s0_ref, st_ref, nb, transpose=False)
        sh_ref[...] = sh0_ref[...]

    masks = _head_masks()
    ones_bd = _head_ones()
    n = NH * c
    row_t = _iota2((c, 2 * n), 0)
    col_t = _iota2((c, 2 * n), 1) % c
    bd_mask = (_iota2((n, n), 0) // c == _iota2((n, n), 1) // c).astype(F32)
    eye_flat = (_iota2((c, n), 1) % c == _iota2((c, n), 0)).astype(F32)
    consts = (masks, ones_bd, bd_mask, eye_flat, (_iota2((c, c), 0) >= _iota2((c, c), 1)).astype(F32),
              col_t < row_t, col_t <= row_t)
    first_row = _iota2((tb, GW), 0) == 0

    for b in range(nb):
        def mixed(x_ref, j):
            x = x_ref[b]
            prev = jnp.where(first_row, sh_ref[b, j:j + 1, :], pltpu.roll(x, 1, axis=0))
            sh_ref[b, j:j + 1, :] = x[tb - 1:tb, :]
            return x + (prev - x) * mu_ref[j:j + 1, :]

        r = mixed(x_r_ref, 0)
        k = mixed(x_k_ref, 1)
        v = mixed(x_v_ref, 2)
        xl = mixed(x_l_ref, 3)
        log_w = -_softplus(-(w0_ref[...] + _mm(jnp.tanh(xl), ww_ref[...]))) - 0.5
        a = _sigmoid(a0_ref[...] + _mm(xl, wa_ref[...]))
        gate = _mm(_sigmoid(xl), wg_ref[...])
        kk = k * kk_ref[...]
        kk = kk * lax.rsqrt(jnp.maximum(_head_sum(kk * kk, ones_bd), 1e-24))
        k = k * (1.0 + (a - 1.0) * ka_ref[...])
        r_s[b, 0:tb, :] = r
        lw_s[b, 0:tb, :] = -jnp.exp(log_w)
        k_s[b, 0:tb, :] = k
        v_s[b, 0:tb, :] = v
        kkn_s[b, 0:tb, :] = kk
        a_s[b, 0:tb, :] = a
        bonus_s[b] = _head_sum(r * k * rk_ref[...], ones_bd) * v
        gate_s[b] = gate
        if tbp > tb:
            zeros = jnp.zeros((tbp - tb, GW), F32)
            for s in (r_s, lw_s, k_s, v_s, kkn_s, a_s):
                s[b, tb:tbp, :] = zeros

    n_chunks = tbp // c
    group = 4 if n_chunks % 4 == 0 else 1

    def chunks(gi, carry):
        r0 = [pl.multiple_of((gi * group + j) * c, c) for j in range(group)]
        prep = _rwkv_prepare([tuple(s[b, pl.ds(r0[j], c), :] for s in (r_s, lw_s, k_s, v_s, kkn_s, a_s))
                              for j in range(group) for b in range(nb)], consts)
        st = [st_ref[b] for b in range(nb)]
        for j in range(group):
            outs = _rwkv_advance(prep[j * nb:(j + 1) * nb], st, consts)
            st = [o[1] for o in outs]
            for b in range(nb):
                y_s[b, pl.ds(r0[j], c), :] = outs[b][0]
        for b in range(nb):
            st_ref[b] = st[b]
        return carry

    lax.fori_loop(0, n_chunks // group, chunks, 0)

    for b in range(nb):
        y = y_s[b, 0:tb, :]
        mean = _head_sum(y, ones_bd) * (1.0 / HD)
        d = y - mean
        var = _head_sum(d * d, ones_bd) * (1.0 / HD)
        y = d * lax.rsqrt(var + RW_LN_EPS) * lng_ref[...] + lnb_ref[...]
        o_ref[b] = (y + bonus_s[b]) * gate_s[b]
    shfin_ref[...] = sh_ref[...]

    @pl.when(pl.program_id(1) == pl.num_programs(1) - 1)
    def _():
        _store_state(st_ref, sfin_ref, nb, transpose=False, own_layer=own_layer)


def rwkv_apply(proj, s0, s0_layer, states, shift0, p, l, *, tb, nb):
    bsz, seq, _ = proj.shape
    states, st_spec, st_shape, aliases, own_layer = _collector(states, bsz, nb, l, 17)
    c = CHUNK
    tbp = max(tb, c)
    row = pl.BlockSpec((1, GW), lambda b, t: (0, 0))
    mat = pl.BlockSpec((GW, GW), lambda b, t: (0, 0))
    sh_spec = pl.BlockSpec((nb, 4, GW), lambda b, t: (b, 0, 0))
    zeros = lambda r: jnp.zeros((r, GW), F32)
    ww = jnp.concatenate([p['rw_w_w2'][l], zeros(192)], axis=0).astype(BF16)
    wa = jnp.concatenate([zeros(64), p['rw_w_a2'][l], zeros(128)], axis=0).astype(BF16)
    wg = jnp.concatenate([zeros(128), p['rw_w_g2'][l]], axis=0).astype(BF16)
    r1 = lambda name: p[name][l].reshape(1, GW)
    seq_buf = pltpu.VMEM((nb, tbp, GW), F32)
    blk_buf = pltpu.VMEM((nb, tb, GW), F32)
    y, st, sh = pl.pallas_call(
        functools.partial(_rwkv_kernel, nb=nb, tb=tb, c=c, own_layer=own_layer),
        out_shape=(jax.ShapeDtypeStruct((bsz, seq, GW), F32), st_shape,
                   jax.ShapeDtypeStruct((bsz, 4, GW), F32)),
        grid=(bsz // nb, seq // tb),
        in_specs=[_proj_spec(nb, tb, 9), _proj_spec(nb, tb, 10), _proj_spec(nb, tb, 11), _proj_spec(nb, tb, 12),
                  sh_spec, _state_spec(nb, s0_layer), pl.BlockSpec((4, GW), lambda b, t: (0, 0)),
                  row, mat, row, mat, mat, row, row, row, row, row, _ALIASED],
        out_specs=(pl.BlockSpec((nb, tb, GW), lambda b, t: (b, t, 0)), st_spec, sh_spec),
        scratch_shapes=[pltpu.VMEM((nb, GW, GW), F32), pltpu.VMEM((nb, 4, GW), F32),
                        seq_buf, seq_buf, seq_buf, seq_buf, seq_buf, seq_buf, seq_buf, blk_buf, blk_buf],
        input_output_aliases=aliases,
        compiler_params=_cparams(("parallel", "arbitrary")),
        name="rwkv7",
    )(proj, proj, proj, proj, shift0.reshape(bsz, 4, GW), s0, p['rw_mu'][l].reshape(4, GW),
      r1('rw_w0'), ww, r1('rw_a0'), wa, wg, r1('rw_k_k'), r1('rw_k_a'), r1('rw_r_k'), r1('rw_ln_g'), r1('rw_ln_b'),
      states)
    return y, st, sh.reshape(bsz, RW_PROJ)


TOKEN_TILE = 512
TIME_BLOCK = 128
CACHE_ATTN_SEQS = 4


def _tiles(bsz, seq):
    if seq >= TIME_BLOCK:
        return dict(tb=TIME_BLOCK, s5_nb=bsz, mix_nb=bsz, rw_nb=bsz)
    return dict(tb=seq, s5_nb=bsz, mix_nb=8, rw_nb=16)


def _trunk_layer(x, bsz, seq, pos0, attend, st, mats, p, wb, l, lb, final_norm):
    cfg = _tiles(bsz, seq)
    tm = TOKEN_TILE
    row = lambda name: p[name][l].reshape(1, -1)
    proj = norm_matmul(x, row('norm_mix'), wb['w_in'][l], tm=tm)
    proj = proj.reshape(bsz, seq, IN_WIDTH)
    y_s5, s5_re, s5_im = s5_apply(proj, st['s5_re'], st['s5_im'], p, l, steps=cfg['tb'], nb=cfg['s5_nb'])
    y_hg, hg_s = hgrn_apply(proj, st['hgrn'], st['layer'], mats['hgrn'], l, lb, p['hg_norm'][l],
                            tb=cfg['tb'], nb=cfg['mix_nb'])
    y_rt, rt_s = retention_apply(proj, st['ret'], st['layer'], mats['ret'], l, pos0,
                                 tb=cfg['tb'], nb=cfg['mix_nb'])
    y_rw, rw_s, shift = rwkv_apply(proj, st['rwkv'], st['layer'], mats['rwkv'], st['shift'], p, l,
                                   tb=cfg['tb'], nb=cfg['rw_nb'])
    parts = [y.reshape(bsz * seq, GW) for y in (y_s5, y_hg, y_rt, y_rw)]
    x, q = mix_out_q(parts, wb['w_out'][l], x, row('norm_mem'), wb['mem_w_q'][l], tm=tm)
    x = attend(q, wb['mem_w_o'][l], x)
    x = ffn(x, row('norm_ffn'), wb['ffn_w_up'][l], wb['ffn_w_down'][l], p['norm_final'].reshape(1, -1),
            tm=tm, final_norm=final_norm)
    return x, (s5_re, s5_im, shift), dict(hgrn=hg_s, ret=rt_s, rwkv=rw_s)


def kernel(x_prompt, x_sample, mem_prompt, state_s5_re, state_s5_im, state_hgrn, state_ret, state_rwkv,
           state_rwkv_shift, cache_mem_k, cache_mem_v, norm_mix, w_in, w_out, s5_lam_re, s5_lam_im,
           s5_b_re, s5_b_im, s5_c_re, s5_c_im, s5_d, s5_log_step, s5_w_glu, s5_b_glu, s5_norm,
           hg_lb_logits, hg_norm, rw_mu, rw_w0, rw_w_w2, rw_a0, rw_w_a2, rw_w_g2, rw_k_k, rw_k_a, rw_r_k,
           rw_ln_g, rw_ln_b, norm_mem, mem_w_q, mem_w_k, mem_w_v, mem_w_o, norm_ffn, ffn_w_up, ffn_w_down,
           norm_final):
    p = dict(norm_mix=norm_mix, s5_lam_re=s5_lam_re, s5_lam_im=s5_lam_im, s5_b_re=s5_b_re, s5_b_im=s5_b_im,
             s5_c_re=s5_c_re, s5_c_im=s5_c_im, s5_d=s5_d, s5_log_step=s5_log_step, s5_w_glu=s5_w_glu,
             s5_b_glu=s5_b_glu, s5_norm=s5_norm, hg_norm=hg_norm, rw_mu=rw_mu, rw_w0=rw_w0, rw_w_w2=rw_w_w2,
             rw_a0=rw_a0, rw_w_a2=rw_w_a2, rw_w_g2=rw_w_g2, rw_k_k=rw_k_k, rw_k_a=rw_k_a, rw_r_k=rw_r_k,
             rw_ln_g=rw_ln_g, rw_ln_b=rw_ln_b, norm_mem=norm_mem, norm_ffn=norm_ffn, norm_final=norm_final)
    wb = {name: w.astype(BF16) for name, w in dict(
        w_in=w_in, w_out=w_out, mem_w_q=mem_w_q, mem_w_k=mem_w_k, mem_w_v=mem_w_v, mem_w_o=mem_w_o,
        ffn_w_up=ffn_w_up, ffn_w_down=ffn_w_down).items()}
    lb_all = jnp.cumsum(jax.nn.softmax(hg_lb_logits.astype(F32), axis=0), axis=0)
    lb_all = lb_all - lb_all[0:1]

    bp, lp, _ = x_prompt.shape
    bs, ls, _ = x_sample.shape
    yp = x_prompt.reshape(bp * lp, D_MODEL)
    ys = x_sample.reshape(bs * ls, D_MODEL)
    mem2d = mem_prompt.reshape(bp * MEM_LEN, D_MODEL)
    mat_zero = jnp.zeros((1, bp, NH, HD, HD), F32)
    p_small, s_small = [], []
    p_mats = dict(hgrn=None, ret=None, rwkv=None)
    s_mats = dict(hgrn=None, ret=None, rwkv=None)
    mk = matmul_layers(mem2d, wb['mem_w_k'], tm=TOKEN_TILE)
    mv = matmul_layers(mem2d, wb['mem_w_v'], tm=TOKEN_TILE)
    for l in range(DEPTH):
        final = l == DEPTH - 1
        def attend_p(q, w_o, x, l=l):
            return cross_attention_out(q, mk, mv, l, w_o, x, rows=TOKEN_TILE)

        zero_state = dict(s5_re=jnp.zeros((bp, S5_NG, S5_P), F32), s5_im=jnp.zeros((bp, S5_NG, S5_P), F32),
                          shift=jnp.zeros((bp, RW_PROJ), F32), hgrn=mat_zero, ret=mat_zero, rwkv=mat_zero, layer=0)
        yp, small, p_mats = _trunk_layer(yp, bp, lp, 0, attend_p, zero_state, p_mats, p, wb, l, lb_all[l], final)
        p_small.append(small)
        sst = dict(s5_re=state_s5_re[l], s5_im=state_s5_im[l], shift=state_rwkv_shift[l],
                   hgrn=state_hgrn, ret=state_ret, rwkv=state_rwkv, layer=l)
        def attend_s(q, w_o, x, l=l):
            o = cross_attention_cache(q, cache_mem_k, cache_mem_v, l, nb=CACHE_ATTN_SEQS, rows=ls)
            return matmul_residual(o, w_o, x, tm=TOKEN_TILE)

        ys, small, s_mats = _trunk_layer(ys, bs, ls, PAST_LEN, attend_s, sst, s_mats, p, wb, l, lb_all[l], final)
        s_small.append(small)
    stack = lambda states, i: jnp.stack([s[i] for s in states])
    return (yp.reshape(bp, lp, D_MODEL), ys.reshape(bs, ls, D_MODEL),
            stack(p_small, 0), stack(p_small, 1), p_mats['hgrn'], p_mats['ret'], p_mats['rwkv'], stack(p_small, 2),
            mk.reshape(DEPTH, bp, MEM_LEN, MEM_HEADS, MEM_HD), mv.reshape(DEPTH, bp, MEM_LEN, MEM_HEADS, MEM_HD),
            stack(s_small, 0), stack(s_small, 1), s_mats['hgrn'], s_mats['ret'], s_mats['rwkv'], stack(s_small, 2))
```

```python
import functools
import math

import jax
import jax.numpy as jnp
from jax import lax
from jax.experimental import pallas as pl
from jax.experimental.pallas import tpu as pltpu

F32 = jnp.float32
BF16 = jnp.bfloat16

D_MODEL = 1024
DEPTH = 2
PAST_LEN = 16384
GW = 256
HD = 64
NH = GW // HD
S5_GROUP = 16
S5_NG = GW // S5_GROUP
S5_P = 64
S5_W = S5_NG * S5_P
RW_PROJ = 4 * GW
IN_WIDTH = 13 * GW
MEM_LEN = 256
MEM_HEADS = 4
MEM_HD = D_MODEL // MEM_HEADS
D_FF = 4 * D_MODEL
EPS = 1e-6
RW_LN_EPS = 64e-5
ROPE_BASE = 10000.0
CHUNK = 16
RET_CHUNK = 64

VMEM_LIMIT = 56 * 1024 * 1024
SUBLANES = 8
LANES = 128
LOG2_E = 1.4426950408889634


def _cparams(sem):
    return pltpu.CompilerParams(dimension_semantics=sem, vmem_limit_bytes=VMEM_LIMIT)


def _mm(a, b):
    return jnp.dot(a.astype(BF16), b.astype(BF16), preferred_element_type=F32)


def _mm_nt(a, b):
    return lax.dot_general(a.astype(BF16), b.astype(BF16), (((1,), (1,)), ((), ())),
                           preferred_element_type=F32)


def _mm_tn(a, b):
    return lax.dot_general(a.astype(BF16), b.astype(BF16), (((0,), (0,)), ((), ())),
                           preferred_element_type=F32)


def _split3(x):
    hi = x.astype(BF16)
    r1 = x - hi.astype(F32)
    mid = r1.astype(BF16)
    lo = (r1 - mid.astype(F32)).astype(BF16)
    return hi, mid, lo


def _mm_exact_lhs(sel, x):
    s = sel.astype(BF16)
    hi, mid, lo = _split3(x)
    return (jnp.dot(s, hi, preferred_element_type=F32) + jnp.dot(s, mid, preferred_element_type=F32)
            + jnp.dot(s, lo, preferred_element_type=F32))


def _rms(x, gain):
    return x * lax.rsqrt(jnp.mean(x * x, axis=-1, keepdims=True) + EPS) * gain


def _sigmoid(x):
    return 1.0 / (1.0 + jnp.exp(-x))


def _iota2(shape, axis):
    return lax.broadcasted_iota(jnp.int32, shape, axis)


def _head_ones():
    return (_iota2((GW, GW), 0) // HD == _iota2((GW, GW), 1) // HD).astype(F32)


def _head_sum(x, ones_bd):
    s = ones_bd.astype(BF16)
    hi = x.astype(BF16)
    lo = (x - hi.astype(F32)).astype(BF16)
    return jnp.dot(hi, s, preferred_element_type=F32) + jnp.dot(lo, s, preferred_element_type=F32)


def _norm_mm_kernel(x_ref, g_ref, w_ref, o_ref):
    o_ref[...] = _mm(_rms(x_ref[...], g_ref[...]), w_ref[...])


def norm_matmul(x, gain, w, *, tm):
    t, d = x.shape
    n = w.shape[1]
    return pl.pallas_call(
        _norm_mm_kernel,
        out_shape=jax.ShapeDtypeStruct((t, n), F32),
        grid=(t // tm,),
        in_specs=[pl.BlockSpec((tm, d), lambda i: (i, 0)),
                  pl.BlockSpec((1, d), lambda i: (0, 0)),
                  pl.BlockSpec((d, n), lambda i: (0, 0))],
        out_specs=pl.BlockSpec((tm, n), lambda i: (i, 0)),
        compiler_params=_cparams(("parallel",)),
        name="norm_matmul",
    )(x, gain, w)


def _mm_kernel(a_ref, w_ref, o_ref):
    o_ref[...] = _mm(a_ref[...], w_ref[...])


def matmul_layers(a, w, *, tm):
    t, k = a.shape
    layers, _, n = w.shape
    return pl.pallas_call(
        _mm_kernel,
        out_shape=jax.ShapeDtypeStruct((layers, t, n), F32),
        grid=(layers, t // tm),
        in_specs=[pl.BlockSpec((tm, k), lambda l, i: (i, 0)),
                  pl.BlockSpec((None, k, n), lambda l, i: (l, 0, 0))],
        out_specs=pl.BlockSpec((None, tm, n), lambda l, i: (l, i, 0)),
        compiler_params=_cparams(("parallel", "parallel")),
        name="matmul_layers",
    )(a, w)


def _mm_res_kernel(a_ref, w_ref, r_ref, o_ref):
    o_ref[...] = r_ref[...] + _mm(a_ref[...], w_ref[...])


def matmul_residual(a, w, res, *, tm):
    t, k = a.shape
    n = w.shape[1]
    return pl.pallas_call(
        _mm_res_kernel,
        out_shape=jax.ShapeDtypeStruct((t, n), F32),
        grid=(t // tm,),
        in_specs=[pl.BlockSpec((tm, k), lambda i: (i, 0)),
                  pl.BlockSpec((k, n), lambda i: (0, 0)),
                  pl.BlockSpec((tm, n), lambda i: (i, 0))],
        out_specs=pl.BlockSpec((tm, n), lambda i: (i, 0)),
        compiler_params=_cparams(("parallel",)),
        name="matmul_residual",
    )(a, w, res)


def _mix_out_q_kernel(a0_ref, a1_ref, a2_ref, a3_ref, w_ref, r_ref, g_ref, wq_ref, x_ref, q_ref):
    a = jnp.concatenate([a0_ref[...], a1_ref[...], a2_ref[...], a3_ref[...]], axis=-1)
    x = r_ref[...] + _mm(a, w_ref[...])
    x_ref[...] = x
    q_ref[...] = jnp.dot(_rms(x, g_ref[...]).astype(BF16), wq_ref[...], preferred_element_type=F32).astype(BF16)


def mix_out_q(parts, w, res, gain, w_q, *, tm):
    t, d = res.shape
    part = pl.BlockSpec((tm, GW), lambda i: (i, 0))
    mat = pl.BlockSpec((d, d), lambda i: (0, 0))
    tile = pl.BlockSpec((tm, d), lambda i: (i, 0))
    return pl.pallas_call(
        _mix_out_q_kernel,
        out_shape=(jax.ShapeDtypeStruct((t, d), F32), jax.ShapeDtypeStruct((t, d), BF16)),
        grid=(t // tm,),
        in_specs=[part, part, part, part, mat, tile, pl.BlockSpec((1, d), lambda i: (0, 0)), mat],
        out_specs=(tile, tile),
        compiler_params=_cparams(("parallel",)),
        name="mix_out_q",
    )(*parts, w, res, gain, w_q)


def _ffn_kernel(x_ref, g_ref, wu_ref, wd_ref, gf_ref, o_ref, *, final_norm):
    x = x_ref[...]
    h = jnp.dot(_rms(x, g_ref[...]).astype(BF16), wu_ref[...], preferred_element_type=F32)
    h = jnp.square(jnp.maximum(h, 0.0))
    y = x + jnp.dot(h.astype(BF16), wd_ref[...], preferred_element_type=F32)
    if final_norm:
        y = _rms(y, gf_ref[...])
    o_ref[...] = y


def ffn(x, gain, w_up, w_down, gain_final, *, tm, final_norm):
    t, d = x.shape
    ff = w_up.shape[1]
    resident = lambda shape: pl.BlockSpec(shape, lambda i: (0, 0), pipeline_mode=pl.Buffered(1))
    return pl.pallas_call(
        functools.partial(_ffn_kernel, final_norm=final_norm),
        out_shape=jax.ShapeDtypeStruct((t, d), F32),
        grid=(t // tm,),
        in_specs=[pl.BlockSpec((tm, d), lambda i: (i, 0)),
                  pl.BlockSpec((1, d), lambda i: (0, 0)),
                  resident((d, ff)), resident((ff, d)),
                  pl.BlockSpec((1, d), lambda i: (0, 0))],
        out_specs=pl.BlockSpec((tm, d), lambda i: (i, 0)),
        compiler_params=_cparams(("parallel",)),
        name="ffn",
    )(x, gain, w_up, w_down, gain_final)


def _attn_out_kernel(q_ref, k_ref, v_ref, wo_ref, r_ref, x_ref):
    sls = [slice(h * MEM_HD, (h + 1) * MEM_HD) for h in range(MEM_HEADS)]
    s = [_mm_nt(q_ref[:, sl], k_ref[:, sl]) * (MEM_HD ** -0.5) for sl in sls]
    p = [jnp.exp(sh - jnp.max(sh, axis=-1, keepdims=True)) for sh in s]
    p = [ph * (1.0 / jnp.sum(ph, axis=-1, keepdims=True)) for ph in p]
    heads = [_mm(ph, v_ref[:, sl]) for ph, sl in zip(p, sls)]
    x_ref[...] = r_ref[...] + _mm(jnp.concatenate(heads, axis=-1), wo_ref[...])


def cross_attention_out(q, mem_k, mem_v, layer, w_o, res, *, rows):
    t, d = res.shape
    bsz = mem_k.shape[1] // MEM_LEN
    lt = t // bsz // rows
    tile = pl.BlockSpec((rows, d), lambda b, l: (b * lt + l, 0))
    mem = pl.BlockSpec((None, MEM_LEN, d), lambda b, l: (layer, b, 0))
    return pl.pallas_call(
        _attn_out_kernel,
        out_shape=jax.ShapeDtypeStruct((t, d), F32),
        grid=(bsz, lt),
        in_specs=[tile, mem, mem, pl.BlockSpec((d, d), lambda b, l: (0, 0)), tile],
        out_specs=tile,
        compiler_params=_cparams(("parallel", "arbitrary")),
        name="cross_attention_out",
    )(q, mem_k, mem_v, w_o, res)


def _attn_cache_kernel(q_ref, k_ref, v_ref, o_ref, *, nb, rows):
    nr = MEM_HEADS * rows
    same_head = _iota2((nr, MEM_HEADS * MEM_LEN), 1) % MEM_HEADS == _iota2((nr, MEM_HEADS * MEM_LEN), 0) // rows
    q_all = q_ref[...].astype(F32)
    rng = range(nb)
    qs = [jnp.concatenate([q_all[b * rows:(b + 1) * rows, h * MEM_HD:(h + 1) * MEM_HD] for h in range(MEM_HEADS)],
                          axis=0) for b in rng]
    s = [_mm_nt(qs[b], k_ref[0, b].reshape(MEM_HEADS * MEM_LEN, MEM_HD)) * (MEM_HD ** -0.5) for b in rng]
    s = [jnp.where(same_head, sb, -1e30) for sb in s]
    p = [jnp.exp(sb - jnp.max(sb, axis=-1, keepdims=True)) for sb in s]
    p = [pb * (1.0 / jnp.sum(pb, axis=-1, keepdims=True)) for pb in p]
    o = [_mm(p[b], v_ref[0, b].reshape(MEM_HEADS * MEM_LEN, MEM_HD)) for b in rng]
    for b in rng:
        for h in range(MEM_HEADS):
            o_ref[b * rows:(b + 1) * rows, h * MEM_HD:(h + 1) * MEM_HD] = o[b][h * rows:(h + 1) * rows]


def cross_attention_cache(q, cache_k, cache_v, layer, *, nb, rows):
    t = q.shape[0]
    bsz = cache_k.shape[1]
    mem = pl.BlockSpec((1, nb, MEM_LEN, MEM_HEADS, MEM_HD), lambda b: (layer, b, 0, 0, 0))
    return pl.pallas_call(
        functools.partial(_attn_cache_kernel, nb=nb, rows=rows),
        out_shape=jax.ShapeDtypeStruct((t, D_MODEL), F32),
        grid=(bsz // nb,),
        in_specs=[pl.BlockSpec((nb * rows, D_MODEL), lambda b: (b, 0)), mem, mem],
        out_specs=pl.BlockSpec((nb * rows, D_MODEL), lambda b: (b, 0)),
        compiler_params=_cparams(("parallel",)),
        name="cross_attention_cache",
    )(q, cache_k, cache_v)


def _gelu_tanh(x):
    return 0.5 * x * (1.0 + jnp.tanh(math.sqrt(2.0 / math.pi) * (x + 0.044715 * (x * x * x))))


def _s5_kernel(u_ref, h0_ref, lam_ref, bblk_ref, cblk_ref, d_ref, wglu_ref, bglu_ref, gain_ref,
               y_ref, hfin_ref, scr_ref, tm_ref, *, steps, nb):
    @pl.when(pl.program_id(1) == 0)
    def _():
        scr_ref[0:nb, :] = h0_ref[...]

    def to_time_major(b, carry):
        for j in range(GW // LANES):
            tm_ref[j, pl.ds(b, steps, stride=nb), :] = u_ref[b, :, j * LANES:(j + 1) * LANES]
        return carry

    lax.fori_loop(0, nb, to_time_major, 0)
    u = jnp.concatenate([tm_ref[j] for j in range(GW // LANES)], axis=1)
    scr_ref[nb:, :] = _mm(u, bblk_ref[...])
    lam_re = lam_ref[0:1, :]
    lam_im = lam_ref[1:2, :]

    def step(t, carry):
        p0 = pl.multiple_of(t * nb, nb)
        c0 = pl.multiple_of((t + 1) * nb, nb)
        h_re = scr_ref[pl.ds(p0, nb), 0:S5_W]
        h_im = scr_ref[pl.ds(p0, nb), S5_W:]
        scr_ref[pl.ds(c0, nb), 0:S5_W] = scr_ref[pl.ds(c0, nb), 0:S5_W] + lam_re * h_re - lam_im * h_im
        scr_ref[pl.ds(c0, nb), S5_W:] = scr_ref[pl.ds(c0, nb), S5_W:] + lam_re * h_im + lam_im * h_re
        return carry

    lax.fori_loop(0, steps, step, 0)
    h_last = scr_ref[steps * nb:, :]
    hfin_ref[...] = h_last
    y = _mm(scr_ref[nb:, :], cblk_ref[...]) + d_ref[...] * u
    y = _gelu_tanh(y)
    y = y * _sigmoid(_mm(y, wglu_ref[...]) + bglu_ref[...])
    y = _rms(y, gain_ref[...])
    for j in range(GW // LANES):
        tm_ref[j] = y[:, j * LANES:(j + 1) * LANES]

    def to_batch_major(b, carry):
        for j in range(GW // LANES):
            y_ref[b, :, j * LANES:(j + 1) * LANES] = tm_ref[j, pl.ds(b, steps, stride=nb), :]
        return carry

    lax.fori_loop(0, nb, to_batch_major, 0)
    scr_ref[0:nb, :] = h_last


def s5_mixer(proj, h0, lam, bblk, cblk, dvec, wglu, bglu, gain, *, steps, nb):
    bsz, seq, _ = proj.shape
    const = lambda shape: pl.BlockSpec(shape, lambda b, t: (0, 0))
    return pl.pallas_call(
        functools.partial(_s5_kernel, steps=steps, nb=nb),
        out_shape=(jax.ShapeDtypeStruct((bsz, seq, GW), F32),
                   jax.ShapeDtypeStruct((bsz, 2 * S5_W), F32)),
        grid=(bsz // nb, seq // steps),
        in_specs=[pl.BlockSpec((nb, steps, GW), lambda b, t: (b, t, 0)),
                  pl.BlockSpec((nb, 2 * S5_W), lambda b, t: (b, 0)),
                  const((2, S5_W)), const((GW, 2 * S5_W)), const((2 * S5_W, GW)), const((1, GW)),
                  const((GW, GW)), const((1, GW)), const((1, GW))],
        out_specs=(pl.BlockSpec((nb, steps, GW), lambda b, t: (b, t, 0)),
                   pl.BlockSpec((nb, 2 * S5_W), lambda b, t: (b, 0))),
        scratch_shapes=[pltpu.VMEM(((steps + 1) * nb, 2 * S5_W), F32),
                        pltpu.VMEM((GW // LANES, steps * nb, LANES), F32)],
        compiler_params=_cparams(("parallel", "arbitrary")),
        name="s5_mixer",
    )(proj, h0, lam, bblk, cblk, dvec, wglu, bglu, gain)


def s5_params(lam_re, lam_im, b_re, b_im, c_re, c_im, log_step):
    step = jnp.exp(log_step)[:, None]
    mag = jnp.exp(lam_re * step)
    lbar_re = mag * jnp.cos(lam_im * step)
    lbar_im = mag * jnp.sin(lam_im * step)
    den = lam_re * lam_re + lam_im * lam_im
    f_re = ((lbar_re - 1.0) * lam_re + lbar_im * lam_im) / den
    f_im = (lbar_im * lam_re - (lbar_re - 1.0) * lam_im) / den
    bbar_re = f_re[..., None] * b_re - f_im[..., None] * b_im
    bbar_im = f_re[..., None] * b_im + f_im[..., None] * b_re
    eye_g = jnp.eye(S5_NG, dtype=F32)

    def in_blk(m):
        return jnp.einsum('gph,gk->ghkp', m, eye_g).reshape(GW, S5_W)

    def out_blk(m):
        return jnp.einsum('ghp,gk->gpkh', m, eye_g).reshape(S5_W, GW)

    bblk = jnp.concatenate([in_blk(bbar_re), in_blk(bbar_im)], axis=1)
    cblk = jnp.concatenate([out_blk(c_re), -out_blk(c_im)], axis=0)
    lam2 = jnp.stack([lbar_re.reshape(S5_W), lbar_im.reshape(S5_W)])
    return lam2, bblk.astype(BF16), cblk.astype(BF16)


def s5_apply(proj, h0_re, h0_im, p, l, *, steps, nb):
    bsz = proj.shape[0]
    lam2, bblk, cblk = s5_params(p['s5_lam_re'][l], p['s5_lam_im'][l], p['s5_b_re'][l], p['s5_b_im'][l],
                                 p['s5_c_re'][l], p['s5_c_im'][l], p['s5_log_step'][l])
    h0 = jnp.concatenate([h0_re.reshape(bsz, S5_W), h0_im.reshape(bsz, S5_W)], axis=1)
    y, h = s5_mixer(proj, h0, lam2, bblk, cblk, p['s5_d'][l].reshape(1, GW),
                    p['s5_w_glu'][l].astype(BF16), p['s5_b_glu'][l].reshape(1, GW),
                    p['s5_norm'][l].reshape(1, GW), steps=steps, nb=nb)
    return y, h[:, :S5_W].reshape(bsz, S5_NG, S5_P), h[:, S5_W:].reshape(bsz, S5_NG, S5_P)


def _load_state(s0_ref, st_ref, nb, transpose):
    for b in range(nb):
        rows = []
        for h in range(NH):
            pieces = [s0_ref[b, h]]
            if h:
                pieces.insert(0, jnp.zeros((HD, h * HD), F32))
            if h < NH - 1:
                pieces.append(jnp.zeros((HD, (NH - 1 - h) * HD), F32))
            rows.append(jnp.concatenate(pieces, axis=1))
        st = jnp.concatenate(rows, axis=0)
        st_ref[b] = st.T if transpose else st


def _store_state(st_ref, sfin_ref, nb, transpose, own_layer):
    if own_layer is not None:
        for other in range(DEPTH):
            if other != own_layer:
                sfin_ref[other] = jnp.zeros(sfin_ref.shape[1:], F32)
        sfin_ref = sfin_ref.at[own_layer]
    for b in range(nb):
        st = st_ref[b].T if transpose else st_ref[b]
        for h in range(NH):
            sfin_ref[b, h] = st[h * HD:(h + 1) * HD, h * HD:(h + 1) * HD]


def _head_masks():
    lane_head = _iota2((1, GW), 1) // HD
    return [(lane_head == h).astype(F32) for h in range(NH)]


def _stack_heads(x, masks):
    return jnp.concatenate([x * m for m in masks], axis=0)


def _pad_rows(x, rows):
    if x.shape[0] == rows:
        return x
    return jnp.concatenate([x, jnp.zeros((rows - x.shape[0], x.shape[1]), x.dtype)], axis=0)


def _silu(x):
    return x * _sigmoid(x)


def _chunk_rows(ref, b, r0, rows, c):
    return _pad_rows(ref[b, pl.ds(r0, rows), :], c)


def _ret_kernel(q_ref, k_ref, v_ref, g_ref, cos_ref, sin_ref, s0_ref, _states_in, o_ref, sfin_ref, st_ref,
                *, nb, tb, c, c_real, own_layer):
    rows = min(tb, c)

    @pl.when(pl.program_id(1) == 0)
    def _():
        _load_state(s0_ref, st_ref, nb, transpose=False)

    masks = _head_masks()
    ones_bd = _head_ones()
    lane_head = _iota2((1, GW), 1) // HD
    log_gamma = jnp.zeros((1, GW), F32)
    for h in range(NH):
        log_gamma = jnp.where(lane_head == h, math.log(1.0 - 2.0 ** (-5.0 - h)), log_gamma)
    tt = _iota2((c, GW), 0).astype(F32)
    scale = HD ** -0.5
    g_q = jnp.exp(log_gamma * (tt + 1.0))
    g_k = jnp.exp(-log_gamma * (tt + 1.0)) * scale
    g_tail = jnp.exp(log_gamma * (c_real - 1.0 - tt)) * scale
    row_head = _iota2((GW, GW), 0) // HD
    g_chunk = jnp.zeros((GW, GW), F32)
    for h in range(NH):
        g_chunk = jnp.where(row_head == h, math.exp(math.log(1.0 - 2.0 ** (-5.0 - h)) * c_real), g_chunk)
    causal = _iota2((c, NH * c), 0) >= _iota2((c, NH * c), 1) % c
    first_half = _iota2((c, GW), 1) % HD < HD // 2

    def rope(x, cos, sin):
        swapped = jnp.where(first_half, pltpu.roll(x, GW - HD // 2, axis=1), pltpu.roll(x, HD // 2, axis=1))
        return x * cos + swapped * sin

    def chunk(ci, carry):
        r0 = pl.multiple_of(ci * rows, rows)
        cos = _pad_rows(cos_ref[pl.ds(r0, rows), :], c)
        sin = _pad_rows(sin_ref[pl.ds(r0, rows), :], c)
        rng = range(nb)
        qt = [rope(_chunk_rows(q_ref, b, r0, rows, c), cos, sin) * g_q for b in rng]
        k = [rope(_chunk_rows(k_ref, b, r0, rows, c), cos, sin) for b in rng]
        v = [_chunk_rows(v_ref, b, r0, rows, c) for b in rng]
        st = [st_ref[b] for b in rng]
        sc = [_mm_nt(qt[b], _stack_heads(k[b] * g_k, masks)) for b in rng]
        o_in = [_mm(jnp.where(causal, sc[b], 0.0), _stack_heads(v[b], masks)) for b in rng]
        o_st = [_mm(qt[b], st[b]) for b in rng]
        d_st = [_mm_tn(k[b] * g_tail, v[b]) for b in rng]
        o = [o_in[b] + o_st[b] for b in rng]
        ms = [_head_sum(o[b] * o[b], ones_bd) for b in rng]
        for b in rng:
            st_ref[b] = st[b] * g_chunk + ones_bd * d_st[b]
            ob = o[b] * lax.rsqrt(ms[b] * (1.0 / HD) + EPS) * _silu(_chunk_rows(g_ref, b, r0, rows, c))
            o_ref[b, pl.ds(r0, rows), :] = ob[:rows]
        return carry

    lax.fori_loop(0, tb // rows, chunk, 0)

    @pl.when(pl.program_id(1) == pl.num_programs(1) - 1)
    def _():
        _store_state(st_ref, sfin_ref, nb, transpose=False, own_layer=own_layer)


def _rope_tables(pos0, seq):
    half = HD // 2
    inv = ROPE_BASE ** (-jnp.arange(half, dtype=F32) / half)
    pos = pos0 + jnp.arange(seq, dtype=jnp.int32)
    ang = pos.astype(F32)[:, None] * inv[None, :]
    cos, sin = jnp.cos(ang), jnp.sin(ang)
    return (jnp.tile(jnp.concatenate([cos, cos], axis=-1), (1, NH)),
            jnp.tile(jnp.concatenate([-sin, sin], axis=-1), (1, NH)))


def _proj_spec(nb, tb, col):
    return pl.BlockSpec((nb, tb, GW), lambda b, t: (b, t, col))


def _state_spec(nb, layer):
    return pl.BlockSpec((None, nb, NH, HD, HD), lambda b, t: (layer, b, 0, 0, 0))


_ALIASED = pl.BlockSpec(memory_space=pl.ANY)


def _collector(states, bsz, nb, layer, operand_index):
    shape = jax.ShapeDtypeStruct((DEPTH, bsz, NH, HD, HD), F32)
    if states is None:
        spec = pl.BlockSpec((DEPTH, nb, NH, HD, HD), lambda b, t: (0, b, 0, 0, 0))
        return jnp.zeros((1, 1, NH, HD, HD), F32), spec, shape, {}, layer
    return states, _state_spec(nb, layer), shape, {operand_index: 1}, None


def retention_apply(proj, s0, s0_layer, states, layer, pos0, *, tb, nb):
    bsz, seq, _ = proj.shape
    c = RET_CHUNK if tb >= RET_CHUNK else CHUNK
    cos, sin = _rope_tables(pos0, seq)
    states, st_spec, st_shape, aliases, own_layer = _collector(states, bsz, nb, layer, 7)
    return pl.pallas_call(
        functools.partial(_ret_kernel, nb=nb, tb=tb, c=c, c_real=min(tb, c), own_layer=own_layer),
        out_shape=(jax.ShapeDtypeStruct((bsz, seq, GW), F32), st_shape),
        grid=(bsz // nb, seq // tb),
        in_specs=[_proj_spec(nb, tb, 5), _proj_spec(nb, tb, 6), _proj_spec(nb, tb, 7), _proj_spec(nb, tb, 8),
                  pl.BlockSpec((tb, GW), lambda b, t: (t, 0)), pl.BlockSpec((tb, GW), lambda b, t: (t, 0)),
                  _state_spec(nb, s0_layer), _ALIASED],
        out_specs=(pl.BlockSpec((nb, tb, GW), lambda b, t: (b, t, 0)), st_spec),
        scratch_shapes=[pltpu.VMEM((nb, GW, GW), F32)],
        input_output_aliases=aliases,
        compiler_params=_cparams(("parallel", "arbitrary")),
        name="retention",
    )(proj, proj, proj, proj, cos, sin, s0, states)


def _log_sigmoid(z):
    return jnp.minimum(z, 0.0) - jnp.log(1.0 + jnp.exp(-jnp.abs(z)))


def _hgrn_kernel(q_ref, f_ref, i_ref, g_ref, lb_ref, gain_ref, s0_ref, _states_in, o_ref, sfin_ref, st_ref,
                 *, nb, tb, c, own_layer):
    rows = min(tb, c)

    @pl.when(pl.program_id(1) == 0)
    def _():
        _load_state(s0_ref, st_ref, nb, transpose=True)

    t_hi = -(-rows // SUBLANES) * SUBLANES
    ones_bd = _head_ones()
    tri = (_iota2((c, c), 0) >= _iota2((c, c), 1)).astype(F32)
    t_idx = _iota2((c, GW), 0)
    lb = lb_ref[...]
    log_lb = jnp.log(lb)
    log_1m_lb = jnp.log(1.0 - lb)
    gain = gain_ref[...]

    def chunk(ci, carry):
        r0 = pl.multiple_of(ci * rows, rows)
        rng = range(nb)
        valid = t_idx < rows

        def gates(b):
            z = _chunk_rows(f_ref, b, r0, rows, c)
            ls_pos = _log_sigmoid(z)
            ls_neg = ls_pos - z
            b2 = log_lb + ls_neg
            log_f = jnp.maximum(ls_pos, b2) + jnp.log(1.0 + jnp.exp(-jnp.abs(ls_pos - b2)))
            return (jnp.where(valid, log_f, 0.0), jnp.where(valid, (1.0 - lb) * jnp.exp(ls_neg), 0.0),
                    ls_neg + log_1m_lb)

        log_f, key, log_key = zip(*[gates(b) for b in rng])
        q = [_silu(_chunk_rows(q_ref, b, r0, rows, c)) for b in rng]
        v = [_chunk_rows(i_ref, b, r0, rows, c) for b in rng]
        st = [st_ref[b] for b in rng]
        cum = [_mm_exact_lhs(tri, log_f[b]) for b in rng]
        last = [cum[b][c - 1:c, :] for b in rng]
        cum2 = [cum[b] * LOG2_E for b in rng]
        kd2 = [(log_key[b] - cum[b]) * LOG2_E for b in rng]

        def pair_rows(b, s):
            lo = SUBLANES * (s // SUBLANES)
            t_rows = _iota2((t_hi - lo, GW), 0) + lo
            key_decay = jnp.where(t_rows >= s, jnp.exp2(cum2[b][lo:t_hi] + kd2[b][s:s + 1, :]), 0.0)
            return key_decay * q[b][lo:t_hi]

        sc = [_mm(jnp.concatenate([pair_rows(b, s) for s in range(rows)], axis=0), ones_bd) for b in rng]
        o_st = [_mm_nt(q[b] * jnp.exp(cum[b]), st[b]) for b in rng]
        d_st = [_mm_tn(v[b], key[b] * jnp.exp(last[b] - cum[b])) for b in rng]

        def intra(b):
            tiles = [None] * (c // SUBLANES)
            off = 0
            for s in range(rows):
                for ti in range(s // SUBLANES, t_hi // SUBLANES):
                    term = sc[b][off:off + SUBLANES, :] * v[b][s:s + 1, :]
                    tiles[ti] = term if tiles[ti] is None else tiles[ti] + term
                    off += SUBLANES
            zero = jnp.zeros((SUBLANES, GW), F32)
            return jnp.concatenate([zero if t is None else t for t in tiles], axis=0)

        o = [o_st[b] + intra(b) for b in rng]
        ms = [_head_sum(o[b] * o[b], ones_bd) for b in rng]
        for b in rng:
            st_ref[b] = st[b] * jnp.exp(last[b]) + ones_bd * d_st[b]
            ob = o[b] * lax.rsqrt(ms[b] * (1.0 / HD) + EPS) * gain * _silu(_chunk_rows(g_ref, b, r0, rows, c))
            o_ref[b, pl.ds(r0, rows), :] = ob[:rows]
        return carry

    lax.fori_loop(0, tb // rows, chunk, 0)

    @pl.when(pl.program_id(1) == pl.num_programs(1) - 1)
    def _():
        _store_state(st_ref, sfin_ref, nb, transpose=True, own_layer=own_layer)


def hgrn_apply(proj, s0, s0_layer, states, layer, lb, gain, *, tb, nb):
    bsz, seq, _ = proj.shape
    row = pl.BlockSpec((1, GW), lambda b, t: (0, 0))
    states, st_spec, st_shape, aliases, own_layer = _collector(states, bsz, nb, layer, 7)
    return pl.pallas_call(
        functools.partial(_hgrn_kernel, nb=nb, tb=tb, c=CHUNK, own_layer=own_layer),
        out_shape=(jax.ShapeDtypeStruct((bsz, seq, GW), F32), st_shape),
        grid=(bsz // nb, seq // tb),
        in_specs=[_proj_spec(nb, tb, 1), _proj_spec(nb, tb, 2), _proj_spec(nb, tb, 3), _proj_spec(nb, tb, 4),
                  row, row, _state_spec(nb, s0_layer), _ALIASED],
        out_specs=(pl.BlockSpec((nb, tb, GW), lambda b, t: (b, t, 0)), st_spec),
        scratch_shapes=[pltpu.VMEM((nb, GW, GW), F32)],
        input_output_aliases=aliases,
        compiler_params=_cparams(("parallel", "arbitrary")),
        name="hgrn2",
    )(proj, proj, proj, proj, lb.reshape(1, GW), gain.reshape(1, GW), s0, states)


def _softplus(z):
    return jnp.maximum(z, 0.0) + jnp.log(1.0 + jnp.exp(-jnp.abs(z)))


def _rwkv_prepare(ins, consts):
    masks, ones_bd, bd_mask, eye_flat, tri, strict, incl = consts
    rng = range(len(ins))
    r, lw, k, v, kk, a = [[i[j] for i in ins] for j in range(6)]
    c = r[0].shape[0]
    n = NH * c

    def to_bd(flat):
        return jnp.concatenate([flat] * NH, axis=0) * bd_mask

    cum = [_mm_exact_lhs(tri, lw[i]) for i in rng]
    last = [cum[i][c - 1:c, :] for i in rng]
    p_inv = [jnp.exp(-cum[i]) for i in rng]
    p_tail = [jnp.exp(last[i] - cum[i]) for i in rng]
    ka = [kk[i] * a[i] for i in rng]
    x = [jnp.concatenate([kk[i] * jnp.exp(cum[i] - lw[i]), r[i] * jnp.exp(cum[i])], axis=0) for i in rng]
    g = [_mm_nt(x[i], jnp.concatenate([_stack_heads(ka[i] * p_inv[i], masks),
                                       _stack_heads(k[i] * p_inv[i], masks)], axis=0)) for i in rng]
    m_ak = [jnp.where(strict, g[i][:c], 0.0) for i in rng]
    n_ak = [jnp.where(incl, g[i][c:], 0.0) for i in rng]
    pw = [m_ak[i][:, :n] for i in rng]
    t_inv = [eye_flat - pw[i] for i in rng]
    pw_bd = [to_bd(pw[i]) for i in rng]
    for _ in range(int(math.log2(c)) - 1):
        pw = [_mm(pw[i], pw_bd[i]) for i in rng]
        pw_bd = [to_bd(pw[i]) for i in rng]
        t_inv = [t_inv[i] + _mm(t_inv[i], pw_bd[i]) for i in rng]
    v_stk = [_stack_heads(v[i], masks) for i in rng]
    zeros = jnp.zeros((n, GW), F32)
    mkv = [_mm(m_ak[i], jnp.concatenate([zeros, v_stk[i]], axis=0)) for i in rng]
    tmkv = [_mm(t_inv[i], _stack_heads(mkv[i], masks)) for i in rng]
    return [dict(x=x[i], t_inv=t_inv[i], tmkv=tmkv[i], n_ak=n_ak[i], v=v[i], v_stk=v_stk[i],
                 decay=jnp.exp(last[i]),
                 k_tail=jnp.concatenate([ka[i] * p_tail[i], k[i] * p_tail[i]], axis=0)) for i in rng]


def _rwkv_advance(prep, st, consts):
    masks, ones_bd = consts[0], consts[1]
    rng = range(len(prep))
    c = prep[0]['t_inv'].shape[0]
    xa = [_mm_nt(prep[i]['x'], st[i]) for i in rng]
    u = [-_mm(prep[i]['t_inv'], _stack_heads(xa[i][:c], masks)) - prep[i]['tmkv'] for i in rng]
    nuv = [_mm(prep[i]['n_ak'], jnp.concatenate([_stack_heads(u[i], masks), prep[i]['v_stk']], axis=0))
           for i in rng]
    d_uv = [_mm_tn(jnp.concatenate([u[i], prep[i]['v']], axis=0), prep[i]['k_tail']) for i in rng]
    return [(xa[i][c:] + nuv[i], st[i] * prep[i]['decay'] + ones_bd * d_uv[i]) for i in rng]


def _rwkv_kernel(x_r_ref, x_k_ref, x_v_ref, x_l_ref, sh0_ref, s0_ref, mu_ref, w0_ref, ww_ref, a0_ref,
                 wa_ref, wg_ref, kk_ref, ka_ref, rk_ref, lng_ref, lnb_ref, _states_in,
                 o_ref, sfin_ref, shfin_ref,
                 st_ref, sh_ref, r_s, lw_s, k_s, v_s, kkn_s, a_s, y_s, bonus_s, gate_s, *, nb, tb, c, own_layer):
    tbp = max(tb, c)

    @pl.when(pl.program_id(1) == 0)
    def _():
        _load_state(s0_ref, st_ref, nb, transpose=False)
        sh_ref[...] = sh0_ref[...]

    masks = _head_masks()
    ones_bd = _head_ones()
    n = NH * c
    row_t = _iota2((c, 2 * n), 0)
    col_t = _iota2((c, 2 * n), 1) % c
    bd_mask = (_iota2((n, n), 0) // c == _iota2((n, n), 1) // c).astype(F32)
    eye_flat = (_iota2((c, n), 1) % c == _iota2((c, n), 0)).astype(F32)
    consts = (masks, ones_bd, bd_mask, eye_flat, (_iota2((c, c), 0) >= _iota2((c, c), 1)).astype(F32),
              col_t < row_t, col_t <= row_t)
    first_row = _iota2((tb, GW), 0) == 0

    for b in range(nb):
        def mixed(x_ref, j):
            x = x_ref[b]
            prev = jnp.where(first_row, sh_ref[b, j:j + 1, :], pltpu.roll(x, 1, axis=0))
            sh_ref[b, j:j + 1, :] = x[tb - 1:tb, :]
            return x + (prev - x) * mu_ref[j:j + 1, :]

        r = mixed(x_r_ref, 0)
        k = mixed(x_k_ref, 1)
        v = mixed(x_v_ref, 2)
        xl = mixed(x_l_ref, 3)
        log_w = -_softplus(-(w0_ref[...] + _mm(jnp.tanh(xl), ww_ref[...]))) - 0.5
        a = _sigmoid(a0_ref[...] + _mm(xl, wa_ref[...]))
        gate = _mm(_sigmoid(xl), wg_ref[...])
        kk = k * kk_ref[...]
        kk = kk * lax.rsqrt(jnp.maximum(_head_sum(kk * kk, ones_bd), 1e-24))
        k = k * (1.0 + (a - 1.0) * ka_ref[...])
        r_s[b, 0:tb, :] = r
        lw_s[b, 0:tb, :] = -jnp.exp(log_w)
        k_s[b, 0:tb, :] = k
        v_s[b, 0:tb, :] = v
        kkn_s[b, 0:tb, :] = kk
        a_s[b, 0:tb, :] = a
        bonus_s[b] = _head_sum(r * k * rk_ref[...], ones_bd) * v
        gate_s[b] = gate
        if tbp > tb:
            zeros = jnp.zeros((tbp - tb, GW), F32)
            for s in (r_s, lw_s, k_s, v_s, kkn_s, a_s):
                s[b, tb:tbp, :] = zeros

    n_chunks = tbp // c
    group = 4 if n_chunks % 4 == 0 else 1

    def chunks(gi, carry):
        r0 = [pl.multiple_of((gi * group + j) * c, c) for j in range(group)]
        prep = _rwkv_prepare([tuple(s[b, pl.ds(r0[j], c), :] for s in (r_s, lw_s, k_s, v_s, kkn_s, a_s))
                              for j in range(group) for b in range(nb)], consts)
        st = [st_ref[b] for b in range(nb)]
        for j in range(group):
            outs = _rwkv_advance(prep[j * nb:(j + 1) * nb], st, consts)
            st = [o[1] for o in outs]
            for b in range(nb):
                y_s[b, pl.ds(r0[j], c), :] = outs[b][0]
        for b in range(nb):
            st_ref[b] = st[b]
        return carry

    lax.fori_loop(0, n_chunks // group, chunks, 0)

    for b in range(nb):
        y = y_s[b, 0:tb, :]
        mean = _head_sum(y, ones_bd) * (1.0 / HD)
        d = y - mean
        var = _head_sum(d * d, ones_bd) * (1.0 / HD)
        y = d * lax.rsqrt(var + RW_LN_EPS) * lng_ref[...] + lnb_ref[...]
        o_ref[b] = (y + bonus_s[b]) * gate_s[b]
    shfin_ref[...] = sh_ref[...]

    @pl.when(pl.program_id(1) == pl.num_programs(1) - 1)
    def _():
        _store_state(st_ref, sfin_ref, nb, transpose=False, own_layer=own_layer)


def rwkv_apply(proj, s0, s0_layer, states, shift0, p, l, *, tb, nb):
    bsz, seq, _ = proj.shape
    states, st_spec, st_shape, aliases, own_layer = _collector(states, bsz, nb, l, 17)
    c = CHUNK
    tbp = max(tb, c)
    row = pl.BlockSpec((1, GW), lambda b, t: (0, 0))
    mat = pl.BlockSpec((GW, GW), lambda b, t: (0, 0))
    sh_spec = pl.BlockSpec((nb, 4, GW), lambda b, t: (b, 0, 0))
    zeros = lambda r: jnp.zeros((r, GW), F32)
    ww = jnp.concatenate([p['rw_w_w2'][l], zeros(192)], axis=0).astype(BF16)
    wa = jnp.concatenate([zeros(64), p['rw_w_a2'][l], zeros(128)], axis=0).astype(BF16)
    wg = jnp.concatenate([zeros(128), p['rw_w_g2'][l]], axis=0).astype(BF16)
    r1 = lambda name: p[name][l].reshape(1, GW)
    seq_buf = pltpu.VMEM((nb, tbp, GW), F32)
    blk_buf = pltpu.VMEM((nb, tb, GW), F32)
    y, st, sh = pl.pallas_call(
        functools.partial(_rwkv_kernel, nb=nb, tb=tb, c=c, own_layer=own_layer),
        out_shape=(jax.ShapeDtypeStruct((bsz, seq, GW), F32), st_shape,
                   jax.ShapeDtypeStruct((bsz, 4, GW), F32)),
        grid=(bsz // nb, seq // tb),
        in_specs=[_proj_spec(nb, tb, 9), _proj_spec(nb, tb, 10), _proj_spec(nb, tb, 11), _proj_spec(nb, tb, 12),
                  sh_spec, _state_spec(nb, s0_layer), pl.BlockSpec((4, GW), lambda b, t: (0, 0)),
                  row, mat, row, mat, mat, row, row, row, row, row, _ALIASED],
        out_specs=(pl.BlockSpec((nb, tb, GW), lambda b, t: (b, t, 0)), st_spec, sh_spec),
        scratch_shapes=[pltpu.VMEM((nb, GW, GW), F32), pltpu.VMEM((nb, 4, GW), F32),
                        seq_buf, seq_buf, seq_buf, seq_buf, seq_buf, seq_buf, seq_buf, blk_buf, blk_buf],
        input_output_aliases=aliases,
        compiler_params=_cparams(("parallel", "arbitrary")),
        name="rwkv7",
    )(proj, proj, proj, proj, shift0.reshape(bsz, 4, GW), s0, p['rw_mu'][l].reshape(4, GW),
      r1('rw_w0'), ww, r1('rw_a0'), wa, wg, r1('rw_k_k'), r1('rw_k_a'), r1('rw_r_k'), r1('rw_ln_g'), r1('rw_ln_b'),
      states)
    return y, st, sh.reshape(bsz, RW_PROJ)


TOKEN_TILE = 512
TIME_BLOCK = 128
CACHE_ATTN_SEQS = 4


def _tiles(bsz, seq):
    if seq >= TIME_BLOCK:
        return dict(tb=TIME_BLOCK, s5_nb=bsz, mix_nb=bsz, rw_nb=bsz)
    return dict(tb=seq, s5_nb=bsz, mix_nb=8, rw_nb=16)


def _trunk_layer(x, bsz, seq, pos0, attend, st, mats, p, wb, l, lb, final_norm):
    cfg = _tiles(bsz, seq)
    tm = TOKEN_TILE
    row = lambda name: p[name][l].reshape(1, -1)
    proj = norm_matmul(x, row('norm_mix'), wb['w_in'][l], tm=tm)
    proj = proj.reshape(bsz, seq, IN_WIDTH)
    y_s5, s5_re, s5_im = s5_apply(proj, st['s5_re'], st['s5_im'], p, l, steps=cfg['tb'], nb=cfg['s5_nb'])
    y_hg, hg_s = hgrn_apply(proj, st['hgrn'], st['layer'], mats['hgrn'], l, lb, p['hg_norm'][l],
                            tb=cfg['tb'], nb=cfg['mix_nb'])
    y_rt, rt_s = retention_apply(proj, st['ret'], st['layer'], mats['ret'], l, pos0,
                                 tb=cfg['tb'], nb=cfg['mix_nb'])
    y_rw, rw_s, shift = rwkv_apply(proj, st['rwkv'], st['layer'], mats['rwkv'], st['shift'], p, l,
                                   tb=cfg['tb'], nb=cfg['rw_nb'])
    parts = [y.reshape(bsz * seq, GW) for y in (y_s5, y_hg, y_rt, y_rw)]
    x, q = mix_out_q(parts, wb['w_out'][l], x, row('norm_mem'), wb['mem_w_q'][l], tm=tm)
    x = attend(q, wb['mem_w_o'][l], x)
    x = ffn(x, row('norm_ffn'), wb['ffn_w_up'][l], wb['ffn_w_down'][l], p['norm_final'].reshape(1, -1),
            tm=tm, final_norm=final_norm)
    return x, (s5_re, s5_im, shift), dict(hgrn=hg_s, ret=rt_s, rwkv=rw_s)


def kernel(x_prompt, x_sample, mem_prompt, state_s5_re, state_s5_im, state_hgrn, state_ret, state_rwkv,
           state_rwkv_shift, cache_mem_k, cache_mem_v, norm_mix, w_in, w_out, s5_lam_re, s5_lam_im,
           s5_b_re, s5_b_im, s5_c_re, s5_c_im, s5_d, s5_log_step, s5_w_glu, s5_b_glu, s5_norm,
           hg_lb_logits, hg_norm, rw_mu, rw_w0, rw_w_w2, rw_a0, rw_w_a2, rw_w_g2, rw_k_k, rw_k_a, rw_r_k,
           rw_ln_g, rw_ln_b, norm_mem, mem_w_q, mem_w_k, mem_w_v, mem_w_o, norm_ffn, ffn_w_up, ffn_w_down,
           norm_final):
    p = dict(norm_mix=norm_mix, s5_lam_re=s5_lam_re, s5_lam_im=s5_lam_im, s5_b_re=s5_b_re, s5_b_im=s5_b_im,
             s5_c_re=s5_c_re, s5_c_im=s5_c_im, s5_d=s5_d, s5_log_step=s5_log_step, s5_w_glu=s5_w_glu,
             s5_b_glu=s5_b_glu, s5_norm=s5_norm, hg_norm=hg_norm, rw_mu=rw_mu, rw_w0=rw_w0, rw_w_w2=rw_w_w2,
             rw_a0=rw_a0, rw_w_a2=rw_w_a2, rw_w_g2=rw_w_g2, rw_k_k=rw_k_k, rw_k_a=rw_k_a, rw_r_k=rw_r_k,
             rw_ln_g=rw_ln_g, rw_ln_b=rw_ln_b, norm_mem=norm_mem, norm_ffn=norm_ffn, norm_final=norm_final)
    wb = {name: w.astype(BF16) for name, w in dict(
        w_in=w_in, w_out=w_out, mem_w_q=mem_w_q, mem_w_k=mem_w_k, mem_w_v=mem_w_v, mem_w_o=mem_w_o,
        ffn_w_up=ffn_w_up, ffn_w_down=ffn_w_down).items()}
    lb_all = jnp.cumsum(jax.nn.softmax(hg_lb_logits.astype(F32), axis=0), axis=0)
    lb_all = lb_all - lb_all[0:1]

    bp, lp, _ = x_prompt.shape
    bs, ls, _ = x_sample.shape
    yp = x_prompt.reshape(bp * lp, D_MODEL)
    ys = x_sample.reshape(bs * ls, D_MODEL)
    mem2d = mem_prompt.reshape(bp * MEM_LEN, D_MODEL)
    mat_zero = jnp.zeros((1, bp, NH, HD, HD), F32)
    p_small, s_small = [], []
    p_mats = dict(hgrn=None, ret=None, rwkv=None)
    s_mats = dict(hgrn=None, ret=None, rwkv=None)
    mk = matmul_layers(mem2d, wb['mem_w_k'], tm=TOKEN_TILE)
    mv = matmul_layers(mem2d, wb['mem_w_v'], tm=TOKEN_TILE)
    for l in range(DEPTH):
        final = l == DEPTH - 1
        def attend_p(q, w_o, x, l=l):
            return cross_attention_out(q, mk, mv, l, w_o, x, rows=TOKEN_TILE)

        zero_state = dict(s5_re=jnp.zeros((bp, S5_NG, S5_P), F32), s5_im=jnp.zeros((bp, S5_NG, S5_P), F32),
                          shift=jnp.zeros((bp, RW_PROJ), F32), hgrn=mat_zero, ret=mat_zero, rwkv=mat_zero, layer=0)
        yp, small, p_mats = _trunk_layer(yp, bp, lp, 0, attend_p, zero_state, p_mats, p, wb, l, lb_all[l], final)
        p_small.append(small)
        sst = dict(s5_re=state_s5_re[l], s5_im=state_s5_im[l], shift=state_rwkv_shift[l],
                   hgrn=state_hgrn, ret=state_ret, rwkv=state_rwkv, layer=l)
        def attend_s(q, w_o, x, l=l):
            o = cross_attention_cache(q, cache_mem_k, cache_mem_v, l, nb=CACHE_ATTN_SEQS, rows=ls)
            return matmul_residual(o, w_o, x, tm=TOKEN_TILE)

        ys, small, s_mats = _trunk_layer(ys, bs, ls, PAST_LEN, attend_s, sst, s_mats, p, wb, l, lb_all[l], final)
        s_small.append(small)
    stack = lambda states, i: jnp.stack([s[i] for s in states])
    return (yp.reshape(bp, lp, D_MODEL), ys.reshape(bs, ls, D_MODEL),
            stack(p_small, 0), stack(p_small, 1), p_mats['hgrn'], p_mats['ret'], p_mats['rwkv'], stack(p_small, 2),
            mk.reshape(DEPTH, bp, MEM_LEN, MEM_HEADS, MEM_HD), mv.reshape(DEPTH, bp, MEM_LEN, MEM_HEADS, MEM_HD),
            stack(s_small, 0), stack(s_small, 1), s_mats['hgrn'], s_mats['ret'], s_mats['rwkv'], stack(s_small, 2))
```

```python
import functools
import math

import jax
import jax.numpy as jnp
from jax import lax
from jax.experimental import pallas as pl
from jax.experimental.pallas import tpu as pltpu

F32 = jnp.float32
BF16 = jnp.bfloat16

D_MODEL = 1024
DEPTH = 2
PAST_LEN = 16384
GW = 256
HD = 64
NH = GW // HD
S5_GROUP = 16
S5_NG = GW // S5_GROUP
S5_P = 64
S5_W = S5_NG * S5_P
RW_PROJ = 4 * GW
IN_WIDTH = 13 * GW
MEM_LEN = 256
MEM_HEADS = 4
MEM_HD = D_MODEL // MEM_HEADS
D_FF = 4 * D_MODEL
EPS = 1e-6
RW_LN_EPS = 64e-5
ROPE_BASE = 10000.0
CHUNK = 16
RET_CHUNK = 64

VMEM_LIMIT = 56 * 1024 * 1024
SUBLANES = 8
LANES = 128
LOG2_E = 1.4426950408889634


def _cparams(sem):
    return pltpu.CompilerParams(dimension_semantics=sem, vmem_limit_bytes=VMEM_LIMIT)


def _mm(a, b):
    return jnp.dot(a.astype(BF16), b.astype(BF16), preferred_element_type=F32)


def _mm_nt(a, b):
    return lax.dot_general(a.astype(BF16), b.astype(BF16), (((1,), (1,)), ((), ())),
                           preferred_element_type=F32)


def _mm_tn(a, b):
    return lax.dot_general(a.astype(BF16), b.astype(BF16), (((0,), (0,)), ((), ())),
                           preferred_element_type=F32)


def _split3(x):
    hi = x.astype(BF16)
    r1 = x - hi.astype(F32)
    mid = r1.astype(BF16)
    lo = (r1 - mid.astype(F32)).astype(BF16)
    return hi, mid, lo


def _mm_exact_lhs(sel, x):
    s = sel.astype(BF16)
    hi, mid, lo = _split3(x)
    return (jnp.dot(s, hi, preferred_element_type=F32) + jnp.dot(s, mid, preferred_element_type=F32)
            + jnp.dot(s, lo, preferred_element_type=F32))


def _rms(x, gain):
    return x * lax.rsqrt(jnp.mean(x * x, axis=-1, keepdims=True) + EPS) * gain


def _sigmoid(x):
    return 1.0 / (1.0 + jnp.exp(-x))


def _iota2(shape, axis):
    return lax.broadcasted_iota(jnp.int32, shape, axis)


def _head_ones():
    return (_iota2((GW, GW), 0) // HD == _iota2((GW, GW), 1) // HD).astype(F32)


def _head_sum(x, ones_bd):
    s = ones_bd.astype(BF16)
    hi = x.astype(BF16)
    lo = (x - hi.astype(F32)).astype(BF16)
    return jnp.dot(hi, s, preferred_element_type=F32) + jnp.dot(lo, s, preferred_element_type=F32)


def _norm_mm_kernel(x_ref, g_ref, w_ref, o_ref):
    o_ref[...] = _mm(_rms(x_ref[...], g_ref[...]), w_ref[...])


def norm_matmul(x, gain, w, *, tm):
    t, d = x.shape
    n = w.shape[1]
    return pl.pallas_call(
        _norm_mm_kernel,
        out_shape=jax.ShapeDtypeStruct((t, n), F32),
        grid=(t // tm,),
        in_specs=[pl.BlockSpec((tm, d), lambda i: (i, 0)),
                  pl.BlockSpec((1, d), lambda i: (0, 0)),
                  pl.BlockSpec((d, n), lambda i: (0, 0))],
        out_specs=pl.BlockSpec((tm, n), lambda i: (i, 0)),
        compiler_params=_cparams(("parallel",)),
        name="norm_matmul",
    )(x, gain, w)


def _mm_kernel(a_ref, w_ref, o_ref):
    o_ref[...] = _mm(a_ref[...], w_ref[...])


def matmul_layers(a, w, *, tm):
    t, k = a.shape
    layers, _, n = w.shape
    return pl.pallas_call(
        _mm_kernel,
        out_shape=jax.ShapeDtypeStruct((layers, t, n), F32),
        grid=(layers, t // tm),
        in_specs=[pl.BlockSpec((tm, k), lambda l, i: (i, 0)),
                  pl.BlockSpec((None, k, n), lambda l, i: (l, 0, 0))],
        out_specs=pl.BlockSpec((None, tm, n), lambda l, i: (l, i, 0)),
        compiler_params=_cparams(("parallel", "parallel")),
        name="matmul_layers",
    )(a, w)


def _mm_res_kernel(a_ref, w_ref, r_ref, o_ref):
    o_ref[...] = r_ref[...] + _mm(a_ref[...], w_ref[...])


def matmul_residual(a, w, res, *, tm):
    t, k = a.shape
    n = w.shape[1]
    return pl.pallas_call(
        _mm_res_kernel,
        out_shape=jax.ShapeDtypeStruct((t, n), F32),
        grid=(t // tm,),
        in_specs=[pl.BlockSpec((tm, k), lambda i: (i, 0)),
                  pl.BlockSpec((k, n), lambda i: (0, 0)),
                  pl.BlockSpec((tm, n), lambda i: (i, 0))],
        out_specs=pl.BlockSpec((tm, n), lambda i: (i, 0)),
        compiler_params=_cparams(("parallel",)),
        name="matmul_residual",
    )(a, w, res)


def _mix_out_q_kernel(a0_ref, a1_ref, a2_ref, a3_ref, w_ref, r_ref, g_ref, wq_ref, x_ref, q_ref):
    a = jnp.concatenate([a0_ref[...], a1_ref[...], a2_ref[...], a3_ref[...]], axis=-1)
    x = r_ref[...] + _mm(a, w_ref[...])
    x_ref[...] = x
    q_ref[...] = jnp.dot(_rms(x, g_ref[...]).astype(BF16), wq_ref[...], preferred_element_type=F32).astype(BF16)


def mix_out_q(parts, w, res, gain, w_q, *, tm):
    t, d = res.shape
    part = pl.BlockSpec((tm, GW), lambda i: (i, 0))
    mat = pl.BlockSpec((d, d), lambda i: (0, 0))
    tile = pl.BlockSpec((tm, d), lambda i: (i, 0))
    return pl.pallas_call(
        _mix_out_q_kernel,
        out_shape=(jax.ShapeDtypeStruct((t, d), F32), jax.ShapeDtypeStruct((t, d), BF16)),
        grid=(t // tm,),
        in_specs=[part, part, part, part, mat, tile, pl.BlockSpec((1, d), lambda i: (0, 0)), mat],
        out_specs=(tile, tile),
        compiler_params=_cparams(("parallel",)),
        name="mix_out_q",
    )(*parts, w, res, gain, w_q)


def _ffn_kernel(x_ref, g_ref, wu_ref, wd_ref, gf_ref, o_ref, *, final_norm):
    x = x_ref[...]
    h = jnp.dot(_rms(x, g_ref[...]).astype(BF16), wu_ref[...], preferred_element_type=F32)
    h = jnp.square(jnp.maximum(h, 0.0))
    y = x + jnp.dot(h.astype(BF16), wd_ref[...], preferred_element_type=F32)
    if final_norm:
        y = _rms(y, gf_ref[...])
    o_ref[...] = y


def ffn(x, gain, w_up, w_down, gain_final, *, tm, final_norm):
    t, d = x.shape
    ff = w_up.shape[1]
    resident = lambda shape: pl.BlockSpec(shape, lambda i: (0, 0), pipeline_mode=pl.Buffered(1))
    return pl.pallas_call(
        functools.partial(_ffn_kernel, final_norm=final_norm),
        out_shape=jax.ShapeDtypeStruct((t, d), F32),
        grid=(t // tm,),
        in_specs=[pl.BlockSpec((tm, d), lambda i: (i, 0)),
                  pl.BlockSpec((1, d), lambda i: (0, 0)),
                  resident((d, ff)), resident((ff, d)),
                  pl.BlockSpec((1, d), lambda i: (0, 0))],
        out_specs=pl.BlockSpec((tm, d), lambda i: (i, 0)),
        compiler_params=_cparams(("parallel",)),
        name="ffn",
    )(x, gain, w_up, w_down, gain_final)


def _attn_out_kernel(q_ref, k_ref, v_ref, wo_ref, r_ref, x_ref):
    sls = [slice(h * MEM_HD, (h + 1) * MEM_HD) for h in range(MEM_HEADS)]
    s = [_mm_nt(q_ref[:, sl], k_ref[:, sl]) * (MEM_HD ** -0.5) for sl in sls]
    p = [jnp.exp(sh - jnp.max(sh, axis=-1, keepdims=True)) for sh in s]
    p = [ph * (1.0 / jnp.sum(ph, axis=-1, keepdims=True)) for ph in p]
    heads = [_mm(ph, v_ref[:, sl]) for ph, sl in zip(p, sls)]
    x_ref[...] = r_ref[...] + _mm(jnp.concatenate(heads, axis=-1), wo_ref[...])


def cross_attention_out(q, mem_k, mem_v, layer, w_o, res, *, rows):
    t, d = res.shape
    bsz = mem_k.shape[1] // MEM_LEN
    lt = t // bsz // rows
    tile = pl.BlockSpec((rows, d), lambda b, l: (b * lt + l, 0))
    mem = pl.BlockSpec((None, MEM_LEN, d), lambda b, l: (layer, b, 0))
    return pl.pallas_call(
        _attn_out_kernel,
        out_shape=jax.ShapeDtypeStruct((t, d), F32),
        grid=(bsz, lt),
        in_specs=[tile, mem, mem, pl.BlockSpec((d, d), lambda b, l: (0, 0)), tile],
        out_specs=tile,
        compiler_params=_cparams(("parallel", "arbitrary")),
        name="cross_attention_out",
    )(q, mem_k, mem_v, w_o, res)


def _attn_cache_kernel(q_ref, k_ref, v_ref, o_ref, *, nb, rows):
    nr = MEM_HEADS * rows
    same_head = _iota2((nr, MEM_HEADS * MEM_LEN), 1) % MEM_HEADS == _iota2((nr, MEM_HEADS * MEM_LEN), 0) // rows
    q_all = q_ref[...].astype(F32)
    rng = range(nb)
    qs = [jnp.concatenate([q_all[b * rows:(b + 1) * rows, h * MEM_HD:(h + 1) * MEM_HD] for h in range(MEM_HEADS)],
                          axis=0) for b in rng]
    s = [_mm_nt(qs[b], k_ref[0, b].reshape(MEM_HEADS * MEM_LEN, MEM_HD)) * (MEM_HD ** -0.5) for b in rng]
    s = [jnp.where(same_head, sb, -1e30) for sb in s]
    p = [jnp.exp(sb - jnp.max(sb, axis=-1, keepdims=True)) for sb in s]
    p = [pb * (1.0 / jnp.sum(pb, axis=-1, keepdims=True)) for pb in p]
    o = [_mm(p[b], v_ref[0, b].reshape(MEM_HEADS * MEM_LEN, MEM_HD)) for b in rng]
    for b in rng:
        for h in range(MEM_HEADS):
            o_ref[b * rows:(b + 1) * rows, h * MEM_HD:(h + 1) * MEM_HD] = o[b][h * rows:(h + 1) * rows]


def cross_attention_cache(q, cache_k, cache_v, layer, *, nb, rows):
    t = q.shape[0]
    bsz = cache_k.shape[1]
    mem = pl.BlockSpec((1, nb, MEM_LEN, MEM_HEADS, MEM_HD), lambda b: (layer, b, 0, 0, 0))
    return pl.pallas_call(
        functools.partial(_attn_cache_kernel, nb=nb, rows=rows),
        out_shape=jax.ShapeDtypeStruct((t, D_MODEL), F32),
        grid=(bsz // nb,),
        in_specs=[pl.BlockSpec((nb * rows, D_MODEL), lambda b: (b, 0)), mem, mem],
        out_specs=pl.BlockSpec((nb * rows, D_MODEL), lambda b: (b, 0)),
        compiler_params=_cparams(("parallel",)),
        name="cross_attention_cache",
    )(q, cache_k, cache_v)


def _gelu_tanh(x):
    return 0.5 * x * (1.0 + jnp.tanh(math.sqrt(2.0 / math.pi) * (x + 0.044715 * (x * x * x))))


def _s5_kernel(u_ref, h0_ref, lam_ref, bblk_ref, cblk_ref, d_ref, wglu_ref, bglu_ref, gain_ref,
               y_ref, hfin_ref, scr_ref, tm_ref, *, steps, nb):
    @pl.when(pl.program_id(1) == 0)
    def _():
        scr_ref[0:nb, :] = h0_ref[...]

    def to_time_major(b, carry):
        for j in range(GW // LANES):
            tm_ref[j, pl.ds(b, steps, stride=nb), :] = u_ref[b, :, j * LANES:(j + 1) * LANES]
        return carry

    lax.fori_loop(0, nb, to_time_major, 0)
    u = jnp.concatenate([tm_ref[j] for j in range(GW // LANES)], axis=1)
    scr_ref[nb:, :] = _mm(u, bblk_ref[...])
    lam_re = lam_ref[0:1, :]
    lam_im = lam_ref[1:2, :]

    def step(t, carry):
        p0 = pl.multiple_of(t * nb, nb)
        c0 = pl.multiple_of((t + 1) * nb, nb)
        h_re = scr_ref[pl.ds(p0, nb), 0:S5_W]
        h_im = scr_ref[pl.ds(p0, nb), S5_W:]
        scr_ref[pl.ds(c0, nb), 0:S5_W] = scr_ref[pl.ds(c0, nb), 0:S5_W] + lam_re * h_re - lam_im * h_im
        scr_ref[pl.ds(c0, nb), S5_W:] = scr_ref[pl.ds(c0, nb), S5_W:] + lam_re * h_im + lam_im * h_re
        return carry

    lax.fori_loop(0, steps, step, 0)
    h_last = scr_ref[steps * nb:, :]
    hfin_ref[...] = h_last
    y = _mm(scr_ref[nb:, :], cblk_ref[...]) + d_ref[...] * u
    y = _gelu_tanh(y)
    y = y * _sigmoid(_mm(y, wglu_ref[...]) + bglu_ref[...])
    y = _rms(y, gain_ref[...])
    for j in range(GW // LANES):
        tm_ref[j] = y[:, j * LANES:(j + 1) * LANES]

    def to_batch_major(b, carry):
        for j in range(GW // LANES):
            y_ref[b, :, j * LANES:(j + 1) * LANES] = tm_ref[j, pl.ds(b, steps, stride=nb), :]
        return carry

    lax.fori_loop(0, nb, to_batch_major, 0)
    scr_ref[0:nb, :] = h_last


def s5_mixer(proj, h0, lam, bblk, cblk, dvec, wglu, bglu, gain, *, steps, nb):
    bsz, seq, _ = proj.shape
    const = lambda shape: pl.BlockSpec(shape, lambda b, t: (0, 0))
    return pl.pallas_call(
        functools.partial(_s5_kernel, steps=steps, nb=nb),
        out_shape=(jax.ShapeDtypeStruct((bsz, seq, GW), F32),
                   jax.ShapeDtypeStruct((bsz, 2 * S5_W), F32)),
        grid=(bsz // nb, seq // steps),
        in_specs=[pl.BlockSpec((nb, steps, GW), lambda b, t: (b, t, 0)),
                  pl.BlockSpec((nb, 2 * S5_W), lambda b, t: (b, 0)),
                  const((2, S5_W)), const((GW, 2 * S5_W)), const((2 * S5_W, GW)), const((1, GW)),
                  const((GW, GW)), const((1, GW)), const((1, GW))],
        out_specs=(pl.BlockSpec((nb, steps, GW), lambda b, t: (b, t, 0)),
                   pl.BlockSpec((nb, 2 * S5_W), lambda b, t: (b, 0))),
        scratch_shapes=[pltpu.VMEM(((steps + 1) * nb, 2 * S5_W), F32),
                        pltpu.VMEM((GW // LANES, steps * nb, LANES), F32)],
        compiler_params=_cparams(("parallel", "arbitrary")),
        name="s5_mixer",
    )(proj, h0, lam, bblk, cblk, dvec, wglu, bglu, gain)


def s5_params(lam_re, lam_im, b_re, b_im, c_re, c_im, log_step):
    step = jnp.exp(log_step)[:, None]
    mag = jnp.exp(lam_re * step)
    lbar_re = mag * jnp.cos(lam_im * step)
    lbar_im = mag * jnp.sin(lam_im * step)
    den = lam_re * lam_re + lam_im * lam_im
    f_re = ((lbar_re - 1.0) * lam_re + lbar_im * lam_im) / den
    f_im = (lbar_im * lam_re - (lbar_re - 1.0) * lam_im) / den
    bbar_re = f_re[..., None] * b_re - f_im[..., None] * b_im
    bbar_im = f_re[..., None] * b_im + f_im[..., None] * b_re
    eye_g = jnp.eye(S5_NG, dtype=F32)

    def in_blk(m):
        return jnp.einsum('gph,gk->ghkp', m, eye_g).reshape(GW, S5_W)

    def out_blk(m):
        return jnp.einsum('ghp,gk->gpkh', m, eye_g).reshape(S5_W, GW)

    bblk = jnp.concatenate([in_blk(bbar_re), in_blk(bbar_im)], axis=1)
    cblk = jnp.concatenate([out_blk(c_re), -out_blk(c_im)], axis=0)
    lam2 = jnp.stack([lbar_re.reshape(S5_W), lbar_im.reshape(S5_W)])
    return lam2, bblk.astype(BF16), cblk.astype(BF16)


def s5_apply(proj, h0_re, h0_im, p, l, *, steps, nb):
    bsz = proj.shape[0]
    lam2, bblk, cblk = s5_params(p['s5_lam_re'][l], p['s5_lam_im'][l], p['s5_b_re'][l], p['s5_b_im'][l],
                                 p['s5_c_re'][l], p['s5_c_im'][l], p['s5_log_step'][l])
    h0 = jnp.concatenate([h0_re.reshape(bsz, S5_W), h0_im.reshape(bsz, S5_W)], axis=1)
    y, h = s5_mixer(proj, h0, lam2, bblk, cblk, p['s5_d'][l].reshape(1, GW),
                    p['s5_w_glu'][l].astype(BF16), p['s5_b_glu'][l].reshape(1, GW),
                    p['s5_norm'][l].reshape(1, GW), steps=steps, nb=nb)
    return y, h[:, :S5_W].reshape(bsz, S5_NG, S5_P), h[:, S5_W:].reshape(bsz, S5_NG, S5_P)


def _load_state(s0_ref, st_ref, nb, transpose):
    for b in range(nb):
        rows = []
        for h in range(NH):
            pieces = [s0_ref[b, h]]
            if h:
                pieces.insert(0, jnp.zeros((HD, h * HD), F32))
            if h < NH - 1:
                pieces.append(jnp.zeros((HD, (NH - 1 - h) * HD), F32))
            rows.append(jnp.concatenate(pieces, axis=1))
        st = jnp.concatenate(rows, axis=0)
        st_ref[b] = st.T if transpose else st


def _store_state(st_ref, sfin_ref, nb, transpose, own_layer):
    if own_layer is not None:
        for other in range(DEPTH):
            if other != own_layer:
                sfin_ref[other] = jnp.zeros(sfin_ref.shape[1:], F32)
        sfin_ref = sfin_ref.at[own_layer]
    for b in range(nb):
        st = st_ref[b].T if transpose else st_ref[b]
        for h in range(NH):
            sfin_ref[b, h] = st[h * HD:(h + 1) * HD, h * HD:(h + 1) * HD]


def _head_masks():
    lane_head = _iota2((1, GW), 1) // HD
    return [(lane_head == h).astype(F32) for h in range(NH)]


def _stack_heads(x, masks):
    return jnp.concatenate([x * m for m in masks], axis=0)


def _pad_rows(x, rows):
    if x.shape[0] == rows:
        return x
    return jnp.concatenate([x, jnp.zeros((rows - x.shape[0], x.shape[1]), x.dtype)], axis=0)


def _silu(x):
    return x * _sigmoid(x)


def _chunk_rows(ref, b, r0, rows, c):
    return _pad_rows(ref[b, pl.ds(r0, rows), :], c)


def _ret_kernel(q_ref, k_ref, v_ref, g_ref, cos_ref, sin_ref, s0_ref, _states_in, o_ref, sfin_ref, st_ref,
                *, nb, tb, c, c_real, own_layer):
    rows = min(tb, c)

    @pl.when(pl.program_id(1) == 0)
    def _():
        _load_state(s0_ref, st_ref, nb, transpose=False)

    masks = _head_masks()
    ones_bd = _head_ones()
    lane_head = _iota2((1, GW), 1) // HD
    log_gamma = jnp.zeros((1, GW), F32)
    for h in range(NH):
        log_gamma = jnp.where(lane_head == h, math.log(1.0 - 2.0 ** (-5.0 - h)), log_gamma)
    tt = _iota2((c, GW), 0).astype(F32)
    scale = HD ** -0.5
    g_q = jnp.exp(log_gamma * (tt + 1.0))
    g_k = jnp.exp(-log_gamma * (tt + 1.0)) * scale
    g_tail = jnp.exp(log_gamma * (c_real - 1.0 - tt)) * scale
    row_head = _iota2((GW, GW), 0) // HD
    g_chunk = jnp.zeros((GW, GW), F32)
    for h in range(NH):
        g_chunk = jnp.where(row_head == h, math.exp(math.log(1.0 - 2.0 ** (-5.0 - h)) * c_real), g_chunk)
    causal = _iota2((c, NH * c), 0) >= _iota2((c, NH * c), 1) % c
    first_half = _iota2((c, GW), 1) % HD < HD // 2

    def rope(x, cos, sin):
        swapped = jnp.where(first_half, pltpu.roll(x, GW - HD // 2, axis=1), pltpu.roll(x, HD // 2, axis=1))
        return x * cos + swapped * sin

    def chunk(ci, carry):
        r0 = pl.multiple_of(ci * rows, rows)
        cos = _pad_rows(cos_ref[pl.ds(r0, rows), :], c)
        sin = _pad_rows(sin_ref[pl.ds(r0, rows), :], c)
        rng = range(nb)
        qt = [rope(_chunk_rows(q_ref, b, r0, rows, c), cos, sin) * g_q for b in rng]
        k = [rope(_chunk_rows(k_ref, b, r0, rows, c), cos, sin) for b in rng]
        v = [_chunk_rows(v_ref, b, r0, rows, c) for b in rng]
        st = [st_ref[b] for b in rng]
        sc = [_mm_nt(qt[b], _stack_heads(k[b] * g_k, masks)) for b in rng]
        o_in = [_mm(jnp.where(causal, sc[b], 0.0), _stack_heads(v[b], masks)) for b in rng]
        o_st = [_mm(qt[b], st[b]) for b in rng]
        d_st = [_mm_tn(k[b] * g_tail, v[b]) for b in rng]
        o = [o_in[b] + o_st[b] for b in rng]
        ms = [_head_sum(o[b] * o[b], ones_bd) for b in rng]
        for b in rng:
            st_ref[b] = st[b] * g_chunk + ones_bd * d_st[b]
            ob = o[b] * lax.rsqrt(ms[b] * (1.0 / HD) + EPS) * _silu(_chunk_rows(g_ref, b, r0, rows, c))
            o_ref[b, pl.ds(r0, rows), :] = ob[:rows]
        return carry

    lax.fori_loop(0, tb // rows, chunk, 0)

    @pl.when(pl.program_id(1) == pl.num_programs(1) - 1)
    def _():
        _store_state(st_ref, sfin_ref, nb, transpose=False, own_layer=own_layer)


def _rope_tables(pos0, seq):
    half = HD // 2
    inv = ROPE_BASE ** (-jnp.arange(half, dtype=F32) / half)
    pos = pos0 + jnp.arange(seq, dtype=jnp.int32)
    ang = pos.astype(F32)[:, None] * inv[None, :]
    cos, sin = jnp.cos(ang), jnp.sin(ang)
    return (jnp.tile(jnp.concatenate([cos, cos], axis=-1), (1, NH)),
            jnp.tile(jnp.concatenate([-sin, sin], axis=-1), (1, NH)))


def _proj_spec(nb, tb, col):
    return pl.BlockSpec((nb, tb, GW), lambda b, t: (b, t, col))


def _state_spec(nb, layer):
    return pl.BlockSpec((None, nb, NH, HD, HD), lambda b, t: (layer, b, 0, 0, 0))


_ALIASED = pl.BlockSpec(memory_space=pl.ANY)


def _collector(states, bsz, nb, layer, operand_index):
    shape = jax.ShapeDtypeStruct((DEPTH, bsz, NH, HD, HD), F32)
    if states is None:
        spec = pl.BlockSpec((DEPTH, nb, NH, HD, HD), lambda b, t: (0, b, 0, 0, 0))
        return jnp.zeros((1, 1, NH, HD, HD), F32), spec, shape, {}, layer
    return states, _state_spec(nb, layer), shape, {operand_index: 1}, None


def retention_apply(proj, s0, s0_layer, states, layer, pos0, *, tb, nb):
    bsz, seq, _ = proj.shape
    c = RET_CHUNK if tb >= RET_CHUNK else CHUNK
    cos, sin = _rope_tables(pos0, seq)
    states, st_spec, st_shape, aliases, own_layer = _collector(states, bsz, nb, layer, 7)
    return pl.pallas_call(
        functools.partial(_ret_kernel, nb=nb, tb=tb, c=c, c_real=min(tb, c), own_layer=own_layer),
        out_shape=(jax.ShapeDtypeStruct((bsz, seq, GW), F32), st_shape),
        grid=(bsz // nb, seq // tb),
        in_specs=[_proj_spec(nb, tb, 5), _proj_spec(nb, tb, 6), _proj_spec(nb, tb, 7), _proj_spec(nb, tb, 8),
                  pl.BlockSpec((tb, GW), lambda b, t: (t, 0)), pl.BlockSpec((tb, GW), lambda b, t: (t, 0)),
                  _state_spec(nb, s0_layer), _ALIASED],
        out_specs=(pl.BlockSpec((nb, tb, GW), lambda b, t: (b, t, 0)), st_spec),
        scratch_shapes=[pltpu.VMEM((nb, GW, GW), F32)],
        input_output_aliases=aliases,
        compiler_params=_cparams(("parallel", "arbitrary")),
        name="retention",
    )(proj, proj, proj, proj, cos, sin, s0, states)


def _log_sigmoid(z):
    return jnp.minimum(z, 0.0) - jnp.log(1.0 + jnp.exp(-jnp.abs(z)))


def _hgrn_kernel(q_ref, f_ref, i_ref, g_ref, lb_ref, gain_ref, s0_ref, _states_in, o_ref, sfin_ref, st_ref,
                 *, nb, tb, c, own_layer):
    rows = min(tb, c)

    @pl.when(pl.program_id(1) == 0)
    def _():
        _load_state(s0_ref, st_ref, nb, transpose=True)

    t_hi = -(-rows // SUBLANES) * SUBLANES
    ones_bd = _head_ones()
    tri = (_iota2((c, c), 0) >= _iota2((c, c), 1)).astype(F32)
    t_idx = _iota2((c, GW), 0)
    lb = lb_ref[...]
    log_lb = jnp.log(lb)
    log_1m_lb = jnp.log(1.0 - lb)
    gain = gain_ref[...]

    def chunk(ci, carry):
        r0 = pl.multiple_of(ci * rows, rows)
        rng = range(nb)
        valid = t_idx < rows

        def gates(b):
            z = _chunk_rows(f_ref, b, r0, rows, c)
            ls_pos = _log_sigmoid(z)
            ls_neg = ls_pos - z
            b2 = log_lb + ls_neg
            log_f = jnp.maximum(ls_pos, b2) + jnp.log(1.0 + jnp.exp(-jnp.abs(ls_pos - b2)))
            return (jnp.where(valid, log_f, 0.0), jnp.where(valid, (1.0 - lb) * jnp.exp(ls_neg), 0.0),
                    ls_neg + log_1m_lb)

        log_f, key, log_key = zip(*[gates(b) for b in rng])
        q = [_silu(_chunk_rows(q_ref, b, r0, rows, c)) for b in rng]
        v = [_chunk_rows(i_ref, b, r0, rows, c) for b in rng]
        st = [st_ref[b] for b in rng]
        cum = [_mm_exact_lhs(tri, log_f[b]) for b in rng]
        last = [cum[b][c - 1:c, :] for b in rng]
        cum2 = [cum[b] * LOG2_E for b in rng]
        kd2 = [(log_key[b] - cum[b]) * LOG2_E for b in rng]

        def pair_rows(b, s):
            lo = SUBLANES * (s // SUBLANES)
            t_rows = _iota2((t_hi - lo, GW), 0) + lo
            key_decay = jnp.where(t_rows >= s, jnp.exp2(cum2[b][lo:t_hi] + kd2[b][s:s + 1, :]), 0.0)
            return key_decay * q[b][lo:t_hi]

        sc = [_mm(jnp.concatenate([pair_rows(b, s) for s in range(rows)], axis=0), ones_bd) for b in rng]
        o_st = [_mm_nt(q[b] * jnp.exp(cum[b]), st[b]) for b in rng]
        d_st = [_mm_tn(v[b], key[b] * jnp.exp(last[b] - cum[b])) for b in rng]

        def intra(b):
            tiles = [None] * (c // SUBLANES)
            off = 0
            for s in range(rows):
                for ti in range(s // SUBLANES, t_hi // SUBLANES):
                    term = sc[b][off:off + SUBLANES, :] * v[b][s:s + 1, :]
                    tiles[ti] = term if tiles[ti] is None else tiles[ti] + term
                    off += SUBLANES
            zero = jnp.zeros((SUBLANES, GW), F32)
            return jnp.concatenate([zero if t is None else t for t in tiles], axis=0)

        o = [o_st[b] + intra(b) for b in rng]
        ms = [_head_sum(o[b] * o[b], ones_bd) for b in rng]
        for b in rng:
            st_ref[b] = st[b] * jnp.exp(last[b]) + ones_bd * d_st[b]
            ob = o[b] * lax.rsqrt(ms[b] * (1.0 / HD) + EPS) * gain * _silu(_chunk_rows(g_ref, b, r0, rows, c))
            o_ref[b, pl.ds(r0, rows), :] = ob[:rows]
        return carry

    lax.fori_loop(0, tb // rows, chunk, 0)

    @pl.when(pl.program_id(1) == pl.num_programs(1) - 1)
    def _():
        _store_state(st_ref, sfin_ref, nb, transpose=True, own_layer=own_layer)


def hgrn_apply(proj, s0, s0_layer, states, layer, lb, gain, *, tb, nb):
    bsz, seq, _ = proj.shape
    row = pl.BlockSpec((1, GW), lambda b, t: (0, 0))
    states, st_spec, st_shape, aliases, own_layer = _collector(states, bsz, nb, layer, 7)
    return pl.pallas_call(
        functools.partial(_hgrn_kernel, nb=nb, tb=tb, c=CHUNK, own_layer=own_layer),
        out_shape=(jax.ShapeDtypeStruct((bsz, seq, GW), F32), st_shape),
        grid=(bsz // nb, seq // tb),
        in_specs=[_proj_spec(nb, tb, 1), _proj_spec(nb, tb, 2), _proj_spec(nb, tb, 3), _proj_spec(nb, tb, 4),
                  row, row, _state_spec(nb, s0_layer), _ALIASED],
        out_specs=(pl.BlockSpec((nb, tb, GW), lambda b, t: (b, t, 0)), st_spec),
        scratch_shapes=[pltpu.VMEM((nb, GW, GW), F32)],
        input_output_aliases=aliases,
        compiler_params=_cparams(("parallel", "arbitrary")),
        name="hgrn2",
    )(proj, proj, proj, proj, lb.reshape(1, GW), gain.reshape(1, GW), s0, states)


def _softplus(z):
    return jnp.maximum(z, 0.0) + jnp.log(1.0 + jnp.exp(-jnp.abs(z)))


def _rwkv_prepare(ins, consts):
    masks, ones_bd, bd_mask, eye_flat, tri, strict, incl = consts
    rng = range(len(ins))
    r, lw, k, v, kk, a = [[i[j] for i in ins] for j in range(6)]
    c = r[0].shape[0]
    n = NH * c

    def to_bd(flat):
        return jnp.concatenate([flat] * NH, axis=0) * bd_mask

    cum = [_mm_exact_lhs(tri, lw[i]) for i in rng]
    last = [cum[i][c - 1:c, :] for i in rng]
    p_inv = [jnp.exp(-cum[i]) for i in rng]
    p_tail = [jnp.exp(last[i] - cum[i]) for i in rng]
    ka = [kk[i] * a[i] for i in rng]
    x = [jnp.concatenate([kk[i] * jnp.exp(cum[i] - lw[i]), r[i] * jnp.exp(cum[i])], axis=0) for i in rng]
    g = [_mm_nt(x[i], jnp.concatenate([_stack_heads(ka[i] * p_inv[i], masks),
                                       _stack_heads(k[i] * p_inv[i], masks)], axis=0)) for i in rng]
    m_ak = [jnp.where(strict, g[i][:c], 0.0) for i in rng]
    n_ak = [jnp.where(incl, g[i][c:], 0.0) for i in rng]
    pw = [m_ak[i][:, :n] for i in rng]
    t_inv = [eye_flat - pw[i] for i in rng]
    pw_bd = [to_bd(pw[i]) for i in rng]
    for _ in range(int(math.log2(c)) - 1):
        pw = [_mm(pw[i], pw_bd[i]) for i in rng]
        pw_bd = [to_bd(pw[i]) for i in rng]
        t_inv = [t_inv[i] + _mm(t_inv[i], pw_bd[i]) for i in rng]
    v_stk = [_stack_heads(v[i], masks) for i in rng]
    zeros = jnp.zeros((n, GW), F32)
    mkv = [_mm(m_ak[i], jnp.concatenate([zeros, v_stk[i]], axis=0)) for i in rng]
    tmkv = [_mm(t_inv[i], _stack_heads(mkv[i], masks)) for i in rng]
    return [dict(x=x[i], t_inv=t_inv[i], tmkv=tmkv[i], n_ak=n_ak[i], v=v[i], v_stk=v_stk[i],
                 decay=jnp.exp(last[i]),
                 k_tail=jnp.concatenate([ka[i] * p_tail[i], k[i] * p_tail[i]], axis=0)) for i in rng]


def _rwkv_advance(prep, st, consts):
    masks, ones_bd = consts[0], consts[1]
    rng = range(len(prep))
    c = prep[0]['t_inv'].shape[0]
    xa = [_mm_nt(prep[i]['x'], st[i]) for i in rng]
    u = [-_mm(prep[i]['t_inv'], _stack_heads(xa[i][:c], masks)) - prep[i]['tmkv'] for i in rng]
    nuv = [_mm(prep[i]['n_ak'], jnp.concatenate([_stack_heads(u[i], masks), prep[i]['v_stk']], axis=0))
           for i in rng]
    d_uv = [_mm_tn(jnp.concatenate([u[i], prep[i]['v']], axis=0), prep[i]['k_tail']) for i in rng]
    return [(xa[i][c:] + nuv[i], st[i] * prep[i]['decay'] + ones_bd * d_uv[i]) for i in rng]


def _rwkv_kernel(x_r_ref, x_k_ref, x_v_ref, x_l_ref, sh0_ref, s0_ref, mu_ref, w0_ref, ww_ref, a0_ref,
                 wa_ref, wg_ref, kk_ref, ka_ref, rk_ref, lng_ref, lnb_ref, _states_in,
                 o_ref, sfin_ref, shfin_ref,
                 st_ref, sh_ref, r_s, lw_s, k_s, v_s, kkn_s, a_s, y_s, bonus_s, gate_s, *, nb, tb, c, own_layer):
    tbp = max(tb, c)

    @pl.when(pl.program_id(1) == 0)
    def _():
        _load_state(s0_ref, st_ref, nb, transpose=False)
        sh_ref[...] = sh0_ref[...]

    masks = _head_masks()
    ones_bd = _head_ones()
    n = NH * c
    row_t = _iota2((c, 2 * n), 0)
    col_t = _iota2((c, 2 * n), 1) % c
    bd_mask = (_iota2((n, n), 0) // c == _iota2((n, n), 1) // c).astype(F32)
    eye_flat = (_iota2((c, n), 1) % c == _iota2((c, n), 0)).astype(F32)
    consts = (masks, ones_bd, bd_mask, eye_flat, (_iota2((c, c), 0) >= _iota2((c, c), 1)).astype(F32),
              col_t < row_t, col_t <= row_t)
    first_row = _iota2((nb * tb, GW), 0) % tb == 0

    def mixed(x_ref, j):
        x = x_ref[...].reshape(nb * tb, GW)
        carried = jnp.broadcast_to(sh_ref[:, j:j + 1, :], (nb, tb, GW)).reshape(nb * tb, GW)
        prev = jnp.where(first_row, carried, pltpu.roll(x, 1, axis=0))
        sh_ref[:, j:j + 1, :] = x_ref[:, tb - 1:tb, :]
        return x + (prev - x) * mu_ref[j:j + 1, :]

    r = mixed(x_r_ref, 0)
    k = mixed(x_k_ref, 1)
    v = mixed(x_v_ref, 2)
    xl = mixed(x_l_ref, 3)
    log_w = -_softplus(-(w0_ref[...] + _mm(jnp.tanh(xl), ww_ref[...]))) - 0.5
    a = _sigmoid(a0_ref[...] + _mm(xl, wa_ref[...]))
    gate_s[...] = _mm(_sigmoid(xl), wg_ref[...]).reshape(nb, tb, GW)
    kk = k * kk_ref[...]
    kk = kk * lax.rsqrt(jnp.maximum(_head_sum(kk * kk, ones_bd), 1e-24))
    k = k * (1.0 + (a - 1.0) * ka_ref[...])
    bonus_s[...] = (_head_sum(r * k * rk_ref[...], ones_bd) * v).reshape(nb, tb, GW)
    for s, val in ((r_s, r), (lw_s, -jnp.exp(log_w)), (k_s, k), (v_s, v), (kkn_s, kk), (a_s, a)):
        s[:, 0:tb, :] = val.reshape(nb, tb, GW)
        if tbp > tb:
            s[:, tb:tbp, :] = jnp.zeros((nb, tbp - tb, GW), F32)

    n_chunks = tbp // c
    group = 4 if n_chunks % 4 == 0 else 1

    def chunks(gi, carry):
        r0 = [pl.multiple_of((gi * group + j) * c, c) for j in range(group)]
        prep = _rwkv_prepare([tuple(s[b, pl.ds(r0[j], c), :] for s in (r_s, lw_s, k_s, v_s, kkn_s, a_s))
                              for j in range(group) for b in range(nb)], consts)
        st = [st_ref[b] for b in range(nb)]
        for j in range(group):
            outs = _rwkv_advance(prep[j * nb:(j + 1) * nb], st, consts)
            st = [o[1] for o in outs]
            for b in range(nb):
                y_s[b, pl.ds(r0[j], c), :] = outs[b][0]
        for b in range(nb):
            st_ref[b] = st[b]
        return carry

    lax.fori_loop(0, n_chunks // group, chunks, 0)

    y = y_s[:, 0:tb, :].reshape(nb * tb, GW)
    mean = _head_sum(y, ones_bd) * (1.0 / HD)
    d = y - mean
    var = _head_sum(d * d, ones_bd) * (1.0 / HD)
    y = d * lax.rsqrt(var + RW_LN_EPS) * lng_ref[...] + lnb_ref[...]
    o_ref[...] = ((y.reshape(nb, tb, GW) + bonus_s[...]) * gate_s[...])
    shfin_ref[...] = sh_ref[...]

    @pl.when(pl.program_id(1) == pl.num_programs(1) - 1)
    def _():
        _store_state(st_ref, sfin_ref, nb, transpose=False, own_layer=own_layer)


def rwkv_apply(proj, s0, s0_layer, states, shift0, p, l, *, tb, nb):
    bsz, seq, _ = proj.shape
    states, st_spec, st_shape, aliases, own_layer = _collector(states, bsz, nb, l, 17)
    c = CHUNK
    tbp = max(tb, c)
    row = pl.BlockSpec((1, GW), lambda b, t: (0, 0))
    mat = pl.BlockSpec((GW, GW), lambda b, t: (0, 0))
    sh_spec = pl.BlockSpec((nb, 4, GW), lambda b, t: (b, 0, 0))
    zeros = lambda r: jnp.zeros((r, GW), F32)
    ww = jnp.concatenate([p['rw_w_w2'][l], zeros(192)], axis=0).astype(BF16)
    wa = jnp.concatenate([zeros(64), p['rw_w_a2'][l], zeros(128)], axis=0).astype(BF16)
    wg = jnp.concatenate([zeros(128), p['rw_w_g2'][l]], axis=0).astype(BF16)
    r1 = lambda name: p[name][l].reshape(1, GW)
    seq_buf = pltpu.VMEM((nb, tbp, GW), F32)
    blk_buf = pltpu.VMEM((nb, tb, GW), F32)
    y, st, sh = pl.pallas_call(
        functools.partial(_rwkv_kernel, nb=nb, tb=tb, c=c, own_layer=own_layer),
        out_shape=(jax.ShapeDtypeStruct((bsz, seq, GW), F32), st_shape,
                   jax.ShapeDtypeStruct((bsz, 4, GW), F32)),
        grid=(bsz // nb, seq // tb),
        in_specs=[_proj_spec(nb, tb, 9), _proj_spec(nb, tb, 10), _proj_spec(nb, tb, 11), _proj_spec(nb, tb, 12),
                  sh_spec, _state_spec(nb, s0_layer), pl.BlockSpec((4, GW), lambda b, t: (0, 0)),
                  row, mat, row, mat, mat, row, row, row, row, row, _ALIASED],
        out_specs=(pl.BlockSpec((nb, tb, GW), lambda b, t: (b, t, 0)), st_spec, sh_spec),
        scratch_shapes=[pltpu.VMEM((nb, GW, GW), F32), pltpu.VMEM((nb, 4, GW), F32),
                        seq_buf, seq_buf, seq_buf, seq_buf, seq_buf, seq_buf, seq_buf, blk_buf, blk_buf],
        input_output_aliases=aliases,
        compiler_params=_cparams(("parallel", "arbitrary")),
        name="rwkv7",
    )(proj, proj, proj, proj, shift0.reshape(bsz, 4, GW), s0, p['rw_mu'][l].reshape(4, GW),
      r1('rw_w0'), ww, r1('rw_a0'), wa, wg, r1('rw_k_k'), r1('rw_k_a'), r1('rw_r_k'), r1('rw_ln_g'), r1('rw_ln_b'),
      states)
    return y, st, sh.reshape(bsz, RW_PROJ)


TOKEN_TILE = 512
TIME_BLOCK = 128
CACHE_ATTN_SEQS = 4


def _tiles(bsz, seq):
    if seq >= TIME_BLOCK:
        return dict(tb=TIME_BLOCK, s5_nb=bsz, mix_nb=bsz, rw_nb=bsz)
    return dict(tb=seq, s5_nb=bsz, mix_nb=8, rw_nb=16)


def _trunk_layer(x, bsz, seq, pos0, attend, st, mats, p, wb, l, lb, final_norm):
    cfg = _tiles(bsz, seq)
    tm = TOKEN_TILE
    row = lambda name: p[name][l].reshape(1, -1)
    proj = norm_matmul(x, row('norm_mix'), wb['w_in'][l], tm=tm)
    proj = proj.reshape(bsz, seq, IN_WIDTH)
    y_s5, s5_re, s5_im = s5_apply(proj, st['s5_re'], st['s5_im'], p, l, steps=cfg['tb'], nb=cfg['s5_nb'])
    y_hg, hg_s = hgrn_apply(proj, st['hgrn'], st['layer'], mats['hgrn'], l, lb, p['hg_norm'][l],
                            tb=cfg['tb'], nb=cfg['mix_nb'])
    y_rt, rt_s = retention_apply(proj, st['ret'], st['layer'], mats['ret'], l, pos0,
                                 tb=cfg['tb'], nb=cfg['mix_nb'])
    y_rw, rw_s, shift = rwkv_apply(proj, st['rwkv'], st['layer'], mats['rwkv'], st['shift'], p, l,
                                   tb=cfg['tb'], nb=cfg['rw_nb'])
    parts = [y.reshape(bsz * seq, GW) for y in (y_s5, y_hg, y_rt, y_rw)]
    x, q = mix_out_q(parts, wb['w_out'][l], x, row('norm_mem'), wb['mem_w_q'][l], tm=tm)
    x = attend(q, wb['mem_w_o'][l], x)
    x = ffn(x, row('norm_ffn'), wb['ffn_w_up'][l], wb['ffn_w_down'][l], p['norm_final'].reshape(1, -1),
            tm=tm, final_norm=final_norm)
    return x, (s5_re, s5_im, shift), dict(hgrn=hg_s, ret=rt_s, rwkv=rw_s)


def kernel(x_prompt, x_sample, mem_prompt, state_s5_re, state_s5_im, state_hgrn, state_ret, state_rwkv,
           state_rwkv_shift, cache_mem_k, cache_mem_v, norm_mix, w_in, w_out, s5_lam_re, s5_lam_im,
           s5_b_re, s5_b_im, s5_c_re, s5_c_im, s5_d, s5_log_step, s5_w_glu, s5_b_glu, s5_norm,
           hg_lb_logits, hg_norm, rw_mu, rw_w0, rw_w_w2, rw_a0, rw_w_a2, rw_w_g2, rw_k_k, rw_k_a, rw_r_k,
           rw_ln_g, rw_ln_b, norm_mem, mem_w_q, mem_w_k, mem_w_v, mem_w_o, norm_ffn, ffn_w_up, ffn_w_down,
           norm_final):
    p = dict(norm_mix=norm_mix, s5_lam_re=s5_lam_re, s5_lam_im=s5_lam_im, s5_b_re=s5_b_re, s5_b_im=s5_b_im,
             s5_c_re=s5_c_re, s5_c_im=s5_c_im, s5_d=s5_d, s5_log_step=s5_log_step, s5_w_glu=s5_w_glu,
             s5_b_glu=s5_b_glu, s5_norm=s5_norm, hg_norm=hg_norm, rw_mu=rw_mu, rw_w0=rw_w0, rw_w_w2=rw_w_w2,
             rw_a0=rw_a0, rw_w_a2=rw_w_a2, rw_w_g2=rw_w_g2, rw_k_k=rw_k_k, rw_k_a=rw_k_a, rw_r_k=rw_r_k,
             rw_ln_g=rw_ln_g, rw_ln_b=rw_ln_b, norm_mem=norm_mem, norm_ffn=norm_ffn, norm_final=norm_final)
    wb = {name: w.astype(BF16) for name, w in dict(
        w_in=w_in, w_out=w_out, mem_w_q=mem_w_q, mem_w_k=mem_w_k, mem_w_v=mem_w_v, mem_w_o=mem_w_o,
        ffn_w_up=ffn_w_up, ffn_w_down=ffn_w_down).items()}
    lb_all = jnp.cumsum(jax.nn.softmax(hg_lb_logits.astype(F32), axis=0), axis=0)
    lb_all = lb_all - lb_all[0:1]

    bp, lp, _ = x_prompt.shape
    bs, ls, _ = x_sample.shape
    yp = x_prompt.reshape(bp * lp, D_MODEL)
    ys = x_sample.reshape(bs * ls, D_MODEL)
    mem2d = mem_prompt.reshape(bp * MEM_LEN, D_MODEL)
    mat_zero = jnp.zeros((1, bp, NH, HD, HD), F32)
    p_small, s_small = [], []
    p_mats = dict(hgrn=None, ret=None, rwkv=None)
    s_mats = dict(hgrn=None, ret=None, rwkv=None)
    mk = matmul_layers(mem2d, wb['mem_w_k'], tm=TOKEN_TILE)
    mv = matmul_layers(mem2d, wb['mem_w_v'], tm=TOKEN_TILE)
    for l in range(DEPTH):
        final = l == DEPTH - 1
        def attend_p(q, w_o, x, l=l):
            return cross_attention_out(q, mk, mv, l, w_o, x, rows=TOKEN_TILE)

        zero_state = dict(s5_re=jnp.zeros((bp, S5_NG, S5_P), F32), s5_im=jnp.zeros((bp, S5_NG, S5_P), F32),
                          shift=jnp.zeros((bp, RW_PROJ), F32), hgrn=mat_zero, ret=mat_zero, rwkv=mat_zero, layer=0)
        yp, small, p_mats = _trunk_layer(yp, bp, lp, 0, attend_p, zero_state, p_mats, p, wb, l, lb_all[l], final)
        p_small.append(small)
        sst = dict(s5_re=state_s5_re[l], s5_im=state_s5_im[l], shift=state_rwkv_shift[l],
                   hgrn=state_hgrn, ret=state_ret, rwkv=state_rwkv, layer=l)
        def attend_s(q, w_o, x, l=l):
            o = cross_attention_cache(q, cache_mem_k, cache_mem_v, l, nb=CACHE_ATTN_SEQS, rows=ls)
            return matmul_residual(o, w_o, x, tm=TOKEN_TILE)

        ys, small, s_mats = _trunk_layer(ys, bs, ls, PAST_LEN, attend_s, sst, s_mats, p, wb, l, lb_all[l], final)
        s_small.append(small)
    stack = lambda states, i: jnp.stack([s[i] for s in states])
    return (yp.reshape(bp, lp, D_MODEL), ys.reshape(bs, ls, D_MODEL),
            stack(p_small, 0), stack(p_small, 1), p_mats['hgrn'], p_mats['ret'], p_mats['rwkv'], stack(p_small, 2),
            mk.reshape(DEPTH, bp, MEM_LEN, MEM_HEADS, MEM_HD), mv.reshape(DEPTH, bp, MEM_LEN, MEM_HEADS, MEM_HD),
            stack(s_small, 0), stack(s_small, 1), s_mats['hgrn'], s_mats['ret'], s_mats['rwkv'], stack(s_small, 2))
```

```python
import functools
import math

import jax
import jax.numpy as jnp
from jax import lax
from jax.experimental import pallas as pl
from jax.experimental.pallas import tpu as pltpu

F32 = jnp.float32
BF16 = jnp.bfloat16

D_MODEL = 1024
DEPTH = 2
PAST_LEN = 16384
GW = 256
HD = 64
NH = GW // HD
S5_GROUP = 16
S5_NG = GW // S5_GROUP
S5_P = 64
S5_W = S5_NG * S5_P
RW_PROJ = 4 * GW
IN_WIDTH = 13 * GW
MEM_LEN = 256
MEM_HEADS = 4
MEM_HD = D_MODEL // MEM_HEADS
D_FF = 4 * D_MODEL
EPS = 1e-6
RW_LN_EPS = 64e-5
ROPE_BASE = 10000.0
CHUNK = 16
RET_CHUNK = 64

VMEM_LIMIT = 56 * 1024 * 1024
SUBLANES = 8
LANES = 128
LOG2_E = 1.4426950408889634


def _cparams(sem):
    return pltpu.CompilerParams(dimension_semantics=sem, vmem_limit_bytes=VMEM_LIMIT)


def _mm(a, b):
    return jnp.dot(a.astype(BF16), b.astype(BF16), preferred_element_type=F32)


def _mm_nt(a, b):
    return lax.dot_general(a.astype(BF16), b.astype(BF16), (((1,), (1,)), ((), ())),
                           preferred_element_type=F32)


def _mm_tn(a, b):
    return lax.dot_general(a.astype(BF16), b.astype(BF16), (((0,), (0,)), ((), ())),
                           preferred_element_type=F32)


def _split3(x):
    hi = x.astype(BF16)
    r1 = x - hi.astype(F32)
    mid = r1.astype(BF16)
    lo = (r1 - mid.astype(F32)).astype(BF16)
    return hi, mid, lo


def _mm_exact_lhs(sel, x):
    s = sel.astype(BF16)
    hi, mid, lo = _split3(x)
    return (jnp.dot(s, hi, preferred_element_type=F32) + jnp.dot(s, mid, preferred_element_type=F32)
            + jnp.dot(s, lo, preferred_element_type=F32))


def _rms(x, gain):
    return x * lax.rsqrt(jnp.mean(x * x, axis=-1, keepdims=True) + EPS) * gain


def _sigmoid(x):
    return 1.0 / (1.0 + jnp.exp(-x))


def _iota2(shape, axis):
    return lax.broadcasted_iota(jnp.int32, shape, axis)


def _head_ones():
    return (_iota2((GW, GW), 0) // HD == _iota2((GW, GW), 1) // HD).astype(F32)


def _head_sum(x, ones_bd):
    s = ones_bd.astype(BF16)
    hi = x.astype(BF16)
    lo = (x - hi.astype(F32)).astype(BF16)
    return jnp.dot(hi, s, preferred_element_type=F32) + jnp.dot(lo, s, preferred_element_type=F32)


ROW_PARTS = 2


def _row_parts(n_rows, n_parts=ROW_PARTS):
    if n_rows % (n_parts * 2 * SUBLANES):
        return [slice(0, n_rows)]
    step = n_rows // n_parts
    return [slice(i * step, (i + 1) * step) for i in range(n_parts)]


def _norm_mm_kernel(x_ref, g_ref, w_ref, o_ref):
    parts = _row_parts(x_ref.shape[0])
    xn = [_rms(x_ref[r, :], g_ref[...]) for r in parts]
    for r, xp in zip(parts, xn):
        o_ref[r, :] = _mm(xp, w_ref[...])


def norm_matmul(x, gain, w, *, tm):
    t, d = x.shape
    n = w.shape[1]
    return pl.pallas_call(
        _norm_mm_kernel,
        out_shape=jax.ShapeDtypeStruct((t, n), F32),
        grid=(t // tm,),
        in_specs=[pl.BlockSpec((tm, d), lambda i: (i, 0)),
                  pl.BlockSpec((1, d), lambda i: (0, 0)),
                  pl.BlockSpec((d, n), lambda i: (0, 0))],
        out_specs=pl.BlockSpec((tm, n), lambda i: (i, 0)),
        compiler_params=_cparams(("parallel",)),
        name="norm_matmul",
    )(x, gain, w)


def _mm_kernel(a_ref, w_ref, o_ref):
    o_ref[...] = _mm(a_ref[...], w_ref[...])


def matmul_layers(a, w, *, tm):
    t, k = a.shape
    layers, _, n = w.shape
    return pl.pallas_call(
        _mm_kernel,
        out_shape=jax.ShapeDtypeStruct((layers, t, n), F32),
        grid=(layers, t // tm),
        in_specs=[pl.BlockSpec((tm, k), lambda l, i: (i, 0)),
                  pl.BlockSpec((None, k, n), lambda l, i: (l, 0, 0))],
        out_specs=pl.BlockSpec((None, tm, n), lambda l, i: (l, i, 0)),
        compiler_params=_cparams(("parallel", "parallel")),
        name="matmul_layers",
    )(a, w)


def _mm_res_kernel(a_ref, w_ref, r_ref, o_ref):
    o_ref[...] = r_ref[...] + _mm(a_ref[...], w_ref[...])


def matmul_residual(a, w, res, *, tm):
    t, k = a.shape
    n = w.shape[1]
    return pl.pallas_call(
        _mm_res_kernel,
        out_shape=jax.ShapeDtypeStruct((t, n), F32),
        grid=(t // tm,),
        in_specs=[pl.BlockSpec((tm, k), lambda i: (i, 0)),
                  pl.BlockSpec((k, n), lambda i: (0, 0)),
                  pl.BlockSpec((tm, n), lambda i: (i, 0))],
        out_specs=pl.BlockSpec((tm, n), lambda i: (i, 0)),
        compiler_params=_cparams(("parallel",)),
        name="matmul_residual",
    )(a, w, res)


def _mix_out_q_kernel(a0_ref, a1_ref, a2_ref, a3_ref, w_ref, r_ref, g_ref, wq_ref, x_ref, q_ref):
    parts = _row_parts(r_ref.shape[0])
    a = [jnp.concatenate([a0_ref[r, :], a1_ref[r, :], a2_ref[r, :], a3_ref[r, :]], axis=-1) for r in parts]
    x = [r_ref[r, :] + _mm(ap, w_ref[...]) for r, ap in zip(parts, a)]
    xn = [_rms(xp, g_ref[...]) for xp in x]
    for r, xp, xnp in zip(parts, x, xn):
        x_ref[r, :] = xp
        q_ref[r, :] = _mm(xnp, wq_ref[...]).astype(BF16)


def mix_out_q(parts, w, res, gain, w_q, *, tm):
    t, d = res.shape
    part = pl.BlockSpec((tm, GW), lambda i: (i, 0))
    mat = pl.BlockSpec((d, d), lambda i: (0, 0))
    tile = pl.BlockSpec((tm, d), lambda i: (i, 0))
    return pl.pallas_call(
        _mix_out_q_kernel,
        out_shape=(jax.ShapeDtypeStruct((t, d), F32), jax.ShapeDtypeStruct((t, d), BF16)),
        grid=(t // tm,),
        in_specs=[part, part, part, part, mat, tile, pl.BlockSpec((1, d), lambda i: (0, 0)), mat],
        out_specs=(tile, tile),
        compiler_params=_cparams(("parallel",)),
        name="mix_out_q",
    )(*parts, w, res, gain, w_q)


def _ffn_kernel(x_ref, g_ref, wu_ref, wd_ref, gf_ref, o_ref, *, final_norm):
    x = x_ref[...]
    h = jnp.dot(_rms(x, g_ref[...]).astype(BF16), wu_ref[...], preferred_element_type=F32)
    h = jnp.square(jnp.maximum(h, 0.0))
    y = x + jnp.dot(h.astype(BF16), wd_ref[...], preferred_element_type=F32)
    if final_norm:
        y = _rms(y, gf_ref[...])
    o_ref[...] = y


def ffn(x, gain, w_up, w_down, gain_final, *, tm, final_norm):
    t, d = x.shape
    ff = w_up.shape[1]
    resident = lambda shape: pl.BlockSpec(shape, lambda i: (0, 0), pipeline_mode=pl.Buffered(1))
    return pl.pallas_call(
        functools.partial(_ffn_kernel, final_norm=final_norm),
        out_shape=jax.ShapeDtypeStruct((t, d), F32),
        grid=(t // tm,),
        in_specs=[pl.BlockSpec((tm, d), lambda i: (i, 0)),
                  pl.BlockSpec((1, d), lambda i: (0, 0)),
                  resident((d, ff)), resident((ff, d)),
                  pl.BlockSpec((1, d), lambda i: (0, 0))],
        out_specs=pl.BlockSpec((tm, d), lambda i: (i, 0)),
        compiler_params=_cparams(("parallel",)),
        name="ffn",
    )(x, gain, w_up, w_down, gain_final)


def _attn_out_kernel(q_ref, k_ref, v_ref, wo_ref, r_ref, x_ref):
    sls = [slice(h * MEM_HD, (h + 1) * MEM_HD) for h in range(MEM_HEADS)]
    s = [_mm_nt(q_ref[:, sl], k_ref[:, sl]) * (MEM_HD ** -0.5) for sl in sls]
    p = [jnp.exp(sh - jnp.max(sh, axis=-1, keepdims=True)) for sh in s]
    p = [ph * (1.0 / jnp.sum(ph, axis=-1, keepdims=True)) for ph in p]
    heads = [_mm(ph, v_ref[:, sl]) for ph, sl in zip(p, sls)]
    x_ref[...] = r_ref[...] + _mm(jnp.concatenate(heads, axis=-1), wo_ref[...])


def cross_attention_out(q, mem_k, mem_v, layer, w_o, res, *, rows):
    t, d = res.shape
    bsz = mem_k.shape[1] // MEM_LEN
    lt = t // bsz // rows
    tile = pl.BlockSpec((rows, d), lambda b, l: (b * lt + l, 0))
    mem = pl.BlockSpec((None, MEM_LEN, d), lambda b, l: (layer, b, 0))
    return pl.pallas_call(
        _attn_out_kernel,
        out_shape=jax.ShapeDtypeStruct((t, d), F32),
        grid=(bsz, lt),
        in_specs=[tile, mem, mem, pl.BlockSpec((d, d), lambda b, l: (0, 0)), tile],
        out_specs=tile,
        compiler_params=_cparams(("parallel", "arbitrary")),
        name="cross_attention_out",
    )(q, mem_k, mem_v, w_o, res)


def _attn_cache_kernel(q_ref, k_ref, v_ref, o_ref, *, nb, rows):
    nr = MEM_HEADS * rows
    same_head = _iota2((nr, MEM_HEADS * MEM_LEN), 1) % MEM_HEADS == _iota2((nr, MEM_HEADS * MEM_LEN), 0) // rows
    q_all = q_ref[...].astype(F32)
    rng = range(nb)
    qs = [jnp.concatenate([q_all[b * rows:(b + 1) * rows, h * MEM_HD:(h + 1) * MEM_HD] for h in range(MEM_HEADS)],
                          axis=0) for b in rng]
    s = [_mm_nt(qs[b], k_ref[0, b].reshape(MEM_HEADS * MEM_LEN, MEM_HD)) * (MEM_HD ** -0.5) for b in rng]
    s = [jnp.where(same_head, sb, -1e30) for sb in s]
    p = [jnp.exp(sb - jnp.max(sb, axis=-1, keepdims=True)) for sb in s]
    p = [pb * (1.0 / jnp.sum(pb, axis=-1, keepdims=True)) for pb in p]
    o = [_mm(p[b], v_ref[0, b].reshape(MEM_HEADS * MEM_LEN, MEM_HD)) for b in rng]
    for b in rng:
        for h in range(MEM_HEADS):
            o_ref[b * rows:(b + 1) * rows, h * MEM_HD:(h + 1) * MEM_HD] = o[b][h * rows:(h + 1) * rows]


def cross_attention_cache(q, cache_k, cache_v, layer, *, nb, rows):
    t = q.shape[0]
    bsz = cache_k.shape[1]
    mem = pl.BlockSpec((1, nb, MEM_LEN, MEM_HEADS, MEM_HD), lambda b: (layer, b, 0, 0, 0))
    return pl.pallas_call(
        functools.partial(_attn_cache_kernel, nb=nb, rows=rows),
        out_shape=jax.ShapeDtypeStruct((t, D_MODEL), F32),
        grid=(bsz // nb,),
        in_specs=[pl.BlockSpec((nb * rows, D_MODEL), lambda b: (b, 0)), mem, mem],
        out_specs=pl.BlockSpec((nb * rows, D_MODEL), lambda b: (b, 0)),
        compiler_params=_cparams(("parallel",)),
        name="cross_attention_cache",
    )(q, cache_k, cache_v)


def _gelu_tanh(x):
    return 0.5 * x * (1.0 + jnp.tanh(math.sqrt(2.0 / math.pi) * (x + 0.044715 * (x * x * x))))


def _s5_kernel(u_ref, h0_ref, lam_ref, bblk_ref, cblk_ref, d_ref, wglu_ref, bglu_ref, gain_ref,
               y_ref, hfin_ref, scr_ref, tm_ref, *, steps, nb):
    @pl.when(pl.program_id(1) == 0)
    def _():
        scr_ref[0:nb, :] = h0_ref[...]

    def to_time_major(b, carry):
        for j in range(GW // LANES):
            tm_ref[j, pl.ds(b, steps, stride=nb), :] = u_ref[b, :, j * LANES:(j + 1) * LANES]
        return carry

    lax.fori_loop(0, nb, to_time_major, 0)
    u = jnp.concatenate([tm_ref[j] for j in range(GW // LANES)], axis=1)
    scr_ref[nb:, :] = _mm(u, bblk_ref[...])
    lam_re = jnp.broadcast_to(lam_ref[0:1, :], (nb, S5_W))
    lam_im = jnp.broadcast_to(lam_ref[1:2, :], (nb, S5_W))

    def step(t, carry):
        c0 = pl.multiple_of((t + 1) * nb, nb)
        if nb <= SUBLANES:
            h_re, h_im = carry
        else:
            p0 = pl.multiple_of(t * nb, nb)
            h_re, h_im = scr_ref[pl.ds(p0, nb), 0:S5_W], scr_ref[pl.ds(p0, nb), S5_W:]
        n_re = scr_ref[pl.ds(c0, nb), 0:S5_W] + lam_re * h_re - lam_im * h_im
        n_im = scr_ref[pl.ds(c0, nb), S5_W:] + lam_re * h_im + lam_im * h_re
        scr_ref[pl.ds(c0, nb), 0:S5_W] = n_re
        scr_ref[pl.ds(c0, nb), S5_W:] = n_im
        return (n_re, n_im) if nb <= SUBLANES else carry

    init = (scr_ref[0:nb, 0:S5_W], scr_ref[0:nb, S5_W:]) if nb <= SUBLANES else 0
    lax.fori_loop(0, steps, step, init)
    h_last = scr_ref[steps * nb:, :]
    hfin_ref[...] = h_last
    rows = _row_parts(steps * nb, 4)
    y = [_mm(scr_ref[nb + r.start:nb + r.stop, :], cblk_ref[...]) + d_ref[...] * u[r] for r in rows]
    y = [_gelu_tanh(yp) for yp in y]
    z = [_mm(yp, wglu_ref[...]) for yp in y]
    y = [_rms(yp * _sigmoid(zp + bglu_ref[...]), gain_ref[...]) for yp, zp in zip(y, z)]
    for r, yp in zip(rows, y):
        for j in range(GW // LANES):
            tm_ref[j, r, :] = yp[:, j * LANES:(j + 1) * LANES]

    def to_batch_major(b, carry):
        for j in range(GW // LANES):
            y_ref[b, :, j * LANES:(j + 1) * LANES] = tm_ref[j, pl.ds(b, steps, stride=nb), :]
        return carry

    lax.fori_loop(0, nb, to_batch_major, 0)
    scr_ref[0:nb, :] = h_last


def s5_mixer(proj, h0, lam, bblk, cblk, dvec, wglu, bglu, gain, *, steps, nb):
    bsz, seq, _ = proj.shape
    const = lambda shape: pl.BlockSpec(shape, lambda b, t: (0, 0))
    return pl.pallas_call(
        functools.partial(_s5_kernel, steps=steps, nb=nb),
        out_shape=(jax.ShapeDtypeStruct((bsz, seq, GW), F32),
                   jax.ShapeDtypeStruct((bsz, 2 * S5_W), F32)),
        grid=(bsz // nb, seq // steps),
        in_specs=[pl.BlockSpec((nb, steps, GW), lambda b, t: (b, t, 0)),
                  pl.BlockSpec((nb, 2 * S5_W), lambda b, t: (b, 0)),
                  const((2, S5_W)), const((GW, 2 * S5_W)), const((2 * S5_W, GW)), const((1, GW)),
                  const((GW, GW)), const((1, GW)), const((1, GW))],
        out_specs=(pl.BlockSpec((nb, steps, GW), lambda b, t: (b, t, 0)),
                   pl.BlockSpec((nb, 2 * S5_W), lambda b, t: (b, 0))),
        scratch_shapes=[pltpu.VMEM(((steps + 1) * nb, 2 * S5_W), F32),
                        pltpu.VMEM((GW // LANES, steps * nb, LANES), F32)],
        compiler_params=_cparams(("parallel", "arbitrary")),
        name="s5_mixer",
    )(proj, h0, lam, bblk, cblk, dvec, wglu, bglu, gain)


def s5_params(lam_re, lam_im, b_re, b_im, c_re, c_im, log_step):
    step = jnp.exp(log_step)[:, None]
    mag = jnp.exp(lam_re * step)
    lbar_re = mag * jnp.cos(lam_im * step)
    lbar_im = mag * jnp.sin(lam_im * step)
    den = lam_re * lam_re + lam_im * lam_im
    f_re = ((lbar_re - 1.0) * lam_re + lbar_im * lam_im) / den
    f_im = (lbar_im * lam_re - (lbar_re - 1.0) * lam_im) / den
    bbar_re = f_re[..., None] * b_re - f_im[..., None] * b_im
    bbar_im = f_re[..., None] * b_im + f_im[..., None] * b_re
    eye_g = jnp.eye(S5_NG, dtype=F32)

    def in_blk(m):
        return jnp.einsum('gph,gk->ghkp', m, eye_g).reshape(GW, S5_W)

    def out_blk(m):
        return jnp.einsum('ghp,gk->gpkh', m, eye_g).reshape(S5_W, GW)

    bblk = jnp.concatenate([in_blk(bbar_re), in_blk(bbar_im)], axis=1)
    cblk = jnp.concatenate([out_blk(c_re), -out_blk(c_im)], axis=0)
    lam2 = jnp.stack([lbar_re.reshape(S5_W), lbar_im.reshape(S5_W)])
    return lam2, bblk.astype(BF16), cblk.astype(BF16)


def s5_apply(proj, h0_re, h0_im, p, l, *, steps, nb):
    bsz = proj.shape[0]
    lam2, bblk, cblk = s5_params(p['s5_lam_re'][l], p['s5_lam_im'][l], p['s5_b_re'][l], p['s5_b_im'][l],
                                 p['s5_c_re'][l], p['s5_c_im'][l], p['s5_log_step'][l])
    h0 = jnp.concatenate([h0_re.reshape(bsz, S5_W), h0_im.reshape(bsz, S5_W)], axis=1)
    y, h = s5_mixer(proj, h0, lam2, bblk, cblk, p['s5_d'][l].reshape(1, GW),
                    p['s5_w_glu'][l].astype(BF16), p['s5_b_glu'][l].reshape(1, GW),
                    p['s5_norm'][l].reshape(1, GW), steps=steps, nb=nb)
    return y, h[:, :S5_W].reshape(bsz, S5_NG, S5_P), h[:, S5_W:].reshape(bsz, S5_NG, S5_P)


def _load_state(s0_ref, st_ref, nb, transpose):
    for b in range(nb):
        rows = []
        for h in range(NH):
            pieces = [s0_ref[b, h]]
            if h:
                pieces.insert(0, jnp.zeros((HD, h * HD), F32))
            if h < NH - 1:
                pieces.append(jnp.zeros((HD, (NH - 1 - h) * HD), F32))
            rows.append(jnp.concatenate(pieces, axis=1))
        st = jnp.concatenate(rows, axis=0)
        st_ref[b] = st.T if transpose else st


def _store_state(st_ref, sfin_ref, nb, transpose, own_layer):
    if own_layer is not None:
        for other in range(DEPTH):
            if other != own_layer:
                sfin_ref[other] = jnp.zeros(sfin_ref.shape[1:], F32)
        sfin_ref = sfin_ref.at[own_layer]
    for b in range(nb):
        st = st_ref[b].T if transpose else st_ref[b]
        for h in range(NH):
            sfin_ref[b, h] = st[h * HD:(h + 1) * HD, h * HD:(h + 1) * HD]


def _head_masks():
    lane_head = _iota2((1, GW), 1) // HD
    return [(lane_head == h).astype(F32) for h in range(NH)]


def _stack_heads(x, masks):
    return jnp.concatenate([x * m for m in masks], axis=0)


def _pad_rows(x, rows):
    if x.shape[0] == rows:
        return x
    return jnp.concatenate([x, jnp.zeros((rows - x.shape[0], x.shape[1]), x.dtype)], axis=0)


def _silu(x):
    return x * _sigmoid(x)


def _chunk_rows(ref, b, r0, rows, c):
    return _pad_rows(ref[b, pl.ds(r0, rows), :], c)


def _ret_kernel(q_ref, k_ref, v_ref, g_ref, cos_ref, sin_ref, s0_ref, _states_in, o_ref, sfin_ref, st_ref,
                *, nb, tb, c, c_real, own_layer):
    rows = min(tb, c)

    @pl.when(pl.program_id(1) == 0)
    def _():
        _load_state(s0_ref, st_ref, nb, transpose=False)

    masks = _head_masks()
    ones_bd = _head_ones()
    lane_head = _iota2((1, GW), 1) // HD
    log_gamma = jnp.zeros((1, GW), F32)
    for h in range(NH):
        log_gamma = jnp.where(lane_head == h, math.log(1.0 - 2.0 ** (-5.0 - h)), log_gamma)
    tt = _iota2((c, GW), 0).astype(F32)
    scale = HD ** -0.5
    g_q = jnp.exp(log_gamma * (tt + 1.0))
    g_k = jnp.exp(-log_gamma * (tt + 1.0)) * scale
    g_tail = jnp.exp(log_gamma * (c_real - 1.0 - tt)) * scale
    row_head = _iota2((GW, GW), 0) // HD
    g_chunk = jnp.zeros((GW, GW), F32)
    for h in range(NH):
        g_chunk = jnp.where(row_head == h, math.exp(math.log(1.0 - 2.0 ** (-5.0 - h)) * c_real), g_chunk)
    causal = _iota2((c, NH * c), 0) >= _iota2((c, NH * c), 1) % c
    first_half = _iota2((c, GW), 1) % HD < HD // 2

    def rope(x, cos, sin):
        swapped = jnp.where(first_half, pltpu.roll(x, GW - HD // 2, axis=1), pltpu.roll(x, HD // 2, axis=1))
        return x * cos + swapped * sin

    def chunk(ci, carry):
        r0 = pl.multiple_of(ci * rows, rows)
        cos = _pad_rows(cos_ref[pl.ds(r0, rows), :], c)
        sin = _pad_rows(sin_ref[pl.ds(r0, rows), :], c)
        rng = range(nb)
        qt = [rope(_chunk_rows(q_ref, b, r0, rows, c), cos, sin) * g_q for b in rng]
        k = [rope(_chunk_rows(k_ref, b, r0, rows, c), cos, sin) for b in rng]
        v = [_chunk_rows(v_ref, b, r0, rows, c) for b in rng]
        st = [st_ref[b] for b in rng]
        sc = [_mm_nt(qt[b], _stack_heads(k[b] * g_k, masks)) for b in rng]
        o_in = [_mm(jnp.where(causal, sc[b], 0.0), _stack_heads(v[b], masks)) for b in rng]
        o_st = [_mm(qt[b], st[b]) for b in rng]
        d_st = [_mm_tn(k[b] * g_tail, v[b]) for b in rng]
        o = [o_in[b] + o_st[b] for b in rng]
        ms = [_head_sum(o[b] * o[b], ones_bd) for b in rng]
        for b in rng:
            st_ref[b] = st[b] * g_chunk + ones_bd * d_st[b]
            ob = o[b] * lax.rsqrt(ms[b] * (1.0 / HD) + EPS) * _silu(_chunk_rows(g_ref, b, r0, rows, c))
            o_ref[b, pl.ds(r0, rows), :] = ob[:rows]
        return carry

    lax.fori_loop(0, tb // rows, chunk, 0)

    @pl.when(pl.program_id(1) == pl.num_programs(1) - 1)
    def _():
        _store_state(st_ref, sfin_ref, nb, transpose=False, own_layer=own_layer)


def _rope_tables(pos0, seq):
    half = HD // 2
    inv = ROPE_BASE ** (-jnp.arange(half, dtype=F32) / half)
    pos = pos0 + jnp.arange(seq, dtype=jnp.int32)
    ang = pos.astype(F32)[:, None] * inv[None, :]
    cos, sin = jnp.cos(ang), jnp.sin(ang)
    return (jnp.tile(jnp.concatenate([cos, cos], axis=-1), (1, NH)),
            jnp.tile(jnp.concatenate([-sin, sin], axis=-1), (1, NH)))


def _proj_spec(nb, tb, col):
    return pl.BlockSpec((nb, tb, GW), lambda b, t: (b, t, col))


def _state_spec(nb, layer):
    return pl.BlockSpec((None, nb, NH, HD, HD), lambda b, t: (layer, b, 0, 0, 0))


_ALIASED = pl.BlockSpec(memory_space=pl.ANY)


def _collector(states, bsz, nb, layer, operand_index):
    shape = jax.ShapeDtypeStruct((DEPTH, bsz, NH, HD, HD), F32)
    if states is None:
        spec = pl.BlockSpec((DEPTH, nb, NH, HD, HD), lambda b, t: (0, b, 0, 0, 0))
        return jnp.zeros((1, 1, NH, HD, HD), F32), spec, shape, {}, layer
    return states, _state_spec(nb, layer), shape, {operand_index: 1}, None


def retention_apply(proj, s0, s0_layer, states, layer, pos0, *, tb, nb):
    bsz, seq, _ = proj.shape
    c = RET_CHUNK if tb >= RET_CHUNK else CHUNK
    cos, sin = _rope_tables(pos0, seq)
    states, st_spec, st_shape, aliases, own_layer = _collector(states, bsz, nb, layer, 7)
    return pl.pallas_call(
        functools.partial(_ret_kernel, nb=nb, tb=tb, c=c, c_real=min(tb, c), own_layer=own_layer),
        out_shape=(jax.ShapeDtypeStruct((bsz, seq, GW), F32), st_shape),
        grid=(bsz // nb, seq // tb),
        in_specs=[_proj_spec(nb, tb, 5), _proj_spec(nb, tb, 6), _proj_spec(nb, tb, 7), _proj_spec(nb, tb, 8),
                  pl.BlockSpec((tb, GW), lambda b, t: (t, 0)), pl.BlockSpec((tb, GW), lambda b, t: (t, 0)),
                  _state_spec(nb, s0_layer), _ALIASED],
        out_specs=(pl.BlockSpec((nb, tb, GW), lambda b, t: (b, t, 0)), st_spec),
        scratch_shapes=[pltpu.VMEM((nb, GW, GW), F32)],
        input_output_aliases=aliases,
        compiler_params=_cparams(("parallel", "arbitrary")),
        name="retention",
    )(proj, proj, proj, proj, cos, sin, s0, states)


def _log_sigmoid(z):
    return jnp.minimum(z, 0.0) - jnp.log(1.0 + jnp.exp(-jnp.abs(z)))


def _hgrn_kernel(q_ref, f_ref, i_ref, g_ref, lb_ref, gain_ref, s0_ref, _states_in, o_ref, sfin_ref, st_ref,
                 *, nb, tb, c, own_layer):
    rows = min(tb, c)

    @pl.when(pl.program_id(1) == 0)
    def _():
        _load_state(s0_ref, st_ref, nb, transpose=True)

    t_hi = -(-rows // SUBLANES) * SUBLANES
    ones_bd = _head_ones()
    tri = (_iota2((c, c), 0) >= _iota2((c, c), 1)).astype(F32)
    t_idx = _iota2((c, GW), 0)
    lb = lb_ref[...]
    log_lb = jnp.log(lb)
    log_1m_lb = jnp.log(1.0 - lb)
    gain = gain_ref[...]

    def chunk(ci, carry):
        r0 = pl.multiple_of(ci * rows, rows)
        rng = range(nb)
        valid = t_idx < rows

        def gates(b):
            z = _chunk_rows(f_ref, b, r0, rows, c)
            ls_pos = _log_sigmoid(z)
            ls_neg = ls_pos - z
            b2 = log_lb + ls_neg
            log_f = jnp.maximum(ls_pos, b2) + jnp.log(1.0 + jnp.exp(-jnp.abs(ls_pos - b2)))
            return (jnp.where(valid, log_f, 0.0), jnp.where(valid, (1.0 - lb) * jnp.exp(ls_neg), 0.0),
                    ls_neg + log_1m_lb)

        log_f, key, log_key = zip(*[gates(b) for b in rng])
        q = [_silu(_chunk_rows(q_ref, b, r0, rows, c)) for b in rng]
        v = [_chunk_rows(i_ref, b, r0, rows, c) for b in rng]
        st = [st_ref[b] for b in rng]
        cum = [_mm_exact_lhs(tri, log_f[b]) for b in rng]
        last = [cum[b][c - 1:c, :] for b in rng]
        cum2 = [cum[b] * LOG2_E for b in rng]
        kd2 = [(log_key[b] - cum[b]) * LOG2_E for b in rng]

        def pair_rows(b, s):
            lo = SUBLANES * (s // SUBLANES)
            t_rows = _iota2((t_hi - lo, GW), 0) + lo
            key_decay = jnp.where(t_rows >= s, jnp.exp2(cum2[b][lo:t_hi] + kd2[b][s:s + 1, :]), 0.0)
            return key_decay * q[b][lo:t_hi]

        sc = [_mm(jnp.concatenate([pair_rows(b, s) for s in range(rows)], axis=0), ones_bd) for b in rng]
        o_st = [_mm_nt(q[b] * jnp.exp(cum[b]), st[b]) for b in rng]
        d_st = [_mm_tn(v[b], key[b] * jnp.exp(last[b] - cum[b])) for b in rng]

        def intra(b):
            tiles = [None] * (c // SUBLANES)
            off = 0
            for s in range(rows):
                for ti in range(s // SUBLANES, t_hi // SUBLANES):
                    term = sc[b][off:off + SUBLANES, :] * v[b][s:s + 1, :]
                    tiles[ti] = term if tiles[ti] is None else tiles[ti] + term
                    off += SUBLANES
            zero = jnp.zeros((SUBLANES, GW), F32)
            return jnp.concatenate([zero if t is None else t for t in tiles], axis=0)

        o = [o_st[b] + intra(b) for b in rng]
        ms = [_head_sum(o[b] * o[b], ones_bd) for b in rng]
        for b in rng:
            st_ref[b] = st[b] * jnp.exp(last[b]) + ones_bd * d_st[b]
            ob = o[b] * lax.rsqrt(ms[b] * (1.0 / HD) + EPS) * gain * _silu(_chunk_rows(g_ref, b, r0, rows, c))
            o_ref[b, pl.ds(r0, rows), :] = ob[:rows]
        return carry

    lax.fori_loop(0, tb // rows, chunk, 0)

    @pl.when(pl.program_id(1) == pl.num_programs(1) - 1)
    def _():
        _store_state(st_ref, sfin_ref, nb, transpose=True, own_layer=own_layer)


def hgrn_apply(proj, s0, s0_layer, states, layer, lb, gain, *, tb, nb):
    bsz, seq, _ = proj.shape
    row = pl.BlockSpec((1, GW), lambda b, t: (0, 0))
    states, st_spec, st_shape, aliases, own_layer = _collector(states, bsz, nb, layer, 7)
    return pl.pallas_call(
        functools.partial(_hgrn_kernel, nb=nb, tb=tb, c=CHUNK, own_layer=own_layer),
        out_shape=(jax.ShapeDtypeStruct((bsz, seq, GW), F32), st_shape),
        grid=(bsz // nb, seq // tb),
        in_specs=[_proj_spec(nb, tb, 1), _proj_spec(nb, tb, 2), _proj_spec(nb, tb, 3), _proj_spec(nb, tb, 4),
                  row, row, _state_spec(nb, s0_layer), _ALIASED],
        out_specs=(pl.BlockSpec((nb, tb, GW), lambda b, t: (b, t, 0)), st_spec),
        scratch_shapes=[pltpu.VMEM((nb, GW, GW), F32)],
        input_output_aliases=aliases,
        compiler_params=_cparams(("parallel", "arbitrary")),
        name="hgrn2",
    )(proj, proj, proj, proj, lb.reshape(1, GW), gain.reshape(1, GW), s0, states)


def _softplus(z):
    return jnp.maximum(z, 0.0) + jnp.log(1.0 + jnp.exp(-jnp.abs(z)))


def _rwkv_prepare(ins, consts):
    masks, ones_bd, bd_mask, eye_flat, tri, strict, incl = consts
    rng = range(len(ins))
    r, lw, k, v, kk, a = [[i[j] for i in ins] for j in range(6)]
    c = r[0].shape[0]
    n = NH * c

    def to_bd(flat):
        return jnp.concatenate([flat] * NH, axis=0) * bd_mask

    cum = [_mm_exact_lhs(tri, lw[i]) for i in rng]
    last = [cum[i][c - 1:c, :] for i in rng]
    p_inv = [jnp.exp(-cum[i]) for i in rng]
    p_tail = [jnp.exp(last[i] - cum[i]) for i in rng]
    ka = [kk[i] * a[i] for i in rng]
    x = [jnp.concatenate([kk[i] * jnp.exp(cum[i] - lw[i]), r[i] * jnp.exp(cum[i])], axis=0) for i in rng]
    g = [_mm_nt(x[i], jnp.concatenate([_stack_heads(ka[i] * p_inv[i], masks),
                                       _stack_heads(k[i] * p_inv[i], masks)], axis=0)) for i in rng]
    m_ak = [jnp.where(strict, g[i][:c], 0.0) for i in rng]
    n_ak = [jnp.where(incl, g[i][c:], 0.0) for i in rng]
    pw = [m_ak[i][:, :n] for i in rng]
    t_inv = [eye_flat - pw[i] for i in rng]
    pw_bd = [to_bd(pw[i]) for i in rng]
    for _ in range(int(math.log2(c)) - 1):
        pw = [_mm(pw[i], pw_bd[i]) for i in rng]
        pw_bd = [to_bd(pw[i]) for i in rng]
        t_inv = [t_inv[i] + _mm(t_inv[i], pw_bd[i]) for i in rng]
    v_stk = [_stack_heads(v[i], masks) for i in rng]
    zeros = jnp.zeros((n, GW), F32)
    mkv = [_mm(m_ak[i], jnp.concatenate([zeros, v_stk[i]], axis=0)) for i in rng]
    tmkv = [_mm(t_inv[i], _stack_heads(mkv[i], masks)) for i in rng]
    return [dict(x=x[i], t_inv=t_inv[i], tmkv=tmkv[i], n_ak=n_ak[i], v=v[i], v_stk=v_stk[i],
                 decay=jnp.exp(last[i]),
                 k_tail=jnp.concatenate([ka[i] * p_tail[i], k[i] * p_tail[i]], axis=0)) for i in rng]


def _rwkv_advance(prep, st, consts):
    masks, ones_bd = consts[0], consts[1]
    rng = range(len(prep))
    c = prep[0]['t_inv'].shape[0]
    xa = [_mm_nt(prep[i]['x'], st[i]) for i in rng]
    u = [-_mm(prep[i]['t_inv'], _stack_heads(xa[i][:c], masks)) - prep[i]['tmkv'] for i in rng]
    nuv = [_mm(prep[i]['n_ak'], jnp.concatenate([_stack_heads(u[i], masks), prep[i]['v_stk']], axis=0))
           for i in rng]
    d_uv = [_mm_tn(jnp.concatenate([u[i], prep[i]['v']], axis=0), prep[i]['k_tail']) for i in rng]
    return [(xa[i][c:] + nuv[i], st[i] * prep[i]['decay'] + ones_bd * d_uv[i]) for i in rng]


def _rwkv_kernel(x_r_ref, x_k_ref, x_v_ref, x_l_ref, sh0_ref, s0_ref, mu_ref, w0_ref, ww_ref, a0_ref,
                 wa_ref, wg_ref, kk_ref, ka_ref, rk_ref, lng_ref, lnb_ref, _states_in,
                 o_ref, sfin_ref, shfin_ref,
                 st_ref, sh_ref, r_s, lw_s, k_s, v_s, kkn_s, a_s, y_s, bonus_s, gate_s, *, nb, tb, c, own_layer):
    tbp = max(tb, c)

    @pl.when(pl.program_id(1) == 0)
    def _():
        _load_state(s0_ref, st_ref, nb, transpose=False)
        sh_ref[...] = sh0_ref[...]

    masks = _head_masks()
    ones_bd = _head_ones()
    n = NH * c
    row_t = _iota2((c, 2 * n), 0)
    col_t = _iota2((c, 2 * n), 1) % c
    bd_mask = (_iota2((n, n), 0) // c == _iota2((n, n), 1) // c).astype(F32)
    eye_flat = (_iota2((c, n), 1) % c == _iota2((c, n), 0)).astype(F32)
    consts = (masks, ones_bd, bd_mask, eye_flat, (_iota2((c, c), 0) >= _iota2((c, c), 1)).astype(F32),
              col_t < row_t, col_t <= row_t)
    first_row = _iota2((nb * tb, GW), 0) % tb == 0

    def mixed(x_ref, j):
        x = x_ref[...].reshape(nb * tb, GW)
        carried = jnp.broadcast_to(sh_ref[:, j:j + 1, :], (nb, tb, GW)).reshape(nb * tb, GW)
        prev = jnp.where(first_row, carried, pltpu.roll(x, 1, axis=0))
        sh_ref[:, j:j + 1, :] = x_ref[:, tb - 1:tb, :]
        return x + (prev - x) * mu_ref[j:j + 1, :]

    r = mixed(x_r_ref, 0)
    k = mixed(x_k_ref, 1)
    v = mixed(x_v_ref, 2)
    xl = mixed(x_l_ref, 3)
    log_w = -_softplus(-(w0_ref[...] + _mm(jnp.tanh(xl), ww_ref[...]))) - 0.5
    a = _sigmoid(a0_ref[...] + _mm(xl, wa_ref[...]))
    gate_s[...] = _mm(_sigmoid(xl), wg_ref[...]).reshape(nb, tb, GW)
    kk = k * kk_ref[...]
    kk = kk * lax.rsqrt(jnp.maximum(_head_sum(kk * kk, ones_bd), 1e-24))
    k = k * (1.0 + (a - 1.0) * ka_ref[...])
    bonus_s[...] = (_head_sum(r * k * rk_ref[...], ones_bd) * v).reshape(nb, tb, GW)
    for s, val in ((r_s, r), (lw_s, -jnp.exp(log_w)), (k_s, k), (v_s, v), (kkn_s, kk), (a_s, a)):
        s[:, 0:tb, :] = val.reshape(nb, tb, GW)
        if tbp > tb:
            s[:, tb:tbp, :] = jnp.zeros((nb, tbp - tb, GW), F32)

    n_chunks = tbp // c
    group = 4 if n_chunks % 4 == 0 else 1

    def chunks(gi, carry):
        r0 = [pl.multiple_of((gi * group + j) * c, c) for j in range(group)]
        prep = _rwkv_prepare([tuple(s[b, pl.ds(r0[j], c), :] for s in (r_s, lw_s, k_s, v_s, kkn_s, a_s))
                              for j in range(group) for b in range(nb)], consts)
        st = [st_ref[b] for b in range(nb)]
        for j in range(group):
            outs = _rwkv_advance(prep[j * nb:(j + 1) * nb], st, consts)
            st = [o[1] for o in outs]
            for b in range(nb):
                y_s[b, pl.ds(r0[j], c), :] = outs[b][0]
        for b in range(nb):
            st_ref[b] = st[b]
        return carry

    lax.fori_loop(0, n_chunks // group, chunks, 0)

    y = y_s[:, 0:tb, :].reshape(nb * tb, GW)
    mean = _head_sum(y, ones_bd) * (1.0 / HD)
    d = y - mean
    var = _head_sum(d * d, ones_bd) * (1.0 / HD)
    y = d * lax.rsqrt(var + RW_LN_EPS) * lng_ref[...] + lnb_ref[...]
    o_ref[...] = ((y.reshape(nb, tb, GW) + bonus_s[...]) * gate_s[...])
    shfin_ref[...] = sh_ref[...]

    @pl.when(pl.program_id(1) == pl.num_programs(1) - 1)
    def _():
        _store_state(st_ref, sfin_ref, nb, transpose=False, own_layer=own_layer)


def rwkv_apply(proj, s0, s0_layer, states, shift0, p, l, *, tb, nb):
    bsz, seq, _ = proj.shape
    states, st_spec, st_shape, aliases, own_layer = _collector(states, bsz, nb, l, 17)
    c = CHUNK
    tbp = max(tb, c)
    row = pl.BlockSpec((1, GW), lambda b, t: (0, 0))
    mat = pl.BlockSpec((GW, GW), lambda b, t: (0, 0))
    sh_spec = pl.BlockSpec((nb, 4, GW), lambda b, t: (b, 0, 0))
    zeros = lambda r: jnp.zeros((r, GW), F32)
    ww = jnp.concatenate([p['rw_w_w2'][l], zeros(192)], axis=0).astype(BF16)
    wa = jnp.concatenate([zeros(64), p['rw_w_a2'][l], zeros(128)], axis=0).astype(BF16)
    wg = jnp.concatenate([zeros(128), p['rw_w_g2'][l]], axis=0).astype(BF16)
    r1 = lambda name: p[name][l].reshape(1, GW)
    seq_buf = pltpu.VMEM((nb, tbp, GW), F32)
    blk_buf = pltpu.VMEM((nb, tb, GW), F32)
    y, st, sh = pl.pallas_call(
        functools.partial(_rwkv_kernel, nb=nb, tb=tb, c=c, own_layer=own_layer),
        out_shape=(jax.ShapeDtypeStruct((bsz, seq, GW), F32), st_shape,
                   jax.ShapeDtypeStruct((bsz, 4, GW), F32)),
        grid=(bsz // nb, seq // tb),
        in_specs=[_proj_spec(nb, tb, 9), _proj_spec(nb, tb, 10), _proj_spec(nb, tb, 11), _proj_spec(nb, tb, 12),
                  sh_spec, _state_spec(nb, s0_layer), pl.BlockSpec((4, GW), lambda b, t: (0, 0)),
                  row, mat, row, mat, mat, row, row, row, row, row, _ALIASED],
        out_specs=(pl.BlockSpec((nb, tb, GW), lambda b, t: (b, t, 0)), st_spec, sh_spec),
        scratch_shapes=[pltpu.VMEM((nb, GW, GW), F32), pltpu.VMEM((nb, 4, GW), F32),
                        seq_buf, seq_buf, seq_buf, seq_buf, seq_buf, seq_buf, seq_buf, blk_buf, blk_buf],
        input_output_aliases=aliases,
        compiler_params=_cparams(("parallel", "arbitrary")),
        name="rwkv7",
    )(proj, proj, proj, proj, shift0.reshape(bsz, 4, GW), s0, p['rw_mu'][l].reshape(4, GW),
      r1('rw_w0'), ww, r1('rw_a0'), wa, wg, r1('rw_k_k'), r1('rw_k_a'), r1('rw_r_k'), r1('rw_ln_g'), r1('rw_ln_b'),
      states)
    return y, st, sh.reshape(bsz, RW_PROJ)


TOKEN_TILE = 512
TIME_BLOCK = 128
CACHE_ATTN_SEQS = 4


def _tiles(bsz, seq):
    if seq >= TIME_BLOCK:
        return dict(tb=TIME_BLOCK, s5_nb=bsz, mix_nb=bsz, rw_nb=bsz)
    return dict(tb=seq, s5_nb=bsz, mix_nb=8, rw_nb=16)


def _trunk_layer(x, bsz, seq, pos0, attend, st, mats, p, wb, l, lb, final_norm):
    cfg = _tiles(bsz, seq)
    tm = TOKEN_TILE
    row = lambda name: p[name][l].reshape(1, -1)
    proj = norm_matmul(x, row('norm_mix'), wb['w_in'][l], tm=tm)
    proj = proj.reshape(bsz, seq, IN_WIDTH)
    y_s5, s5_re, s5_im = s5_apply(proj, st['s5_re'], st['s5_im'], p, l, steps=cfg['tb'], nb=cfg['s5_nb'])
    y_hg, hg_s = hgrn_apply(proj, st['hgrn'], st['layer'], mats['hgrn'], l, lb, p['hg_norm'][l],
                            tb=cfg['tb'], nb=cfg['mix_nb'])
    y_rt, rt_s = retention_apply(proj, st['ret'], st['layer'], mats['ret'], l, pos0,
                                 tb=cfg['tb'], nb=cfg['mix_nb'])
    y_rw, rw_s, shift = rwkv_apply(proj, st['rwkv'], st['layer'], mats['rwkv'], st['shift'], p, l,
                                   tb=cfg['tb'], nb=cfg['rw_nb'])
    parts = [y.reshape(bsz * seq, GW) for y in (y_s5, y_hg, y_rt, y_rw)]
    x, q = mix_out_q(parts, wb['w_out'][l], x, row('norm_mem'), wb['mem_w_q'][l], tm=tm)
    x = attend(q, wb['mem_w_o'][l], x)
    x = ffn(x, row('norm_ffn'), wb['ffn_w_up'][l], wb['ffn_w_down'][l], p['norm_final'].reshape(1, -1),
            tm=tm, final_norm=final_norm)
    return x, (s5_re, s5_im, shift), dict(hgrn=hg_s, ret=rt_s, rwkv=rw_s)


def kernel(x_prompt, x_sample, mem_prompt, state_s5_re, state_s5_im, state_hgrn, state_ret, state_rwkv,
           state_rwkv_shift, cache_mem_k, cache_mem_v, norm_mix, w_in, w_out, s5_lam_re, s5_lam_im,
           s5_b_re, s5_b_im, s5_c_re, s5_c_im, s5_d, s5_log_step, s5_w_glu, s5_b_glu, s5_norm,
           hg_lb_logits, hg_norm, rw_mu, rw_w0, rw_w_w2, rw_a0, rw_w_a2, rw_w_g2, rw_k_k, rw_k_a, rw_r_k,
           rw_ln_g, rw_ln_b, norm_mem, mem_w_q, mem_w_k, mem_w_v, mem_w_o, norm_ffn, ffn_w_up, ffn_w_down,
           norm_final):
    p = dict(norm_mix=norm_mix, s5_lam_re=s5_lam_re, s5_lam_im=s5_lam_im, s5_b_re=s5_b_re, s5_b_im=s5_b_im,
             s5_c_re=s5_c_re, s5_c_im=s5_c_im, s5_d=s5_d, s5_log_step=s5_log_step, s5_w_glu=s5_w_glu,
             s5_b_glu=s5_b_glu, s5_norm=s5_norm, hg_norm=hg_norm, rw_mu=rw_mu, rw_w0=rw_w0, rw_w_w2=rw_w_w2,
             rw_a0=rw_a0, rw_w_a2=rw_w_a2, rw_w_g2=rw_w_g2, rw_k_k=rw_k_k, rw_k_a=rw_k_a, rw_r_k=rw_r_k,
             rw_ln_g=rw_ln_g, rw_ln_b=rw_ln_b, norm_mem=norm_mem, norm_ffn=norm_ffn, norm_final=norm_final)
    wb = {name: w.astype(BF16) for name, w in dict(
        w_in=w_in, w_out=w_out, mem_w_q=mem_w_q, mem_w_k=mem_w_k, mem_w_v=mem_w_v, mem_w_o=mem_w_o,
        ffn_w_up=ffn_w_up, ffn_w_down=ffn_w_down).items()}
    lb_all = jnp.cumsum(jax.nn.softmax(hg_lb_logits.astype(F32), axis=0), axis=0)
    lb_all = lb_all - lb_all[0:1]

    bp, lp, _ = x_prompt.shape
    bs, ls, _ = x_sample.shape
    yp = x_prompt.reshape(bp * lp, D_MODEL)
    ys = x_sample.reshape(bs * ls, D_MODEL)
    mem2d = mem_prompt.reshape(bp * MEM_LEN, D_MODEL)
    mat_zero = jnp.zeros((1, bp, NH, HD, HD), F32)
    p_small, s_small = [], []
    p_mats = dict(hgrn=None, ret=None, rwkv=None)
    s_mats = dict(hgrn=None, ret=None, rwkv=None)
    mk = matmul_layers(mem2d, wb['mem_w_k'], tm=TOKEN_TILE)
    mv = matmul_layers(mem2d, wb['mem_w_v'], tm=TOKEN_TILE)
    for l in range(DEPTH):
        final = l == DEPTH - 1
        def attend_p(q, w_o, x, l=l):
            return cross_attention_out(q, mk, mv, l, w_o, x, rows=TOKEN_TILE)

        zero_state = dict(s5_re=jnp.zeros((bp, S5_NG, S5_P), F32), s5_im=jnp.zeros((bp, S5_NG, S5_P), F32),
                          shift=jnp.zeros((bp, RW_PROJ), F32), hgrn=mat_zero, ret=mat_zero, rwkv=mat_zero, layer=0)
        yp, small, p_mats = _trunk_layer(yp, bp, lp, 0, attend_p, zero_state, p_mats, p, wb, l, lb_all[l], final)
        p_small.append(small)
        sst = dict(s5_re=state_s5_re[l], s5_im=state_s5_im[l], shift=state_rwkv_shift[l],
                   hgrn=state_hgrn, ret=state_ret, rwkv=state_rwkv, layer=l)
        def attend_s(q, w_o, x, l=l):
            o = cross_attention_cache(q, cache_mem_k, cache_mem_v, l, nb=CACHE_ATTN_SEQS, rows=ls)
            return matmul_residual(o, w_o, x, tm=TOKEN_TILE)

        ys, small, s_mats = _trunk_layer(ys, bs, ls, PAST_LEN, attend_s, sst, s_mats, p, wb, l, lb_all[l], final)
        s_small.append(small)
    stack = lambda states, i: jnp.stack([s[i] for s in states])
    return (yp.reshape(bp, lp, D_MODEL), ys.reshape(bs, ls, D_MODEL),
            stack(p_small, 0), stack(p_small, 1), p_mats['hgrn'], p_mats['ret'], p_mats['rwkv'], stack(p_small, 2),
            mk.reshape(DEPTH, bp, MEM_LEN, MEM_HEADS, MEM_HD), mv.reshape(DEPTH, bp, MEM_LEN, MEM_HEADS, MEM_HD),
            stack(s_small, 0), stack(s_small, 1), s_mats['hgrn'], s_mats['ret'], s_mats['rwkv'], stack(s_small, 2))
```

```python
import functools
import math

import jax
import jax.numpy as jnp
from jax import lax
from jax.experimental import pallas as pl
from jax.experimental.pallas import tpu as pltpu

F32 = jnp.float32
BF16 = jnp.bfloat16

D_MODEL = 1024
DEPTH = 2
PAST_LEN = 16384
GW = 256
HD = 64
NH = GW // HD
S5_GROUP = 16
S5_NG = GW // S5_GROUP
S5_P = 64
S5_W = S5_NG * S5_P
RW_PROJ = 4 * GW
IN_WIDTH = 13 * GW
MEM_LEN = 256
MEM_HEADS = 4
MEM_HD = D_MODEL // MEM_HEADS
D_FF = 4 * D_MODEL
EPS = 1e-6
RW_LN_EPS = 64e-5
ROPE_BASE = 10000.0
CHUNK = 16
RET_CHUNK = 64

VMEM_LIMIT = 56 * 1024 * 1024
SUBLANES = 8
LANES = 128
LOG2_E = 1.4426950408889634


def _cparams(sem):
    return pltpu.CompilerParams(dimension_semantics=sem, vmem_limit_bytes=VMEM_LIMIT)


def _mm(a, b):
    return jnp.dot(a.astype(BF16), b.astype(BF16), preferred_element_type=F32)


def _mm_nt(a, b):
    return lax.dot_general(a.astype(BF16), b.astype(BF16), (((1,), (1,)), ((), ())),
                           preferred_element_type=F32)


def _mm_tn(a, b):
    return lax.dot_general(a.astype(BF16), b.astype(BF16), (((0,), (0,)), ((), ())),
                           preferred_element_type=F32)


def _split3(x):
    hi = x.astype(BF16)
    r1 = x - hi.astype(F32)
    mid = r1.astype(BF16)
    lo = (r1 - mid.astype(F32)).astype(BF16)
    return hi, mid, lo


def _mm_exact_lhs(sel, x):
    s = sel.astype(BF16)
    hi, mid, lo = _split3(x)
    return (jnp.dot(s, hi, preferred_element_type=F32) + jnp.dot(s, mid, preferred_element_type=F32)
            + jnp.dot(s, lo, preferred_element_type=F32))


def _rms(x, gain):
    return x * lax.rsqrt(jnp.mean(x * x, axis=-1, keepdims=True) + EPS) * gain


def _sigmoid(x):
    return 1.0 / (1.0 + jnp.exp(-x))


def _iota2(shape, axis):
    return lax.broadcasted_iota(jnp.int32, shape, axis)


def _head_ones():
    return (_iota2((GW, GW), 0) // HD == _iota2((GW, GW), 1) // HD).astype(F32)


def _head_sum(x, ones_bd):
    s = ones_bd.astype(BF16)
    hi = x.astype(BF16)
    lo = (x - hi.astype(F32)).astype(BF16)
    return jnp.dot(hi, s, preferred_element_type=F32) + jnp.dot(lo, s, preferred_element_type=F32)


ROW_PARTS = 2


def _row_parts(n_rows, n_parts=ROW_PARTS):
    if n_rows % (n_parts * 2 * SUBLANES):
        return [slice(0, n_rows)]
    step = n_rows // n_parts
    return [slice(i * step, (i + 1) * step) for i in range(n_parts)]


def _norm_mm_kernel(x_ref, g_ref, w_ref, o_ref):
    parts = _row_parts(x_ref.shape[0])
    xn = [_rms(x_ref[r, :], g_ref[...]) for r in parts]
    for r, xp in zip(parts, xn):
        o_ref[r, :] = _mm(xp, w_ref[...])


def norm_matmul(x, gain, w, *, tm):
    t, d = x.shape
    n = w.shape[1]
    return pl.pallas_call(
        _norm_mm_kernel,
        out_shape=jax.ShapeDtypeStruct((t, n), F32),
        grid=(t // tm,),
        in_specs=[pl.BlockSpec((tm, d), lambda i: (i, 0)),
                  pl.BlockSpec((1, d), lambda i: (0, 0)),
                  pl.BlockSpec((d, n), lambda i: (0, 0))],
        out_specs=pl.BlockSpec((tm, n), lambda i: (i, 0)),
        compiler_params=_cparams(("parallel",)),
        name="norm_matmul",
    )(x, gain, w)


def _mm_kernel(a_ref, w_ref, o_ref):
    o_ref[...] = _mm(a_ref[...], w_ref[...])


def matmul_layers(a, w, *, tm):
    t, k = a.shape
    layers, _, n = w.shape
    return pl.pallas_call(
        _mm_kernel,
        out_shape=jax.ShapeDtypeStruct((layers, t, n), F32),
        grid=(layers, t // tm),
        in_specs=[pl.BlockSpec((tm, k), lambda l, i: (i, 0)),
                  pl.BlockSpec((None, k, n), lambda l, i: (l, 0, 0))],
        out_specs=pl.BlockSpec((None, tm, n), lambda l, i: (l, i, 0)),
        compiler_params=_cparams(("parallel", "parallel")),
        name="matmul_layers",
    )(a, w)


def _mm_res_kernel(a_ref, w_ref, r_ref, o_ref):
    o_ref[...] = r_ref[...] + _mm(a_ref[...], w_ref[...])


def matmul_residual(a, w, res, *, tm):
    t, k = a.shape
    n = w.shape[1]
    return pl.pallas_call(
        _mm_res_kernel,
        out_shape=jax.ShapeDtypeStruct((t, n), F32),
        grid=(t // tm,),
        in_specs=[pl.BlockSpec((tm, k), lambda i: (i, 0)),
                  pl.BlockSpec((k, n), lambda i: (0, 0)),
                  pl.BlockSpec((tm, n), lambda i: (i, 0))],
        out_specs=pl.BlockSpec((tm, n), lambda i: (i, 0)),
        compiler_params=_cparams(("parallel",)),
        name="matmul_residual",
    )(a, w, res)


def _mix_out_q_kernel(a0_ref, a1_ref, a2_ref, a3_ref, w_ref, r_ref, g_ref, wq_ref, x_ref, q_ref):
    parts = _row_parts(r_ref.shape[0])
    a = [jnp.concatenate([a0_ref[r, :], a1_ref[r, :], a2_ref[r, :], a3_ref[r, :]], axis=-1) for r in parts]
    x = [r_ref[r, :] + _mm(ap, w_ref[...]) for r, ap in zip(parts, a)]
    xn = [_rms(xp, g_ref[...]) for xp in x]
    for r, xp, xnp in zip(parts, x, xn):
        x_ref[r, :] = xp
        q_ref[r, :] = _mm(xnp, wq_ref[...]).astype(BF16)


def mix_out_q(parts, w, res, gain, w_q, *, tm):
    t, d = res.shape
    part = pl.BlockSpec((tm, GW), lambda i: (i, 0))
    mat = pl.BlockSpec((d, d), lambda i: (0, 0))
    tile = pl.BlockSpec((tm, d), lambda i: (i, 0))
    return pl.pallas_call(
        _mix_out_q_kernel,
        out_shape=(jax.ShapeDtypeStruct((t, d), F32), jax.ShapeDtypeStruct((t, d), BF16)),
        grid=(t // tm,),
        in_specs=[part, part, part, part, mat, tile, pl.BlockSpec((1, d), lambda i: (0, 0)), mat],
        out_specs=(tile, tile),
        compiler_params=_cparams(("parallel",)),
        name="mix_out_q",
    )(*parts, w, res, gain, w_q)


def _ffn_kernel(x_ref, g_ref, wu_ref, wd_ref, gf_ref, o_ref, *, final_norm):
    x = x_ref[...]
    h = jnp.dot(_rms(x, g_ref[...]).astype(BF16), wu_ref[...], preferred_element_type=F32)
    h = jnp.square(jnp.maximum(h, 0.0))
    y = x + jnp.dot(h.astype(BF16), wd_ref[...], preferred_element_type=F32)
    if final_norm:
        y = _rms(y, gf_ref[...])
    o_ref[...] = y


def ffn(x, gain, w_up, w_down, gain_final, *, tm, final_norm):
    t, d = x.shape
    ff = w_up.shape[1]
    resident = lambda shape: pl.BlockSpec(shape, lambda i: (0, 0), pipeline_mode=pl.Buffered(1))
    return pl.pallas_call(
        functools.partial(_ffn_kernel, final_norm=final_norm),
        out_shape=jax.ShapeDtypeStruct((t, d), F32),
        grid=(t // tm,),
        in_specs=[pl.BlockSpec((tm, d), lambda i: (i, 0)),
                  pl.BlockSpec((1, d), lambda i: (0, 0)),
                  resident((d, ff)), resident((ff, d)),
                  pl.BlockSpec((1, d), lambda i: (0, 0))],
        out_specs=pl.BlockSpec((tm, d), lambda i: (i, 0)),
        compiler_params=_cparams(("parallel",)),
        name="ffn",
    )(x, gain, w_up, w_down, gain_final)


def _attn_out_kernel(q_ref, k_ref, v_ref, wo_ref, r_ref, x_ref):
    sls = [slice(h * MEM_HD, (h + 1) * MEM_HD) for h in range(MEM_HEADS)]
    s = [_mm_nt(q_ref[:, sl], k_ref[:, sl]) * (MEM_HD ** -0.5) for sl in sls]
    p = [jnp.exp(sh - jnp.max(sh, axis=-1, keepdims=True)) for sh in s]
    p = [ph * (1.0 / jnp.sum(ph, axis=-1, keepdims=True)) for ph in p]
    heads = [_mm(ph, v_ref[:, sl]) for ph, sl in zip(p, sls)]
    x_ref[...] = r_ref[...] + _mm(jnp.concatenate(heads, axis=-1), wo_ref[...])


def cross_attention_out(q, mem_k, mem_v, layer, w_o, res, *, rows):
    t, d = res.shape
    bsz = mem_k.shape[1] // MEM_LEN
    lt = t // bsz // rows
    tile = pl.BlockSpec((rows, d), lambda b, l: (b * lt + l, 0))
    mem = pl.BlockSpec((None, MEM_LEN, d), lambda b, l: (layer, b, 0))
    return pl.pallas_call(
        _attn_out_kernel,
        out_shape=jax.ShapeDtypeStruct((t, d), F32),
        grid=(bsz, lt),
        in_specs=[tile, mem, mem, pl.BlockSpec((d, d), lambda b, l: (0, 0)), tile],
        out_specs=tile,
        compiler_params=_cparams(("parallel", "arbitrary")),
        name="cross_attention_out",
    )(q, mem_k, mem_v, w_o, res)


def _attn_cache_kernel(q_ref, k_ref, v_ref, o_ref, *, nb, rows):
    nr = MEM_HEADS * rows
    same_head = _iota2((nr, MEM_HEADS * MEM_LEN), 1) % MEM_HEADS == _iota2((nr, MEM_HEADS * MEM_LEN), 0) // rows
    q_all = q_ref[...].astype(F32)
    rng = range(nb)
    qs = [jnp.concatenate([q_all[b * rows:(b + 1) * rows, h * MEM_HD:(h + 1) * MEM_HD] for h in range(MEM_HEADS)],
                          axis=0) for b in rng]
    s = [_mm_nt(qs[b], k_ref[0, b].reshape(MEM_HEADS * MEM_LEN, MEM_HD)) * (MEM_HD ** -0.5) for b in rng]
    s = [jnp.where(same_head, sb, -1e30) for sb in s]
    p = [jnp.exp(sb - jnp.max(sb, axis=-1, keepdims=True)) for sb in s]
    p = [pb * (1.0 / jnp.sum(pb, axis=-1, keepdims=True)) for pb in p]
    o = [_mm(p[b], v_ref[0, b].reshape(MEM_HEADS * MEM_LEN, MEM_HD)) for b in rng]
    for b in rng:
        for h in range(MEM_HEADS):
            o_ref[b * rows:(b + 1) * rows, h * MEM_HD:(h + 1) * MEM_HD] = o[b][h * rows:(h + 1) * rows]


def cross_attention_cache(q, cache_k, cache_v, layer, *, nb, rows):
    t = q.shape[0]
    bsz = cache_k.shape[1]
    mem = pl.BlockSpec((1, nb, MEM_LEN, MEM_HEADS, MEM_HD), lambda b: (layer, b, 0, 0, 0))
    return pl.pallas_call(
        functools.partial(_attn_cache_kernel, nb=nb, rows=rows),
        out_shape=jax.ShapeDtypeStruct((t, D_MODEL), F32),
        grid=(bsz // nb,),
        in_specs=[pl.BlockSpec((nb * rows, D_MODEL), lambda b: (b, 0)), mem, mem],
        out_specs=pl.BlockSpec((nb * rows, D_MODEL), lambda b: (b, 0)),
        compiler_params=_cparams(("parallel",)),
        name="cross_attention_cache",
    )(q, cache_k, cache_v)


def _gelu_tanh(x):
    return 0.5 * x * (1.0 + jnp.tanh(math.sqrt(2.0 / math.pi) * (x + 0.044715 * (x * x * x))))


def _s5_kernel(u_ref, h0_ref, lam_ref, bblk_ref, cblk_ref, d_ref, wglu_ref, bglu_ref, gain_ref,
               y_ref, hfin_ref, scr_ref, tm_ref, *, steps, nb):
    @pl.when(pl.program_id(1) == 0)
    def _():
        scr_ref[0:nb, :] = h0_ref[...]

    def to_time_major(b, carry):
        for j in range(GW // LANES):
            tm_ref[j, pl.ds(b, steps, stride=nb), :] = u_ref[b, :, j * LANES:(j + 1) * LANES]
        return carry

    lax.fori_loop(0, nb, to_time_major, 0)
    u = jnp.concatenate([tm_ref[j] for j in range(GW // LANES)], axis=1)
    rows = _row_parts(steps * nb, 4)
    for r in rows:
        scr_ref[nb + r.start:nb + r.stop, :] = _mm(u[r], bblk_ref[...])
    lam_re = jnp.broadcast_to(lam_ref[0:1, :], (nb, S5_W))
    lam_im = jnp.broadcast_to(lam_ref[1:2, :], (nb, S5_W))

    h_re, h_im = scr_ref[0:nb, 0:S5_W], scr_ref[0:nb, S5_W:]
    for t in range(steps):
        c0 = (t + 1) * nb
        h_re, h_im = (scr_ref[c0:c0 + nb, 0:S5_W] + lam_re * h_re - lam_im * h_im,
                      scr_ref[c0:c0 + nb, S5_W:] + lam_re * h_im + lam_im * h_re)
        scr_ref[c0:c0 + nb, 0:S5_W] = h_re
        scr_ref[c0:c0 + nb, S5_W:] = h_im
    h_last = scr_ref[steps * nb:, :]
    hfin_ref[...] = h_last
    y = [_mm(scr_ref[nb + r.start:nb + r.stop, :], cblk_ref[...]) + d_ref[...] * u[r] for r in rows]
    y = [_gelu_tanh(yp) for yp in y]
    z = [_mm(yp, wglu_ref[...]) for yp in y]
    y = [_rms(yp * _sigmoid(zp + bglu_ref[...]), gain_ref[...]) for yp, zp in zip(y, z)]
    for r, yp in zip(rows, y):
        for j in range(GW // LANES):
            tm_ref[j, r, :] = yp[:, j * LANES:(j + 1) * LANES]

    def to_batch_major(b, carry):
        for j in range(GW // LANES):
            y_ref[b, :, j * LANES:(j + 1) * LANES] = tm_ref[j, pl.ds(b, steps, stride=nb), :]
        return carry

    lax.fori_loop(0, nb, to_batch_major, 0)
    scr_ref[0:nb, :] = h_last


def s5_mixer(proj, h0, lam, bblk, cblk, dvec, wglu, bglu, gain, *, steps, nb):
    bsz, seq, _ = proj.shape
    const = lambda shape: pl.BlockSpec(shape, lambda b, t: (0, 0))
    return pl.pallas_call(
        functools.partial(_s5_kernel, steps=steps, nb=nb),
        out_shape=(jax.ShapeDtypeStruct((bsz, seq, GW), F32),
                   jax.ShapeDtypeStruct((bsz, 2 * S5_W), F32)),
        grid=(bsz // nb, seq // steps),
        in_specs=[pl.BlockSpec((nb, steps, GW), lambda b, t: (b, t, 0)),
                  pl.BlockSpec((nb, 2 * S5_W), lambda b, t: (b, 0)),
                  const((2, S5_W)), const((GW, 2 * S5_W)), const((2 * S5_W, GW)), const((1, GW)),
                  const((GW, GW)), const((1, GW)), const((1, GW))],
        out_specs=(pl.BlockSpec((nb, steps, GW), lambda b, t: (b, t, 0)),
                   pl.BlockSpec((nb, 2 * S5_W), lambda b, t: (b, 0))),
        scratch_shapes=[pltpu.VMEM(((steps + 1) * nb, 2 * S5_W), F32),
                        pltpu.VMEM((GW // LANES, steps * nb, LANES), F32)],
        compiler_params=_cparams(("parallel", "arbitrary")),
        name="s5_mixer",
    )(proj, h0, lam, bblk, cblk, dvec, wglu, bglu, gain)


def s5_params(lam_re, lam_im, b_re, b_im, c_re, c_im, log_step):
    step = jnp.exp(log_step)[:, None]
    mag = jnp.exp(lam_re * step)
    lbar_re = mag * jnp.cos(lam_im * step)
    lbar_im = mag * jnp.sin(lam_im * step)
    den = lam_re * lam_re + lam_im * lam_im
    f_re = ((lbar_re - 1.0) * lam_re + lbar_im * lam_im) / den
    f_im = (lbar_im * lam_re - (lbar_re - 1.0) * lam_im) / den
    bbar_re = f_re[..., None] * b_re - f_im[..., None] * b_im
    bbar_im = f_re[..., None] * b_im + f_im[..., None] * b_re
    eye_g = jnp.eye(S5_NG, dtype=F32)

    def in_blk(m):
        return jnp.einsum('gph,gk->ghkp', m, eye_g).reshape(GW, S5_W)

    def out_blk(m):
        return jnp.einsum('ghp,gk->gpkh', m, eye_g).reshape(S5_W, GW)

    bblk = jnp.concatenate([in_blk(bbar_re), in_blk(bbar_im)], axis=1)
    cblk = jnp.concatenate([out_blk(c_re), -out_blk(c_im)], axis=0)
    lam2 = jnp.stack([lbar_re.reshape(S5_W), lbar_im.reshape(S5_W)])
    return lam2, bblk.astype(BF16), cblk.astype(BF16)


def s5_apply(proj, h0_re, h0_im, p, l, *, steps, nb):
    bsz = proj.shape[0]
    lam2, bblk, cblk = s5_params(p['s5_lam_re'][l], p['s5_lam_im'][l], p['s5_b_re'][l], p['s5_b_im'][l],
                                 p['s5_c_re'][l], p['s5_c_im'][l], p['s5_log_step'][l])
    h0 = jnp.concatenate([h0_re.reshape(bsz, S5_W), h0_im.reshape(bsz, S5_W)], axis=1)
    y, h = s5_mixer(proj, h0, lam2, bblk, cblk, p['s5_d'][l].reshape(1, GW),
                    p['s5_w_glu'][l].astype(BF16), p['s5_b_glu'][l].reshape(1, GW),
                    p['s5_norm'][l].reshape(1, GW), steps=steps, nb=nb)
    return y, h[:, :S5_W].reshape(bsz, S5_NG, S5_P), h[:, S5_W:].reshape(bsz, S5_NG, S5_P)


def _load_state(s0_ref, st_ref, nb, transpose):
    for b in range(nb):
        rows = []
        for h in range(NH):
            pieces = [s0_ref[b, h]]
            if h:
                pieces.insert(0, jnp.zeros((HD, h * HD), F32))
            if h < NH - 1:
                pieces.append(jnp.zeros((HD, (NH - 1 - h) * HD), F32))
            rows.append(jnp.concatenate(pieces, axis=1))
        st = jnp.concatenate(rows, axis=0)
        st_ref[b] = st.T if transpose else st


def _store_state(st_ref, sfin_ref, nb, transpose, own_layer):
    if own_layer is not None:
        for other in range(DEPTH):
            if other != own_layer:
                sfin_ref[other] = jnp.zeros(sfin_ref.shape[1:], F32)
        sfin_ref = sfin_ref.at[own_layer]
    for b in range(nb):
        st = st_ref[b].T if transpose else st_ref[b]
        for h in range(NH):
            sfin_ref[b, h] = st[h * HD:(h + 1) * HD, h * HD:(h + 1) * HD]


def _head_masks():
    lane_head = _iota2((1, GW), 1) // HD
    return [(lane_head == h).astype(F32) for h in range(NH)]


def _stack_heads(x, masks):
    return jnp.concatenate([x * m for m in masks], axis=0)


def _pad_rows(x, rows):
    if x.shape[0] == rows:
        return x
    return jnp.concatenate([x, jnp.zeros((rows - x.shape[0], x.shape[1]), x.dtype)], axis=0)


def _silu(x):
    return x * _sigmoid(x)


def _chunk_rows(ref, b, r0, rows, c):
    return _pad_rows(ref[b, pl.ds(r0, rows), :], c)


def _ret_kernel(q_ref, k_ref, v_ref, g_ref, cos_ref, sin_ref, s0_ref, _states_in, o_ref, sfin_ref, st_ref,
                *, nb, tb, c, c_real, own_layer):
    rows = min(tb, c)

    @pl.when(pl.program_id(1) == 0)
    def _():
        _load_state(s0_ref, st_ref, nb, transpose=False)

    masks = _head_masks()
    ones_bd = _head_ones()
    lane_head = _iota2((1, GW), 1) // HD
    log_gamma = jnp.zeros((1, GW), F32)
    for h in range(NH):
        log_gamma = jnp.where(lane_head == h, math.log(1.0 - 2.0 ** (-5.0 - h)), log_gamma)
    tt = _iota2((c, GW), 0).astype(F32)
    scale = HD ** -0.5
    g_q = jnp.exp(log_gamma * (tt + 1.0))
    g_k = jnp.exp(-log_gamma * (tt + 1.0)) * scale
    g_tail = jnp.exp(log_gamma * (c_real - 1.0 - tt)) * scale
    row_head = _iota2((GW, GW), 0) // HD
    g_chunk = jnp.zeros((GW, GW), F32)
    for h in range(NH):
        g_chunk = jnp.where(row_head == h, math.exp(math.log(1.0 - 2.0 ** (-5.0 - h)) * c_real), g_chunk)
    causal = _iota2((c, NH * c), 0) >= _iota2((c, NH * c), 1) % c
    first_half = _iota2((c, GW), 1) % HD < HD // 2

    def rope(x, cos, sin):
        swapped = jnp.where(first_half, pltpu.roll(x, GW - HD // 2, axis=1), pltpu.roll(x, HD // 2, axis=1))
        return x * cos + swapped * sin

    def chunk(ci, carry):
        r0 = pl.multiple_of(ci * rows, rows)
        cos = _pad_rows(cos_ref[pl.ds(r0, rows), :], c)
        sin = _pad_rows(sin_ref[pl.ds(r0, rows), :], c)
        rng = range(nb)
        qt = [rope(_chunk_rows(q_ref, b, r0, rows, c), cos, sin) * g_q for b in rng]
        k = [rope(_chunk_rows(k_ref, b, r0, rows, c), cos, sin) for b in rng]
        v = [_chunk_rows(v_ref, b, r0, rows, c) for b in rng]
        st = [st_ref[b] for b in rng]
        sc = [_mm_nt(qt[b], _stack_heads(k[b] * g_k, masks)) for b in rng]
        o_in = [_mm(jnp.where(causal, sc[b], 0.0), _stack_heads(v[b], masks)) for b in rng]
        o_st = [_mm(qt[b], st[b]) for b in rng]
        d_st = [_mm_tn(k[b] * g_tail, v[b]) for b in rng]
        o = [o_in[b] + o_st[b] for b in rng]
        ms = [_head_sum(o[b] * o[b], ones_bd) for b in rng]
        for b in rng:
            st_ref[b] = st[b] * g_chunk + ones_bd * d_st[b]
            ob = o[b] * lax.rsqrt(ms[b] * (1.0 / HD) + EPS) * _silu(_chunk_rows(g_ref, b, r0, rows, c))
            o_ref[b, pl.ds(r0, rows), :] = ob[:rows]
        return carry

    lax.fori_loop(0, tb // rows, chunk, 0)

    @pl.when(pl.program_id(1) == pl.num_programs(1) - 1)
    def _():
        _store_state(st_ref, sfin_ref, nb, transpose=False, own_layer=own_layer)


def _rope_tables(pos0, seq):
    half = HD // 2
    inv = ROPE_BASE ** (-jnp.arange(half, dtype=F32) / half)
    pos = pos0 + jnp.arange(seq, dtype=jnp.int32)
    ang = pos.astype(F32)[:, None] * inv[None, :]
    cos, sin = jnp.cos(ang), jnp.sin(ang)
    return (jnp.tile(jnp.concatenate([cos, cos], axis=-1), (1, NH)),
            jnp.tile(jnp.concatenate([-sin, sin], axis=-1), (1, NH)))


def _proj_spec(nb, tb, col):
    return pl.BlockSpec((nb, tb, GW), lambda b, t: (b, t, col))


def _state_spec(nb, layer):
    return pl.BlockSpec((None, nb, NH, HD, HD), lambda b, t: (layer, b, 0, 0, 0))


_ALIASED = pl.BlockSpec(memory_space=pl.ANY)


def _collector(states, bsz, nb, layer, operand_index):
    shape = jax.ShapeDtypeStruct((DEPTH, bsz, NH, HD, HD), F32)
    if states is None:
        spec = pl.BlockSpec((DEPTH, nb, NH, HD, HD), lambda b, t: (0, b, 0, 0, 0))
        return jnp.zeros((1, 1, NH, HD, HD), F32), spec, shape, {}, layer
    return states, _state_spec(nb, layer), shape, {operand_index: 1}, None


def retention_apply(proj, s0, s0_layer, states, layer, pos0, *, tb, nb):
    bsz, seq, _ = proj.shape
    c = RET_CHUNK if tb >= RET_CHUNK else CHUNK
    cos, sin = _rope_tables(pos0, seq)
    states, st_spec, st_shape, aliases, own_layer = _collector(states, bsz, nb, layer, 7)
    return pl.pallas_call(
        functools.partial(_ret_kernel, nb=nb, tb=tb, c=c, c_real=min(tb, c), own_layer=own_layer),
        out_shape=(jax.ShapeDtypeStruct((bsz, seq, GW), F32), st_shape),
        grid=(bsz // nb, seq // tb),
        in_specs=[_proj_spec(nb, tb, 5), _proj_spec(nb, tb, 6), _proj_spec(nb, tb, 7), _proj_spec(nb, tb, 8),
                  pl.BlockSpec((tb, GW), lambda b, t: (t, 0)), pl.BlockSpec((tb, GW), lambda b, t: (t, 0)),
                  _state_spec(nb, s0_layer), _ALIASED],
        out_specs=(pl.BlockSpec((nb, tb, GW), lambda b, t: (b, t, 0)), st_spec),
        scratch_shapes=[pltpu.VMEM((nb, GW, GW), F32)],
        input_output_aliases=aliases,
        compiler_params=_cparams(("parallel", "arbitrary")),
        name="retention",
    )(proj, proj, proj, proj, cos, sin, s0, states)


def _log_sigmoid(z):
    return jnp.minimum(z, 0.0) - jnp.log(1.0 + jnp.exp(-jnp.abs(z)))


def _hgrn_kernel(q_ref, f_ref, i_ref, g_ref, lb_ref, gain_ref, s0_ref, _states_in, o_ref, sfin_ref, st_ref,
                 *, nb, tb, c, own_layer):
    rows = min(tb, c)

    @pl.when(pl.program_id(1) == 0)
    def _():
        _load_state(s0_ref, st_ref, nb, transpose=True)

    t_hi = -(-rows // SUBLANES) * SUBLANES
    ones_bd = _head_ones()
    tri = (_iota2((c, c), 0) >= _iota2((c, c), 1)).astype(F32)
    t_idx = _iota2((c, GW), 0)
    lb = lb_ref[...]
    log_lb = jnp.log(lb)
    log_1m_lb = jnp.log(1.0 - lb)
    gain = gain_ref[...]

    def chunk(ci, carry):
        r0 = pl.multiple_of(ci * rows, rows)
        rng = range(nb)
        valid = t_idx < rows

        def gates(b):
            z = _chunk_rows(f_ref, b, r0, rows, c)
            ls_pos = _log_sigmoid(z)
            ls_neg = ls_pos - z
            b2 = log_lb + ls_neg
            log_f = jnp.maximum(ls_pos, b2) + jnp.log(1.0 + jnp.exp(-jnp.abs(ls_pos - b2)))
            return (jnp.where(valid, log_f, 0.0), jnp.where(valid, (1.0 - lb) * jnp.exp(ls_neg), 0.0),
                    ls_neg + log_1m_lb)

        log_f, key, log_key = zip(*[gates(b) for b in rng])
        q = [_silu(_chunk_rows(q_ref, b, r0, rows, c)) for b in rng]
        v = [_chunk_rows(i_ref, b, r0, rows, c) for b in rng]
        st = [st_ref[b] for b in rng]
        cum = [_mm_exact_lhs(tri, log_f[b]) for b in rng]
        last = [cum[b][c - 1:c, :] for b in rng]
        cum2 = [cum[b] * LOG2_E for b in rng]
        kd2 = [(log_key[b] - cum[b]) * LOG2_E for b in rng]

        def pair_rows(b, s):
            lo = SUBLANES * (s // SUBLANES)
            t_rows = _iota2((t_hi - lo, GW), 0) + lo
            key_decay = jnp.where(t_rows >= s, jnp.exp2(cum2[b][lo:t_hi] + kd2[b][s:s + 1, :]), 0.0)
            return key_decay * q[b][lo:t_hi]

        sc = [_mm(jnp.concatenate([pair_rows(b, s) for s in range(rows)], axis=0), ones_bd) for b in rng]
        o_st = [_mm_nt(q[b] * jnp.exp(cum[b]), st[b]) for b in rng]
        d_st = [_mm_tn(v[b], key[b] * jnp.exp(last[b] - cum[b])) for b in rng]

        def intra(b):
            tiles = [None] * (c // SUBLANES)
            off = 0
            for s in range(rows):
                for ti in range(s // SUBLANES, t_hi // SUBLANES):
                    term = sc[b][off:off + SUBLANES, :] * v[b][s:s + 1, :]
                    tiles[ti] = term if tiles[ti] is None else tiles[ti] + term
                    off += SUBLANES
            zero = jnp.zeros((SUBLANES, GW), F32)
            return jnp.concatenate([zero if t is None else t for t in tiles], axis=0)

        o = [o_st[b] + intra(b) for b in rng]
        ms = [_head_sum(o[b] * o[b], ones_bd) for b in rng]
        for b in rng:
            st_ref[b] = st[b] * jnp.exp(last[b]) + ones_bd * d_st[b]
            ob = o[b] * lax.rsqrt(ms[b] * (1.0 / HD) + EPS) * gain * _silu(_chunk_rows(g_ref, b, r0, rows, c))
            o_ref[b, pl.ds(r0, rows), :] = ob[:rows]
        return carry

    lax.fori_loop(0, tb // rows, chunk, 0)

    @pl.when(pl.program_id(1) == pl.num_programs(1) - 1)
    def _():
        _store_state(st_ref, sfin_ref, nb, transpose=True, own_layer=own_layer)


def hgrn_apply(proj, s0, s0_layer, states, layer, lb, gain, *, tb, nb):
    bsz, seq, _ = proj.shape
    row = pl.BlockSpec((1, GW), lambda b, t: (0, 0))
    states, st_spec, st_shape, aliases, own_layer = _collector(states, bsz, nb, layer, 7)
    return pl.pallas_call(
        functools.partial(_hgrn_kernel, nb=nb, tb=tb, c=CHUNK, own_layer=own_layer),
        out_shape=(jax.ShapeDtypeStruct((bsz, seq, GW), F32), st_shape),
        grid=(bsz // nb, seq // tb),
        in_specs=[_proj_spec(nb, tb, 1), _proj_spec(nb, tb, 2), _proj_spec(nb, tb, 3), _proj_spec(nb, tb, 4),
                  row, row, _state_spec(nb, s0_layer), _ALIASED],
        out_specs=(pl.BlockSpec((nb, tb, GW), lambda b, t: (b, t, 0)), st_spec),
        scratch_shapes=[pltpu.VMEM((nb, GW, GW), F32)],
        input_output_aliases=aliases,
        compiler_params=_cparams(("parallel", "arbitrary")),
        name="hgrn2",
    )(proj, proj, proj, proj, lb.reshape(1, GW), gain.reshape(1, GW), s0, states)


def _softplus(z):
    return jnp.maximum(z, 0.0) + jnp.log(1.0 + jnp.exp(-jnp.abs(z)))


def _rwkv_prepare(ins, consts):
    masks, ones_bd, bd_mask, eye_flat, tri, strict, incl = consts
    rng = range(len(ins))
    r, lw, k, v, kk, a = [[i[j] for i in ins] for j in range(6)]
    c = r[0].shape[0]
    n = NH * c

    def to_bd(flat):
        return jnp.concatenate([flat] * NH, axis=0) * bd_mask

    cum = [_mm_exact_lhs(tri, lw[i]) for i in rng]
    last = [cum[i][c - 1:c, :] for i in rng]
    p_inv = [jnp.exp(-cum[i]) for i in rng]
    p_tail = [jnp.exp(last[i] - cum[i]) for i in rng]
    ka = [kk[i] * a[i] for i in rng]
    x = [jnp.concatenate([kk[i] * jnp.exp(cum[i] - lw[i]), r[i] * jnp.exp(cum[i])], axis=0) for i in rng]
    g = [_mm_nt(x[i], jnp.concatenate([_stack_heads(ka[i] * p_inv[i], masks),
                                       _stack_heads(k[i] * p_inv[i], masks)], axis=0)) for i in rng]
    m_ak = [jnp.where(strict, g[i][:c], 0.0) for i in rng]
    n_ak = [jnp.where(incl, g[i][c:], 0.0) for i in rng]
    pw = [m_ak[i][:, :n] for i in rng]
    t_inv = [eye_flat - pw[i] for i in rng]
    pw_bd = [to_bd(pw[i]) for i in rng]
    for _ in range(int(math.log2(c)) - 1):
        pw = [_mm(pw[i], pw_bd[i]) for i in rng]
        pw_bd = [to_bd(pw[i]) for i in rng]
        t_inv = [t_inv[i] + _mm(t_inv[i], pw_bd[i]) for i in rng]
    v_stk = [_stack_heads(v[i], masks) for i in rng]
    zeros = jnp.zeros((n, GW), F32)
    mkv = [_mm(m_ak[i], jnp.concatenate([zeros, v_stk[i]], axis=0)) for i in rng]
    tmkv = [_mm(t_inv[i], _stack_heads(mkv[i], masks)) for i in rng]
    return [dict(x=x[i], t_inv=t_inv[i], tmkv=tmkv[i], n_ak=n_ak[i], v=v[i], v_stk=v_stk[i],
                 decay=jnp.exp(last[i]),
                 k_tail=jnp.concatenate([ka[i] * p_tail[i], k[i] * p_tail[i]], axis=0)) for i in rng]


def _rwkv_advance(prep, st, consts):
    masks, ones_bd = consts[0], consts[1]
    rng = range(len(prep))
    c = prep[0]['t_inv'].shape[0]
    xa = [_mm_nt(prep[i]['x'], st[i]) for i in rng]
    u = [-_mm(prep[i]['t_inv'], _stack_heads(xa[i][:c], masks)) - prep[i]['tmkv'] for i in rng]
    nuv = [_mm(prep[i]['n_ak'], jnp.concatenate([_stack_heads(u[i], masks), prep[i]['v_stk']], axis=0))
           for i in rng]
    d_uv = [_mm_tn(jnp.concatenate([u[i], prep[i]['v']], axis=0), prep[i]['k_tail']) for i in rng]
    return [(xa[i][c:] + nuv[i], st[i] * prep[i]['decay'] + ones_bd * d_uv[i]) for i in rng]


def _rwkv_kernel(x_r_ref, x_k_ref, x_v_ref, x_l_ref, sh0_ref, s0_ref, mu_ref, w0_ref, ww_ref, a0_ref,
                 wa_ref, wg_ref, kk_ref, ka_ref, rk_ref, lng_ref, lnb_ref, _states_in,
                 o_ref, sfin_ref, shfin_ref,
                 st_ref, sh_ref, r_s, lw_s, k_s, v_s, kkn_s, a_s, y_s, bonus_s, gate_s, *, nb, tb, c, own_layer):
    tbp = max(tb, c)

    @pl.when(pl.program_id(1) == 0)
    def _():
        _load_state(s0_ref, st_ref, nb, transpose=False)
        sh_ref[...] = sh0_ref[...]

    masks = _head_masks()
    ones_bd = _head_ones()
    n = NH * c
    row_t = _iota2((c, 2 * n), 0)
    col_t = _iota2((c, 2 * n), 1) % c
    bd_mask = (_iota2((n, n), 0) // c == _iota2((n, n), 1) // c).astype(F32)
    eye_flat = (_iota2((c, n), 1) % c == _iota2((c, n), 0)).astype(F32)
    consts = (masks, ones_bd, bd_mask, eye_flat, (_iota2((c, c), 0) >= _iota2((c, c), 1)).astype(F32),
              col_t < row_t, col_t <= row_t)
    first_row = _iota2((nb * tb, GW), 0) % tb == 0

    def mixed(x_ref, j):
        x = x_ref[...].reshape(nb * tb, GW)
        carried = jnp.broadcast_to(sh_ref[:, j:j + 1, :], (nb, tb, GW)).reshape(nb * tb, GW)
        prev = jnp.where(first_row, carried, pltpu.roll(x, 1, axis=0))
        sh_ref[:, j:j + 1, :] = x_ref[:, tb - 1:tb, :]
        return x + (prev - x) * mu_ref[j:j + 1, :]

    r = mixed(x_r_ref, 0)
    k = mixed(x_k_ref, 1)
    v = mixed(x_v_ref, 2)
    xl = mixed(x_l_ref, 3)
    log_w = -_softplus(-(w0_ref[...] + _mm(jnp.tanh(xl), ww_ref[...]))) - 0.5
    a = _sigmoid(a0_ref[...] + _mm(xl, wa_ref[...]))
    gate_s[...] = _mm(_sigmoid(xl), wg_ref[...]).reshape(nb, tb, GW)
    kk = k * kk_ref[...]
    kk = kk * lax.rsqrt(jnp.maximum(_head_sum(kk * kk, ones_bd), 1e-24))
    k = k * (1.0 + (a - 1.0) * ka_ref[...])
    bonus_s[...] = (_head_sum(r * k * rk_ref[...], ones_bd) * v).reshape(nb, tb, GW)
    for s, val in ((r_s, r), (lw_s, -jnp.exp(log_w)), (k_s, k), (v_s, v), (kkn_s, kk), (a_s, a)):
        s[:, 0:tb, :] = val.reshape(nb, tb, GW)
        if tbp > tb:
            s[:, tb:tbp, :] = jnp.zeros((nb, tbp - tb, GW), F32)

    n_chunks = tbp // c
    group = 4 if n_chunks % 4 == 0 else 1

    def chunks(gi, carry):
        r0 = [pl.multiple_of((gi * group + j) * c, c) for j in range(group)]
        prep = _rwkv_prepare([tuple(s[b, pl.ds(r0[j], c), :] for s in (r_s, lw_s, k_s, v_s, kkn_s, a_s))
                              for j in range(group) for b in range(nb)], consts)
        st = [st_ref[b] for b in range(nb)]
        for j in range(group):
            outs = _rwkv_advance(prep[j * nb:(j + 1) * nb], st, consts)
            st = [o[1] for o in outs]
            for b in range(nb):
                y_s[b, pl.ds(r0[j], c), :] = outs[b][0]
        for b in range(nb):
            st_ref[b] = st[b]
        return carry

    lax.fori_loop(0, n_chunks // group, chunks, 0)

    y = y_s[:, 0:tb, :].reshape(nb * tb, GW)
    mean = _head_sum(y, ones_bd) * (1.0 / HD)
    d = y - mean
    var = _head_sum(d * d, ones_bd) * (1.0 / HD)
    y = d * lax.rsqrt(var + RW_LN_EPS) * lng_ref[...] + lnb_ref[...]
    o_ref[...] = ((y.reshape(nb, tb, GW) + bonus_s[...]) * gate_s[...])
    shfin_ref[...] = sh_ref[...]

    @pl.when(pl.program_id(1) == pl.num_programs(1) - 1)
    def _():
        _store_state(st_ref, sfin_ref, nb, transpose=False, own_layer=own_layer)


def rwkv_apply(proj, s0, s0_layer, states, shift0, p, l, *, tb, nb):
    bsz, seq, _ = proj.shape
    states, st_spec, st_shape, aliases, own_layer = _collector(states, bsz, nb, l, 17)
    c = CHUNK
    tbp = max(tb, c)
    row = pl.BlockSpec((1, GW), lambda b, t: (0, 0))
    mat = pl.BlockSpec((GW, GW), lambda b, t: (0, 0))
    sh_spec = pl.BlockSpec((nb, 4, GW), lambda b, t: (b, 0, 0))
    zeros = lambda r: jnp.zeros((r, GW), F32)
    ww = jnp.concatenate([p['rw_w_w2'][l], zeros(192)], axis=0).astype(BF16)
    wa = jnp.concatenate([zeros(64), p['rw_w_a2'][l], zeros(128)], axis=0).astype(BF16)
    wg = jnp.concatenate([zeros(128), p['rw_w_g2'][l]], axis=0).astype(BF16)
    r1 = lambda name: p[name][l].reshape(1, GW)
    seq_buf = pltpu.VMEM((nb, tbp, GW), F32)
    blk_buf = pltpu.VMEM((nb, tb, GW), F32)
    y, st, sh = pl.pallas_call(
        functools.partial(_rwkv_kernel, nb=nb, tb=tb, c=c, own_layer=own_layer),
        out_shape=(jax.ShapeDtypeStruct((bsz, seq, GW), F32), st_shape,
                   jax.ShapeDtypeStruct((bsz, 4, GW), F32)),
        grid=(bsz // nb, seq // tb),
        in_specs=[_proj_spec(nb, tb, 9), _proj_spec(nb, tb, 10), _proj_spec(nb, tb, 11), _proj_spec(nb, tb, 12),
                  sh_spec, _state_spec(nb, s0_layer), pl.BlockSpec((4, GW), lambda b, t: (0, 0)),
                  row, mat, row, mat, mat, row, row, row, row, row, _ALIASED],
        out_specs=(pl.BlockSpec((nb, tb, GW), lambda b, t: (b, t, 0)), st_spec, sh_spec),
        scratch_shapes=[pltpu.VMEM((nb, GW, GW), F32), pltpu.VMEM((nb, 4, GW), F32),
                        seq_buf, seq_buf, seq_buf, seq_buf, seq_buf, seq_buf, seq_buf, blk_buf, blk_buf],
        input_output_aliases=aliases,
        compiler_params=_cparams(("parallel", "arbitrary")),
        name="rwkv7",
    )(proj, proj, proj, proj, shift0.reshape(bsz, 4, GW), s0, p['rw_mu'][l].reshape(4, GW),
      r1('rw_w0'), ww, r1('rw_a0'), wa, wg, r1('rw_k_k'), r1('rw_k_a'), r1('rw_r_k'), r1('rw_ln_g'), r1('rw_ln_b'),
      states)
    return y, st, sh.reshape(bsz, RW_PROJ)


TOKEN_TILE = 512
TIME_BLOCK = 128
CACHE_ATTN_SEQS = 4


def _tiles(bsz, seq):
    if seq >= TIME_BLOCK:
        return dict(tb=TIME_BLOCK, s5_nb=bsz, mix_nb=bsz, rw_nb=bsz)
    return dict(tb=seq, s5_nb=bsz, mix_nb=8, rw_nb=16)


def _trunk_layer(x, bsz, seq, pos0, attend, st, mats, p, wb, l, lb, final_norm):
    cfg = _tiles(bsz, seq)
    tm = TOKEN_TILE
    row = lambda name: p[name][l].reshape(1, -1)
    proj = norm_matmul(x, row('norm_mix'), wb['w_in'][l], tm=tm)
    proj = proj.reshape(bsz, seq, IN_WIDTH)
    y_s5, s5_re, s5_im = s5_apply(proj, st['s5_re'], st['s5_im'], p, l, steps=cfg['tb'], nb=cfg['s5_nb'])
    y_hg, hg_s = hgrn_apply(proj, st['hgrn'], st['layer'], mats['hgrn'], l, lb, p['hg_norm'][l],
                            tb=cfg['tb'], nb=cfg['mix_nb'])
    y_rt, rt_s = retention_apply(proj, st['ret'], st['layer'], mats['ret'], l, pos0,
                                 tb=cfg['tb'], nb=cfg['mix_nb'])
    y_rw, rw_s, shift = rwkv_apply(proj, st['rwkv'], st['layer'], mats['rwkv'], st['shift'], p, l,
                                   tb=cfg['tb'], nb=cfg['rw_nb'])
    parts = [y.reshape(bsz * seq, GW) for y in (y_s5, y_hg, y_rt, y_rw)]
    x, q = mix_out_q(parts, wb['w_out'][l], x, row('norm_mem'), wb['mem_w_q'][l], tm=tm)
    x = attend(q, wb['mem_w_o'][l], x)
    x = ffn(x, row('norm_ffn'), wb['ffn_w_up'][l], wb['ffn_w_down'][l], p['norm_final'].reshape(1, -1),
            tm=tm, final_norm=final_norm)
    return x, (s5_re, s5_im, shift), dict(hgrn=hg_s, ret=rt_s, rwkv=rw_s)


def kernel(x_prompt, x_sample, mem_prompt, state_s5_re, state_s5_im, state_hgrn, state_ret, state_rwkv,
           state_rwkv_shift, cache_mem_k, cache_mem_v, norm_mix, w_in, w_out, s5_lam_re, s5_lam_im,
           s5_b_re, s5_b_im, s5_c_re, s5_c_im, s5_d, s5_log_step, s5_w_glu, s5_b_glu, s5_norm,
           hg_lb_logits, hg_norm, rw_mu, rw_w0, rw_w_w2, rw_a0, rw_w_a2, rw_w_g2, rw_k_k, rw_k_a, rw_r_k,
           rw_ln_g, rw_ln_b, norm_mem, mem_w_q, mem_w_k, mem_w_v, mem_w_o, norm_ffn, ffn_w_up, ffn_w_down,
           norm_final):
    p = dict(norm_mix=norm_mix, s5_lam_re=s5_lam_re, s5_lam_im=s5_lam_im, s5_b_re=s5_b_re, s5_b_im=s5_b_im,
             s5_c_re=s5_c_re, s5_c_im=s5_c_im, s5_d=s5_d, s5_log_step=s5_log_step, s5_w_glu=s5_w_glu,
             s5_b_glu=s5_b_glu, s5_norm=s5_norm, hg_norm=hg_norm, rw_mu=rw_mu, rw_w0=rw_w0, rw_w_w2=rw_w_w2,
             rw_a0=rw_a0, rw_w_a2=rw_w_a2, rw_w_g2=rw_w_g2, rw_k_k=rw_k_k, rw_k_a=rw_k_a, rw_r_k=rw_r_k,
             rw_ln_g=rw_ln_g, rw_ln_b=rw_ln_b, norm_mem=norm_mem, norm_ffn=norm_ffn, norm_final=norm_final)
    wb = {name: w.astype(BF16) for name, w in dict(
        w_in=w_in, w_out=w_out, mem_w_q=mem_w_q, mem_w_k=mem_w_k, mem_w_v=mem_w_v, mem_w_o=mem_w_o,
        ffn_w_up=ffn_w_up, ffn_w_down=ffn_w_down).items()}
    lb_all = jnp.cumsum(jax.nn.softmax(hg_lb_logits.astype(F32), axis=0), axis=0)
    lb_all = lb_all - lb_all[0:1]

    bp, lp, _ = x_prompt.shape
    bs, ls, _ = x_sample.shape
    yp = x_prompt.reshape(bp * lp, D_MODEL)
    ys = x_sample.reshape(bs * ls, D_MODEL)
    mem2d = mem_prompt.reshape(bp * MEM_LEN, D_MODEL)
    mat_zero = jnp.zeros((1, bp, NH, HD, HD), F32)
    p_small, s_small = [], []
    p_mats = dict(hgrn=None, ret=None, rwkv=None)
    s_mats = dict(hgrn=None, ret=None, rwkv=None)
    mk = matmul_layers(mem2d, wb['mem_w_k'], tm=TOKEN_TILE)
    mv = matmul_layers(mem2d, wb['mem_w_v'], tm=TOKEN_TILE)
    for l in range(DEPTH):
        final = l == DEPTH - 1
        def attend_p(q, w_o, x, l=l):
            return cross_attention_out(q, mk, mv, l, w_o, x, rows=TOKEN_TILE)

        zero_state = dict(s5_re=jnp.zeros((bp, S5_NG, S5_P), F32), s5_im=jnp.zeros((bp, S5_NG, S5_P), F32),
                          shift=jnp.zeros((bp, RW_PROJ), F32), hgrn=mat_zero, ret=mat_zero, rwkv=mat_zero, layer=0)
        yp, small, p_mats = _trunk_layer(yp, bp, lp, 0, attend_p, zero_state, p_mats, p, wb, l, lb_all[l], final)
        p_small.append(small)
        sst = dict(s5_re=state_s5_re[l], s5_im=state_s5_im[l], shift=state_rwkv_shift[l],
                   hgrn=state_hgrn, ret=state_ret, rwkv=state_rwkv, layer=l)
        def attend_s(q, w_o, x, l=l):
            o = cross_attention_cache(q, cache_mem_k, cache_mem_v, l, nb=CACHE_ATTN_SEQS, rows=ls)
            return matmul_residual(o, w_o, x, tm=TOKEN_TILE)

        ys, small, s_mats = _trunk_layer(ys, bs, ls, PAST_LEN, attend_s, sst, s_mats, p, wb, l, lb_all[l], final)
        s_small.append(small)
    stack = lambda states, i: jnp.stack([s[i] for s in states])
    return (yp.reshape(bp, lp, D_MODEL), ys.reshape(bs, ls, D_MODEL),
            stack(p_small, 0), stack(p_small, 1), p_mats['hgrn'], p_mats['ret'], p_mats['rwkv'], stack(p_small, 2),
            mk.reshape(DEPTH, bp, MEM_LEN, MEM_HEADS, MEM_HD), mv.reshape(DEPTH, bp, MEM_LEN, MEM_HEADS, MEM_HD),
            stack(s_small, 0), stack(s_small, 1), s_mats['hgrn'], s_mats['ret'], s_mats['rwkv'], stack(s_small, 2))
```

```python
import functools
import math

import jax
import jax.numpy as jnp
from jax import lax
from jax.experimental import pallas as pl
from jax.experimental.pallas import tpu as pltpu

F32 = jnp.float32
BF16 = jnp.bfloat16

D_MODEL = 1024
DEPTH = 2
PAST_LEN = 16384
GW = 256
HD = 64
NH = GW // HD
S5_GROUP = 16
S5_NG = GW // S5_GROUP
S5_P = 64
S5_W = S5_NG * S5_P
RW_PROJ = 4 * GW
IN_WIDTH = 13 * GW
MEM_LEN = 256
MEM_HEADS = 4
MEM_HD = D_MODEL // MEM_HEADS
D_FF = 4 * D_MODEL
EPS = 1e-6
RW_LN_EPS = 64e-5
ROPE_BASE = 10000.0
CHUNK = 16
RET_CHUNK = 64

VMEM_LIMIT = 56 * 1024 * 1024
SUBLANES = 8
LANES = 128
LOG2_E = 1.4426950408889634


def _cparams(sem):
    return pltpu.CompilerParams(dimension_semantics=sem, vmem_limit_bytes=VMEM_LIMIT)


def _mm(a, b):
    return jnp.dot(a.astype(BF16), b.astype(BF16), preferred_element_type=F32)


def _mm_nt(a, b):
    return lax.dot_general(a.astype(BF16), b.astype(BF16), (((1,), (1,)), ((), ())),
                           preferred_element_type=F32)


def _mm_tn(a, b):
    return lax.dot_general(a.astype(BF16), b.astype(BF16), (((0,), (0,)), ((), ())),
                           preferred_element_type=F32)


def _split3(x):
    hi = x.astype(BF16)
    r1 = x - hi.astype(F32)
    mid = r1.astype(BF16)
    lo = (r1 - mid.astype(F32)).astype(BF16)
    return hi, mid, lo


def _mm_exact_lhs(sel, x):
    s = sel.astype(BF16)
    hi, mid, lo = _split3(x)
    return (jnp.dot(s, hi, preferred_element_type=F32) + jnp.dot(s, mid, preferred_element_type=F32)
            + jnp.dot(s, lo, preferred_element_type=F32))


def _rms(x, gain):
    return x * lax.rsqrt(jnp.mean(x * x, axis=-1, keepdims=True) + EPS) * gain


def _sigmoid(x):
    return 1.0 / (1.0 + jnp.exp(-x))


def _iota2(shape, axis):
    return lax.broadcasted_iota(jnp.int32, shape, axis)


def _head_ones():
    return (_iota2((GW, GW), 0) // HD == _iota2((GW, GW), 1) // HD).astype(F32)


def _head_sum(x, ones_bd):
    s = ones_bd.astype(BF16)
    hi = x.astype(BF16)
    lo = (x - hi.astype(F32)).astype(BF16)
    return jnp.dot(hi, s, preferred_element_type=F32) + jnp.dot(lo, s, preferred_element_type=F32)


ROW_PARTS = 2


def _row_parts(n_rows, n_parts=ROW_PARTS):
    if n_rows % (n_parts * 2 * SUBLANES):
        return [slice(0, n_rows)]
    step = n_rows // n_parts
    return [slice(i * step, (i + 1) * step) for i in range(n_parts)]


def _norm_mm_kernel(x_ref, g_ref, w_ref, o_ref):
    parts = _row_parts(x_ref.shape[0])
    xn = [_rms(x_ref[r, :], g_ref[...]) for r in parts]
    for r, xp in zip(parts, xn):
        o_ref[r, :] = _mm(xp, w_ref[...])


def norm_matmul(x, gain, w, *, tm):
    t, d = x.shape
    n = w.shape[1]
    return pl.pallas_call(
        _norm_mm_kernel,
        out_shape=jax.ShapeDtypeStruct((t, n), F32),
        grid=(t // tm,),
        in_specs=[pl.BlockSpec((tm, d), lambda i: (i, 0)),
                  pl.BlockSpec((1, d), lambda i: (0, 0)),
                  pl.BlockSpec((d, n), lambda i: (0, 0))],
        out_specs=pl.BlockSpec((tm, n), lambda i: (i, 0)),
        compiler_params=_cparams(("parallel",)),
        name="norm_matmul",
    )(x, gain, w)


def _mm_kernel(a_ref, w_ref, o_ref):
    o_ref[...] = _mm(a_ref[...], w_ref[...])


def matmul_layers(a, w, *, tm):
    t, k = a.shape
    layers, _, n = w.shape
    return pl.pallas_call(
        _mm_kernel,
        out_shape=jax.ShapeDtypeStruct((layers, t, n), F32),
        grid=(layers, t // tm),
        in_specs=[pl.BlockSpec((tm, k), lambda l, i: (i, 0)),
                  pl.BlockSpec((None, k, n), lambda l, i: (l, 0, 0))],
        out_specs=pl.BlockSpec((None, tm, n), lambda l, i: (l, i, 0)),
        compiler_params=_cparams(("parallel", "parallel")),
        name="matmul_layers",
    )(a, w)


def _mm_res_kernel(a_ref, w_ref, r_ref, o_ref):
    o_ref[...] = r_ref[...] + _mm(a_ref[...], w_ref[...])


def matmul_residual(a, w, res, *, tm):
    t, k = a.shape
    n = w.shape[1]
    return pl.pallas_call(
        _mm_res_kernel,
        out_shape=jax.ShapeDtypeStruct((t, n), F32),
        grid=(t // tm,),
        in_specs=[pl.BlockSpec((tm, k), lambda i: (i, 0)),
                  pl.BlockSpec((k, n), lambda i: (0, 0)),
                  pl.BlockSpec((tm, n), lambda i: (i, 0))],
        out_specs=pl.BlockSpec((tm, n), lambda i: (i, 0)),
        compiler_params=_cparams(("parallel",)),
        name="matmul_residual",
    )(a, w, res)


def _mix_out_q_kernel(a0_ref, a1_ref, a2_ref, a3_ref, w_ref, r_ref, g_ref, wq_ref, x_ref, q_ref):
    parts = _row_parts(r_ref.shape[0])
    a = [jnp.concatenate([a0_ref[r, :], a1_ref[r, :], a2_ref[r, :], a3_ref[r, :]], axis=-1) for r in parts]
    x = [r_ref[r, :] + _mm(ap, w_ref[...]) for r, ap in zip(parts, a)]
    xn = [_rms(xp, g_ref[...]) for xp in x]
    for r, xp, xnp in zip(parts, x, xn):
        x_ref[r, :] = xp
        q_ref[r, :] = _mm(xnp, wq_ref[...]).astype(BF16)


def mix_out_q(parts, w, res, gain, w_q, *, tm):
    t, d = res.shape
    part = pl.BlockSpec((tm, GW), lambda i: (i, 0))
    mat = pl.BlockSpec((d, d), lambda i: (0, 0))
    tile = pl.BlockSpec((tm, d), lambda i: (i, 0))
    return pl.pallas_call(
        _mix_out_q_kernel,
        out_shape=(jax.ShapeDtypeStruct((t, d), F32), jax.ShapeDtypeStruct((t, d), BF16)),
        grid=(t // tm,),
        in_specs=[part, part, part, part, mat, tile, pl.BlockSpec((1, d), lambda i: (0, 0)), mat],
        out_specs=(tile, tile),
        compiler_params=_cparams(("parallel",)),
        name="mix_out_q",
    )(*parts, w, res, gain, w_q)


def _ffn_kernel(x_ref, g_ref, wu_ref, wd_ref, gf_ref, o_ref, *, final_norm):
    x = x_ref[...]
    h = jnp.dot(_rms(x, g_ref[...]).astype(BF16), wu_ref[...], preferred_element_type=F32)
    h = jnp.square(jnp.maximum(h, 0.0))
    y = x + jnp.dot(h.astype(BF16), wd_ref[...], preferred_element_type=F32)
    if final_norm:
        y = _rms(y, gf_ref[...])
    o_ref[...] = y


def ffn(x, gain, w_up, w_down, gain_final, *, tm, final_norm):
    t, d = x.shape
    ff = w_up.shape[1]
    resident = lambda shape: pl.BlockSpec(shape, lambda i: (0, 0), pipeline_mode=pl.Buffered(1))
    return pl.pallas_call(
        functools.partial(_ffn_kernel, final_norm=final_norm),
        out_shape=jax.ShapeDtypeStruct((t, d), F32),
        grid=(t // tm,),
        in_specs=[pl.BlockSpec((tm, d), lambda i: (i, 0)),
                  pl.BlockSpec((1, d), lambda i: (0, 0)),
                  resident((d, ff)), resident((ff, d)),
                  pl.BlockSpec((1, d), lambda i: (0, 0))],
        out_specs=pl.BlockSpec((tm, d), lambda i: (i, 0)),
        compiler_params=_cparams(("parallel",)),
        name="ffn",
    )(x, gain, w_up, w_down, gain_final)


def _attn_out_kernel(q_ref, k_ref, v_ref, wo_ref, r_ref, x_ref):
    sls = [slice(h * MEM_HD, (h + 1) * MEM_HD) for h in range(MEM_HEADS)]
    s = [_mm_nt(q_ref[:, sl], k_ref[:, sl]) * (MEM_HD ** -0.5) for sl in sls]
    p = [jnp.exp(sh - jnp.max(sh, axis=-1, keepdims=True)) for sh in s]
    p = [ph * (1.0 / jnp.sum(ph, axis=-1, keepdims=True)) for ph in p]
    heads = [_mm(ph, v_ref[:, sl]) for ph, sl in zip(p, sls)]
    x_ref[...] = r_ref[...] + _mm(jnp.concatenate(heads, axis=-1), wo_ref[...])


def cross_attention_out(q, mem_k, mem_v, layer, w_o, res, *, rows):
    t, d = res.shape
    bsz = mem_k.shape[1] // MEM_LEN
    lt = t // bsz // rows
    tile = pl.BlockSpec((rows, d), lambda b, l: (b * lt + l, 0))
    mem = pl.BlockSpec((None, MEM_LEN, d), lambda b, l: (layer, b, 0))
    return pl.pallas_call(
        _attn_out_kernel,
        out_shape=jax.ShapeDtypeStruct((t, d), F32),
        grid=(bsz, lt),
        in_specs=[tile, mem, mem, pl.BlockSpec((d, d), lambda b, l: (0, 0)), tile],
        out_specs=tile,
        compiler_params=_cparams(("parallel", "arbitrary")),
        name="cross_attention_out",
    )(q, mem_k, mem_v, w_o, res)


def _attn_cache_kernel(q_ref, k_ref, v_ref, o_ref, *, nb, rows):
    nr = MEM_HEADS * rows
    same_head = _iota2((nr, MEM_HEADS * MEM_LEN), 1) % MEM_HEADS == _iota2((nr, MEM_HEADS * MEM_LEN), 0) // rows
    q_all = q_ref[...].astype(F32)
    rng = range(nb)
    qs = [jnp.concatenate([q_all[b * rows:(b + 1) * rows, h * MEM_HD:(h + 1) * MEM_HD] for h in range(MEM_HEADS)],
                          axis=0) for b in rng]
    s = [_mm_nt(qs[b], k_ref[0, b].reshape(MEM_HEADS * MEM_LEN, MEM_HD)) * (MEM_HD ** -0.5) for b in rng]
    s = [jnp.where(same_head, sb, -1e30) for sb in s]
    p = [jnp.exp(sb - jnp.max(sb, axis=-1, keepdims=True)) for sb in s]
    p = [pb * (1.0 / jnp.sum(pb, axis=-1, keepdims=True)) for pb in p]
    o = [_mm(p[b], v_ref[0, b].reshape(MEM_HEADS * MEM_LEN, MEM_HD)) for b in rng]
    for b in rng:
        for h in range(MEM_HEADS):
            o_ref[b * rows:(b + 1) * rows, h * MEM_HD:(h + 1) * MEM_HD] = o[b][h * rows:(h + 1) * rows]


def cross_attention_cache(q, cache_k, cache_v, layer, *, nb, rows):
    t = q.shape[0]
    bsz = cache_k.shape[1]
    mem = pl.BlockSpec((1, nb, MEM_LEN, MEM_HEADS, MEM_HD), lambda b: (layer, b, 0, 0, 0))
    return pl.pallas_call(
        functools.partial(_attn_cache_kernel, nb=nb, rows=rows),
        out_shape=jax.ShapeDtypeStruct((t, D_MODEL), F32),
        grid=(bsz // nb,),
        in_specs=[pl.BlockSpec((nb * rows, D_MODEL), lambda b: (b, 0)), mem, mem],
        out_specs=pl.BlockSpec((nb * rows, D_MODEL), lambda b: (b, 0)),
        compiler_params=_cparams(("parallel",)),
        name="cross_attention_cache",
    )(q, cache_k, cache_v)


def _gelu_tanh(x):
    return 0.5 * x * (1.0 + jnp.tanh(math.sqrt(2.0 / math.pi) * (x + 0.044715 * (x * x * x))))


def _s5_kernel(u_ref, h0_ref, lam_ref, bblk_ref, cblk_ref, d_ref, wglu_ref, bglu_ref, gain_ref,
               y_ref, hfin_ref, scr_ref, tm_ref, *, steps, nb):
    @pl.when(pl.program_id(1) == 0)
    def _():
        scr_ref[0:nb, :] = h0_ref[...]

    def to_time_major(b, carry):
        for j in range(GW // LANES):
            tm_ref[j, pl.ds(b, steps, stride=nb), :] = u_ref[b, :, j * LANES:(j + 1) * LANES]
        return carry

    lax.fori_loop(0, nb, to_time_major, 0)
    u = jnp.concatenate([tm_ref[j] for j in range(GW // LANES)], axis=1)
    rows = _row_parts(steps * nb, 4)
    for r in rows:
        scr_ref[nb + r.start:nb + r.stop, :] = _mm(u[r], bblk_ref[...])
    lam_re = jnp.broadcast_to(lam_ref[0:1, :], (nb, S5_W))
    lam_im = jnp.broadcast_to(lam_ref[1:2, :], (nb, S5_W))

    h_re, h_im = scr_ref[0:nb, 0:S5_W], scr_ref[0:nb, S5_W:]
    for t in range(steps):
        c0 = (t + 1) * nb
        h_re, h_im = (scr_ref[c0:c0 + nb, 0:S5_W] + lam_re * h_re - lam_im * h_im,
                      scr_ref[c0:c0 + nb, S5_W:] + lam_re * h_im + lam_im * h_re)
        scr_ref[c0:c0 + nb, 0:S5_W] = h_re
        scr_ref[c0:c0 + nb, S5_W:] = h_im
    h_last = scr_ref[steps * nb:, :]
    hfin_ref[...] = h_last
    y = [_mm(scr_ref[nb + r.start:nb + r.stop, :], cblk_ref[...]) + d_ref[...] * u[r] for r in rows]
    y = [_gelu_tanh(yp) for yp in y]
    z = [_mm(yp, wglu_ref[...]) for yp in y]
    y = [_rms(yp * _sigmoid(zp + bglu_ref[...]), gain_ref[...]) for yp, zp in zip(y, z)]
    for r, yp in zip(rows, y):
        for j in range(GW // LANES):
            tm_ref[j, r, :] = yp[:, j * LANES:(j + 1) * LANES]

    def to_batch_major(b, carry):
        for j in range(GW // LANES):
            y_ref[b, :, j * LANES:(j + 1) * LANES] = tm_ref[j, pl.ds(b, steps, stride=nb), :]
        return carry

    lax.fori_loop(0, nb, to_batch_major, 0)
    scr_ref[0:nb, :] = h_last


def s5_mixer(proj, h0, lam, bblk, cblk, dvec, wglu, bglu, gain, *, steps, nb):
    bsz, seq, _ = proj.shape
    const = lambda shape: pl.BlockSpec(shape, lambda b, t: (0, 0))
    return pl.pallas_call(
        functools.partial(_s5_kernel, steps=steps, nb=nb),
        out_shape=(jax.ShapeDtypeStruct((bsz, seq, GW), F32),
                   jax.ShapeDtypeStruct((bsz, 2 * S5_W), F32)),
        grid=(bsz // nb, seq // steps),
        in_specs=[pl.BlockSpec((nb, steps, GW), lambda b, t: (b, t, 0)),
                  pl.BlockSpec((nb, 2 * S5_W), lambda b, t: (b, 0)),
                  const((2, S5_W)), const((GW, 2 * S5_W)), const((2 * S5_W, GW)), const((1, GW)),
                  const((GW, GW)), const((1, GW)), const((1, GW))],
        out_specs=(pl.BlockSpec((nb, steps, GW), lambda b, t: (b, t, 0)),
                   pl.BlockSpec((nb, 2 * S5_W), lambda b, t: (b, 0))),
        scratch_shapes=[pltpu.VMEM(((steps + 1) * nb, 2 * S5_W), F32),
                        pltpu.VMEM((GW // LANES, steps * nb, LANES), F32)],
        compiler_params=_cparams(("parallel", "arbitrary")),
        name="s5_mixer",
    )(proj, h0, lam, bblk, cblk, dvec, wglu, bglu, gain)


def s5_params(lam_re, lam_im, b_re, b_im, c_re, c_im, log_step):
    step = jnp.exp(log_step)[:, None]
    mag = jnp.exp(lam_re * step)
    lbar_re = mag * jnp.cos(lam_im * step)
    lbar_im = mag * jnp.sin(lam_im * step)
    den = lam_re * lam_re + lam_im * lam_im
    f_re = ((lbar_re - 1.0) * lam_re + lbar_im * lam_im) / den
    f_im = (lbar_im * lam_re - (lbar_re - 1.0) * lam_im) / den
    bbar_re = f_re[..., None] * b_re - f_im[..., None] * b_im
    bbar_im = f_re[..., None] * b_im + f_im[..., None] * b_re
    eye_g = jnp.eye(S5_NG, dtype=F32)

    def in_blk(m):
        return jnp.einsum('gph,gk->ghkp', m, eye_g).reshape(GW, S5_W)

    def out_blk(m):
        return jnp.einsum('ghp,gk->gpkh', m, eye_g).reshape(S5_W, GW)

    bblk = jnp.concatenate([in_blk(bbar_re), in_blk(bbar_im)], axis=1)
    cblk = jnp.concatenate([out_blk(c_re), -out_blk(c_im)], axis=0)
    lam2 = jnp.stack([lbar_re.reshape(S5_W), lbar_im.reshape(S5_W)])
    return lam2, bblk.astype(BF16), cblk.astype(BF16)


def s5_apply(proj, h0_re, h0_im, p, l, *, steps, nb):
    bsz = proj.shape[0]
    lam2, bblk, cblk = s5_params(p['s5_lam_re'][l], p['s5_lam_im'][l], p['s5_b_re'][l], p['s5_b_im'][l],
                                 p['s5_c_re'][l], p['s5_c_im'][l], p['s5_log_step'][l])
    h0 = jnp.concatenate([h0_re.reshape(bsz, S5_W), h0_im.reshape(bsz, S5_W)], axis=1)
    y, h = s5_mixer(proj, h0, lam2, bblk, cblk, p['s5_d'][l].reshape(1, GW),
                    p['s5_w_glu'][l].astype(BF16), p['s5_b_glu'][l].reshape(1, GW),
                    p['s5_norm'][l].reshape(1, GW), steps=steps, nb=nb)
    return y, h[:, :S5_W].reshape(bsz, S5_NG, S5_P), h[:, S5_W:].reshape(bsz, S5_NG, S5_P)


def _load_state(s0_ref, st_ref, nb, transpose):
    for b in range(nb):
        rows = []
        for h in range(NH):
            pieces = [s0_ref[b, h]]
            if h:
                pieces.insert(0, jnp.zeros((HD, h * HD), F32))
            if h < NH - 1:
                pieces.append(jnp.zeros((HD, (NH - 1 - h) * HD), F32))
            rows.append(jnp.concatenate(pieces, axis=1))
        st = jnp.concatenate(rows, axis=0)
        st_ref[b] = st.T if transpose else st


def _store_state(st_ref, sfin_ref, nb, transpose, own_layer):
    if own_layer is not None:
        for other in range(DEPTH):
            if other != own_layer:
                sfin_ref[other] = jnp.zeros(sfin_ref.shape[1:], F32)
        sfin_ref = sfin_ref.at[own_layer]
    for b in range(nb):
        st = st_ref[b].T if transpose else st_ref[b]
        for h in range(NH):
            sfin_ref[b, h] = st[h * HD:(h + 1) * HD, h * HD:(h + 1) * HD]


def _head_masks():
    lane_head = _iota2((1, GW), 1) // HD
    return [(lane_head == h).astype(F32) for h in range(NH)]


def _stack_heads(x, masks):
    return jnp.concatenate([x * m for m in masks], axis=0)


def _pad_rows(x, rows):
    if x.shape[0] == rows:
        return x
    return jnp.concatenate([x, jnp.zeros((rows - x.shape[0], x.shape[1]), x.dtype)], axis=0)


def _silu(x):
    return x * _sigmoid(x)


def _chunk_rows(ref, b, r0, rows, c):
    return _pad_rows(ref[b, pl.ds(r0, rows), :], c)


CHUNK_UNROLL = 4


def _chunk_loop(n_chunks, rows, body):
    unroll = math.gcd(n_chunks, CHUNK_UNROLL)

    def step(i, carry):
        for j in range(unroll):
            body(pl.multiple_of((i * unroll + j) * rows, rows))
        return carry

    lax.fori_loop(0, n_chunks // unroll, step, 0)


def _ret_kernel(q_ref, k_ref, v_ref, g_ref, cos_ref, sin_ref, s0_ref, _states_in, o_ref, sfin_ref, st_ref,
                *, nb, tb, c, c_real, own_layer):
    rows = min(tb, c)

    @pl.when(pl.program_id(1) == 0)
    def _():
        _load_state(s0_ref, st_ref, nb, transpose=False)

    masks = _head_masks()
    ones_bd = _head_ones()
    lane_head = _iota2((1, GW), 1) // HD
    log_gamma = jnp.zeros((1, GW), F32)
    for h in range(NH):
        log_gamma = jnp.where(lane_head == h, math.log(1.0 - 2.0 ** (-5.0 - h)), log_gamma)
    tt = _iota2((c, GW), 0).astype(F32)
    scale = HD ** -0.5
    g_q = jnp.exp(log_gamma * (tt + 1.0))
    g_k = jnp.exp(-log_gamma * (tt + 1.0)) * scale
    g_tail = jnp.exp(log_gamma * (c_real - 1.0 - tt)) * scale
    row_head = _iota2((GW, GW), 0) // HD
    g_chunk = jnp.zeros((GW, GW), F32)
    for h in range(NH):
        g_chunk = jnp.where(row_head == h, math.exp(math.log(1.0 - 2.0 ** (-5.0 - h)) * c_real), g_chunk)
    causal = _iota2((c, NH * c), 0) >= _iota2((c, NH * c), 1) % c
    first_half = _iota2((c, GW), 1) % HD < HD // 2

    def rope(x, cos, sin):
        swapped = jnp.where(first_half, pltpu.roll(x, GW - HD // 2, axis=1), pltpu.roll(x, HD // 2, axis=1))
        return x * cos + swapped * sin

    def chunk(r0):
        cos = _pad_rows(cos_ref[pl.ds(r0, rows), :], c)
        sin = _pad_rows(sin_ref[pl.ds(r0, rows), :], c)
        rng = range(nb)
        qt = [rope(_chunk_rows(q_ref, b, r0, rows, c), cos, sin) * g_q for b in rng]
        k = [rope(_chunk_rows(k_ref, b, r0, rows, c), cos, sin) for b in rng]
        v = [_chunk_rows(v_ref, b, r0, rows, c) for b in rng]
        st = [st_ref[b] for b in rng]
        sc = [_mm_nt(qt[b], _stack_heads(k[b] * g_k, masks)) for b in rng]
        o_in = [_mm(jnp.where(causal, sc[b], 0.0), _stack_heads(v[b], masks)) for b in rng]
        o_st = [_mm(qt[b], st[b]) for b in rng]
        d_st = [_mm_tn(k[b] * g_tail, v[b]) for b in rng]
        o = [o_in[b] + o_st[b] for b in rng]
        ms = [_head_sum(o[b] * o[b], ones_bd) for b in rng]
        for b in rng:
            st_ref[b] = st[b] * g_chunk + ones_bd * d_st[b]
            ob = o[b] * lax.rsqrt(ms[b] * (1.0 / HD) + EPS) * _silu(_chunk_rows(g_ref, b, r0, rows, c))
            o_ref[b, pl.ds(r0, rows), :] = ob[:rows]

    _chunk_loop(tb // rows, rows, chunk)

    @pl.when(pl.program_id(1) == pl.num_programs(1) - 1)
    def _():
        _store_state(st_ref, sfin_ref, nb, transpose=False, own_layer=own_layer)


def _rope_tables(pos0, seq):
    half = HD // 2
    inv = ROPE_BASE ** (-jnp.arange(half, dtype=F32) / half)
    pos = pos0 + jnp.arange(seq, dtype=jnp.int32)
    ang = pos.astype(F32)[:, None] * inv[None, :]
    cos, sin = jnp.cos(ang), jnp.sin(ang)
    return (jnp.tile(jnp.concatenate([cos, cos], axis=-1), (1, NH)),
            jnp.tile(jnp.concatenate([-sin, sin], axis=-1), (1, NH)))


def _proj_spec(nb, tb, col):
    return pl.BlockSpec((nb, tb, GW), lambda b, t: (b, t, col))


def _state_spec(nb, layer):
    return pl.BlockSpec((None, nb, NH, HD, HD), lambda b, t: (layer, b, 0, 0, 0))


_ALIASED = pl.BlockSpec(memory_space=pl.ANY)


def _collector(states, bsz, nb, layer, operand_index):
    shape = jax.ShapeDtypeStruct((DEPTH, bsz, NH, HD, HD), F32)
    if states is None:
        spec = pl.BlockSpec((DEPTH, nb, NH, HD, HD), lambda b, t: (0, b, 0, 0, 0))
        return jnp.zeros((1, 1, NH, HD, HD), F32), spec, shape, {}, layer
    return states, _state_spec(nb, layer), shape, {operand_index: 1}, None


def retention_apply(proj, s0, s0_layer, states, layer, pos0, *, tb, nb):
    bsz, seq, _ = proj.shape
    c = RET_CHUNK if tb >= RET_CHUNK else CHUNK
    cos, sin = _rope_tables(pos0, seq)
    states, st_spec, st_shape, aliases, own_layer = _collector(states, bsz, nb, layer, 7)
    return pl.pallas_call(
        functools.partial(_ret_kernel, nb=nb, tb=tb, c=c, c_real=min(tb, c), own_layer=own_layer),
        out_shape=(jax.ShapeDtypeStruct((bsz, seq, GW), F32), st_shape),
        grid=(bsz // nb, seq // tb),
        in_specs=[_proj_spec(nb, tb, 5), _proj_spec(nb, tb, 6), _proj_spec(nb, tb, 7), _proj_spec(nb, tb, 8),
                  pl.BlockSpec((tb, GW), lambda b, t: (t, 0)), pl.BlockSpec((tb, GW), lambda b, t: (t, 0)),
                  _state_spec(nb, s0_layer), _ALIASED],
        out_specs=(pl.BlockSpec((nb, tb, GW), lambda b, t: (b, t, 0)), st_spec),
        scratch_shapes=[pltpu.VMEM((nb, GW, GW), F32)],
        input_output_aliases=aliases,
        compiler_params=_cparams(("parallel", "arbitrary")),
        name="retention",
    )(proj, proj, proj, proj, cos, sin, s0, states)


def _log_sigmoid(z):
    return jnp.minimum(z, 0.0) - jnp.log(1.0 + jnp.exp(-jnp.abs(z)))


def _hgrn_kernel(q_ref, f_ref, i_ref, g_ref, lb_ref, gain_ref, s0_ref, _states_in, o_ref, sfin_ref, st_ref,
                 *, nb, tb, c, own_layer):
    rows = min(tb, c)

    @pl.when(pl.program_id(1) == 0)
    def _():
        _load_state(s0_ref, st_ref, nb, transpose=True)

    t_hi = -(-rows // SUBLANES) * SUBLANES
    ones_bd = _head_ones()
    tri = (_iota2((c, c), 0) >= _iota2((c, c), 1)).astype(F32)
    t_idx = _iota2((c, GW), 0)
    lb = lb_ref[...]
    log_lb = jnp.log(lb)
    log_1m_lb = jnp.log(1.0 - lb)
    gain = gain_ref[...]

    def chunk(r0):
        rng = range(nb)
        valid = t_idx < rows

        def gates(b):
            z = _chunk_rows(f_ref, b, r0, rows, c)
            ls_pos = _log_sigmoid(z)
            ls_neg = ls_pos - z
            b2 = log_lb + ls_neg
            log_f = jnp.maximum(ls_pos, b2) + jnp.log(1.0 + jnp.exp(-jnp.abs(ls_pos - b2)))
            return (jnp.where(valid, log_f, 0.0), jnp.where(valid, (1.0 - lb) * jnp.exp(ls_neg), 0.0),
                    ls_neg + log_1m_lb)

        log_f, key, log_key = zip(*[gates(b) for b in rng])
        q = [_silu(_chunk_rows(q_ref, b, r0, rows, c)) for b in rng]
        v = [_chunk_rows(i_ref, b, r0, rows, c) for b in rng]
        st = [st_ref[b] for b in rng]
        cum = [_mm_exact_lhs(tri, log_f[b]) for b in rng]
        last = [cum[b][c - 1:c, :] for b in rng]
        cum2 = [cum[b] * LOG2_E for b in rng]
        kd2 = [(log_key[b] - cum[b]) * LOG2_E for b in rng]

        def pair_rows(b, s):
            lo = SUBLANES * (s // SUBLANES)
            t_rows = _iota2((t_hi - lo, GW), 0) + lo
            key_decay = jnp.where(t_rows >= s, jnp.exp2(cum2[b][lo:t_hi] + kd2[b][s:s + 1, :]), 0.0)
            return key_decay * q[b][lo:t_hi]

        sc = [_mm(jnp.concatenate([pair_rows(b, s) for s in range(rows)], axis=0), ones_bd) for b in rng]
        o_st = [_mm_nt(q[b] * jnp.exp(cum[b]), st[b]) for b in rng]
        d_st = [_mm_tn(v[b], key[b] * jnp.exp(last[b] - cum[b])) for b in rng]

        def intra(b):
            tiles = [None] * (c // SUBLANES)
            off = 0
            for s in range(rows):
                for ti in range(s // SUBLANES, t_hi // SUBLANES):
                    term = sc[b][off:off + SUBLANES, :] * v[b][s:s + 1, :]
                    tiles[ti] = term if tiles[ti] is None else tiles[ti] + term
                    off += SUBLANES
            zero = jnp.zeros((SUBLANES, GW), F32)
            return jnp.concatenate([zero if t is None else t for t in tiles], axis=0)

        o = [o_st[b] + intra(b) for b in rng]
        ms = [_head_sum(o[b] * o[b], ones_bd) for b in rng]
        for b in rng:
            st_ref[b] = st[b] * jnp.exp(last[b]) + ones_bd * d_st[b]
            ob = o[b] * lax.rsqrt(ms[b] * (1.0 / HD) + EPS) * gain * _silu(_chunk_rows(g_ref, b, r0, rows, c))
            o_ref[b, pl.ds(r0, rows), :] = ob[:rows]

    _chunk_loop(tb // rows, rows, chunk)

    @pl.when(pl.program_id(1) == pl.num_programs(1) - 1)
    def _():
        _store_state(st_ref, sfin_ref, nb, transpose=True, own_layer=own_layer)


def hgrn_apply(proj, s0, s0_layer, states, layer, lb, gain, *, tb, nb):
    bsz, seq, _ = proj.shape
    row = pl.BlockSpec((1, GW), lambda b, t: (0, 0))
    states, st_spec, st_shape, aliases, own_layer = _collector(states, bsz, nb, layer, 7)
    return pl.pallas_call(
        functools.partial(_hgrn_kernel, nb=nb, tb=tb, c=CHUNK, own_layer=own_layer),
        out_shape=(jax.ShapeDtypeStruct((bsz, seq, GW), F32), st_shape),
        grid=(bsz // nb, seq // tb),
        in_specs=[_proj_spec(nb, tb, 1), _proj_spec(nb, tb, 2), _proj_spec(nb, tb, 3), _proj_spec(nb, tb, 4),
                  row, row, _state_spec(nb, s0_layer), _ALIASED],
        out_specs=(pl.BlockSpec((nb, tb, GW), lambda b, t: (b, t, 0)), st_spec),
        scratch_shapes=[pltpu.VMEM((nb, GW, GW), F32)],
        input_output_aliases=aliases,
        compiler_params=_cparams(("parallel", "arbitrary")),
        name="hgrn2",
    )(proj, proj, proj, proj, lb.reshape(1, GW), gain.reshape(1, GW), s0, states)


def _softplus(z):
    return jnp.maximum(z, 0.0) + jnp.log(1.0 + jnp.exp(-jnp.abs(z)))


def _rwkv_prepare(ins, consts):
    masks, ones_bd, bd_mask, eye_flat, tri, strict, incl = consts
    rng = range(len(ins))
    r, lw, k, v, kk, a = [[i[j] for i in ins] for j in range(6)]
    c = r[0].shape[0]
    n = NH * c

    def to_bd(flat):
        return jnp.concatenate([flat] * NH, axis=0) * bd_mask

    cum = [_mm_exact_lhs(tri, lw[i]) for i in rng]
    last = [cum[i][c - 1:c, :] for i in rng]
    p_inv = [jnp.exp(-cum[i]) for i in rng]
    p_tail = [jnp.exp(last[i] - cum[i]) for i in rng]
    ka = [kk[i] * a[i] for i in rng]
    x = [jnp.concatenate([kk[i] * jnp.exp(cum[i] - lw[i]), r[i] * jnp.exp(cum[i])], axis=0) for i in rng]
    g = [_mm_nt(x[i], jnp.concatenate([_stack_heads(ka[i] * p_inv[i], masks),
                                       _stack_heads(k[i] * p_inv[i], masks)], axis=0)) for i in rng]
    m_ak = [jnp.where(strict, g[i][:c], 0.0) for i in rng]
    n_ak = [jnp.where(incl, g[i][c:], 0.0) for i in rng]
    pw = [m_ak[i][:, :n] for i in rng]
    t_inv = [eye_flat - pw[i] for i in rng]
    pw_bd = [to_bd(pw[i]) for i in rng]
    for _ in range(int(math.log2(c)) - 1):
        pw = [_mm(pw[i], pw_bd[i]) for i in rng]
        pw_bd = [to_bd(pw[i]) for i in rng]
        t_inv = [t_inv[i] + _mm(t_inv[i], pw_bd[i]) for i in rng]
    v_stk = [_stack_heads(v[i], masks) for i in rng]
    zeros = jnp.zeros((n, GW), F32)
    mkv = [_mm(m_ak[i], jnp.concatenate([zeros, v_stk[i]], axis=0)) for i in rng]
    tmkv = [_mm(t_inv[i], _stack_heads(mkv[i], masks)) for i in rng]
    return [dict(x=x[i], t_inv=t_inv[i], tmkv=tmkv[i], n_ak=n_ak[i], v=v[i], v_stk=v_stk[i],
                 decay=jnp.exp(last[i]),
                 k_tail=jnp.concatenate([ka[i] * p_tail[i], k[i] * p_tail[i]], axis=0)) for i in rng]


def _rwkv_advance(prep, st, consts):
    masks, ones_bd = consts[0], consts[1]
    rng = range(len(prep))
    c = prep[0]['t_inv'].shape[0]
    xa = [_mm_nt(prep[i]['x'], st[i]) for i in rng]
    u = [-_mm(prep[i]['t_inv'], _stack_heads(xa[i][:c], masks)) - prep[i]['tmkv'] for i in rng]
    nuv = [_mm(prep[i]['n_ak'], jnp.concatenate([_stack_heads(u[i], masks), prep[i]['v_stk']], axis=0))
           for i in rng]
    d_uv = [_mm_tn(jnp.concatenate([u[i], prep[i]['v']], axis=0), prep[i]['k_tail']) for i in rng]
    return [(xa[i][c:] + nuv[i], st[i] * prep[i]['decay'] + ones_bd * d_uv[i]) for i in rng]


def _rwkv_kernel(x_r_ref, x_k_ref, x_v_ref, x_l_ref, sh0_ref, s0_ref, mu_ref, w0_ref, ww_ref, a0_ref,
                 wa_ref, wg_ref, kk_ref, ka_ref, rk_ref, lng_ref, lnb_ref, _states_in,
                 o_ref, sfin_ref, shfin_ref,
                 st_ref, sh_ref, r_s, lw_s, k_s, v_s, kkn_s, a_s, y_s, bonus_s, gate_s, *, nb, tb, c, own_layer):
    tbp = max(tb, c)

    @pl.when(pl.program_id(1) == 0)
    def _():
        _load_state(s0_ref, st_ref, nb, transpose=False)
        sh_ref[...] = sh0_ref[...]

    masks = _head_masks()
    ones_bd = _head_ones()
    n = NH * c
    row_t = _iota2((c, 2 * n), 0)
    col_t = _iota2((c, 2 * n), 1) % c
    bd_mask = (_iota2((n, n), 0) // c == _iota2((n, n), 1) // c).astype(F32)
    eye_flat = (_iota2((c, n), 1) % c == _iota2((c, n), 0)).astype(F32)
    consts = (masks, ones_bd, bd_mask, eye_flat, (_iota2((c, c), 0) >= _iota2((c, c), 1)).astype(F32),
              col_t < row_t, col_t <= row_t)
    first_row = _iota2((nb * tb, GW), 0) % tb == 0

    def mixed(x_ref, j):
        x = x_ref[...].reshape(nb * tb, GW)
        carried = jnp.broadcast_to(sh_ref[:, j:j + 1, :], (nb, tb, GW)).reshape(nb * tb, GW)
        prev = jnp.where(first_row, carried, pltpu.roll(x, 1, axis=0))
        sh_ref[:, j:j + 1, :] = x_ref[:, tb - 1:tb, :]
        return x + (prev - x) * mu_ref[j:j + 1, :]

    r = mixed(x_r_ref, 0)
    k = mixed(x_k_ref, 1)
    v = mixed(x_v_ref, 2)
    xl = mixed(x_l_ref, 3)
    log_w = -_softplus(-(w0_ref[...] + _mm(jnp.tanh(xl), ww_ref[...]))) - 0.5
    a = _sigmoid(a0_ref[...] + _mm(xl, wa_ref[...]))
    gate_s[...] = _mm(_sigmoid(xl), wg_ref[...]).reshape(nb, tb, GW)
    kk = k * kk_ref[...]
    kk = kk * lax.rsqrt(jnp.maximum(_head_sum(kk * kk, ones_bd), 1e-24))
    k = k * (1.0 + (a - 1.0) * ka_ref[...])
    bonus_s[...] = (_head_sum(r * k * rk_ref[...], ones_bd) * v).reshape(nb, tb, GW)
    for s, val in ((r_s, r), (lw_s, -jnp.exp(log_w)), (k_s, k), (v_s, v), (kkn_s, kk), (a_s, a)):
        s[:, 0:tb, :] = val.reshape(nb, tb, GW)
        if tbp > tb:
            s[:, tb:tbp, :] = jnp.zeros((nb, tbp - tb, GW), F32)

    n_chunks = tbp // c
    group = 8 if n_chunks % 8 == 0 else 1

    def chunks(gi, carry):
        r0 = [pl.multiple_of((gi * group + j) * c, c) for j in range(group)]
        prep = _rwkv_prepare([tuple(s[b, pl.ds(r0[j], c), :] for s in (r_s, lw_s, k_s, v_s, kkn_s, a_s))
                              for j in range(group) for b in range(nb)], consts)
        st = [st_ref[b] for b in range(nb)]
        for j in range(group):
            outs = _rwkv_advance(prep[j * nb:(j + 1) * nb], st, consts)
            st = [o[1] for o in outs]
            for b in range(nb):
                y_s[b, pl.ds(r0[j], c), :] = outs[b][0]
        for b in range(nb):
            st_ref[b] = st[b]
        return carry

    lax.fori_loop(0, n_chunks // group, chunks, 0)

    y = y_s[:, 0:tb, :].reshape(nb * tb, GW)
    mean = _head_sum(y, ones_bd) * (1.0 / HD)
    d = y - mean
    var = _head_sum(d * d, ones_bd) * (1.0 / HD)
    y = d * lax.rsqrt(var + RW_LN_EPS) * lng_ref[...] + lnb_ref[...]
    o_ref[...] = ((y.reshape(nb, tb, GW) + bonus_s[...]) * gate_s[...])
    shfin_ref[...] = sh_ref[...]

    @pl.when(pl.program_id(1) == pl.num_programs(1) - 1)
    def _():
        _store_state(st_ref, sfin_ref, nb, transpose=False, own_layer=own_layer)


def rwkv_apply(proj, s0, s0_layer, states, shift0, p, l, *, tb, nb):
    bsz, seq, _ = proj.shape
    states, st_spec, st_shape, aliases, own_layer = _collector(states, bsz, nb, l, 17)
    c = CHUNK
    tbp = max(tb, c)
    row = pl.BlockSpec((1, GW), lambda b, t: (0, 0))
    mat = pl.BlockSpec((GW, GW), lambda b, t: (0, 0))
    sh_spec = pl.BlockSpec((nb, 4, GW), lambda b, t: (b, 0, 0))
    zeros = lambda r: jnp.zeros((r, GW), F32)
    ww = jnp.concatenate([p['rw_w_w2'][l], zeros(192)], axis=0).astype(BF16)
    wa = jnp.concatenate([zeros(64), p['rw_w_a2'][l], zeros(128)], axis=0).astype(BF16)
    wg = jnp.concatenate([zeros(128), p['rw_w_g2'][l]], axis=0).astype(BF16)
    r1 = lambda name: p[name][l].reshape(1, GW)
    seq_buf = pltpu.VMEM((nb, tbp, GW), F32)
    blk_buf = pltpu.VMEM((nb, tb, GW), F32)
    y, st, sh = pl.pallas_call(
        functools.partial(_rwkv_kernel, nb=nb, tb=tb, c=c, own_layer=own_layer),
        out_shape=(jax.ShapeDtypeStruct((bsz, seq, GW), F32), st_shape,
                   jax.ShapeDtypeStruct((bsz, 4, GW), F32)),
        grid=(bsz // nb, seq // tb),
        in_specs=[_proj_spec(nb, tb, 9), _proj_spec(nb, tb, 10), _proj_spec(nb, tb, 11), _proj_spec(nb, tb, 12),
                  sh_spec, _state_spec(nb, s0_layer), pl.BlockSpec((4, GW), lambda b, t: (0, 0)),
                  row, mat, row, mat, mat, row, row, row, row, row, _ALIASED],
        out_specs=(pl.BlockSpec((nb, tb, GW), lambda b, t: (b, t, 0)), st_spec, sh_spec),
        scratch_shapes=[pltpu.VMEM((nb, GW, GW), F32), pltpu.VMEM((nb, 4, GW), F32),
                        seq_buf, seq_buf, seq_buf, seq_buf, seq_buf, seq_buf, seq_buf, blk_buf, blk_buf],
        input_output_aliases=aliases,
        compiler_params=_cparams(("parallel", "arbitrary")),
        name="rwkv7",
    )(proj, proj, proj, proj, shift0.reshape(bsz, 4, GW), s0, p['rw_mu'][l].reshape(4, GW),
      r1('rw_w0'), ww, r1('rw_a0'), wa, wg, r1('rw_k_k'), r1('rw_k_a'), r1('rw_r_k'), r1('rw_ln_g'), r1('rw_ln_b'),
      states)
    return y, st, sh.reshape(bsz, RW_PROJ)


TOKEN_TILE = 512
TIME_BLOCK = 128
CACHE_ATTN_SEQS = 4


def _tiles(bsz, seq):
    if seq >= TIME_BLOCK:
        return dict(tb=TIME_BLOCK, s5_nb=bsz, mix_nb=bsz, rw_nb=bsz)
    return dict(tb=seq, s5_nb=bsz, mix_nb=8, rw_nb=16)


def _trunk_layer(x, bsz, seq, pos0, attend, st, mats, p, wb, l, lb, final_norm):
    cfg = _tiles(bsz, seq)
    tm = TOKEN_TILE
    row = lambda name: p[name][l].reshape(1, -1)
    proj = norm_matmul(x, row('norm_mix'), wb['w_in'][l], tm=tm)
    proj = proj.reshape(bsz, seq, IN_WIDTH)
    y_s5, s5_re, s5_im = s5_apply(proj, st['s5_re'], st['s5_im'], p, l, steps=cfg['tb'], nb=cfg['s5_nb'])
    y_hg, hg_s = hgrn_apply(proj, st['hgrn'], st['layer'], mats['hgrn'], l, lb, p['hg_norm'][l],
                            tb=cfg['tb'], nb=cfg['mix_nb'])
    y_rt, rt_s = retention_apply(proj, st['ret'], st['layer'], mats['ret'], l, pos0,
                                 tb=cfg['tb'], nb=cfg['mix_nb'])
    y_rw, rw_s, shift = rwkv_apply(proj, st['rwkv'], st['layer'], mats['rwkv'], st['shift'], p, l,
                                   tb=cfg['tb'], nb=cfg['rw_nb'])
    parts = [y.reshape(bsz * seq, GW) for y in (y_s5, y_hg, y_rt, y_rw)]
    x, q = mix_out_q(parts, wb['w_out'][l], x, row('norm_mem'), wb['mem_w_q'][l], tm=tm)
    x = attend(q, wb['mem_w_o'][l], x)
    x = ffn(x, row('norm_ffn'), wb['ffn_w_up'][l], wb['ffn_w_down'][l], p['norm_final'].reshape(1, -1),
            tm=tm, final_norm=final_norm)
    return x, (s5_re, s5_im, shift), dict(hgrn=hg_s, ret=rt_s, rwkv=rw_s)


def kernel(x_prompt, x_sample, mem_prompt, state_s5_re, state_s5_im, state_hgrn, state_ret, state_rwkv,
           state_rwkv_shift, cache_mem_k, cache_mem_v, norm_mix, w_in, w_out, s5_lam_re, s5_lam_im,
           s5_b_re, s5_b_im, s5_c_re, s5_c_im, s5_d, s5_log_step, s5_w_glu, s5_b_glu, s5_norm,
           hg_lb_logits, hg_norm, rw_mu, rw_w0, rw_w_w2, rw_a0, rw_w_a2, rw_w_g2, rw_k_k, rw_k_a, rw_r_k,
           rw_ln_g, rw_ln_b, norm_mem, mem_w_q, mem_w_k, mem_w_v, mem_w_o, norm_ffn, ffn_w_up, ffn_w_down,
           norm_final):
    p = dict(norm_mix=norm_mix, s5_lam_re=s5_lam_re, s5_lam_im=s5_lam_im, s5_b_re=s5_b_re, s5_b_im=s5_b_im,
             s5_c_re=s5_c_re, s5_c_im=s5_c_im, s5_d=s5_d, s5_log_step=s5_log_step, s5_w_glu=s5_w_glu,
             s5_b_glu=s5_b_glu, s5_norm=s5_norm, hg_norm=hg_norm, rw_mu=rw_mu, rw_w0=rw_w0, rw_w_w2=rw_w_w2,
             rw_a0=rw_a0, rw_w_a2=rw_w_a2, rw_w_g2=rw_w_g2, rw_k_k=rw_k_k, rw_k_a=rw_k_a, rw_r_k=rw_r_k,
             rw_ln_g=rw_ln_g, rw_ln_b=rw_ln_b, norm_mem=norm_mem, norm_ffn=norm_ffn, norm_final=norm_final)
    wb = {name: w.astype(BF16) for name, w in dict(
        w_in=w_in, w_out=w_out, mem_w_q=mem_w_q, mem_w_k=mem_w_k, mem_w_v=mem_w_v, mem_w_o=mem_w_o,
        ffn_w_up=ffn_w_up, ffn_w_down=ffn_w_down).items()}
    lb_all = jnp.cumsum(jax.nn.softmax(hg_lb_logits.astype(F32), axis=0), axis=0)
    lb_all = lb_all - lb_all[0:1]

    bp, lp, _ = x_prompt.shape
    bs, ls, _ = x_sample.shape
    yp = x_prompt.reshape(bp * lp, D_MODEL)
    ys = x_sample.reshape(bs * ls, D_MODEL)
    mem2d = mem_prompt.reshape(bp * MEM_LEN, D_MODEL)
    mat_zero = jnp.zeros((1, bp, NH, HD, HD), F32)
    p_small, s_small = [], []
    p_mats = dict(hgrn=None, ret=None, rwkv=None)
    s_mats = dict(hgrn=None, ret=None, rwkv=None)
    mk = matmul_layers(mem2d, wb['mem_w_k'], tm=TOKEN_TILE)
    mv = matmul_layers(mem2d, wb['mem_w_v'], tm=TOKEN_TILE)
    for l in range(DEPTH):
        final = l == DEPTH - 1
        def attend_p(q, w_o, x, l=l):
            return cross_attention_out(q, mk, mv, l, w_o, x, rows=TOKEN_TILE)

        zero_state = dict(s5_re=jnp.zeros((bp, S5_NG, S5_P), F32), s5_im=jnp.zeros((bp, S5_NG, S5_P), F32),
                          shift=jnp.zeros((bp, RW_PROJ), F32), hgrn=mat_zero, ret=mat_zero, rwkv=mat_zero, layer=0)
        yp, small, p_mats = _trunk_layer(yp, bp, lp, 0, attend_p, zero_state, p_mats, p, wb, l, lb_all[l], final)
        p_small.append(small)
        sst = dict(s5_re=state_s5_re[l], s5_im=state_s5_im[l], shift=state_rwkv_shift[l],
                   hgrn=state_hgrn, ret=state_ret, rwkv=state_rwkv, layer=l)
        def attend_s(q, w_o, x, l=l):
            o = cross_attention_cache(q, cache_mem_k, cache_mem_v, l, nb=CACHE_ATTN_SEQS, rows=ls)
            return matmul_residual(o, w_o, x, tm=TOKEN_TILE)

        ys, small, s_mats = _trunk_layer(ys, bs, ls, PAST_LEN, attend_s, sst, s_mats, p, wb, l, lb_all[l], final)
        s_small.append(small)
    stack = lambda states, i: jnp.stack([s[i] for s in states])
    return (yp.reshape(bp, lp, D_MODEL), ys.reshape(bs, ls, D_MODEL),
            stack(p_small, 0), stack(p_small, 1), p_mats['hgrn'], p_mats['ret'], p_mats['rwkv'], stack(p_small, 2),
            mk.reshape(DEPTH, bp, MEM_LEN, MEM_HEADS, MEM_HD), mv.reshape(DEPTH, bp, MEM_LEN, MEM_HEADS, MEM_HD),
            stack(s_small, 0), stack(s_small, 1), s_mats['hgrn'], s_mats['ret'], s_mats['rwkv'], stack(s_small, 2))
```

```python
import functools
import math

import jax
import jax.numpy as jnp
from jax import lax
from jax.experimental import pallas as pl
from jax.experimental.pallas import tpu as pltpu

F32 = jnp.float32
BF16 = jnp.bfloat16

D_MODEL = 1024
DEPTH = 2
PAST_LEN = 16384
GW = 256
HD = 64
NH = GW // HD
S5_GROUP = 16
S5_NG = GW // S5_GROUP
S5_P = 64
S5_W = S5_NG * S5_P
RW_PROJ = 4 * GW
IN_WIDTH = 13 * GW
MEM_LEN = 256
MEM_HEADS = 4
MEM_HD = D_MODEL // MEM_HEADS
D_FF = 4 * D_MODEL
EPS = 1e-6
RW_LN_EPS = 64e-5
ROPE_BASE = 10000.0
CHUNK = 16
RET_CHUNK = 64

VMEM_LIMIT = 56 * 1024 * 1024
SUBLANES = 8
LANES = 128
LOG2_E = 1.4426950408889634


def _cparams(sem):
    return pltpu.CompilerParams(dimension_semantics=sem, vmem_limit_bytes=VMEM_LIMIT)


def _mm(a, b):
    return jnp.dot(a.astype(BF16), b.astype(BF16), preferred_element_type=F32)


def _mm_nt(a, b):
    return lax.dot_general(a.astype(BF16), b.astype(BF16), (((1,), (1,)), ((), ())),
                           preferred_element_type=F32)


def _mm_tn(a, b):
    return lax.dot_general(a.astype(BF16), b.astype(BF16), (((0,), (0,)), ((), ())),
                           preferred_element_type=F32)


def _split3(x):
    hi = x.astype(BF16)
    r1 = x - hi.astype(F32)
    mid = r1.astype(BF16)
    lo = (r1 - mid.astype(F32)).astype(BF16)
    return hi, mid, lo


def _mm_exact_lhs(sel, x):
    s = sel.astype(BF16)
    hi, mid, lo = _split3(x)
    return (jnp.dot(s, hi, preferred_element_type=F32) + jnp.dot(s, mid, preferred_element_type=F32)
            + jnp.dot(s, lo, preferred_element_type=F32))


def _rms(x, gain):
    return x * lax.rsqrt(jnp.mean(x * x, axis=-1, keepdims=True) + EPS) * gain


def _sigmoid(x):
    return 1.0 / (1.0 + jnp.exp(-x))


def _iota2(shape, axis):
    return lax.broadcasted_iota(jnp.int32, shape, axis)


def _head_ones():
    return (_iota2((GW, GW), 0) // HD == _iota2((GW, GW), 1) // HD).astype(F32)


def _head_sum(x, ones_bd):
    s = ones_bd.astype(BF16)
    hi = x.astype(BF16)
    lo = (x - hi.astype(F32)).astype(BF16)
    return jnp.dot(hi, s, preferred_element_type=F32) + jnp.dot(lo, s, preferred_element_type=F32)


ROW_PARTS = 2


def _row_parts(n_rows, n_parts=ROW_PARTS):
    if n_rows % (n_parts * 2 * SUBLANES):
        return [slice(0, n_rows)]
    step = n_rows // n_parts
    return [slice(i * step, (i + 1) * step) for i in range(n_parts)]


def _norm_mm_kernel(x_ref, g_ref, w_ref, o_ref):
    parts = _row_parts(x_ref.shape[0])
    xn = [_rms(x_ref[r, :], g_ref[...]) for r in parts]
    for r, xp in zip(parts, xn):
        o_ref[r, :] = _mm(xp, w_ref[...])


def norm_matmul(x, gain, w, *, tm):
    t, d = x.shape
    n = w.shape[1]
    return pl.pallas_call(
        _norm_mm_kernel,
        out_shape=jax.ShapeDtypeStruct((t, n), F32),
        grid=(t // tm,),
        in_specs=[pl.BlockSpec((tm, d), lambda i: (i, 0)),
                  pl.BlockSpec((1, d), lambda i: (0, 0)),
                  pl.BlockSpec((d, n), lambda i: (0, 0))],
        out_specs=pl.BlockSpec((tm, n), lambda i: (i, 0)),
        compiler_params=_cparams(("parallel",)),
        name="norm_matmul",
    )(x, gain, w)


def _mm_kernel(a_ref, w_ref, o_ref):
    o_ref[...] = _mm(a_ref[...], w_ref[...])


def matmul_layers(a, w, *, tm):
    t, k = a.shape
    layers, _, n = w.shape
    return pl.pallas_call(
        _mm_kernel,
        out_shape=jax.ShapeDtypeStruct((layers, t, n), F32),
        grid=(layers, t // tm),
        in_specs=[pl.BlockSpec((tm, k), lambda l, i: (i, 0)),
                  pl.BlockSpec((None, k, n), lambda l, i: (l, 0, 0))],
        out_specs=pl.BlockSpec((None, tm, n), lambda l, i: (l, i, 0)),
        compiler_params=_cparams(("parallel", "parallel")),
        name="matmul_layers",
    )(a, w)


def _mm_res_kernel(a_ref, w_ref, r_ref, o_ref):
    o_ref[...] = r_ref[...] + _mm(a_ref[...], w_ref[...])


def matmul_residual(a, w, res, *, tm):
    t, k = a.shape
    n = w.shape[1]
    return pl.pallas_call(
        _mm_res_kernel,
        out_shape=jax.ShapeDtypeStruct((t, n), F32),
        grid=(t // tm,),
        in_specs=[pl.BlockSpec((tm, k), lambda i: (i, 0)),
                  pl.BlockSpec((k, n), lambda i: (0, 0)),
                  pl.BlockSpec((tm, n), lambda i: (i, 0))],
        out_specs=pl.BlockSpec((tm, n), lambda i: (i, 0)),
        compiler_params=_cparams(("parallel",)),
        name="matmul_residual",
    )(a, w, res)


def _mix_out_q_kernel(a0_ref, a1_ref, a2_ref, a3_ref, w_ref, r_ref, g_ref, wq_ref, x_ref, q_ref):
    parts = _row_parts(r_ref.shape[0])
    a = [jnp.concatenate([a0_ref[r, :], a1_ref[r, :], a2_ref[r, :], a3_ref[r, :]], axis=-1) for r in parts]
    x = [r_ref[r, :] + _mm(ap, w_ref[...]) for r, ap in zip(parts, a)]
    xn = [_rms(xp, g_ref[...]) for xp in x]
    for r, xp, xnp in zip(parts, x, xn):
        x_ref[r, :] = xp
        q_ref[r, :] = _mm(xnp, wq_ref[...]).astype(BF16)


def mix_out_q(parts, w, res, gain, w_q, *, tm):
    t, d = res.shape
    part = pl.BlockSpec((tm, GW), lambda i: (i, 0))
    mat = pl.BlockSpec((d, d), lambda i: (0, 0))
    tile = pl.BlockSpec((tm, d), lambda i: (i, 0))
    return pl.pallas_call(
        _mix_out_q_kernel,
        out_shape=(jax.ShapeDtypeStruct((t, d), F32), jax.ShapeDtypeStruct((t, d), BF16)),
        grid=(t // tm,),
        in_specs=[part, part, part, part, mat, tile, pl.BlockSpec((1, d), lambda i: (0, 0)), mat],
        out_specs=(tile, tile),
        compiler_params=_cparams(("parallel",)),
        name="mix_out_q",
    )(*parts, w, res, gain, w_q)


def _ffn_kernel(x_ref, g_ref, wu_ref, wd_ref, gf_ref, o_ref, *, final_norm):
    x = x_ref[...]
    h = jnp.dot(_rms(x, g_ref[...]).astype(BF16), wu_ref[...], preferred_element_type=F32)
    h = jnp.square(jnp.maximum(h, 0.0))
    y = x + jnp.dot(h.astype(BF16), wd_ref[...], preferred_element_type=F32)
    if final_norm:
        y = _rms(y, gf_ref[...])
    o_ref[...] = y


def ffn(x, gain, w_up, w_down, gain_final, *, tm, final_norm):
    t, d = x.shape
    ff = w_up.shape[1]
    resident = lambda shape: pl.BlockSpec(shape, lambda i: (0, 0), pipeline_mode=pl.Buffered(1))
    return pl.pallas_call(
        functools.partial(_ffn_kernel, final_norm=final_norm),
        out_shape=jax.ShapeDtypeStruct((t, d), F32),
        grid=(t // tm,),
        in_specs=[pl.BlockSpec((tm, d), lambda i: (i, 0)),
                  pl.BlockSpec((1, d), lambda i: (0, 0)),
                  resident((d, ff)), resident((ff, d)),
                  pl.BlockSpec((1, d), lambda i: (0, 0))],
        out_specs=pl.BlockSpec((tm, d), lambda i: (i, 0)),
        compiler_params=_cparams(("parallel",)),
        name="ffn",
    )(x, gain, w_up, w_down, gain_final)


def _attn_out_kernel(q_ref, k_ref, v_ref, wo_ref, r_ref, x_ref):
    sls = [slice(h * MEM_HD, (h + 1) * MEM_HD) for h in range(MEM_HEADS)]
    s = [_mm_nt(q_ref[:, sl], k_ref[:, sl]) * (MEM_HD ** -0.5) for sl in sls]
    p = [jnp.exp(sh - jnp.max(sh, axis=-1, keepdims=True)) for sh in s]
    p = [ph * (1.0 / jnp.sum(ph, axis=-1, keepdims=True)) for ph in p]
    heads = [_mm(ph, v_ref[:, sl]) for ph, sl in zip(p, sls)]
    x_ref[...] = r_ref[...] + _mm(jnp.concatenate(heads, axis=-1), wo_ref[...])


def cross_attention_out(q, mem_k, mem_v, layer, w_o, res, *, rows):
    t, d = res.shape
    bsz = mem_k.shape[1] // MEM_LEN
    lt = t // bsz // rows
    tile = pl.BlockSpec((rows, d), lambda b, l: (b * lt + l, 0))
    mem = pl.BlockSpec((None, MEM_LEN, d), lambda b, l: (layer, b, 0))
    return pl.pallas_call(
        _attn_out_kernel,
        out_shape=jax.ShapeDtypeStruct((t, d), F32),
        grid=(bsz, lt),
        in_specs=[tile, mem, mem, pl.BlockSpec((d, d), lambda b, l: (0, 0)), tile],
        out_specs=tile,
        compiler_params=_cparams(("parallel", "arbitrary")),
        name="cross_attention_out",
    )(q, mem_k, mem_v, w_o, res)


def _attn_cache_kernel(q_ref, k_ref, v_ref, o_ref, *, nb, rows):
    nr = MEM_HEADS * rows
    same_head = _iota2((nr, MEM_HEADS * MEM_LEN), 1) % MEM_HEADS == _iota2((nr, MEM_HEADS * MEM_LEN), 0) // rows
    q_all = q_ref[...].astype(F32)
    rng = range(nb)
    qs = [jnp.concatenate([q_all[b * rows:(b + 1) * rows, h * MEM_HD:(h + 1) * MEM_HD] for h in range(MEM_HEADS)],
                          axis=0) for b in rng]
    s = [_mm_nt(qs[b], k_ref[0, b].reshape(MEM_HEADS * MEM_LEN, MEM_HD)) * (MEM_HD ** -0.5) for b in rng]
    s = [jnp.where(same_head, sb, -1e30) for sb in s]
    p = [jnp.exp(sb - jnp.max(sb, axis=-1, keepdims=True)) for sb in s]
    p = [pb * (1.0 / jnp.sum(pb, axis=-1, keepdims=True)) for pb in p]
    o = [_mm(p[b], v_ref[0, b].reshape(MEM_HEADS * MEM_LEN, MEM_HD)) for b in rng]
    for b in rng:
        for h in range(MEM_HEADS):
            o_ref[b * rows:(b + 1) * rows, h * MEM_HD:(h + 1) * MEM_HD] = o[b][h * rows:(h + 1) * rows]


def cross_attention_cache(q, cache_k, cache_v, layer, *, nb, rows):
    t = q.shape[0]
    bsz = cache_k.shape[1]
    mem = pl.BlockSpec((1, nb, MEM_LEN, MEM_HEADS, MEM_HD), lambda b: (layer, b, 0, 0, 0))
    return pl.pallas_call(
        functools.partial(_attn_cache_kernel, nb=nb, rows=rows),
        out_shape=jax.ShapeDtypeStruct((t, D_MODEL), F32),
        grid=(bsz // nb,),
        in_specs=[pl.BlockSpec((nb * rows, D_MODEL), lambda b: (b, 0)), mem, mem],
        out_specs=pl.BlockSpec((nb * rows, D_MODEL), lambda b: (b, 0)),
        compiler_params=_cparams(("parallel",)),
        name="cross_attention_cache",
    )(q, cache_k, cache_v)


def _gelu_tanh(x):
    return 0.5 * x * (1.0 + jnp.tanh(math.sqrt(2.0 / math.pi) * (x + 0.044715 * (x * x * x))))


def _s5_kernel(u_ref, h0_ref, lam_ref, bblk_ref, cblk_ref, d_ref, wglu_ref, bglu_ref, gain_ref,
               y_ref, hfin_ref, scr_ref, tm_ref, *, steps, nb):
    @pl.when(pl.program_id(1) == 0)
    def _():
        scr_ref[0:nb, :] = h0_ref[...]

    def to_time_major(b, carry):
        for j in range(GW // LANES):
            tm_ref[j, pl.ds(b, steps, stride=nb), :] = u_ref[b, :, j * LANES:(j + 1) * LANES]
        return carry

    lax.fori_loop(0, nb, to_time_major, 0)
    u = jnp.concatenate([tm_ref[j] for j in range(GW // LANES)], axis=1)
    rows = _row_parts(steps * nb, 4)
    for r in rows:
        scr_ref[nb + r.start:nb + r.stop, :] = _mm(u[r], bblk_ref[...])
    lam_re = jnp.broadcast_to(lam_ref[0:1, :], (nb, S5_W))
    lam_im = jnp.broadcast_to(lam_ref[1:2, :], (nb, S5_W))

    h_re, h_im = scr_ref[0:nb, 0:S5_W], scr_ref[0:nb, S5_W:]
    for t in range(steps):
        c0 = (t + 1) * nb
        h_re, h_im = (scr_ref[c0:c0 + nb, 0:S5_W] + lam_re * h_re - lam_im * h_im,
                      scr_ref[c0:c0 + nb, S5_W:] + lam_re * h_im + lam_im * h_re)
        scr_ref[c0:c0 + nb, 0:S5_W] = h_re
        scr_ref[c0:c0 + nb, S5_W:] = h_im
    h_last = scr_ref[steps * nb:, :]
    hfin_ref[...] = h_last
    y = [_mm(scr_ref[nb + r.start:nb + r.stop, :], cblk_ref[...]) + d_ref[...] * u[r] for r in rows]
    y = [_gelu_tanh(yp) for yp in y]
    z = [_mm(yp, wglu_ref[...]) for yp in y]
    y = [_rms(yp * _sigmoid(zp + bglu_ref[...]), gain_ref[...]) for yp, zp in zip(y, z)]
    for r, yp in zip(rows, y):
        for j in range(GW // LANES):
            tm_ref[j, r, :] = yp[:, j * LANES:(j + 1) * LANES]

    def to_batch_major(b, carry):
        for j in range(GW // LANES):
            y_ref[b, :, j * LANES:(j + 1) * LANES] = tm_ref[j, pl.ds(b, steps, stride=nb), :]
        return carry

    lax.fori_loop(0, nb, to_batch_major, 0)
    scr_ref[0:nb, :] = h_last


def s5_mixer(proj, h0, lam, bblk, cblk, dvec, wglu, bglu, gain, *, steps, nb):
    bsz, seq, _ = proj.shape
    const = lambda shape: pl.BlockSpec(shape, lambda b, t: (0, 0))
    return pl.pallas_call(
        functools.partial(_s5_kernel, steps=steps, nb=nb),
        out_shape=(jax.ShapeDtypeStruct((bsz, seq, GW), F32),
                   jax.ShapeDtypeStruct((bsz, 2 * S5_W), F32)),
        grid=(bsz // nb, seq // steps),
        in_specs=[pl.BlockSpec((nb, steps, GW), lambda b, t: (b, t, 0)),
                  pl.BlockSpec((nb, 2 * S5_W), lambda b, t: (b, 0)),
                  const((2, S5_W)), const((GW, 2 * S5_W)), const((2 * S5_W, GW)), const((1, GW)),
                  const((GW, GW)), const((1, GW)), const((1, GW))],
        out_specs=(pl.BlockSpec((nb, steps, GW), lambda b, t: (b, t, 0)),
                   pl.BlockSpec((nb, 2 * S5_W), lambda b, t: (b, 0))),
        scratch_shapes=[pltpu.VMEM(((steps + 1) * nb, 2 * S5_W), F32),
                        pltpu.VMEM((GW // LANES, steps * nb, LANES), F32)],
        compiler_params=_cparams(("parallel", "arbitrary")),
        name="s5_mixer",
    )(proj, h0, lam, bblk, cblk, dvec, wglu, bglu, gain)


def s5_params(lam_re, lam_im, b_re, b_im, c_re, c_im, log_step):
    step = jnp.exp(log_step)[:, None]
    mag = jnp.exp(lam_re * step)
    lbar_re = mag * jnp.cos(lam_im * step)
    lbar_im = mag * jnp.sin(lam_im * step)
    den = lam_re * lam_re + lam_im * lam_im
    f_re = ((lbar_re - 1.0) * lam_re + lbar_im * lam_im) / den
    f_im = (lbar_im * lam_re - (lbar_re - 1.0) * lam_im) / den
    bbar_re = f_re[..., None] * b_re - f_im[..., None] * b_im
    bbar_im = f_re[..., None] * b_im + f_im[..., None] * b_re
    eye_g = jnp.eye(S5_NG, dtype=F32)

    def in_blk(m):
        return jnp.einsum('gph,gk->ghkp', m, eye_g).reshape(GW, S5_W)

    def out_blk(m):
        return jnp.einsum('ghp,gk->gpkh', m, eye_g).reshape(S5_W, GW)

    bblk = jnp.concatenate([in_blk(bbar_re), in_blk(bbar_im)], axis=1)
    cblk = jnp.concatenate([out_blk(c_re), -out_blk(c_im)], axis=0)
    lam2 = jnp.stack([lbar_re.reshape(S5_W), lbar_im.reshape(S5_W)])
    return lam2, bblk.astype(BF16), cblk.astype(BF16)


def s5_apply(proj, h0_re, h0_im, p, l, *, steps, nb):
    bsz = proj.shape[0]
    lam2, bblk, cblk = s5_params(p['s5_lam_re'][l], p['s5_lam_im'][l], p['s5_b_re'][l], p['s5_b_im'][l],
                                 p['s5_c_re'][l], p['s5_c_im'][l], p['s5_log_step'][l])
    h0 = jnp.concatenate([h0_re.reshape(bsz, S5_W), h0_im.reshape(bsz, S5_W)], axis=1)
    y, h = s5_mixer(proj, h0, lam2, bblk, cblk, p['s5_d'][l].reshape(1, GW),
                    p['s5_w_glu'][l].astype(BF16), p['s5_b_glu'][l].reshape(1, GW),
                    p['s5_norm'][l].reshape(1, GW), steps=steps, nb=nb)
    return y, h[:, :S5_W].reshape(bsz, S5_NG, S5_P), h[:, S5_W:].reshape(bsz, S5_NG, S5_P)


def _load_state(s0_ref, st_ref, nb, transpose):
    for b in range(nb):
        rows = []
        for h in range(NH):
            pieces = [s0_ref[b, h]]
            if h:
                pieces.insert(0, jnp.zeros((HD, h * HD), F32))
            if h < NH - 1:
                pieces.append(jnp.zeros((HD, (NH - 1 - h) * HD), F32))
            rows.append(jnp.concatenate(pieces, axis=1))
        st = jnp.concatenate(rows, axis=0)
        st_ref[b] = st.T if transpose else st


def _store_state(st_ref, sfin_ref, nb, transpose, own_layer):
    if own_layer is not None:
        for other in range(DEPTH):
            if other != own_layer:
                sfin_ref[other] = jnp.zeros(sfin_ref.shape[1:], F32)
        sfin_ref = sfin_ref.at[own_layer]
    for b in range(nb):
        st = st_ref[b].T if transpose else st_ref[b]
        for h in range(NH):
            sfin_ref[b, h] = st[h * HD:(h + 1) * HD, h * HD:(h + 1) * HD]


def _head_masks():
    lane_head = _iota2((1, GW), 1) // HD
    return [(lane_head == h).astype(F32) for h in range(NH)]


def _stack_heads(x, masks):
    return jnp.concatenate([x * m for m in masks], axis=0)


def _pad_rows(x, rows):
    if x.shape[0] == rows:
        return x
    return jnp.concatenate([x, jnp.zeros((rows - x.shape[0], x.shape[1]), x.dtype)], axis=0)


def _silu(x):
    return x * _sigmoid(x)


def _chunk_rows(ref, b, r0, rows, c):
    return _pad_rows(ref[b, pl.ds(r0, rows), :], c)


CHUNK_UNROLL = 4


def _chunk_loop(n_chunks, rows, body):
    unroll = math.gcd(n_chunks, CHUNK_UNROLL)

    def step(i, carry):
        for j in range(unroll):
            body(pl.multiple_of((i * unroll + j) * rows, rows))
        return carry

    lax.fori_loop(0, n_chunks // unroll, step, 0)


def _ret_kernel(q_ref, k_ref, v_ref, g_ref, cos_ref, sin_ref, s0_ref, _states_in, o_ref, sfin_ref, st_ref,
                *, nb, tb, c, c_real, own_layer):
    rows = min(tb, c)

    @pl.when(pl.program_id(1) == 0)
    def _():
        _load_state(s0_ref, st_ref, nb, transpose=False)

    masks = _head_masks()
    ones_bd = _head_ones()
    lane_head = _iota2((1, GW), 1) // HD
    log_gamma = jnp.zeros((1, GW), F32)
    for h in range(NH):
        log_gamma = jnp.where(lane_head == h, math.log(1.0 - 2.0 ** (-5.0 - h)), log_gamma)
    tt = _iota2((c, GW), 0).astype(F32)
    scale = HD ** -0.5
    g_q = jnp.exp(log_gamma * (tt + 1.0))
    g_k = jnp.exp(-log_gamma * (tt + 1.0)) * scale
    g_tail = jnp.exp(log_gamma * (c_real - 1.0 - tt)) * scale
    row_head = _iota2((GW, GW), 0) // HD
    g_chunk = jnp.zeros((GW, GW), F32)
    for h in range(NH):
        g_chunk = jnp.where(row_head == h, math.exp(math.log(1.0 - 2.0 ** (-5.0 - h)) * c_real), g_chunk)
    causal = _iota2((c, NH * c), 0) >= _iota2((c, NH * c), 1) % c
    first_half = _iota2((c, GW), 1) % HD < HD // 2

    def rope(x, cos, sin):
        swapped = jnp.where(first_half, pltpu.roll(x, GW - HD // 2, axis=1), pltpu.roll(x, HD // 2, axis=1))
        return x * cos + swapped * sin

    def chunk(r0):
        cos = _pad_rows(cos_ref[pl.ds(r0, rows), :], c)
        sin = _pad_rows(sin_ref[pl.ds(r0, rows), :], c)
        rng = range(nb)
        qt = [rope(_chunk_rows(q_ref, b, r0, rows, c), cos, sin) * g_q for b in rng]
        k = [rope(_chunk_rows(k_ref, b, r0, rows, c), cos, sin) for b in rng]
        v = [_chunk_rows(v_ref, b, r0, rows, c) for b in rng]
        st = [st_ref[b] for b in rng]
        sc = [_mm_nt(qt[b], _stack_heads(k[b] * g_k, masks)) for b in rng]
        o_in = [_mm(jnp.where(causal, sc[b], 0.0), _stack_heads(v[b], masks)) for b in rng]
        o_st = [_mm(qt[b], st[b]) for b in rng]
        d_st = [_mm_tn(k[b] * g_tail, v[b]) for b in rng]
        o = [o_in[b] + o_st[b] for b in rng]
        ms = [_head_sum(o[b] * o[b], ones_bd) for b in rng]
        for b in rng:
            st_ref[b] = st[b] * g_chunk + ones_bd * d_st[b]
            ob = o[b] * lax.rsqrt(ms[b] * (1.0 / HD) + EPS) * _silu(_chunk_rows(g_ref, b, r0, rows, c))
            o_ref[b, pl.ds(r0, rows), :] = ob[:rows]

    _chunk_loop(tb // rows, rows, chunk)

    @pl.when(pl.program_id(1) == pl.num_programs(1) - 1)
    def _():
        _store_state(st_ref, sfin_ref, nb, transpose=False, own_layer=own_layer)


def _rope_tables(pos0, seq):
    half = HD // 2
    inv = ROPE_BASE ** (-jnp.arange(half, dtype=F32) / half)
    pos = pos0 + jnp.arange(seq, dtype=jnp.int32)
    ang = pos.astype(F32)[:, None] * inv[None, :]
    cos, sin = jnp.cos(ang), jnp.sin(ang)
    return (jnp.tile(jnp.concatenate([cos, cos], axis=-1), (1, NH)),
            jnp.tile(jnp.concatenate([-sin, sin], axis=-1), (1, NH)))


def _proj_spec(nb, tb, col):
    return pl.BlockSpec((nb, tb, GW), lambda b, t: (b, t, col))


def _state_spec(nb, layer):
    return pl.BlockSpec((None, nb, NH, HD, HD), lambda b, t: (layer, b, 0, 0, 0))


_ALIASED = pl.BlockSpec(memory_space=pl.ANY)


def _collector(states, bsz, nb, layer, operand_index):
    shape = jax.ShapeDtypeStruct((DEPTH, bsz, NH, HD, HD), F32)
    if states is None:
        spec = pl.BlockSpec((DEPTH, nb, NH, HD, HD), lambda b, t: (0, b, 0, 0, 0))
        return jnp.zeros((1, 1, NH, HD, HD), F32), spec, shape, {}, layer
    return states, _state_spec(nb, layer), shape, {operand_index: 1}, None


def retention_apply(proj, s0, s0_layer, states, layer, pos0, *, tb, nb):
    bsz, seq, _ = proj.shape
    c = RET_CHUNK if tb >= RET_CHUNK else CHUNK
    cos, sin = _rope_tables(pos0, seq)
    states, st_spec, st_shape, aliases, own_layer = _collector(states, bsz, nb, layer, 7)
    return pl.pallas_call(
        functools.partial(_ret_kernel, nb=nb, tb=tb, c=c, c_real=min(tb, c), own_layer=own_layer),
        out_shape=(jax.ShapeDtypeStruct((bsz, seq, GW), F32), st_shape),
        grid=(bsz // nb, seq // tb),
        in_specs=[_proj_spec(nb, tb, 5), _proj_spec(nb, tb, 6), _proj_spec(nb, tb, 7), _proj_spec(nb, tb, 8),
                  pl.BlockSpec((tb, GW), lambda b, t: (t, 0)), pl.BlockSpec((tb, GW), lambda b, t: (t, 0)),
                  _state_spec(nb, s0_layer), _ALIASED],
        out_specs=(pl.BlockSpec((nb, tb, GW), lambda b, t: (b, t, 0)), st_spec),
        scratch_shapes=[pltpu.VMEM((nb, GW, GW), F32)],
        input_output_aliases=aliases,
        compiler_params=_cparams(("parallel", "arbitrary")),
        name="retention",
    )(proj, proj, proj, proj, cos, sin, s0, states)


def _log_sigmoid(z):
    return jnp.minimum(z, 0.0) - jnp.log(1.0 + jnp.exp(-jnp.abs(z)))


def _hgrn_kernel(q_ref, f_ref, i_ref, g_ref, lb_ref, gain_ref, s0_ref, _states_in, o_ref, sfin_ref, st_ref,
                 *, nb, tb, c, own_layer):
    rows = min(tb, c)

    @pl.when(pl.program_id(1) == 0)
    def _():
        _load_state(s0_ref, st_ref, nb, transpose=True)

    t_hi = -(-rows // SUBLANES) * SUBLANES
    ones_bd = _head_ones()
    tri = (_iota2((c, c), 0) >= _iota2((c, c), 1)).astype(F32)
    t_idx = _iota2((c, GW), 0)
    lb = lb_ref[...]
    log_lb = jnp.log(lb)
    log_1m_lb = jnp.log(1.0 - lb)
    gain = gain_ref[...]

    def chunk(r0):
        rng = range(nb)
        valid = t_idx < rows

        def gates(b):
            z = _chunk_rows(f_ref, b, r0, rows, c)
            ls_pos = _log_sigmoid(z)
            ls_neg = ls_pos - z
            b2 = log_lb + ls_neg
            log_f = jnp.maximum(ls_pos, b2) + jnp.log(1.0 + jnp.exp(-jnp.abs(ls_pos - b2)))
            return (jnp.where(valid, log_f, 0.0), jnp.where(valid, (1.0 - lb) * jnp.exp(ls_neg), 0.0),
                    ls_neg + log_1m_lb)

        log_f, key, log_key = zip(*[gates(b) for b in rng])
        q = [_silu(_chunk_rows(q_ref, b, r0, rows, c)) for b in rng]
        v = [_chunk_rows(i_ref, b, r0, rows, c) for b in rng]
        st = [st_ref[b] for b in rng]
        cum = [_mm_exact_lhs(tri, log_f[b]) for b in rng]
        last = [cum[b][c - 1:c, :] for b in rng]
        cum2 = [cum[b] * LOG2_E for b in rng]
        kd2 = [(log_key[b] - cum[b]) * LOG2_E for b in rng]

        def pair_rows(b, s):
            lo = SUBLANES * (s // SUBLANES)
            t_rows = _iota2((t_hi - lo, GW), 0) + lo
            key_decay = jnp.where(t_rows >= s, jnp.exp2(cum2[b][lo:t_hi] + kd2[b][s:s + 1, :]), 0.0)
            return key_decay * q[b][lo:t_hi]

        sc = [_mm(jnp.concatenate([pair_rows(b, s) for s in range(rows)], axis=0), ones_bd) for b in rng]
        o_st = [_mm_nt(q[b] * jnp.exp(cum[b]), st[b]) for b in rng]
        d_st = [_mm_tn(v[b], key[b] * jnp.exp(last[b] - cum[b])) for b in rng]

        def intra(b):
            tiles = [None] * (c // SUBLANES)
            off = 0
            for s in range(rows):
                for ti in range(s // SUBLANES, t_hi // SUBLANES):
                    term = sc[b][off:off + SUBLANES, :] * v[b][s:s + 1, :]
                    tiles[ti] = term if tiles[ti] is None else tiles[ti] + term
                    off += SUBLANES
            zero = jnp.zeros((SUBLANES, GW), F32)
            return jnp.concatenate([zero if t is None else t for t in tiles], axis=0)

        o = [o_st[b] + intra(b) for b in rng]
        ms = [_head_sum(o[b] * o[b], ones_bd) for b in rng]
        for b in rng:
            st_ref[b] = st[b] * jnp.exp(last[b]) + ones_bd * d_st[b]
            ob = o[b] * lax.rsqrt(ms[b] * (1.0 / HD) + EPS) * gain * _silu(_chunk_rows(g_ref, b, r0, rows, c))
            o_ref[b, pl.ds(r0, rows), :] = ob[:rows]

    _chunk_loop(tb // rows, rows, chunk)

    @pl.when(pl.program_id(1) == pl.num_programs(1) - 1)
    def _():
        _store_state(st_ref, sfin_ref, nb, transpose=True, own_layer=own_layer)


def hgrn_apply(proj, s0, s0_layer, states, layer, lb, gain, *, tb, nb):
    bsz, seq, _ = proj.shape
    row = pl.BlockSpec((1, GW), lambda b, t: (0, 0))
    states, st_spec, st_shape, aliases, own_layer = _collector(states, bsz, nb, layer, 7)
    return pl.pallas_call(
        functools.partial(_hgrn_kernel, nb=nb, tb=tb, c=CHUNK, own_layer=own_layer),
        out_shape=(jax.ShapeDtypeStruct((bsz, seq, GW), F32), st_shape),
        grid=(bsz // nb, seq // tb),
        in_specs=[_proj_spec(nb, tb, 1), _proj_spec(nb, tb, 2), _proj_spec(nb, tb, 3), _proj_spec(nb, tb, 4),
                  row, row, _state_spec(nb, s0_layer), _ALIASED],
        out_specs=(pl.BlockSpec((nb, tb, GW), lambda b, t: (b, t, 0)), st_spec),
        scratch_shapes=[pltpu.VMEM((nb, GW, GW), F32)],
        input_output_aliases=aliases,
        compiler_params=_cparams(("parallel", "arbitrary")),
        name="hgrn2",
    )(proj, proj, proj, proj, lb.reshape(1, GW), gain.reshape(1, GW), s0, states)


def _softplus(z):
    return jnp.maximum(z, 0.0) + jnp.log(1.0 + jnp.exp(-jnp.abs(z)))


def _rwkv_prepare(ins, consts):
    masks, ones_bd, bd_mask, eye_flat, tri, strict, incl = consts
    rng = range(len(ins))
    r, lw, k, v, kk, a = [[i[j] for i in ins] for j in range(6)]
    c = r[0].shape[0]
    n = NH * c

    def to_bd(flat):
        return jnp.concatenate([flat] * NH, axis=0) * bd_mask

    cum = [_mm_exact_lhs(tri, lw[i]) for i in rng]
    last = [cum[i][c - 1:c, :] for i in rng]
    p_inv = [jnp.exp(-cum[i]) for i in rng]
    p_tail = [jnp.exp(last[i] - cum[i]) for i in rng]
    ka = [kk[i] * a[i] for i in rng]
    x = [jnp.concatenate([kk[i] * jnp.exp(cum[i] - lw[i]), r[i] * jnp.exp(cum[i])], axis=0) for i in rng]
    g = [_mm_nt(x[i], jnp.concatenate([_stack_heads(ka[i] * p_inv[i], masks),
                                       _stack_heads(k[i] * p_inv[i], masks)], axis=0)) for i in rng]
    m_ak = [jnp.where(strict, g[i][:c], 0.0) for i in rng]
    n_ak = [jnp.where(incl, g[i][c:], 0.0) for i in rng]
    pw = [m_ak[i][:, :n] for i in rng]
    t_inv = [eye_flat - pw[i] for i in rng]
    pw_bd = [to_bd(pw[i]) for i in rng]
    for _ in range(int(math.log2(c)) - 1):
        pw = [_mm(pw[i], pw_bd[i]) for i in rng]
        pw_bd = [to_bd(pw[i]) for i in rng]
        t_inv = [t_inv[i] + _mm(t_inv[i], pw_bd[i]) for i in rng]
    v_stk = [_stack_heads(v[i], masks) for i in rng]
    zeros = jnp.zeros((n, GW), F32)
    mkv = [_mm(m_ak[i], jnp.concatenate([zeros, v_stk[i]], axis=0)) for i in rng]
    tmkv = [_mm(t_inv[i], _stack_heads(mkv[i], masks)) for i in rng]
    return [dict(x=x[i], t_inv=t_inv[i], tmkv=tmkv[i], n_ak=n_ak[i], v=v[i], v_stk=v_stk[i],
                 decay=jnp.exp(last[i]),
                 k_tail=jnp.concatenate([ka[i] * p_tail[i], k[i] * p_tail[i]], axis=0)) for i in rng]


def _rwkv_advance(prep, st, consts):
    masks, ones_bd = consts[0], consts[1]
    rng = range(len(prep))
    c = prep[0]['t_inv'].shape[0]
    xa = [_mm_nt(prep[i]['x'], st[i]) for i in rng]
    u = [-_mm(prep[i]['t_inv'], _stack_heads(xa[i][:c], masks)) - prep[i]['tmkv'] for i in rng]
    nuv = [_mm(prep[i]['n_ak'], jnp.concatenate([_stack_heads(u[i], masks), prep[i]['v_stk']], axis=0))
           for i in rng]
    d_uv = [_mm_tn(jnp.concatenate([u[i], prep[i]['v']], axis=0), prep[i]['k_tail']) for i in rng]
    return [(xa[i][c:] + nuv[i], st[i] * prep[i]['decay'] + ones_bd * d_uv[i]) for i in rng]


def _rwkv_kernel(x_r_ref, x_k_ref, x_v_ref, x_l_ref, sh0_ref, s0_ref, mu_ref, w0_ref, ww_ref, a0_ref,
                 wa_ref, wg_ref, kk_ref, ka_ref, rk_ref, lng_ref, lnb_ref, _states_in,
                 o_ref, sfin_ref, shfin_ref,
                 st_ref, sh_ref, r_s, lw_s, k_s, v_s, kkn_s, a_s, y_s, bonus_s, gate_s, *, nb, tb, c, own_layer):
    tbp = max(tb, c)

    @pl.when(pl.program_id(1) == 0)
    def _():
        _load_state(s0_ref, st_ref, nb, transpose=False)
        sh_ref[...] = sh0_ref[...]

    masks = _head_masks()
    ones_bd = _head_ones()
    n = NH * c
    row_t = _iota2((c, 2 * n), 0)
    col_t = _iota2((c, 2 * n), 1) % c
    bd_mask = (_iota2((n, n), 0) // c == _iota2((n, n), 1) // c).astype(F32)
    eye_flat = (_iota2((c, n), 1) % c == _iota2((c, n), 0)).astype(F32)
    consts = (masks, ones_bd, bd_mask, eye_flat, (_iota2((c, c), 0) >= _iota2((c, c), 1)).astype(F32),
              col_t < row_t, col_t <= row_t)
    first_row = _iota2((nb * tb, GW), 0) % tb == 0

    def mixed(x_ref, j):
        x = x_ref[...].reshape(nb * tb, GW)
        carried = jnp.broadcast_to(sh_ref[:, j:j + 1, :], (nb, tb, GW)).reshape(nb * tb, GW)
        prev = jnp.where(first_row, carried, pltpu.roll(x, 1, axis=0))
        sh_ref[:, j:j + 1, :] = x_ref[:, tb - 1:tb, :]
        return x + (prev - x) * mu_ref[j:j + 1, :]

    r = mixed(x_r_ref, 0)
    k = mixed(x_k_ref, 1)
    v = mixed(x_v_ref, 2)
    xl = mixed(x_l_ref, 3)
    log_w = -_softplus(-(w0_ref[...] + _mm(jnp.tanh(xl), ww_ref[...]))) - 0.5
    a = _sigmoid(a0_ref[...] + _mm(xl, wa_ref[...]))
    gate_s[...] = _mm(_sigmoid(xl), wg_ref[...]).reshape(nb, tb, GW)
    kk = k * kk_ref[...]
    kk = kk * lax.rsqrt(jnp.maximum(_head_sum(kk * kk, ones_bd), 1e-24))
    k = k * (1.0 + (a - 1.0) * ka_ref[...])
    bonus_s[...] = (_head_sum(r * k * rk_ref[...], ones_bd) * v).reshape(nb, tb, GW)
    for s, val in ((r_s, r), (lw_s, -jnp.exp(log_w)), (k_s, k), (v_s, v), (kkn_s, kk), (a_s, a)):
        s[:, 0:tb, :] = val.reshape(nb, tb, GW)
        if tbp > tb:
            s[:, tb:tbp, :] = jnp.zeros((nb, tbp - tb, GW), F32)

    n_chunks = tbp // c
    group = 8 if n_chunks % 8 == 0 else 1

    def chunks(gi, carry):
        r0 = [pl.multiple_of((gi * group + j) * c, c) for j in range(group)]
        prep = _rwkv_prepare([tuple(s[b, pl.ds(r0[j], c), :] for s in (r_s, lw_s, k_s, v_s, kkn_s, a_s))
                              for j in range(group) for b in range(nb)], consts)
        st = [st_ref[b] for b in range(nb)]
        for j in range(group):
            outs = _rwkv_advance(prep[j * nb:(j + 1) * nb], st, consts)
            st = [o[1] for o in outs]
            for b in range(nb):
                y_s[b, pl.ds(r0[j], c), :] = outs[b][0]
        for b in range(nb):
            st_ref[b] = st[b]
        return carry

    lax.fori_loop(0, n_chunks // group, chunks, 0)

    y = y_s[:, 0:tb, :].reshape(nb * tb, GW)
    mean = _head_sum(y, ones_bd) * (1.0 / HD)
    d = y - mean
    var = _head_sum(d * d, ones_bd) * (1.0 / HD)
    y = d * lax.rsqrt(var + RW_LN_EPS) * lng_ref[...] + lnb_ref[...]
    o_ref[...] = ((y.reshape(nb, tb, GW) + bonus_s[...]) * gate_s[...])
    shfin_ref[...] = sh_ref[...]

    @pl.when(pl.program_id(1) == pl.num_programs(1) - 1)
    def _():
        _store_state(st_ref, sfin_ref, nb, transpose=False, own_layer=own_layer)


def rwkv_apply(proj, s0, s0_layer, states, shift0, p, l, *, tb, nb):
    bsz, seq, _ = proj.shape
    states, st_spec, st_shape, aliases, own_layer = _collector(states, bsz, nb, l, 17)
    c = CHUNK
    tbp = max(tb, c)
    row = pl.BlockSpec((1, GW), lambda b, t: (0, 0))
    mat = pl.BlockSpec((GW, GW), lambda b, t: (0, 0))
    sh_spec = pl.BlockSpec((nb, 4, GW), lambda b, t: (b, 0, 0))
    zeros = lambda r: jnp.zeros((r, GW), F32)
    ww = jnp.concatenate([p['rw_w_w2'][l], zeros(192)], axis=0).astype(BF16)
    wa = jnp.concatenate([zeros(64), p['rw_w_a2'][l], zeros(128)], axis=0).astype(BF16)
    wg = jnp.concatenate([zeros(128), p['rw_w_g2'][l]], axis=0).astype(BF16)
    r1 = lambda name: p[name][l].reshape(1, GW)
    seq_buf = pltpu.VMEM((nb, tbp, GW), F32)
    blk_buf = pltpu.VMEM((nb, tb, GW), F32)
    y, st, sh = pl.pallas_call(
        functools.partial(_rwkv_kernel, nb=nb, tb=tb, c=c, own_layer=own_layer),
        out_shape=(jax.ShapeDtypeStruct((bsz, seq, GW), F32), st_shape,
                   jax.ShapeDtypeStruct((bsz, 4, GW), F32)),
        grid=(bsz // nb, seq // tb),
        in_specs=[_proj_spec(nb, tb, 9), _proj_spec(nb, tb, 10), _proj_spec(nb, tb, 11), _proj_spec(nb, tb, 12),
                  sh_spec, _state_spec(nb, s0_layer), pl.BlockSpec((4, GW), lambda b, t: (0, 0)),
                  row, mat, row, mat, mat, row, row, row, row, row, _ALIASED],
        out_specs=(pl.BlockSpec((nb, tb, GW), lambda b, t: (b, t, 0)), st_spec, sh_spec),
        scratch_shapes=[pltpu.VMEM((nb, GW, GW), F32), pltpu.VMEM((nb, 4, GW), F32),
                        seq_buf, seq_buf, seq_buf, seq_buf, seq_buf, seq_buf, seq_buf, blk_buf, blk_buf],
        input_output_aliases=aliases,
        compiler_params=_cparams(("parallel", "arbitrary")),
        name="rwkv7",
    )(proj, proj, proj, proj, shift0.reshape(bsz, 4, GW), s0, p['rw_mu'][l].reshape(4, GW),
      r1('rw_w0'), ww, r1('rw_a0'), wa, wg, r1('rw_k_k'), r1('rw_k_a'), r1('rw_r_k'), r1('rw_ln_g'), r1('rw_ln_b'),
      states)
    return y, st, sh.reshape(bsz, RW_PROJ)


TOKEN_TILE = 512
TIME_BLOCK = 128
CACHE_ATTN_SEQS = 4


def _tiles(bsz, seq):
    if seq >= TIME_BLOCK:
        return dict(tb=TIME_BLOCK, s5_nb=bsz, mix_nb=bsz, rw_nb=bsz)
    return dict(tb=seq, s5_nb=bsz, mix_nb=16, rw_nb=32)


def _trunk_layer(x, bsz, seq, pos0, attend, st, mats, p, wb, l, lb, final_norm):
    cfg = _tiles(bsz, seq)
    tm = TOKEN_TILE
    row = lambda name: p[name][l].reshape(1, -1)
    proj = norm_matmul(x, row('norm_mix'), wb['w_in'][l], tm=tm)
    proj = proj.reshape(bsz, seq, IN_WIDTH)
    y_s5, s5_re, s5_im = s5_apply(proj, st['s5_re'], st['s5_im'], p, l, steps=cfg['tb'], nb=cfg['s5_nb'])
    y_hg, hg_s = hgrn_apply(proj, st['hgrn'], st['layer'], mats['hgrn'], l, lb, p['hg_norm'][l],
                            tb=cfg['tb'], nb=cfg['mix_nb'])
    y_rt, rt_s = retention_apply(proj, st['ret'], st['layer'], mats['ret'], l, pos0,
                                 tb=cfg['tb'], nb=cfg['mix_nb'])
    y_rw, rw_s, shift = rwkv_apply(proj, st['rwkv'], st['layer'], mats['rwkv'], st['shift'], p, l,
                                   tb=cfg['tb'], nb=cfg['rw_nb'])
    parts = [y.reshape(bsz * seq, GW) for y in (y_s5, y_hg, y_rt, y_rw)]
    x, q = mix_out_q(parts, wb['w_out'][l], x, row('norm_mem'), wb['mem_w_q'][l], tm=tm)
    x = attend(q, wb['mem_w_o'][l], x)
    x = ffn(x, row('norm_ffn'), wb['ffn_w_up'][l], wb['ffn_w_down'][l], p['norm_final'].reshape(1, -1),
            tm=tm, final_norm=final_norm)
    return x, (s5_re, s5_im, shift), dict(hgrn=hg_s, ret=rt_s, rwkv=rw_s)


def kernel(x_prompt, x_sample, mem_prompt, state_s5_re, state_s5_im, state_hgrn, state_ret, state_rwkv,
           state_rwkv_shift, cache_mem_k, cache_mem_v, norm_mix, w_in, w_out, s5_lam_re, s5_lam_im,
           s5_b_re, s5_b_im, s5_c_re, s5_c_im, s5_d, s5_log_step, s5_w_glu, s5_b_glu, s5_norm,
           hg_lb_logits, hg_norm, rw_mu, rw_w0, rw_w_w2, rw_a0, rw_w_a2, rw_w_g2, rw_k_k, rw_k_a, rw_r_k,
           rw_ln_g, rw_ln_b, norm_mem, mem_w_q, mem_w_k, mem_w_v, mem_w_o, norm_ffn, ffn_w_up, ffn_w_down,
           norm_final):
    p = dict(norm_mix=norm_mix, s5_lam_re=s5_lam_re, s5_lam_im=s5_lam_im, s5_b_re=s5_b_re, s5_b_im=s5_b_im,
             s5_c_re=s5_c_re, s5_c_im=s5_c_im, s5_d=s5_d, s5_log_step=s5_log_step, s5_w_glu=s5_w_glu,
             s5_b_glu=s5_b_glu, s5_norm=s5_norm, hg_norm=hg_norm, rw_mu=rw_mu, rw_w0=rw_w0, rw_w_w2=rw_w_w2,
             rw_a0=rw_a0, rw_w_a2=rw_w_a2, rw_w_g2=rw_w_g2, rw_k_k=rw_k_k, rw_k_a=rw_k_a, rw_r_k=rw_r_k,
             rw_ln_g=rw_ln_g, rw_ln_b=rw_ln_b, norm_mem=norm_mem, norm_ffn=norm_ffn, norm_final=norm_final)
    wb = {name: w.astype(BF16) for name, w in dict(
        w_in=w_in, w_out=w_out, mem_w_q=mem_w_q, mem_w_k=mem_w_k, mem_w_v=mem_w_v, mem_w_o=mem_w_o,
        ffn_w_up=ffn_w_up, ffn_w_down=ffn_w_down).items()}
    lb_all = jnp.cumsum(jax.nn.softmax(hg_lb_logits.astype(F32), axis=0), axis=0)
    lb_all = lb_all - lb_all[0:1]

    bp, lp, _ = x_prompt.shape
    bs, ls, _ = x_sample.shape
    yp = x_prompt.reshape(bp * lp, D_MODEL)
    ys = x_sample.reshape(bs * ls, D_MODEL)
    mem2d = mem_prompt.reshape(bp * MEM_LEN, D_MODEL)
    mat_zero = jnp.zeros((1, bp, NH, HD, HD), F32)
    p_small, s_small = [], []
    p_mats = dict(hgrn=None, ret=None, rwkv=None)
    s_mats = dict(hgrn=None, ret=None, rwkv=None)
    mk = matmul_layers(mem2d, wb['mem_w_k'], tm=TOKEN_TILE)
    mv = matmul_layers(mem2d, wb['mem_w_v'], tm=TOKEN_TILE)
    for l in range(DEPTH):
        final = l == DEPTH - 1
        def attend_p(q, w_o, x, l=l):
            return cross_attention_out(q, mk, mv, l, w_o, x, rows=TOKEN_TILE)

        zero_state = dict(s5_re=jnp.zeros((bp, S5_NG, S5_P), F32), s5_im=jnp.zeros((bp, S5_NG, S5_P), F32),
                          shift=jnp.zeros((bp, RW_PROJ), F32), hgrn=mat_zero, ret=mat_zero, rwkv=mat_zero, layer=0)
        yp, small, p_mats = _trunk_layer(yp, bp, lp, 0, attend_p, zero_state, p_mats, p, wb, l, lb_all[l], final)
        p_small.append(small)
        sst = dict(s5_re=state_s5_re[l], s5_im=state_s5_im[l], shift=state_rwkv_shift[l],
                   hgrn=state_hgrn, ret=state_ret, rwkv=state_rwkv, layer=l)
        def attend_s(q, w_o, x, l=l):
            o = cross_attention_cache(q, cache_mem_k, cache_mem_v, l, nb=CACHE_ATTN_SEQS, rows=ls)
            return matmul_residual(o, w_o, x, tm=TOKEN_TILE)

        ys, small, s_mats = _trunk_layer(ys, bs, ls, PAST_LEN, attend_s, sst, s_mats, p, wb, l, lb_all[l], final)
        s_small.append(small)
    stack = lambda states, i: jnp.stack([s[i] for s in states])
    return (yp.reshape(bp, lp, D_MODEL), ys.reshape(bs, ls, D_MODEL),
            stack(p_small, 0), stack(p_small, 1), p_mats['hgrn'], p_mats['ret'], p_mats['rwkv'], stack(p_small, 2),
            mk.reshape(DEPTH, bp, MEM_LEN, MEM_HEADS, MEM_HD), mv.reshape(DEPTH, bp, MEM_LEN, MEM_HEADS, MEM_HD),
            stack(s_small, 0), stack(s_small, 1), s_mats['hgrn'], s_mats['ret'], s_mats['rwkv'], stack(s_small, 2))
```

```python
import functools
import math

import jax
import jax.numpy as jnp
from jax import lax
from jax.experimental import pallas as pl
from jax.experimental.pallas import tpu as pltpu

F32 = jnp.float32
BF16 = jnp.bfloat16

D_MODEL = 1024
DEPTH = 2
PAST_LEN = 16384
GW = 256
HD = 64
NH = GW // HD
S5_GROUP = 16
S5_NG = GW // S5_GROUP
S5_P = 64
S5_W = S5_NG * S5_P
RW_PROJ = 4 * GW
IN_WIDTH = 13 * GW
MEM_LEN = 256
MEM_HEADS = 4
MEM_HD = D_MODEL // MEM_HEADS
D_FF = 4 * D_MODEL
EPS = 1e-6
RW_LN_EPS = 64e-5
ROPE_BASE = 10000.0
CHUNK = 16
RET_CHUNK = 64

VMEM_LIMIT = 56 * 1024 * 1024
SUBLANES = 8
LANES = 128
LOG2_E = 1.4426950408889634


def _cparams(sem):
    return pltpu.CompilerParams(dimension_semantics=sem, vmem_limit_bytes=VMEM_LIMIT)


def _mm(a, b):
    return jnp.dot(a.astype(BF16), b.astype(BF16), preferred_element_type=F32)


def _mm_nt(a, b):
    return lax.dot_general(a.astype(BF16), b.astype(BF16), (((1,), (1,)), ((), ())),
                           preferred_element_type=F32)


def _mm_tn(a, b):
    return lax.dot_general(a.astype(BF16), b.astype(BF16), (((0,), (0,)), ((), ())),
                           preferred_element_type=F32)


def _split3(x):
    hi = x.astype(BF16)
    r1 = x - hi.astype(F32)
    mid = r1.astype(BF16)
    lo = (r1 - mid.astype(F32)).astype(BF16)
    return hi, mid, lo


def _mm_exact_lhs(sel, x):
    s = sel.astype(BF16)
    hi, mid, lo = _split3(x)
    return (jnp.dot(s, hi, preferred_element_type=F32) + jnp.dot(s, mid, preferred_element_type=F32)
            + jnp.dot(s, lo, preferred_element_type=F32))


def _rms(x, gain):
    return x * lax.rsqrt(jnp.mean(x * x, axis=-1, keepdims=True) + EPS) * gain


def _sigmoid(x):
    return 1.0 / (1.0 + jnp.exp(-x))


def _iota2(shape, axis):
    return lax.broadcasted_iota(jnp.int32, shape, axis)


def _head_ones():
    return (_iota2((GW, GW), 0) // HD == _iota2((GW, GW), 1) // HD).astype(F32)


def _head_sum(x, ones_bd):
    s = ones_bd.astype(BF16)
    hi = x.astype(BF16)
    lo = (x - hi.astype(F32)).astype(BF16)
    return jnp.dot(hi, s, preferred_element_type=F32) + jnp.dot(lo, s, preferred_element_type=F32)


ROW_PARTS = 2


def _row_parts(n_rows, n_parts=ROW_PARTS):
    if n_rows % (n_parts * 2 * SUBLANES):
        return [slice(0, n_rows)]
    step = n_rows // n_parts
    return [slice(i * step, (i + 1) * step) for i in range(n_parts)]


def _norm_mm_kernel(x_ref, g_ref, w_ref, o_ref):
    parts = _row_parts(x_ref.shape[0])
    xn = [_rms(x_ref[r, :], g_ref[...]) for r in parts]
    for r, xp in zip(parts, xn):
        o_ref[r, :] = _mm(xp, w_ref[...])


def norm_matmul(x, gain, w, *, tm):
    t, d = x.shape
    n = w.shape[1]
    return pl.pallas_call(
        _norm_mm_kernel,
        out_shape=jax.ShapeDtypeStruct((t, n), F32),
        grid=(t // tm,),
        in_specs=[pl.BlockSpec((tm, d), lambda i: (i, 0)),
                  pl.BlockSpec((1, d), lambda i: (0, 0)),
                  pl.BlockSpec((d, n), lambda i: (0, 0))],
        out_specs=pl.BlockSpec((tm, n), lambda i: (i, 0)),
        compiler_params=_cparams(("parallel",)),
        name="norm_matmul",
    )(x, gain, w)


def _mm_kernel(a_ref, w_ref, o_ref):
    o_ref[...] = _mm(a_ref[...], w_ref[...])


def matmul_layers(a, w, *, tm):
    t, k = a.shape
    layers, _, n = w.shape
    return pl.pallas_call(
        _mm_kernel,
        out_shape=jax.ShapeDtypeStruct((layers, t, n), F32),
        grid=(layers, t // tm),
        in_specs=[pl.BlockSpec((tm, k), lambda l, i: (i, 0)),
                  pl.BlockSpec((None, k, n), lambda l, i: (l, 0, 0))],
        out_specs=pl.BlockSpec((None, tm, n), lambda l, i: (l, i, 0)),
        compiler_params=_cparams(("parallel", "parallel")),
        name="matmul_layers",
    )(a, w)


def _mm_res_kernel(a_ref, w_ref, r_ref, o_ref):
    o_ref[...] = r_ref[...] + _mm(a_ref[...], w_ref[...])


def matmul_residual(a, w, res, *, tm):
    t, k = a.shape
    n = w.shape[1]
    return pl.pallas_call(
        _mm_res_kernel,
        out_shape=jax.ShapeDtypeStruct((t, n), F32),
        grid=(t // tm,),
        in_specs=[pl.BlockSpec((tm, k), lambda i: (i, 0)),
                  pl.BlockSpec((k, n), lambda i: (0, 0)),
                  pl.BlockSpec((tm, n), lambda i: (i, 0))],
        out_specs=pl.BlockSpec((tm, n), lambda i: (i, 0)),
        compiler_params=_cparams(("parallel",)),
        name="matmul_residual",
    )(a, w, res)


def _mix_out_q_kernel(a0_ref, a1_ref, a2_ref, a3_ref, w_ref, r_ref, g_ref, wq_ref, x_ref, q_ref):
    parts = _row_parts(r_ref.shape[0])
    a = [jnp.concatenate([a0_ref[r, :], a1_ref[r, :], a2_ref[r, :], a3_ref[r, :]], axis=-1) for r in parts]
    x = [r_ref[r, :] + _mm(ap, w_ref[...]) for r, ap in zip(parts, a)]
    xn = [_rms(xp, g_ref[...]) for xp in x]
    for r, xp, xnp in zip(parts, x, xn):
        x_ref[r, :] = xp
        q_ref[r, :] = _mm(xnp, wq_ref[...]).astype(BF16)


def mix_out_q(parts, w, res, gain, w_q, *, tm):
    t, d = res.shape
    part = pl.BlockSpec((tm, GW), lambda i: (i, 0))
    mat = pl.BlockSpec((d, d), lambda i: (0, 0))
    tile = pl.BlockSpec((tm, d), lambda i: (i, 0))
    return pl.pallas_call(
        _mix_out_q_kernel,
        out_shape=(jax.ShapeDtypeStruct((t, d), F32), jax.ShapeDtypeStruct((t, d), BF16)),
        grid=(t // tm,),
        in_specs=[part, part, part, part, mat, tile, pl.BlockSpec((1, d), lambda i: (0, 0)), mat],
        out_specs=(tile, tile),
        compiler_params=_cparams(("parallel",)),
        name="mix_out_q",
    )(*parts, w, res, gain, w_q)


def _ffn_kernel(x_ref, g_ref, wu_ref, wd_ref, gf_ref, o_ref, *, final_norm):
    x = x_ref[...]
    h = jnp.dot(_rms(x, g_ref[...]).astype(BF16), wu_ref[...], preferred_element_type=F32)
    h = jnp.square(jnp.maximum(h, 0.0))
    y = x + jnp.dot(h.astype(BF16), wd_ref[...], preferred_element_type=F32)
    if final_norm:
        y = _rms(y, gf_ref[...])
    o_ref[...] = y


def ffn(x, gain, w_up, w_down, gain_final, *, tm, final_norm):
    t, d = x.shape
    ff = w_up.shape[1]
    resident = lambda shape: pl.BlockSpec(shape, lambda i: (0, 0), pipeline_mode=pl.Buffered(1))
    return pl.pallas_call(
        functools.partial(_ffn_kernel, final_norm=final_norm),
        out_shape=jax.ShapeDtypeStruct((t, d), F32),
        grid=(t // tm,),
        in_specs=[pl.BlockSpec((tm, d), lambda i: (i, 0)),
                  pl.BlockSpec((1, d), lambda i: (0, 0)),
                  resident((d, ff)), resident((ff, d)),
                  pl.BlockSpec((1, d), lambda i: (0, 0))],
        out_specs=pl.BlockSpec((tm, d), lambda i: (i, 0)),
        compiler_params=_cparams(("parallel",)),
        name="ffn",
    )(x, gain, w_up, w_down, gain_final)


def _attn_out_kernel(q_ref, k_ref, v_ref, wo_ref, r_ref, x_ref):
    sls = [slice(h * MEM_HD, (h + 1) * MEM_HD) for h in range(MEM_HEADS)]
    s = [_mm_nt(q_ref[:, sl], k_ref[:, sl]) * (MEM_HD ** -0.5) for sl in sls]
    p = [jnp.exp(sh - jnp.max(sh, axis=-1, keepdims=True)) for sh in s]
    p = [ph * (1.0 / jnp.sum(ph, axis=-1, keepdims=True)) for ph in p]
    heads = [_mm(ph, v_ref[:, sl]) for ph, sl in zip(p, sls)]
    x_ref[...] = r_ref[...] + _mm(jnp.concatenate(heads, axis=-1), wo_ref[...])


def cross_attention_out(q, mem_k, mem_v, layer, w_o, res, *, rows):
    t, d = res.shape
    bsz = mem_k.shape[1] // MEM_LEN
    lt = t // bsz // rows
    tile = pl.BlockSpec((rows, d), lambda b, l: (b * lt + l, 0))
    mem = pl.BlockSpec((None, MEM_LEN, d), lambda b, l: (layer, b, 0))
    return pl.pallas_call(
        _attn_out_kernel,
        out_shape=jax.ShapeDtypeStruct((t, d), F32),
        grid=(bsz, lt),
        in_specs=[tile, mem, mem, pl.BlockSpec((d, d), lambda b, l: (0, 0)), tile],
        out_specs=tile,
        compiler_params=_cparams(("parallel", "arbitrary")),
        name="cross_attention_out",
    )(q, mem_k, mem_v, w_o, res)


def _attn_cache_kernel(q_ref, k_ref, v_ref, o_ref, *, nb, rows):
    nr = MEM_HEADS * rows
    same_head = _iota2((nr, MEM_HEADS * MEM_LEN), 1) % MEM_HEADS == _iota2((nr, MEM_HEADS * MEM_LEN), 0) // rows
    q_all = q_ref[...].astype(F32)
    rng = range(nb)
    qs = [jnp.concatenate([q_all[b * rows:(b + 1) * rows, h * MEM_HD:(h + 1) * MEM_HD] for h in range(MEM_HEADS)],
                          axis=0) for b in rng]
    s = [_mm_nt(qs[b], k_ref[0, b].reshape(MEM_HEADS * MEM_LEN, MEM_HD)) * (MEM_HD ** -0.5) for b in rng]
    s = [jnp.where(same_head, sb, -1e30) for sb in s]
    p = [jnp.exp(sb - jnp.max(sb, axis=-1, keepdims=True)) for sb in s]
    p = [pb * (1.0 / jnp.sum(pb, axis=-1, keepdims=True)) for pb in p]
    o = [_mm(p[b], v_ref[0, b].reshape(MEM_HEADS * MEM_LEN, MEM_HD)) for b in rng]
    for b in rng:
        for h in range(MEM_HEADS):
            o_ref[b * rows:(b + 1) * rows, h * MEM_HD:(h + 1) * MEM_HD] = o[b][h * rows:(h + 1) * rows]


def cross_attention_cache(q, cache_k, cache_v, layer, *, nb, rows):
    t = q.shape[0]
    bsz = cache_k.shape[1]
    mem = pl.BlockSpec((1, nb, MEM_LEN, MEM_HEADS, MEM_HD), lambda b: (layer, b, 0, 0, 0))
    return pl.pallas_call(
        functools.partial(_attn_cache_kernel, nb=nb, rows=rows),
        out_shape=jax.ShapeDtypeStruct((t, D_MODEL), F32),
        grid=(bsz // nb,),
        in_specs=[pl.BlockSpec((nb * rows, D_MODEL), lambda b: (b, 0)), mem, mem],
        out_specs=pl.BlockSpec((nb * rows, D_MODEL), lambda b: (b, 0)),
        compiler_params=_cparams(("parallel",)),
        name="cross_attention_cache",
    )(q, cache_k, cache_v)


def _gelu_tanh(x):
    return 0.5 * x * (1.0 + jnp.tanh(math.sqrt(2.0 / math.pi) * (x + 0.044715 * (x * x * x))))


def _s5_kernel(u_ref, h0_ref, lam_ref, bblk_ref, cblk_ref, d_ref, wglu_ref, bglu_ref, gain_ref,
               y_ref, hfin_ref, scr_ref, tm_ref, *, steps, nb):
    @pl.when(pl.program_id(1) == 0)
    def _():
        scr_ref[0:nb, :] = h0_ref[...]

    def to_time_major(b, carry):
        for j in range(GW // LANES):
            tm_ref[j, pl.ds(b, steps, stride=nb), :] = u_ref[b, :, j * LANES:(j + 1) * LANES]
        return carry

    lax.fori_loop(0, nb, to_time_major, 0)
    u = jnp.concatenate([tm_ref[j] for j in range(GW // LANES)], axis=1)
    rows = _row_parts(steps * nb, 4)
    for r in rows:
        scr_ref[nb + r.start:nb + r.stop, :] = _mm(u[r], bblk_ref[...])
    lam_re = jnp.broadcast_to(lam_ref[0:1, :], (nb, S5_W))
    lam_im = jnp.broadcast_to(lam_ref[1:2, :], (nb, S5_W))

    h_re, h_im = scr_ref[0:nb, 0:S5_W], scr_ref[0:nb, S5_W:]
    for t in range(steps):
        c0 = (t + 1) * nb
        h_re, h_im = (scr_ref[c0:c0 + nb, 0:S5_W] + lam_re * h_re - lam_im * h_im,
                      scr_ref[c0:c0 + nb, S5_W:] + lam_re * h_im + lam_im * h_re)
        scr_ref[c0:c0 + nb, 0:S5_W] = h_re
        scr_ref[c0:c0 + nb, S5_W:] = h_im
    h_last = scr_ref[steps * nb:, :]
    hfin_ref[...] = h_last
    y = [_mm(scr_ref[nb + r.start:nb + r.stop, :], cblk_ref[...]) + d_ref[...] * u[r] for r in rows]
    y = [_gelu_tanh(yp) for yp in y]
    z = [_mm(yp, wglu_ref[...]) for yp in y]
    y = [_rms(yp * _sigmoid(zp + bglu_ref[...]), gain_ref[...]) for yp, zp in zip(y, z)]
    for r, yp in zip(rows, y):
        for j in range(GW // LANES):
            tm_ref[j, r, :] = yp[:, j * LANES:(j + 1) * LANES]

    def to_batch_major(b, carry):
        for j in range(GW // LANES):
            y_ref[b, :, j * LANES:(j + 1) * LANES] = tm_ref[j, pl.ds(b, steps, stride=nb), :]
        return carry

    lax.fori_loop(0, nb, to_batch_major, 0)
    scr_ref[0:nb, :] = h_last


def s5_mixer(proj, h0, lam, bblk, cblk, dvec, wglu, bglu, gain, *, steps, nb):
    bsz, seq, _ = proj.shape
    const = lambda shape: pl.BlockSpec(shape, lambda b, t: (0, 0))
    return pl.pallas_call(
        functools.partial(_s5_kernel, steps=steps, nb=nb),
        out_shape=(jax.ShapeDtypeStruct((bsz, seq, GW), F32),
                   jax.ShapeDtypeStruct((bsz, 2 * S5_W), F32)),
        grid=(bsz // nb, seq // steps),
        in_specs=[pl.BlockSpec((nb, steps, GW), lambda b, t: (b, t, 0)),
                  pl.BlockSpec((nb, 2 * S5_W), lambda b, t: (b, 0)),
                  const((2, S5_W)), const((GW, 2 * S5_W)), const((2 * S5_W, GW)), const((1, GW)),
                  const((GW, GW)), const((1, GW)), const((1, GW))],
        out_specs=(pl.BlockSpec((nb, steps, GW), lambda b, t: (b, t, 0)),
                   pl.BlockSpec((nb, 2 * S5_W), lambda b, t: (b, 0))),
        scratch_shapes=[pltpu.VMEM(((steps + 1) * nb, 2 * S5_W), F32),
                        pltpu.VMEM((GW // LANES, steps * nb, LANES), F32)],
        compiler_params=_cparams(("parallel", "arbitrary")),
        name="s5_mixer",
    )(proj, h0, lam, bblk, cblk, dvec, wglu, bglu, gain)


def s5_params(lam_re, lam_im, b_re, b_im, c_re, c_im, log_step):
    step = jnp.exp(log_step)[:, None]
    mag = jnp.exp(lam_re * step)
    lbar_re = mag * jnp.cos(lam_im * step)
    lbar_im = mag * jnp.sin(lam_im * step)
    den = lam_re * lam_re + lam_im * lam_im
    f_re = ((lbar_re - 1.0) * lam_re + lbar_im * lam_im) / den
    f_im = (lbar_im * lam_re - (lbar_re - 1.0) * lam_im) / den
    bbar_re = f_re[..., None] * b_re - f_im[..., None] * b_im
    bbar_im = f_re[..., None] * b_im + f_im[..., None] * b_re
    eye_g = jnp.eye(S5_NG, dtype=F32)

    def in_blk(m):
        return jnp.einsum('gph,gk->ghkp', m, eye_g).reshape(GW, S5_W)

    def out_blk(m):
        return jnp.einsum('ghp,gk->gpkh', m, eye_g).reshape(S5_W, GW)

    bblk = jnp.concatenate([in_blk(bbar_re), in_blk(bbar_im)], axis=1)
    cblk = jnp.concatenate([out_blk(c_re), -out_blk(c_im)], axis=0)
    lam2 = jnp.stack([lbar_re.reshape(S5_W), lbar_im.reshape(S5_W)])
    return lam2, bblk.astype(BF16), cblk.astype(BF16)


def s5_apply(proj, h0_re, h0_im, p, l, *, steps, nb):
    bsz = proj.shape[0]
    lam2, bblk, cblk = s5_params(p['s5_lam_re'][l], p['s5_lam_im'][l], p['s5_b_re'][l], p['s5_b_im'][l],
                                 p['s5_c_re'][l], p['s5_c_im'][l], p['s5_log_step'][l])
    h0 = jnp.concatenate([h0_re.reshape(bsz, S5_W), h0_im.reshape(bsz, S5_W)], axis=1)
    y, h = s5_mixer(proj, h0, lam2, bblk, cblk, p['s5_d'][l].reshape(1, GW),
                    p['s5_w_glu'][l].astype(BF16), p['s5_b_glu'][l].reshape(1, GW),
                    p['s5_norm'][l].reshape(1, GW), steps=steps, nb=nb)
    return y, h[:, :S5_W].reshape(bsz, S5_NG, S5_P), h[:, S5_W:].reshape(bsz, S5_NG, S5_P)


def _load_state(s0_ref, st_ref, nb, transpose):
    for b in range(nb):
        rows = []
        for h in range(NH):
            pieces = [s0_ref[b, h]]
            if h:
                pieces.insert(0, jnp.zeros((HD, h * HD), F32))
            if h < NH - 1:
                pieces.append(jnp.zeros((HD, (NH - 1 - h) * HD), F32))
            rows.append(jnp.concatenate(pieces, axis=1))
        st = jnp.concatenate(rows, axis=0)
        st_ref[b] = st.T if transpose else st


def _store_state(st_ref, sfin_ref, nb, transpose, own_layer):
    if own_layer is not None:
        for other in range(DEPTH):
            if other != own_layer:
                sfin_ref[other] = jnp.zeros(sfin_ref.shape[1:], F32)
        sfin_ref = sfin_ref.at[own_layer]
    for b in range(nb):
        st = st_ref[b].T if transpose else st_ref[b]
        for h in range(NH):
            sfin_ref[b, h] = st[h * HD:(h + 1) * HD, h * HD:(h + 1) * HD]


def _head_masks():
    lane_head = _iota2((1, GW), 1) // HD
    return [(lane_head == h).astype(F32) for h in range(NH)]


def _stack_heads(x, masks):
    return jnp.concatenate([x * m for m in masks], axis=0)


def _pad_rows(x, rows):
    if x.shape[0] == rows:
        return x
    return jnp.concatenate([x, jnp.zeros((rows - x.shape[0], x.shape[1]), x.dtype)], axis=0)


def _silu(x):
    return x * _sigmoid(x)


def _chunk_rows(ref, b, r0, rows, c):
    return _pad_rows(ref[b, pl.ds(r0, rows), :], c)


CHUNK_UNROLL = 4


def _chunk_loop(n_chunks, rows, body):
    unroll = math.gcd(n_chunks, CHUNK_UNROLL)

    def step(i, carry):
        for j in range(unroll):
            body(pl.multiple_of((i * unroll + j) * rows, rows))
        return carry

    lax.fori_loop(0, n_chunks // unroll, step, 0)


def _ret_kernel(q_ref, k_ref, v_ref, g_ref, cos_ref, sin_ref, s0_ref, _states_in, o_ref, sfin_ref, st_ref,
                *, nb, tb, c, c_real, own_layer):
    rows = min(tb, c)

    @pl.when(pl.program_id(1) == 0)
    def _():
        _load_state(s0_ref, st_ref, nb, transpose=False)

    masks = _head_masks()
    ones_bd = _head_ones()
    lane_head = _iota2((1, GW), 1) // HD
    log_gamma = jnp.zeros((1, GW), F32)
    for h in range(NH):
        log_gamma = jnp.where(lane_head == h, math.log(1.0 - 2.0 ** (-5.0 - h)), log_gamma)
    tt = _iota2((c, GW), 0).astype(F32)
    scale = HD ** -0.5
    g_q = jnp.exp(log_gamma * (tt + 1.0))
    g_k = jnp.exp(-log_gamma * (tt + 1.0)) * scale
    g_tail = jnp.exp(log_gamma * (c_real - 1.0 - tt)) * scale
    row_head = _iota2((GW, GW), 0) // HD
    g_chunk = jnp.zeros((GW, GW), F32)
    for h in range(NH):
        g_chunk = jnp.where(row_head == h, math.exp(math.log(1.0 - 2.0 ** (-5.0 - h)) * c_real), g_chunk)
    causal = _iota2((c, NH * c), 0) >= _iota2((c, NH * c), 1) % c
    first_half = _iota2((c, GW), 1) % HD < HD // 2

    def rope(x, cos, sin):
        swapped = jnp.where(first_half, pltpu.roll(x, GW - HD // 2, axis=1), pltpu.roll(x, HD // 2, axis=1))
        return x * cos + swapped * sin

    def chunk(r0):
        cos = _pad_rows(cos_ref[pl.ds(r0, rows), :], c)
        sin = _pad_rows(sin_ref[pl.ds(r0, rows), :], c)
        rng = range(nb)
        qt = [rope(_chunk_rows(q_ref, b, r0, rows, c), cos, sin) * g_q for b in rng]
        k = [rope(_chunk_rows(k_ref, b, r0, rows, c), cos, sin) for b in rng]
        v = [_chunk_rows(v_ref, b, r0, rows, c) for b in rng]
        st = [st_ref[b] for b in rng]
        sc = [_mm_nt(qt[b], _stack_heads(k[b] * g_k, masks)) for b in rng]
        o_in = [_mm(jnp.where(causal, sc[b], 0.0), _stack_heads(v[b], masks)) for b in rng]
        o_st = [_mm(qt[b], st[b]) for b in rng]
        d_st = [_mm_tn(k[b] * g_tail, v[b]) for b in rng]
        o = [o_in[b] + o_st[b] for b in rng]
        ms = [_head_sum(o[b] * o[b], ones_bd) for b in rng]
        for b in rng:
            st_ref[b] = st[b] * g_chunk + ones_bd * d_st[b]
            ob = o[b] * lax.rsqrt(ms[b] * (1.0 / HD) + EPS) * _silu(_chunk_rows(g_ref, b, r0, rows, c))
            o_ref[b, pl.ds(r0, rows), :] = ob[:rows]

    _chunk_loop(tb // rows, rows, chunk)

    @pl.when(pl.program_id(1) == pl.num_programs(1) - 1)
    def _():
        _store_state(st_ref, sfin_ref, nb, transpose=False, own_layer=own_layer)


def _rope_tables(pos0, seq):
    half = HD // 2
    inv = ROPE_BASE ** (-jnp.arange(half, dtype=F32) / half)
    pos = pos0 + jnp.arange(seq, dtype=jnp.int32)
    ang = pos.astype(F32)[:, None] * inv[None, :]
    cos, sin = jnp.cos(ang), jnp.sin(ang)
    return (jnp.tile(jnp.concatenate([cos, cos], axis=-1), (1, NH)),
            jnp.tile(jnp.concatenate([-sin, sin], axis=-1), (1, NH)))


def _proj_spec(nb, tb, col):
    return pl.BlockSpec((nb, tb, GW), lambda b, t: (b, t, col))


def _state_spec(nb, layer):
    return pl.BlockSpec((None, nb, NH, HD, HD), lambda b, t: (layer, b, 0, 0, 0))


_ALIASED = pl.BlockSpec(memory_space=pl.ANY)


def _collector(states, bsz, nb, layer, operand_index):
    shape = jax.ShapeDtypeStruct((DEPTH, bsz, NH, HD, HD), F32)
    if states is None:
        spec = pl.BlockSpec((DEPTH, nb, NH, HD, HD), lambda b, t: (0, b, 0, 0, 0))
        return jnp.zeros((1, 1, NH, HD, HD), F32), spec, shape, {}, layer
    return states, _state_spec(nb, layer), shape, {operand_index: 1}, None


def retention_apply(proj, s0, s0_layer, states, layer, pos0, *, tb, nb):
    bsz, seq, _ = proj.shape
    c = RET_CHUNK if tb >= RET_CHUNK else CHUNK
    cos, sin = _rope_tables(pos0, seq)
    states, st_spec, st_shape, aliases, own_layer = _collector(states, bsz, nb, layer, 7)
    return pl.pallas_call(
        functools.partial(_ret_kernel, nb=nb, tb=tb, c=c, c_real=min(tb, c), own_layer=own_layer),
        out_shape=(jax.ShapeDtypeStruct((bsz, seq, GW), F32), st_shape),
        grid=(bsz // nb, seq // tb),
        in_specs=[_proj_spec(nb, tb, 5), _proj_spec(nb, tb, 6), _proj_spec(nb, tb, 7), _proj_spec(nb, tb, 8),
                  pl.BlockSpec((tb, GW), lambda b, t: (t, 0)), pl.BlockSpec((tb, GW), lambda b, t: (t, 0)),
                  _state_spec(nb, s0_layer), _ALIASED],
        out_specs=(pl.BlockSpec((nb, tb, GW), lambda b, t: (b, t, 0)), st_spec),
        scratch_shapes=[pltpu.VMEM((nb, GW, GW), F32)],
        input_output_aliases=aliases,
        compiler_params=_cparams(("parallel", "arbitrary")),
        name="retention",
    )(proj, proj, proj, proj, cos, sin, s0, states)


def _log_sigmoid(z):
    return jnp.minimum(z, 0.0) - jnp.log(1.0 + jnp.exp(-jnp.abs(z)))


def _hgrn_kernel(q_ref, f_ref, i_ref, g_ref, lb_ref, gain_ref, s0_ref, _states_in, o_ref, sfin_ref, st_ref,
                 *, nb, tb, c, own_layer):
    rows = min(tb, c)

    @pl.when(pl.program_id(1) == 0)
    def _():
        _load_state(s0_ref, st_ref, nb, transpose=True)

    t_hi = -(-rows // SUBLANES) * SUBLANES
    ones_bd = _head_ones()
    tri = (_iota2((c, c), 0) >= _iota2((c, c), 1)).astype(F32)
    t_idx = _iota2((c, GW), 0)
    lb = lb_ref[...]
    log_lb = jnp.log(lb)
    log_1m_lb = jnp.log(1.0 - lb)
    gain = gain_ref[...]

    def chunk(r0):
        rng = range(nb)
        valid = t_idx < rows

        def gates(b):
            z = _chunk_rows(f_ref, b, r0, rows, c)
            ls_pos = _log_sigmoid(z)
            ls_neg = ls_pos - z
            b2 = log_lb + ls_neg
            log_f = jnp.maximum(ls_pos, b2) + jnp.log(1.0 + jnp.exp(-jnp.abs(ls_pos - b2)))
            return (jnp.where(valid, log_f, 0.0), jnp.where(valid, (1.0 - lb) * jnp.exp(ls_neg), 0.0),
                    ls_neg + log_1m_lb)

        log_f, key, log_key = zip(*[gates(b) for b in rng])
        q = [_silu(_chunk_rows(q_ref, b, r0, rows, c)) for b in rng]
        v = [_chunk_rows(i_ref, b, r0, rows, c) for b in rng]
        st = [st_ref[b] for b in rng]
        cum = [_mm_exact_lhs(tri, log_f[b]) for b in rng]
        last = [cum[b][c - 1:c, :] for b in rng]
        cum2 = [cum[b] * LOG2_E for b in rng]
        kd2 = [(log_key[b] - cum[b]) * LOG2_E for b in rng]

        def pair_rows(b, s):
            lo = SUBLANES * (s // SUBLANES)
            t_rows = _iota2((t_hi - lo, GW), 0) + lo
            key_decay = jnp.where(t_rows >= s, jnp.exp2(cum2[b][lo:t_hi] + kd2[b][s:s + 1, :]), 0.0)
            return key_decay * q[b][lo:t_hi]

        sc = [_mm(jnp.concatenate([pair_rows(b, s) for s in range(rows)], axis=0), ones_bd) for b in rng]
        o_st = [_mm_nt(q[b] * jnp.exp(cum[b]), st[b]) for b in rng]
        d_st = [_mm_tn(v[b], key[b] * jnp.exp(last[b] - cum[b])) for b in rng]

        def intra(b):
            tiles = [None] * (c // SUBLANES)
            off = 0
            for s in range(rows):
                for ti in range(s // SUBLANES, t_hi // SUBLANES):
                    term = sc[b][off:off + SUBLANES, :] * v[b][s:s + 1, :]
                    tiles[ti] = term if tiles[ti] is None else tiles[ti] + term
                    off += SUBLANES
            zero = jnp.zeros((SUBLANES, GW), F32)
            return jnp.concatenate([zero if t is None else t for t in tiles], axis=0)

        o = [o_st[b] + intra(b) for b in rng]
        ms = [_head_sum(o[b] * o[b], ones_bd) for b in rng]
        for b in rng:
            st_ref[b] = st[b] * jnp.exp(last[b]) + ones_bd * d_st[b]
            ob = o[b] * lax.rsqrt(ms[b] * (1.0 / HD) + EPS) * gain * _silu(_chunk_rows(g_ref, b, r0, rows, c))
            o_ref[b, pl.ds(r0, rows), :] = ob[:rows]

    _chunk_loop(tb // rows, rows, chunk)

    @pl.when(pl.program_id(1) == pl.num_programs(1) - 1)
    def _():
        _store_state(st_ref, sfin_ref, nb, transpose=True, own_layer=own_layer)


def hgrn_apply(proj, s0, s0_layer, states, layer, lb, gain, *, tb, nb):
    bsz, seq, _ = proj.shape
    row = pl.BlockSpec((1, GW), lambda b, t: (0, 0))
    states, st_spec, st_shape, aliases, own_layer = _collector(states, bsz, nb, layer, 7)
    return pl.pallas_call(
        functools.partial(_hgrn_kernel, nb=nb, tb=tb, c=CHUNK, own_layer=own_layer),
        out_shape=(jax.ShapeDtypeStruct((bsz, seq, GW), F32), st_shape),
        grid=(bsz // nb, seq // tb),
        in_specs=[_proj_spec(nb, tb, 1), _proj_spec(nb, tb, 2), _proj_spec(nb, tb, 3), _proj_spec(nb, tb, 4),
                  row, row, _state_spec(nb, s0_layer), _ALIASED],
        out_specs=(pl.BlockSpec((nb, tb, GW), lambda b, t: (b, t, 0)), st_spec),
        scratch_shapes=[pltpu.VMEM((nb, GW, GW), F32)],
        input_output_aliases=aliases,
        compiler_params=_cparams(("parallel", "arbitrary")),
        name="hgrn2",
    )(proj, proj, proj, proj, lb.reshape(1, GW), gain.reshape(1, GW), s0, states)


def _softplus(z):
    return jnp.maximum(z, 0.0) + jnp.log(1.0 + jnp.exp(-jnp.abs(z)))


def _rwkv_prepare(ins, consts):
    masks, ones_bd, bd_mask, eye_flat, tri, strict, incl = consts
    rng = range(len(ins))
    r, lw, k, v, kk, a = [[i[j] for i in ins] for j in range(6)]
    c = r[0].shape[0]
    n = NH * c

    def to_bd(flat):
        return jnp.concatenate([flat] * NH, axis=0) * bd_mask

    cum = [_mm_exact_lhs(tri, lw[i]) for i in rng]
    last = [cum[i][c - 1:c, :] for i in rng]
    p_inv = [jnp.exp(-cum[i]) for i in rng]
    p_tail = [jnp.exp(last[i] - cum[i]) for i in rng]
    ka = [kk[i] * a[i] for i in rng]
    x = [jnp.concatenate([kk[i] * jnp.exp(cum[i] - lw[i]), r[i] * jnp.exp(cum[i])], axis=0) for i in rng]
    g = [_mm_nt(x[i], jnp.concatenate([_stack_heads(ka[i] * p_inv[i], masks),
                                       _stack_heads(k[i] * p_inv[i], masks)], axis=0)) for i in rng]
    m_ak = [jnp.where(strict, g[i][:c], 0.0) for i in rng]
    n_ak = [jnp.where(incl, g[i][c:], 0.0) for i in rng]
    pw = [m_ak[i][:, :n] for i in rng]
    t_inv = [eye_flat - pw[i] for i in rng]
    pw_bd = [to_bd(pw[i]) for i in rng]
    for _ in range(int(math.log2(c)) - 1):
        pw = [_mm(pw[i], pw_bd[i]) for i in rng]
        pw_bd = [to_bd(pw[i]) for i in rng]
        t_inv = [t_inv[i] + _mm(t_inv[i], pw_bd[i]) for i in rng]
    v_stk = [_stack_heads(v[i], masks) for i in rng]
    zeros = jnp.zeros((n, GW), F32)
    mkv = [_mm(m_ak[i], jnp.concatenate([zeros, v_stk[i]], axis=0)) for i in rng]
    tmkv = [_mm(t_inv[i], _stack_heads(mkv[i], masks)) for i in rng]
    return [dict(x=x[i], t_inv=t_inv[i], tmkv=tmkv[i], n_ak=n_ak[i], v=v[i], v_stk=v_stk[i],
                 decay=jnp.exp(last[i]),
                 k_tail=jnp.concatenate([ka[i] * p_tail[i], k[i] * p_tail[i]], axis=0)) for i in rng]


def _rwkv_advance(prep, st, consts):
    masks, ones_bd = consts[0], consts[1]
    rng = range(len(prep))
    c = prep[0]['t_inv'].shape[0]
    xa = [_mm_nt(prep[i]['x'], st[i]) for i in rng]
    u = [-_mm(prep[i]['t_inv'], _stack_heads(xa[i][:c], masks)) - prep[i]['tmkv'] for i in rng]
    nuv = [_mm(prep[i]['n_ak'], jnp.concatenate([_stack_heads(u[i], masks), prep[i]['v_stk']], axis=0))
           for i in rng]
    d_uv = [_mm_tn(jnp.concatenate([u[i], prep[i]['v']], axis=0), prep[i]['k_tail']) for i in rng]
    return [(xa[i][c:] + nuv[i], st[i] * prep[i]['decay'] + ones_bd * d_uv[i]) for i in rng]


def _rwkv_kernel(x_r_ref, x_k_ref, x_v_ref, x_l_ref, sh0_ref, s0_ref, mu_ref, w0_ref, ww_ref, a0_ref,
                 wa_ref, wg_ref, kk_ref, ka_ref, rk_ref, lng_ref, lnb_ref, _states_in,
                 o_ref, sfin_ref, shfin_ref,
                 st_ref, sh_ref, r_s, lw_s, k_s, v_s, kkn_s, a_s, y_s, bonus_s, gate_s, *, nb, tb, c, own_layer):
    tbp = max(tb, c)

    @pl.when(pl.program_id(1) == 0)
    def _():
        _load_state(s0_ref, st_ref, nb, transpose=False)
        sh_ref[...] = sh0_ref[...]

    masks = _head_masks()
    ones_bd = _head_ones()
    n = NH * c
    row_t = _iota2((c, 2 * n), 0)
    col_t = _iota2((c, 2 * n), 1) % c
    bd_mask = (_iota2((n, n), 0) // c == _iota2((n, n), 1) // c).astype(F32)
    eye_flat = (_iota2((c, n), 1) % c == _iota2((c, n), 0)).astype(F32)
    consts = (masks, ones_bd, bd_mask, eye_flat, (_iota2((c, c), 0) >= _iota2((c, c), 1)).astype(F32),
              col_t < row_t, col_t <= row_t)
    first_row = _iota2((nb * tb, GW), 0) % tb == 0

    def mixed(x_ref, j):
        x = x_ref[...].reshape(nb * tb, GW)
        carried = jnp.broadcast_to(sh_ref[:, j:j + 1, :], (nb, tb, GW)).reshape(nb * tb, GW)
        prev = jnp.where(first_row, carried, pltpu.roll(x, 1, axis=0))
        sh_ref[:, j:j + 1, :] = x_ref[:, tb - 1:tb, :]
        return x + (prev - x) * mu_ref[j:j + 1, :]

    r = mixed(x_r_ref, 0)
    k = mixed(x_k_ref, 1)
    v = mixed(x_v_ref, 2)
    xl = mixed(x_l_ref, 3)
    log_w = -_softplus(-(w0_ref[...] + _mm(jnp.tanh(xl), ww_ref[...]))) - 0.5
    a = _sigmoid(a0_ref[...] + _mm(xl, wa_ref[...]))
    gate_s[...] = _mm(_sigmoid(xl), wg_ref[...]).reshape(nb, tb, GW)
    kk = k * kk_ref[...]
    kk = kk * lax.rsqrt(jnp.maximum(_head_sum(kk * kk, ones_bd), 1e-24))
    k = k * (1.0 + (a - 1.0) * ka_ref[...])
    bonus_s[...] = (_head_sum(r * k * rk_ref[...], ones_bd) * v).reshape(nb, tb, GW)
    for s, val in ((r_s, r), (lw_s, -jnp.exp(log_w)), (k_s, k), (v_s, v), (kkn_s, kk), (a_s, a)):
        s[:, 0:tb, :] = val.reshape(nb, tb, GW)
        if tbp > tb:
            s[:, tb:tbp, :] = jnp.zeros((nb, tbp - tb, GW), F32)

    n_chunks = tbp // c
    group = 8 if n_chunks % 8 == 0 else 1

    def chunks(gi, carry):
        r0 = [pl.multiple_of((gi * group + j) * c, c) for j in range(group)]
        prep = _rwkv_prepare([tuple(s[b, pl.ds(r0[j], c), :] for s in (r_s, lw_s, k_s, v_s, kkn_s, a_s))
                              for j in range(group) for b in range(nb)], consts)
        st = [st_ref[b] for b in range(nb)]
        for j in range(group):
            outs = _rwkv_advance(prep[j * nb:(j + 1) * nb], st, consts)
            st = [o[1] for o in outs]
            for b in range(nb):
                y_s[b, pl.ds(r0[j], c), :] = outs[b][0]
        for b in range(nb):
            st_ref[b] = st[b]
        return carry

    lax.fori_loop(0, n_chunks // group, chunks, 0)

    y = y_s[:, 0:tb, :].reshape(nb * tb, GW)
    mean = _head_sum(y, ones_bd) * (1.0 / HD)
    d = y - mean
    var = _head_sum(d * d, ones_bd) * (1.0 / HD)
    y = d * lax.rsqrt(var + RW_LN_EPS) * lng_ref[...] + lnb_ref[...]
    o_ref[...] = ((y.reshape(nb, tb, GW) + bonus_s[...]) * gate_s[...])
    shfin_ref[...] = sh_ref[...]

    @pl.when(pl.program_id(1) == pl.num_programs(1) - 1)
    def _():
        _store_state(st_ref, sfin_ref, nb, transpose=False, own_layer=own_layer)


def rwkv_apply(proj, s0, s0_layer, states, shift0, p, l, *, tb, nb):
    bsz, seq, _ = proj.shape
    states, st_spec, st_shape, aliases, own_layer = _collector(states, bsz, nb, l, 17)
    c = CHUNK
    tbp = max(tb, c)
    row = pl.BlockSpec((1, GW), lambda b, t: (0, 0))
    mat = pl.BlockSpec((GW, GW), lambda b, t: (0, 0))
    sh_spec = pl.BlockSpec((nb, 4, GW), lambda b, t: (b, 0, 0))
    zeros = lambda r: jnp.zeros((r, GW), F32)
    ww = jnp.concatenate([p['rw_w_w2'][l], zeros(192)], axis=0).astype(BF16)
    wa = jnp.concatenate([zeros(64), p['rw_w_a2'][l], zeros(128)], axis=0).astype(BF16)
    wg = jnp.concatenate([zeros(128), p['rw_w_g2'][l]], axis=0).astype(BF16)
    r1 = lambda name: p[name][l].reshape(1, GW)
    seq_buf = pltpu.VMEM((nb, tbp, GW), F32)
    blk_buf = pltpu.VMEM((nb, tb, GW), F32)
    y, st, sh = pl.pallas_call(
        functools.partial(_rwkv_kernel, nb=nb, tb=tb, c=c, own_layer=own_layer),
        out_shape=(jax.ShapeDtypeStruct((bsz, seq, GW), F32), st_shape,
                   jax.ShapeDtypeStruct((bsz, 4, GW), F32)),
        grid=(bsz // nb, seq // tb),
        in_specs=[_proj_spec(nb, tb, 9), _proj_spec(nb, tb, 10), _proj_spec(nb, tb, 11), _proj_spec(nb, tb, 12),
                  sh_spec, _state_spec(nb, s0_layer), pl.BlockSpec((4, GW), lambda b, t: (0, 0)),
                  row, mat, row, mat, mat, row, row, row, row, row, _ALIASED],
        out_specs=(pl.BlockSpec((nb, tb, GW), lambda b, t: (b, t, 0)), st_spec, sh_spec),
        scratch_shapes=[pltpu.VMEM((nb, GW, GW), F32), pltpu.VMEM((nb, 4, GW), F32),
                        seq_buf, seq_buf, seq_buf, seq_buf, seq_buf, seq_buf, seq_buf, blk_buf, blk_buf],
        input_output_aliases=aliases,
        compiler_params=_cparams(("parallel", "arbitrary")),
        name="rwkv7",
    )(proj, proj, proj, proj, shift0.reshape(bsz, 4, GW), s0, p['rw_mu'][l].reshape(4, GW),
      r1('rw_w0'), ww, r1('rw_a0'), wa, wg, r1('rw_k_k'), r1('rw_k_a'), r1('rw_r_k'), r1('rw_ln_g'), r1('rw_ln_b'),
      states)
    return y, st, sh.reshape(bsz, RW_PROJ)


TOKEN_TILE = 512
TIME_BLOCK = 128
CACHE_ATTN_SEQS = 4


def _tiles(bsz, seq):
    if seq >= TIME_BLOCK:
        return dict(tb=TIME_BLOCK, s5_nb=bsz, mix_nb=bsz, rw_nb=bsz)
    return dict(tb=seq, s5_nb=bsz, mix_nb=32, rw_nb=32)


def _trunk_layer(x, bsz, seq, pos0, attend, st, mats, p, wb, l, lb, final_norm):
    cfg = _tiles(bsz, seq)
    tm = TOKEN_TILE
    row = lambda name: p[name][l].reshape(1, -1)
    proj = norm_matmul(x, row('norm_mix'), wb['w_in'][l], tm=tm)
    proj = proj.reshape(bsz, seq, IN_WIDTH)
    y_s5, s5_re, s5_im = s5_apply(proj, st['s5_re'], st['s5_im'], p, l, steps=cfg['tb'], nb=cfg['s5_nb'])
    y_hg, hg_s = hgrn_apply(proj, st['hgrn'], st['layer'], mats['hgrn'], l, lb, p['hg_norm'][l],
                            tb=cfg['tb'], nb=cfg['mix_nb'])
    y_rt, rt_s = retention_apply(proj, st['ret'], st['layer'], mats['ret'], l, pos0,
                                 tb=cfg['tb'], nb=cfg['mix_nb'])
    y_rw, rw_s, shift = rwkv_apply(proj, st['rwkv'], st['layer'], mats['rwkv'], st['shift'], p, l,
                                   tb=cfg['tb'], nb=cfg['rw_nb'])
    parts = [y.reshape(bsz * seq, GW) for y in (y_s5, y_hg, y_rt, y_rw)]
    x, q = mix_out_q(parts, wb['w_out'][l], x, row('norm_mem'), wb['mem_w_q'][l], tm=tm)
    x = attend(q, wb['mem_w_o'][l], x)
    x = ffn(x, row('norm_ffn'), wb['ffn_w_up'][l], wb['ffn_w_down'][l], p['norm_final'].reshape(1, -1),
            tm=tm, final_norm=final_norm)
    return x, (s5_re, s5_im, shift), dict(hgrn=hg_s, ret=rt_s, rwkv=rw_s)


def kernel(x_prompt, x_sample, mem_prompt, state_s5_re, state_s5_im, state_hgrn, state_ret, state_rwkv,
           state_rwkv_shift, cache_mem_k, cache_mem_v, norm_mix, w_in, w_out, s5_lam_re, s5_lam_im,
           s5_b_re, s5_b_im, s5_c_re, s5_c_im, s5_d, s5_log_step, s5_w_glu, s5_b_glu, s5_norm,
           hg_lb_logits, hg_norm, rw_mu, rw_w0, rw_w_w2, rw_a0, rw_w_a2, rw_w_g2, rw_k_k, rw_k_a, rw_r_k,
           rw_ln_g, rw_ln_b, norm_mem, mem_w_q, mem_w_k, mem_w_v, mem_w_o, norm_ffn, ffn_w_up, ffn_w_down,
           norm_final):
    p = dict(norm_mix=norm_mix, s5_lam_re=s5_lam_re, s5_lam_im=s5_lam_im, s5_b_re=s5_b_re, s5_b_im=s5_b_im,
             s5_c_re=s5_c_re, s5_c_im=s5_c_im, s5_d=s5_d, s5_log_step=s5_log_step, s5_w_glu=s5_w_glu,
             s5_b_glu=s5_b_glu, s5_norm=s5_norm, hg_norm=hg_norm, rw_mu=rw_mu, rw_w0=rw_w0, rw_w_w2=rw_w_w2,
             rw_a0=rw_a0, rw_w_a2=rw_w_a2, rw_w_g2=rw_w_g2, rw_k_k=rw_k_k, rw_k_a=rw_k_a, rw_r_k=rw_r_k,
             rw_ln_g=rw_ln_g, rw_ln_b=rw_ln_b, norm_mem=norm_mem, norm_ffn=norm_ffn, norm_final=norm_final)
    wb = {name: w.astype(BF16) for name, w in dict(
        w_in=w_in, w_out=w_out, mem_w_q=mem_w_q, mem_w_k=mem_w_k, mem_w_v=mem_w_v, mem_w_o=mem_w_o,
        ffn_w_up=ffn_w_up, ffn_w_down=ffn_w_down).items()}
    lb_all = jnp.cumsum(jax.nn.softmax(hg_lb_logits.astype(F32), axis=0), axis=0)
    lb_all = lb_all - lb_all[0:1]

    bp, lp, _ = x_prompt.shape
    bs, ls, _ = x_sample.shape
    yp = x_prompt.reshape(bp * lp, D_MODEL)
    ys = x_sample.reshape(bs * ls, D_MODEL)
    mem2d = mem_prompt.reshape(bp * MEM_LEN, D_MODEL)
    mat_zero = jnp.zeros((1, bp, NH, HD, HD), F32)
    p_small, s_small = [], []
    p_mats = dict(hgrn=None, ret=None, rwkv=None)
    s_mats = dict(hgrn=None, ret=None, rwkv=None)
    mk = matmul_layers(mem2d, wb['mem_w_k'], tm=TOKEN_TILE)
    mv = matmul_layers(mem2d, wb['mem_w_v'], tm=TOKEN_TILE)
    for l in range(DEPTH):
        final = l == DEPTH - 1
        def attend_p(q, w_o, x, l=l):
            return cross_attention_out(q, mk, mv, l, w_o, x, rows=TOKEN_TILE)

        zero_state = dict(s5_re=jnp.zeros((bp, S5_NG, S5_P), F32), s5_im=jnp.zeros((bp, S5_NG, S5_P), F32),
                          shift=jnp.zeros((bp, RW_PROJ), F32), hgrn=mat_zero, ret=mat_zero, rwkv=mat_zero, layer=0)
        yp, small, p_mats = _trunk_layer(yp, bp, lp, 0, attend_p, zero_state, p_mats, p, wb, l, lb_all[l], final)
        p_small.append(small)
        sst = dict(s5_re=state_s5_re[l], s5_im=state_s5_im[l], shift=state_rwkv_shift[l],
                   hgrn=state_hgrn, ret=state_ret, rwkv=state_rwkv, layer=l)
        def attend_s(q, w_o, x, l=l):
            o = cross_attention_cache(q, cache_mem_k, cache_mem_v, l, nb=CACHE_ATTN_SEQS, rows=ls)
            return matmul_residual(o, w_o, x, tm=TOKEN_TILE)

        ys, small, s_mats = _trunk_layer(ys, bs, ls, PAST_LEN, attend_s, sst, s_mats, p, wb, l, lb_all[l], final)
        s_small.append(small)
    stack = lambda states, i: jnp.stack([s[i] for s in states])
    return (yp.reshape(bp, lp, D_MODEL), ys.reshape(bs, ls, D_MODEL),
            stack(p_small, 0), stack(p_small, 1), p_mats['hgrn'], p_mats['ret'], p_mats['rwkv'], stack(p_small, 2),
            mk.reshape(DEPTH, bp, MEM_LEN, MEM_HEADS, MEM_HD), mv.reshape(DEPTH, bp, MEM_LEN, MEM_HEADS, MEM_HD),
            stack(s_small, 0), stack(s_small, 1), s_mats['hgrn'], s_mats['ret'], s_mats['rwkv'], stack(s_small, 2))
```

```python
import functools
import math

import jax
import jax.numpy as jnp
from jax import lax
from jax.experimental import pallas as pl
from jax.experimental.pallas import tpu as pltpu

F32 = jnp.float32
BF16 = jnp.bfloat16

D_MODEL = 1024
DEPTH = 2
PAST_LEN = 16384
GW = 256
HD = 64
NH = GW // HD
S5_GROUP = 16
S5_NG = GW // S5_GROUP
S5_P = 64
S5_W = S5_NG * S5_P
RW_PROJ = 4 * GW
IN_WIDTH = 13 * GW
MEM_LEN = 256
MEM_HEADS = 4
MEM_HD = D_MODEL // MEM_HEADS
D_FF = 4 * D_MODEL
EPS = 1e-6
RW_LN_EPS = 64e-5
ROPE_BASE = 10000.0
CHUNK = 16
RET_CHUNK = 64

VMEM_LIMIT = 56 * 1024 * 1024
SUBLANES = 8
LANES = 128
LOG2_E = 1.4426950408889634


def _cparams(sem):
    return pltpu.CompilerParams(dimension_semantics=sem, vmem_limit_bytes=VMEM_LIMIT)


def _mm(a, b):
    return jnp.dot(a.astype(BF16), b.astype(BF16), preferred_element_type=F32)


def _mm_nt(a, b):
    return lax.dot_general(a.astype(BF16), b.astype(BF16), (((1,), (1,)), ((), ())),
                           preferred_element_type=F32)


def _mm_tn(a, b):
    return lax.dot_general(a.astype(BF16), b.astype(BF16), (((0,), (0,)), ((), ())),
                           preferred_element_type=F32)


def _split3(x):
    hi = x.astype(BF16)
    r1 = x - hi.astype(F32)
    mid = r1.astype(BF16)
    lo = (r1 - mid.astype(F32)).astype(BF16)
    return hi, mid, lo


def _mm_exact_lhs(sel, x):
    s = sel.astype(BF16)
    hi, mid, lo = _split3(x)
    return (jnp.dot(s, hi, preferred_element_type=F32) + jnp.dot(s, mid, preferred_element_type=F32)
            + jnp.dot(s, lo, preferred_element_type=F32))


def _rms(x, gain):
    return x * lax.rsqrt(jnp.mean(x * x, axis=-1, keepdims=True) + EPS) * gain


def _sigmoid(x):
    return 1.0 / (1.0 + jnp.exp(-x))


def _iota2(shape, axis):
    return lax.broadcasted_iota(jnp.int32, shape, axis)


def _head_ones():
    return (_iota2((GW, GW), 0) // HD == _iota2((GW, GW), 1) // HD).astype(F32)


def _head_sum(x, ones_bd):
    s = ones_bd.astype(BF16)
    hi = x.astype(BF16)
    lo = (x - hi.astype(F32)).astype(BF16)
    return jnp.dot(hi, s, preferred_element_type=F32) + jnp.dot(lo, s, preferred_element_type=F32)


ROW_PARTS = 2


def _row_parts(n_rows, n_parts=ROW_PARTS):
    if n_rows % (n_parts * 2 * SUBLANES):
        return [slice(0, n_rows)]
    step = n_rows // n_parts
    return [slice(i * step, (i + 1) * step) for i in range(n_parts)]


def _norm_mm_kernel(x_ref, g_ref, w_ref, o_ref):
    parts = _row_parts(x_ref.shape[0])
    xn = [_rms(x_ref[r, :], g_ref[...]) for r in parts]
    for r, xp in zip(parts, xn):
        o_ref[r, :] = _mm(xp, w_ref[...])


def norm_matmul(x, gain, w, *, tm):
    t, d = x.shape
    n = w.shape[1]
    return pl.pallas_call(
        _norm_mm_kernel,
        out_shape=jax.ShapeDtypeStruct((t, n), F32),
        grid=(t // tm,),
        in_specs=[pl.BlockSpec((tm, d), lambda i: (i, 0)),
                  pl.BlockSpec((1, d), lambda i: (0, 0)),
                  pl.BlockSpec((d, n), lambda i: (0, 0))],
        out_specs=pl.BlockSpec((tm, n), lambda i: (i, 0)),
        compiler_params=_cparams(("parallel",)),
        name="norm_matmul",
    )(x, gain, w)


def _mm_kernel(a_ref, w_ref, o_ref):
    o_ref[...] = _mm(a_ref[...], w_ref[...])


def matmul_layers(a, w, *, tm):
    t, k = a.shape
    layers, _, n = w.shape
    return pl.pallas_call(
        _mm_kernel,
        out_shape=jax.ShapeDtypeStruct((layers, t, n), F32),
        grid=(layers, t // tm),
        in_specs=[pl.BlockSpec((tm, k), lambda l, i: (i, 0)),
                  pl.BlockSpec((None, k, n), lambda l, i: (l, 0, 0))],
        out_specs=pl.BlockSpec((None, tm, n), lambda l, i: (l, i, 0)),
        compiler_params=_cparams(("parallel", "parallel")),
        name="matmul_layers",
    )(a, w)


def _mm_res_kernel(a_ref, w_ref, r_ref, o_ref):
    o_ref[...] = r_ref[...] + _mm(a_ref[...], w_ref[...])


def matmul_residual(a, w, res, *, tm):
    t, k = a.shape
    n = w.shape[1]
    return pl.pallas_call(
        _mm_res_kernel,
        out_shape=jax.ShapeDtypeStruct((t, n), F32),
        grid=(t // tm,),
        in_specs=[pl.BlockSpec((tm, k), lambda i: (i, 0)),
                  pl.BlockSpec((k, n), lambda i: (0, 0)),
                  pl.BlockSpec((tm, n), lambda i: (i, 0))],
        out_specs=pl.BlockSpec((tm, n), lambda i: (i, 0)),
        compiler_params=_cparams(("parallel",)),
        name="matmul_residual",
    )(a, w, res)


def _mix_out_q_kernel(a0_ref, a1_ref, a2_ref, a3_ref, w_ref, r_ref, g_ref, wq_ref, x_ref, q_ref):
    parts = _row_parts(r_ref.shape[0])
    a = [jnp.concatenate([a0_ref[r, :], a1_ref[r, :], a2_ref[r, :], a3_ref[r, :]], axis=-1) for r in parts]
    x = [r_ref[r, :] + _mm(ap, w_ref[...]) for r, ap in zip(parts, a)]
    xn = [_rms(xp, g_ref[...]) for xp in x]
    for r, xp, xnp in zip(parts, x, xn):
        x_ref[r, :] = xp
        q_ref[r, :] = _mm(xnp, wq_ref[...]).astype(BF16)


def mix_out_q(parts, w, res, gain, w_q, *, tm):
    t, d = res.shape
    part = pl.BlockSpec((tm, GW), lambda i: (i, 0))
    mat = pl.BlockSpec((d, d), lambda i: (0, 0))
    tile = pl.BlockSpec((tm, d), lambda i: (i, 0))
    return pl.pallas_call(
        _mix_out_q_kernel,
        out_shape=(jax.ShapeDtypeStruct((t, d), F32), jax.ShapeDtypeStruct((t, d), BF16)),
        grid=(t // tm,),
        in_specs=[part, part, part, part, mat, tile, pl.BlockSpec((1, d), lambda i: (0, 0)), mat],
        out_specs=(tile, tile),
        compiler_params=_cparams(("parallel",)),
        name="mix_out_q",
    )(*parts, w, res, gain, w_q)


def _ffn_kernel(x_ref, g_ref, wu_ref, wd_ref, gf_ref, o_ref, *, final_norm):
    x = x_ref[...]
    h = jnp.dot(_rms(x, g_ref[...]).astype(BF16), wu_ref[...], preferred_element_type=F32)
    h = jnp.square(jnp.maximum(h, 0.0))
    y = x + jnp.dot(h.astype(BF16), wd_ref[...], preferred_element_type=F32)
    if final_norm:
        y = _rms(y, gf_ref[...])
    o_ref[...] = y


def ffn(x, gain, w_up, w_down, gain_final, *, tm, final_norm):
    t, d = x.shape
    ff = w_up.shape[1]
    resident = lambda shape: pl.BlockSpec(shape, lambda i: (0, 0), pipeline_mode=pl.Buffered(1))
    return pl.pallas_call(
        functools.partial(_ffn_kernel, final_norm=final_norm),
        out_shape=jax.ShapeDtypeStruct((t, d), F32),
        grid=(t // tm,),
        in_specs=[pl.BlockSpec((tm, d), lambda i: (i, 0)),
                  pl.BlockSpec((1, d), lambda i: (0, 0)),
                  resident((d, ff)), resident((ff, d)),
                  pl.BlockSpec((1, d), lambda i: (0, 0))],
        out_specs=pl.BlockSpec((tm, d), lambda i: (i, 0)),
        compiler_params=_cparams(("parallel",)),
        name="ffn",
    )(x, gain, w_up, w_down, gain_final)


def _attn_out_kernel(q_ref, k_ref, v_ref, wo_ref, r_ref, x_ref):
    sls = [slice(h * MEM_HD, (h + 1) * MEM_HD) for h in range(MEM_HEADS)]
    s = [_mm_nt(q_ref[:, sl], k_ref[:, sl]) * (MEM_HD ** -0.5) for sl in sls]
    p = [jnp.exp(sh - jnp.max(sh, axis=-1, keepdims=True)) for sh in s]
    p = [ph * (1.0 / jnp.sum(ph, axis=-1, keepdims=True)) for ph in p]
    heads = [_mm(ph, v_ref[:, sl]) for ph, sl in zip(p, sls)]
    x_ref[...] = r_ref[...] + _mm(jnp.concatenate(heads, axis=-1), wo_ref[...])


def cross_attention_out(q, mem_k, mem_v, layer, w_o, res, *, rows):
    t, d = res.shape
    bsz = mem_k.shape[1] // MEM_LEN
    lt = t // bsz // rows
    tile = pl.BlockSpec((rows, d), lambda b, l: (b * lt + l, 0))
    mem = pl.BlockSpec((None, MEM_LEN, d), lambda b, l: (layer, b, 0))
    return pl.pallas_call(
        _attn_out_kernel,
        out_shape=jax.ShapeDtypeStruct((t, d), F32),
        grid=(bsz, lt),
        in_specs=[tile, mem, mem, pl.BlockSpec((d, d), lambda b, l: (0, 0)), tile],
        out_specs=tile,
        compiler_params=_cparams(("parallel", "arbitrary")),
        name="cross_attention_out",
    )(q, mem_k, mem_v, w_o, res)


def _attn_cache_kernel(q_ref, k_ref, v_ref, o_ref, *, nb, rows):
    nr = MEM_HEADS * rows
    same_head = _iota2((nr, MEM_HEADS * MEM_LEN), 1) % MEM_HEADS == _iota2((nr, MEM_HEADS * MEM_LEN), 0) // rows
    q_all = q_ref[...].astype(F32)
    rng = range(nb)
    qs = [jnp.concatenate([q_all[b * rows:(b + 1) * rows, h * MEM_HD:(h + 1) * MEM_HD] for h in range(MEM_HEADS)],
                          axis=0) for b in rng]
    s = [_mm_nt(qs[b], k_ref[0, b].reshape(MEM_HEADS * MEM_LEN, MEM_HD)) * (MEM_HD ** -0.5) for b in rng]
    s = [jnp.where(same_head, sb, -1e30) for sb in s]
    p = [jnp.exp(sb - jnp.max(sb, axis=-1, keepdims=True)) for sb in s]
    p = [pb * (1.0 / jnp.sum(pb, axis=-1, keepdims=True)) for pb in p]
    o = [_mm(p[b], v_ref[0, b].reshape(MEM_HEADS * MEM_LEN, MEM_HD)) for b in rng]
    for b in rng:
        for h in range(MEM_HEADS):
            o_ref[b * rows:(b + 1) * rows, h * MEM_HD:(h + 1) * MEM_HD] = o[b][h * rows:(h + 1) * rows]


def cross_attention_cache(q, cache_k, cache_v, layer, *, nb, rows):
    t = q.shape[0]
    bsz = cache_k.shape[1]
    mem = pl.BlockSpec((1, nb, MEM_LEN, MEM_HEADS, MEM_HD), lambda b: (layer, b, 0, 0, 0))
    return pl.pallas_call(
        functools.partial(_attn_cache_kernel, nb=nb, rows=rows),
        out_shape=jax.ShapeDtypeStruct((t, D_MODEL), F32),
        grid=(bsz // nb,),
        in_specs=[pl.BlockSpec((nb * rows, D_MODEL), lambda b: (b, 0)), mem, mem],
        out_specs=pl.BlockSpec((nb * rows, D_MODEL), lambda b: (b, 0)),
        compiler_params=_cparams(("parallel",)),
        name="cross_attention_cache",
    )(q, cache_k, cache_v)


def _gelu_tanh(x):
    return 0.5 * x * (1.0 + jnp.tanh(math.sqrt(2.0 / math.pi) * (x + 0.044715 * (x * x * x))))


def _s5_kernel(u_ref, h0_ref, lam_ref, bblk_ref, cblk_ref, d_ref, wglu_ref, bglu_ref, gain_ref,
               y_ref, hfin_ref, scr_ref, tm_ref, *, steps, nb):
    @pl.when(pl.program_id(1) == 0)
    def _():
        scr_ref[0:nb, :] = h0_ref[...]

    def to_time_major(b, carry):
        for j in range(GW // LANES):
            tm_ref[j, pl.ds(b, steps, stride=nb), :] = u_ref[b, :, j * LANES:(j + 1) * LANES]
        return carry

    lax.fori_loop(0, nb, to_time_major, 0)
    u = jnp.concatenate([tm_ref[j] for j in range(GW // LANES)], axis=1)
    rows = _row_parts(steps * nb, 4)
    for r in rows:
        scr_ref[nb + r.start:nb + r.stop, :] = _mm(u[r], bblk_ref[...])
    lam_re = jnp.broadcast_to(lam_ref[0:1, :], (nb, S5_W))
    lam_im = jnp.broadcast_to(lam_ref[1:2, :], (nb, S5_W))

    h_re, h_im = scr_ref[0:nb, 0:S5_W], scr_ref[0:nb, S5_W:]
    for t in range(steps):
        c0 = (t + 1) * nb
        h_re, h_im = (scr_ref[c0:c0 + nb, 0:S5_W] + lam_re * h_re - lam_im * h_im,
                      scr_ref[c0:c0 + nb, S5_W:] + lam_re * h_im + lam_im * h_re)
        scr_ref[c0:c0 + nb, 0:S5_W] = h_re
        scr_ref[c0:c0 + nb, S5_W:] = h_im
    h_last = scr_ref[steps * nb:, :]
    hfin_ref[...] = h_last
    y = [_mm(scr_ref[nb + r.start:nb + r.stop, :], cblk_ref[...]) + d_ref[...] * u[r] for r in rows]
    y = [_gelu_tanh(yp) for yp in y]
    z = [_mm(yp, wglu_ref[...]) for yp in y]
    y = [_rms(yp * _sigmoid(zp + bglu_ref[...]), gain_ref[...]) for yp, zp in zip(y, z)]
    for r, yp in zip(rows, y):
        for j in range(GW // LANES):
            tm_ref[j, r, :] = yp[:, j * LANES:(j + 1) * LANES]

    def to_batch_major(b, carry):
        for j in range(GW // LANES):
            y_ref[b, :, j * LANES:(j + 1) * LANES] = tm_ref[j, pl.ds(b, steps, stride=nb), :]
        return carry

    lax.fori_loop(0, nb, to_batch_major, 0)
    scr_ref[0:nb, :] = h_last


def s5_mixer(proj, h0, lam, bblk, cblk, dvec, wglu, bglu, gain, *, steps, nb):
    bsz, seq, _ = proj.shape
    const = lambda shape: pl.BlockSpec(shape, lambda b, t: (0, 0))
    return pl.pallas_call(
        functools.partial(_s5_kernel, steps=steps, nb=nb),
        out_shape=(jax.ShapeDtypeStruct((bsz, seq, GW), F32),
                   jax.ShapeDtypeStruct((bsz, 2 * S5_W), F32)),
        grid=(bsz // nb, seq // steps),
        in_specs=[pl.BlockSpec((nb, steps, GW), lambda b, t: (b, t, 0)),
                  pl.BlockSpec((nb, 2 * S5_W), lambda b, t: (b, 0)),
                  const((2, S5_W)), const((GW, 2 * S5_W)), const((2 * S5_W, GW)), const((1, GW)),
                  const((GW, GW)), const((1, GW)), const((1, GW))],
        out_specs=(pl.BlockSpec((nb, steps, GW), lambda b, t: (b, t, 0)),
                   pl.BlockSpec((nb, 2 * S5_W), lambda b, t: (b, 0))),
        scratch_shapes=[pltpu.VMEM(((steps + 1) * nb, 2 * S5_W), F32),
                        pltpu.VMEM((GW // LANES, steps * nb, LANES), F32)],
        compiler_params=_cparams(("parallel", "arbitrary")),
        name="s5_mixer",
    )(proj, h0, lam, bblk, cblk, dvec, wglu, bglu, gain)


def s5_params(lam_re, lam_im, b_re, b_im, c_re, c_im, log_step):
    step = jnp.exp(log_step)[:, None]
    mag = jnp.exp(lam_re * step)
    lbar_re = mag * jnp.cos(lam_im * step)
    lbar_im = mag * jnp.sin(lam_im * step)
    den = lam_re * lam_re + lam_im * lam_im
    f_re = ((lbar_re - 1.0) * lam_re + lbar_im * lam_im) / den
    f_im = (lbar_im * lam_re - (lbar_re - 1.0) * lam_im) / den
    bbar_re = f_re[..., None] * b_re - f_im[..., None] * b_im
    bbar_im = f_re[..., None] * b_im + f_im[..., None] * b_re
    eye_g = jnp.eye(S5_NG, dtype=F32)

    def in_blk(m):
        return jnp.einsum('gph,gk->ghkp', m, eye_g).reshape(GW, S5_W)

    def out_blk(m):
        return jnp.einsum('ghp,gk->gpkh', m, eye_g).reshape(S5_W, GW)

    bblk = jnp.concatenate([in_blk(bbar_re), in_blk(bbar_im)], axis=1)
    cblk = jnp.concatenate([out_blk(c_re), -out_blk(c_im)], axis=0)
    lam2 = jnp.stack([lbar_re.reshape(S5_W), lbar_im.reshape(S5_W)])
    return lam2, bblk.astype(BF16), cblk.astype(BF16)


def s5_apply(proj, h0_re, h0_im, p, l, *, steps, nb):
    bsz = proj.shape[0]
    lam2, bblk, cblk = s5_params(p['s5_lam_re'][l], p['s5_lam_im'][l], p['s5_b_re'][l], p['s5_b_im'][l],
                                 p['s5_c_re'][l], p['s5_c_im'][l], p['s5_log_step'][l])
    h0 = jnp.concatenate([h0_re.reshape(bsz, S5_W), h0_im.reshape(bsz, S5_W)], axis=1)
    y, h = s5_mixer(proj, h0, lam2, bblk, cblk, p['s5_d'][l].reshape(1, GW),
                    p['s5_w_glu'][l].astype(BF16), p['s5_b_glu'][l].reshape(1, GW),
                    p['s5_norm'][l].reshape(1, GW), steps=steps, nb=nb)
    return y, h[:, :S5_W].reshape(bsz, S5_NG, S5_P), h[:, S5_W:].reshape(bsz, S5_NG, S5_P)


def _load_state(s0_ref, st_ref, nb, transpose):
    for b in range(nb):
        rows = []
        for h in range(NH):
            pieces = [s0_ref[b, h]]
            if h:
                pieces.insert(0, jnp.zeros((HD, h * HD), F32))
            if h < NH - 1:
                pieces.append(jnp.zeros((HD, (NH - 1 - h) * HD), F32))
            rows.append(jnp.concatenate(pieces, axis=1))
        st = jnp.concatenate(rows, axis=0)
        st_ref[b] = st.T if transpose else st


def _store_state(st_ref, sfin_ref, nb, transpose, own_layer):
    if own_layer is not None:
        for other in range(DEPTH):
            if other != own_layer:
                sfin_ref[other] = jnp.zeros(sfin_ref.shape[1:], F32)
        sfin_ref = sfin_ref.at[own_layer]
    for b in range(nb):
        st = st_ref[b].T if transpose else st_ref[b]
        for h in range(NH):
            sfin_ref[b, h] = st[h * HD:(h + 1) * HD, h * HD:(h + 1) * HD]


def _head_masks():
    lane_head = _iota2((1, GW), 1) // HD
    return [(lane_head == h).astype(F32) for h in range(NH)]


def _stack_heads(x, masks):
    return jnp.concatenate([x * m for m in masks], axis=0)


def _pad_rows(x, rows):
    if x.shape[0] == rows:
        return x
    return jnp.concatenate([x, jnp.zeros((rows - x.shape[0], x.shape[1]), x.dtype)], axis=0)


def _silu(x):
    return x * _sigmoid(x)


def _chunk_rows(ref, b, r0, rows, c):
    return _pad_rows(ref[b, pl.ds(r0, rows), :], c)


CHUNK_UNROLL = 4


def _chunk_loop(n_chunks, rows, body):
    unroll = math.gcd(n_chunks, CHUNK_UNROLL)

    def step(i, carry):
        for j in range(unroll):
            body(pl.multiple_of((i * unroll + j) * rows, rows))
        return carry

    lax.fori_loop(0, n_chunks // unroll, step, 0)


def _ret_kernel(q_ref, k_ref, v_ref, g_ref, cos_ref, sin_ref, s0_ref, _states_in, o_ref, sfin_ref, st_ref,
                *, nb, tb, c, c_real, own_layer):
    rows = min(tb, c)

    @pl.when(pl.program_id(1) == 0)
    def _():
        _load_state(s0_ref, st_ref, nb, transpose=False)

    masks = _head_masks()
    ones_bd = _head_ones()
    lane_head = _iota2((1, GW), 1) // HD
    log_gamma = jnp.zeros((1, GW), F32)
    for h in range(NH):
        log_gamma = jnp.where(lane_head == h, math.log(1.0 - 2.0 ** (-5.0 - h)), log_gamma)
    tt = _iota2((c, GW), 0).astype(F32)
    scale = HD ** -0.5
    g_q = jnp.exp(log_gamma * (tt + 1.0))
    g_k = jnp.exp(-log_gamma * (tt + 1.0)) * scale
    g_tail = jnp.exp(log_gamma * (c_real - 1.0 - tt)) * scale
    row_head = _iota2((GW, GW), 0) // HD
    g_chunk = jnp.zeros((GW, GW), F32)
    for h in range(NH):
        g_chunk = jnp.where(row_head == h, math.exp(math.log(1.0 - 2.0 ** (-5.0 - h)) * c_real), g_chunk)
    causal = _iota2((c, NH * c), 0) >= _iota2((c, NH * c), 1) % c
    first_half = _iota2((c, GW), 1) % HD < HD // 2

    def rope(x, cos, sin):
        swapped = jnp.where(first_half, pltpu.roll(x, GW - HD // 2, axis=1), pltpu.roll(x, HD // 2, axis=1))
        return x * cos + swapped * sin

    def chunk(r0):
        cos = _pad_rows(cos_ref[pl.ds(r0, rows), :], c)
        sin = _pad_rows(sin_ref[pl.ds(r0, rows), :], c)
        rng = range(nb)
        qt = [rope(_chunk_rows(q_ref, b, r0, rows, c), cos, sin) * g_q for b in rng]
        k = [rope(_chunk_rows(k_ref, b, r0, rows, c), cos, sin) for b in rng]
        v = [_chunk_rows(v_ref, b, r0, rows, c) for b in rng]
        st = [st_ref[b] for b in rng]
        sc = [_mm_nt(qt[b], _stack_heads(k[b] * g_k, masks)) for b in rng]
        o_in = [_mm(jnp.where(causal, sc[b], 0.0), _stack_heads(v[b], masks)) for b in rng]
        o_st = [_mm(qt[b], st[b]) for b in rng]
        d_st = [_mm_tn(k[b] * g_tail, v[b]) for b in rng]
        o = [o_in[b] + o_st[b] for b in rng]
        ms = [_head_sum(o[b] * o[b], ones_bd) for b in rng]
        for b in rng:
            st_ref[b] = st[b] * g_chunk + ones_bd * d_st[b]
            ob = o[b] * lax.rsqrt(ms[b] * (1.0 / HD) + EPS) * _silu(_chunk_rows(g_ref, b, r0, rows, c))
            o_ref[b, pl.ds(r0, rows), :] = ob[:rows]

    _chunk_loop(tb // rows, rows, chunk)

    @pl.when(pl.program_id(1) == pl.num_programs(1) - 1)
    def _():
        _store_state(st_ref, sfin_ref, nb, transpose=False, own_layer=own_layer)


def _rope_tables(pos0, seq):
    half = HD // 2
    inv = ROPE_BASE ** (-jnp.arange(half, dtype=F32) / half)
    pos = pos0 + jnp.arange(seq, dtype=jnp.int32)
    ang = pos.astype(F32)[:, None] * inv[None, :]
    cos, sin = jnp.cos(ang), jnp.sin(ang)
    return (jnp.tile(jnp.concatenate([cos, cos], axis=-1), (1, NH)),
            jnp.tile(jnp.concatenate([-sin, sin], axis=-1), (1, NH)))


def _proj_spec(nb, tb, col):
    return pl.BlockSpec((nb, tb, GW), lambda b, t: (b, t, col))


def _state_spec(nb, layer):
    return pl.BlockSpec((None, nb, NH, HD, HD), lambda b, t: (layer, b, 0, 0, 0))


_ALIASED = pl.BlockSpec(memory_space=pl.ANY)


def _collector(states, bsz, nb, layer, operand_index):
    shape = jax.ShapeDtypeStruct((DEPTH, bsz, NH, HD, HD), F32)
    if states is None:
        spec = pl.BlockSpec((DEPTH, nb, NH, HD, HD), lambda b, t: (0, b, 0, 0, 0))
        return jnp.zeros((1, 1, NH, HD, HD), F32), spec, shape, {}, layer
    return states, _state_spec(nb, layer), shape, {operand_index: 1}, None


def retention_apply(proj, s0, s0_layer, states, layer, pos0, *, tb, nb):
    bsz, seq, _ = proj.shape
    c = RET_CHUNK if tb >= RET_CHUNK else CHUNK
    cos, sin = _rope_tables(pos0, seq)
    states, st_spec, st_shape, aliases, own_layer = _collector(states, bsz, nb, layer, 7)
    return pl.pallas_call(
        functools.partial(_ret_kernel, nb=nb, tb=tb, c=c, c_real=min(tb, c), own_layer=own_layer),
        out_shape=(jax.ShapeDtypeStruct((bsz, seq, GW), F32), st_shape),
        grid=(bsz // nb, seq // tb),
        in_specs=[_proj_spec(nb, tb, 5), _proj_spec(nb, tb, 6), _proj_spec(nb, tb, 7), _proj_spec(nb, tb, 8),
                  pl.BlockSpec((tb, GW), lambda b, t: (t, 0)), pl.BlockSpec((tb, GW), lambda b, t: (t, 0)),
                  _state_spec(nb, s0_layer), _ALIASED],
        out_specs=(pl.BlockSpec((nb, tb, GW), lambda b, t: (b, t, 0)), st_spec),
        scratch_shapes=[pltpu.VMEM((nb, GW, GW), F32)],
        input_output_aliases=aliases,
        compiler_params=_cparams(("parallel", "arbitrary")),
        name="retention",
    )(proj, proj, proj, proj, cos, sin, s0, states)


def _log_sigmoid(z):
    return jnp.minimum(z, 0.0) - jnp.log(1.0 + jnp.exp(-jnp.abs(z)))


def _hgrn_kernel(q_ref, f_ref, i_ref, g_ref, lb_ref, gain_ref, s0_ref, _states_in, o_ref, sfin_ref, st_ref,
                 *, nb, tb, c, own_layer):
    rows = min(tb, c)

    @pl.when(pl.program_id(1) == 0)
    def _():
        _load_state(s0_ref, st_ref, nb, transpose=True)

    t_hi = -(-rows // SUBLANES) * SUBLANES
    ones_bd = _head_ones()
    tri = (_iota2((c, c), 0) >= _iota2((c, c), 1)).astype(F32)
    t_idx = _iota2((c, GW), 0)
    lb = lb_ref[...]
    log_lb = jnp.log(lb)
    log_1m_lb = jnp.log(1.0 - lb)
    gain = gain_ref[...]

    def chunk(r0):
        rng = range(nb)
        valid = t_idx < rows

        def gates(b):
            z = _chunk_rows(f_ref, b, r0, rows, c)
            ls_pos = _log_sigmoid(z)
            ls_neg = ls_pos - z
            b2 = log_lb + ls_neg
            log_f = jnp.maximum(ls_pos, b2) + jnp.log(1.0 + jnp.exp(-jnp.abs(ls_pos - b2)))
            return (jnp.where(valid, log_f, 0.0), jnp.where(valid, (1.0 - lb) * jnp.exp(ls_neg), 0.0),
                    ls_neg + log_1m_lb)

        log_f, key, log_key = zip(*[gates(b) for b in rng])
        q = [_silu(_chunk_rows(q_ref, b, r0, rows, c)) for b in rng]
        v = [_chunk_rows(i_ref, b, r0, rows, c) for b in rng]
        st = [st_ref[b] for b in rng]
        cum = [_mm_exact_lhs(tri, log_f[b]) for b in rng]
        last = [cum[b][c - 1:c, :] for b in rng]
        cum2 = [cum[b] * LOG2_E for b in rng]
        kd2 = [(log_key[b] - cum[b]) * LOG2_E for b in rng]

        def pair_rows(b, s):
            lo = SUBLANES * (s // SUBLANES)
            t_rows = _iota2((t_hi - lo, GW), 0) + lo
            key_decay = jnp.where(t_rows >= s, jnp.exp2(cum2[b][lo:t_hi] + kd2[b][s:s + 1, :]), 0.0)
            return key_decay * q[b][lo:t_hi]

        sc = [_mm(jnp.concatenate([pair_rows(b, s) for s in range(rows)], axis=0), ones_bd) for b in rng]
        o_st = [_mm_nt(q[b] * jnp.exp(cum[b]), st[b]) for b in rng]
        d_st = [_mm_tn(v[b], key[b] * jnp.exp(last[b] - cum[b])) for b in rng]

        def intra(b):
            tiles = [None] * (c // SUBLANES)
            off = 0
            for s in range(rows):
                for ti in range(s // SUBLANES, t_hi // SUBLANES):
                    term = sc[b][off:off + SUBLANES, :] * v[b][s:s + 1, :]
                    tiles[ti] = term if tiles[ti] is None else tiles[ti] + term
                    off += SUBLANES
            zero = jnp.zeros((SUBLANES, GW), F32)
            return jnp.concatenate([zero if t is None else t for t in tiles], axis=0)

        o = [o_st[b] + intra(b) for b in rng]
        ms = [_head_sum(o[b] * o[b], ones_bd) for b in rng]
        for b in rng:
            st_ref[b] = st[b] * jnp.exp(last[b]) + ones_bd * d_st[b]
            ob = o[b] * lax.rsqrt(ms[b] * (1.0 / HD) + EPS) * gain * _silu(_chunk_rows(g_ref, b, r0, rows, c))
            o_ref[b, pl.ds(r0, rows), :] = ob[:rows]

    _chunk_loop(tb // rows, rows, chunk)

    @pl.when(pl.program_id(1) == pl.num_programs(1) - 1)
    def _():
        _store_state(st_ref, sfin_ref, nb, transpose=True, own_layer=own_layer)


def hgrn_apply(proj, s0, s0_layer, states, layer, lb, gain, *, tb, nb):
    bsz, seq, _ = proj.shape
    row = pl.BlockSpec((1, GW), lambda b, t: (0, 0))
    states, st_spec, st_shape, aliases, own_layer = _collector(states, bsz, nb, layer, 7)
    return pl.pallas_call(
        functools.partial(_hgrn_kernel, nb=nb, tb=tb, c=CHUNK, own_layer=own_layer),
        out_shape=(jax.ShapeDtypeStruct((bsz, seq, GW), F32), st_shape),
        grid=(bsz // nb, seq // tb),
        in_specs=[_proj_spec(nb, tb, 1), _proj_spec(nb, tb, 2), _proj_spec(nb, tb, 3), _proj_spec(nb, tb, 4),
                  row, row, _state_spec(nb, s0_layer), _ALIASED],
        out_specs=(pl.BlockSpec((nb, tb, GW), lambda b, t: (b, t, 0)), st_spec),
        scratch_shapes=[pltpu.VMEM((nb, GW, GW), F32)],
        input_output_aliases=aliases,
        compiler_params=_cparams(("parallel", "arbitrary")),
        name="hgrn2",
    )(proj, proj, proj, proj, lb.reshape(1, GW), gain.reshape(1, GW), s0, states)


def _softplus(z):
    return jnp.maximum(z, 0.0) + jnp.log(1.0 + jnp.exp(-jnp.abs(z)))


def _rwkv_prepare(ins, consts):
    masks, ones_bd, bd_mask, eye_flat, tri, strict, incl = consts
    rng = range(len(ins))
    r, lw, k, v, kk, a = [[i[j] for i in ins] for j in range(6)]
    c = r[0].shape[0]
    n = NH * c

    def to_bd(flat):
        return jnp.concatenate([flat] * NH, axis=0) * bd_mask

    cum = [_mm_exact_lhs(tri, lw[i]) for i in rng]
    last = [cum[i][c - 1:c, :] for i in rng]
    p_inv = [jnp.exp(-cum[i]) for i in rng]
    p_tail = [jnp.exp(last[i] - cum[i]) for i in rng]
    ka = [kk[i] * a[i] for i in rng]
    x = [jnp.concatenate([kk[i] * jnp.exp(cum[i] - lw[i]), r[i] * jnp.exp(cum[i])], axis=0) for i in rng]
    g = [_mm_nt(x[i], jnp.concatenate([_stack_heads(ka[i] * p_inv[i], masks),
                                       _stack_heads(k[i] * p_inv[i], masks)], axis=0)) for i in rng]
    m_ak = [jnp.where(strict, g[i][:c], 0.0) for i in rng]
    n_ak = [jnp.where(incl, g[i][c:], 0.0) for i in rng]
    pw = [m_ak[i][:, :n] for i in rng]
    t_inv = [eye_flat - pw[i] for i in rng]
    pw_bd = [to_bd(pw[i]) for i in rng]
    for _ in range(int(math.log2(c)) - 1):
        pw = [_mm(pw[i], pw_bd[i]) for i in rng]
        pw_bd = [to_bd(pw[i]) for i in rng]
        t_inv = [t_inv[i] + _mm(t_inv[i], pw_bd[i]) for i in rng]
    v_stk = [_stack_heads(v[i], masks) for i in rng]
    zeros = jnp.zeros((n, GW), F32)
    mkv = [_mm(m_ak[i], jnp.concatenate([zeros, v_stk[i]], axis=0)) for i in rng]
    tmkv = [_mm(t_inv[i], _stack_heads(mkv[i], masks)) for i in rng]
    return [dict(x=x[i], t_inv=t_inv[i], tmkv=tmkv[i], n_ak=n_ak[i], v=v[i], v_stk=v_stk[i],
                 decay=jnp.exp(last[i]),
                 k_tail=jnp.concatenate([ka[i] * p_tail[i], k[i] * p_tail[i]], axis=0)) for i in rng]


def _rwkv_advance(prep, st, consts):
    masks, ones_bd = consts[0], consts[1]
    rng = range(len(prep))
    c = prep[0]['t_inv'].shape[0]
    xa = [_mm_nt(prep[i]['x'], st[i]) for i in rng]
    u = [-_mm(prep[i]['t_inv'], _stack_heads(xa[i][:c], masks)) - prep[i]['tmkv'] for i in rng]
    nuv = [_mm(prep[i]['n_ak'], jnp.concatenate([_stack_heads(u[i], masks), prep[i]['v_stk']], axis=0))
           for i in rng]
    d_uv = [_mm_tn(jnp.concatenate([u[i], prep[i]['v']], axis=0), prep[i]['k_tail']) for i in rng]
    return [(xa[i][c:] + nuv[i], st[i] * prep[i]['decay'] + ones_bd * d_uv[i]) for i in rng]


def _rwkv_kernel(x_r_ref, x_k_ref, x_v_ref, x_l_ref, sh0_ref, s0_ref, mu_ref, w0_ref, ww_ref, a0_ref,
                 wa_ref, wg_ref, kk_ref, ka_ref, rk_ref, lng_ref, lnb_ref, _states_in,
                 o_ref, sfin_ref, shfin_ref,
                 st_ref, sh_ref, r_s, lw_s, k_s, v_s, kkn_s, a_s, y_s, bonus_s, gate_s, *, nb, tb, c, own_layer):
    tbp = max(tb, c)

    @pl.when(pl.program_id(1) == 0)
    def _():
        _load_state(s0_ref, st_ref, nb, transpose=False)
        sh_ref[...] = sh0_ref[...]

    masks = _head_masks()
    ones_bd = _head_ones()
    n = NH * c
    row_t = _iota2((c, 2 * n), 0)
    col_t = _iota2((c, 2 * n), 1) % c
    bd_mask = (_iota2((n, n), 0) // c == _iota2((n, n), 1) // c).astype(F32)
    eye_flat = (_iota2((c, n), 1) % c == _iota2((c, n), 0)).astype(F32)
    consts = (masks, ones_bd, bd_mask, eye_flat, (_iota2((c, c), 0) >= _iota2((c, c), 1)).astype(F32),
              col_t < row_t, col_t <= row_t)
    first_row = _iota2((nb * tb, GW), 0) % tb == 0

    def mixed(x_ref, j):
        x = x_ref[...].reshape(nb * tb, GW)
        carried = jnp.broadcast_to(sh_ref[:, j:j + 1, :], (nb, tb, GW)).reshape(nb * tb, GW)
        prev = jnp.where(first_row, carried, pltpu.roll(x, 1, axis=0))
        sh_ref[:, j:j + 1, :] = x_ref[:, tb - 1:tb, :]
        return x + (prev - x) * mu_ref[j:j + 1, :]

    r = mixed(x_r_ref, 0)
    k = mixed(x_k_ref, 1)
    v = mixed(x_v_ref, 2)
    xl = mixed(x_l_ref, 3)
    log_w = -_softplus(-(w0_ref[...] + _mm(jnp.tanh(xl), ww_ref[...]))) - 0.5
    a = _sigmoid(a0_ref[...] + _mm(xl, wa_ref[...]))
    gate_s[...] = _mm(_sigmoid(xl), wg_ref[...]).reshape(nb, tb, GW)
    kk = k * kk_ref[...]
    kk = kk * lax.rsqrt(jnp.maximum(_head_sum(kk * kk, ones_bd), 1e-24))
    k = k * (1.0 + (a - 1.0) * ka_ref[...])
    bonus_s[...] = (_head_sum(r * k * rk_ref[...], ones_bd) * v).reshape(nb, tb, GW)
    for s, val in ((r_s, r), (lw_s, -jnp.exp(log_w)), (k_s, k), (v_s, v), (kkn_s, kk), (a_s, a)):
        s[:, 0:tb, :] = val.reshape(nb, tb, GW)
        if tbp > tb:
            s[:, tb:tbp, :] = jnp.zeros((nb, tbp - tb, GW), F32)

    n_chunks = tbp // c
    group = 8 if n_chunks % 8 == 0 else 1

    def chunks(gi, carry):
        r0 = [pl.multiple_of((gi * group + j) * c, c) for j in range(group)]
        prep = _rwkv_prepare([tuple(s[b, pl.ds(r0[j], c), :] for s in (r_s, lw_s, k_s, v_s, kkn_s, a_s))
                              for j in range(group) for b in range(nb)], consts)
        st = [st_ref[b] for b in range(nb)]
        for j in range(group):
            outs = _rwkv_advance(prep[j * nb:(j + 1) * nb], st, consts)
            st = [o[1] for o in outs]
            for b in range(nb):
                y_s[b, pl.ds(r0[j], c), :] = outs[b][0]
        for b in range(nb):
            st_ref[b] = st[b]
        return carry

    lax.fori_loop(0, n_chunks // group, chunks, 0)

    y = y_s[:, 0:tb, :].reshape(nb * tb, GW)
    mean = _head_sum(y, ones_bd) * (1.0 / HD)
    d = y - mean
    var = _head_sum(d * d, ones_bd) * (1.0 / HD)
    y = d * lax.rsqrt(var + RW_LN_EPS) * lng_ref[...] + lnb_ref[...]
    o_ref[...] = ((y.reshape(nb, tb, GW) + bonus_s[...]) * gate_s[...])
    shfin_ref[...] = sh_ref[...]

    @pl.when(pl.program_id(1) == pl.num_programs(1) - 1)
    def _():
        _store_state(st_ref, sfin_ref, nb, transpose=False, own_layer=own_layer)


def rwkv_apply(proj, s0, s0_layer, states, shift0, p, l, *, tb, nb):
    bsz, seq, _ = proj.shape
    states, st_spec, st_shape, aliases, own_layer = _collector(states, bsz, nb, l, 17)
    c = CHUNK
    tbp = max(tb, c)
    row = pl.BlockSpec((1, GW), lambda b, t: (0, 0))
    mat = pl.BlockSpec((GW, GW), lambda b, t: (0, 0))
    sh_spec = pl.BlockSpec((nb, 4, GW), lambda b, t: (b, 0, 0))
    zeros = lambda r: jnp.zeros((r, GW), F32)
    ww = jnp.concatenate([p['rw_w_w2'][l], zeros(192)], axis=0).astype(BF16)
    wa = jnp.concatenate([zeros(64), p['rw_w_a2'][l], zeros(128)], axis=0).astype(BF16)
    wg = jnp.concatenate([zeros(128), p['rw_w_g2'][l]], axis=0).astype(BF16)
    r1 = lambda name: p[name][l].reshape(1, GW)
    seq_buf = pltpu.VMEM((nb, tbp, GW), F32)
    blk_buf = pltpu.VMEM((nb, tb, GW), F32)
    y, st, sh = pl.pallas_call(
        functools.partial(_rwkv_kernel, nb=nb, tb=tb, c=c, own_layer=own_layer),
        out_shape=(jax.ShapeDtypeStruct((bsz, seq, GW), F32), st_shape,
                   jax.ShapeDtypeStruct((bsz, 4, GW), F32)),
        grid=(bsz // nb, seq // tb),
        in_specs=[_proj_spec(nb, tb, 9), _proj_spec(nb, tb, 10), _proj_spec(nb, tb, 11), _proj_spec(nb, tb, 12),
                  sh_spec, _state_spec(nb, s0_layer), pl.BlockSpec((4, GW), lambda b, t: (0, 0)),
                  row, mat, row, mat, mat, row, row, row, row, row, _ALIASED],
        out_specs=(pl.BlockSpec((nb, tb, GW), lambda b, t: (b, t, 0)), st_spec, sh_spec),
        scratch_shapes=[pltpu.VMEM((nb, GW, GW), F32), pltpu.VMEM((nb, 4, GW), F32),
                        seq_buf, seq_buf, seq_buf, seq_buf, seq_buf, seq_buf, seq_buf, blk_buf, blk_buf],
        input_output_aliases=aliases,
        compiler_params=_cparams(("parallel", "arbitrary")),
        name="rwkv7",
    )(proj, proj, proj, proj, shift0.reshape(bsz, 4, GW), s0, p['rw_mu'][l].reshape(4, GW),
      r1('rw_w0'), ww, r1('rw_a0'), wa, wg, r1('rw_k_k'), r1('rw_k_a'), r1('rw_r_k'), r1('rw_ln_g'), r1('rw_ln_b'),
      states)
    return y, st, sh.reshape(bsz, RW_PROJ)


TOKEN_TILE = 512
TIME_BLOCK = 256
RWKV_TIME_BLOCK = 128
CACHE_ATTN_SEQS = 4


def _tiles(bsz, seq):
    if seq >= TIME_BLOCK:
        return dict(tb=TIME_BLOCK, rw_tb=RWKV_TIME_BLOCK, s5_nb=bsz, mix_nb=bsz, rw_nb=bsz)
    return dict(tb=seq, rw_tb=seq, s5_nb=bsz, mix_nb=16, rw_nb=32)


def _trunk_layer(x, bsz, seq, pos0, attend, st, mats, p, wb, l, lb, final_norm):
    cfg = _tiles(bsz, seq)
    tm = TOKEN_TILE
    row = lambda name: p[name][l].reshape(1, -1)
    proj = norm_matmul(x, row('norm_mix'), wb['w_in'][l], tm=tm)
    proj = proj.reshape(bsz, seq, IN_WIDTH)
    y_s5, s5_re, s5_im = s5_apply(proj, st['s5_re'], st['s5_im'], p, l, steps=cfg['tb'], nb=cfg['s5_nb'])
    y_hg, hg_s = hgrn_apply(proj, st['hgrn'], st['layer'], mats['hgrn'], l, lb, p['hg_norm'][l],
                            tb=cfg['tb'], nb=cfg['mix_nb'])
    y_rt, rt_s = retention_apply(proj, st['ret'], st['layer'], mats['ret'], l, pos0,
                                 tb=cfg['tb'], nb=cfg['mix_nb'])
    y_rw, rw_s, shift = rwkv_apply(proj, st['rwkv'], st['layer'], mats['rwkv'], st['shift'], p, l,
                                   tb=cfg['rw_tb'], nb=cfg['rw_nb'])
    parts = [y.reshape(bsz * seq, GW) for y in (y_s5, y_hg, y_rt, y_rw)]
    x, q = mix_out_q(parts, wb['w_out'][l], x, row('norm_mem'), wb['mem_w_q'][l], tm=tm)
    x = attend(q, wb['mem_w_o'][l], x)
    x = ffn(x, row('norm_ffn'), wb['ffn_w_up'][l], wb['ffn_w_down'][l], p['norm_final'].reshape(1, -1),
            tm=tm, final_norm=final_norm)
    return x, (s5_re, s5_im, shift), dict(hgrn=hg_s, ret=rt_s, rwkv=rw_s)


def kernel(x_prompt, x_sample, mem_prompt, state_s5_re, state_s5_im, state_hgrn, state_ret, state_rwkv,
           state_rwkv_shift, cache_mem_k, cache_mem_v, norm_mix, w_in, w_out, s5_lam_re, s5_lam_im,
           s5_b_re, s5_b_im, s5_c_re, s5_c_im, s5_d, s5_log_step, s5_w_glu, s5_b_glu, s5_norm,
           hg_lb_logits, hg_norm, rw_mu, rw_w0, rw_w_w2, rw_a0, rw_w_a2, rw_w_g2, rw_k_k, rw_k_a, rw_r_k,
           rw_ln_g, rw_ln_b, norm_mem, mem_w_q, mem_w_k, mem_w_v, mem_w_o, norm_ffn, ffn_w_up, ffn_w_down,
           norm_final):
    p = dict(norm_mix=norm_mix, s5_lam_re=s5_lam_re, s5_lam_im=s5_lam_im, s5_b_re=s5_b_re, s5_b_im=s5_b_im,
             s5_c_re=s5_c_re, s5_c_im=s5_c_im, s5_d=s5_d, s5_log_step=s5_log_step, s5_w_glu=s5_w_glu,
             s5_b_glu=s5_b_glu, s5_norm=s5_norm, hg_norm=hg_norm, rw_mu=rw_mu, rw_w0=rw_w0, rw_w_w2=rw_w_w2,
             rw_a0=rw_a0, rw_w_a2=rw_w_a2, rw_w_g2=rw_w_g2, rw_k_k=rw_k_k, rw_k_a=rw_k_a, rw_r_k=rw_r_k,
             rw_ln_g=rw_ln_g, rw_ln_b=rw_ln_b, norm_mem=norm_mem, norm_ffn=norm_ffn, norm_final=norm_final)
    wb = {name: w.astype(BF16) for name, w in dict(
        w_in=w_in, w_out=w_out, mem_w_q=mem_w_q, mem_w_k=mem_w_k, mem_w_v=mem_w_v, mem_w_o=mem_w_o,
        ffn_w_up=ffn_w_up, ffn_w_down=ffn_w_down).items()}
    lb_all = jnp.cumsum(jax.nn.softmax(hg_lb_logits.astype(F32), axis=0), axis=0)
    lb_all = lb_all - lb_all[0:1]

    bp, lp, _ = x_prompt.shape
    bs, ls, _ = x_sample.shape
    yp = x_prompt.reshape(bp * lp, D_MODEL)
    ys = x_sample.reshape(bs * ls, D_MODEL)
    mem2d = mem_prompt.reshape(bp * MEM_LEN, D_MODEL)
    mat_zero = jnp.zeros((1, bp, NH, HD, HD), F32)
    p_small, s_small = [], []
    p_mats = dict(hgrn=None, ret=None, rwkv=None)
    s_mats = dict(hgrn=None, ret=None, rwkv=None)
    mk = matmul_layers(mem2d, wb['mem_w_k'], tm=TOKEN_TILE)
    mv = matmul_layers(mem2d, wb['mem_w_v'], tm=TOKEN_TILE)
    for l in range(DEPTH):
        final = l == DEPTH - 1
        def attend_p(q, w_o, x, l=l):
            return cross_attention_out(q, mk, mv, l, w_o, x, rows=TOKEN_TILE)

        zero_state = dict(s5_re=jnp.zeros((bp, S5_NG, S5_P), F32), s5_im=jnp.zeros((bp, S5_NG, S5_P), F32),
                          shift=jnp.zeros((bp, RW_PROJ), F32), hgrn=mat_zero, ret=mat_zero, rwkv=mat_zero, layer=0)
        yp, small, p_mats = _trunk_layer(yp, bp, lp, 0, attend_p, zero_state, p_mats, p, wb, l, lb_all[l], final)
        p_small.append(small)
        sst = dict(s5_re=state_s5_re[l], s5_im=state_s5_im[l], shift=state_rwkv_shift[l],
                   hgrn=state_hgrn, ret=state_ret, rwkv=state_rwkv, layer=l)
        def attend_s(q, w_o, x, l=l):
            o = cross_attention_cache(q, cache_mem_k, cache_mem_v, l, nb=CACHE_ATTN_SEQS, rows=ls)
            return matmul_residual(o, w_o, x, tm=TOKEN_TILE)

        ys, small, s_mats = _trunk_layer(ys, bs, ls, PAST_LEN, attend_s, sst, s_mats, p, wb, l, lb_all[l], final)
        s_small.append(small)
    stack = lambda states, i: jnp.stack([s[i] for s in states])
    return (yp.reshape(bp, lp, D_MODEL), ys.reshape(bs, ls, D_MODEL),
            stack(p_small, 0), stack(p_small, 1), p_mats['hgrn'], p_mats['ret'], p_mats['rwkv'], stack(p_small, 2),
            mk.reshape(DEPTH, bp, MEM_LEN, MEM_HEADS, MEM_HD), mv.reshape(DEPTH, bp, MEM_LEN, MEM_HEADS, MEM_HD),
            stack(s_small, 0), stack(s_small, 1), s_mats['hgrn'], s_mats['ret'], s_mats['rwkv'], stack(s_small, 2))
```

```python
import functools
import math

import jax
import jax.numpy as jnp
from jax import lax
from jax.experimental import pallas as pl
from jax.experimental.pallas import tpu as pltpu

F32 = jnp.float32
BF16 = jnp.bfloat16

D_MODEL = 1024
DEPTH = 2
PAST_LEN = 16384
GW = 256
HD = 64
NH = GW // HD
S5_GROUP = 16
S5_NG = GW // S5_GROUP
S5_P = 64
S5_W = S5_NG * S5_P
RW_PROJ = 4 * GW
IN_WIDTH = 13 * GW
MEM_LEN = 256
MEM_HEADS = 4
MEM_HD = D_MODEL // MEM_HEADS
D_FF = 4 * D_MODEL
EPS = 1e-6
RW_LN_EPS = 64e-5
ROPE_BASE = 10000.0
CHUNK = 16
RET_CHUNK = 64

VMEM_LIMIT = 56 * 1024 * 1024
SUBLANES = 8
LANES = 128
LOG2_E = 1.4426950408889634


def _cparams(sem):
    return pltpu.CompilerParams(dimension_semantics=sem, vmem_limit_bytes=VMEM_LIMIT)


def _mm(a, b):
    return jnp.dot(a.astype(BF16), b.astype(BF16), preferred_element_type=F32)


def _mm_nt(a, b):
    return lax.dot_general(a.astype(BF16), b.astype(BF16), (((1,), (1,)), ((), ())),
                           preferred_element_type=F32)


def _mm_tn(a, b):
    return lax.dot_general(a.astype(BF16), b.astype(BF16), (((0,), (0,)), ((), ())),
                           preferred_element_type=F32)


def _split3(x):
    hi = x.astype(BF16)
    r1 = x - hi.astype(F32)
    mid = r1.astype(BF16)
    lo = (r1 - mid.astype(F32)).astype(BF16)
    return hi, mid, lo


def _mm_exact_lhs(sel, x):
    s = sel.astype(BF16)
    hi, mid, lo = _split3(x)
    return (jnp.dot(s, hi, preferred_element_type=F32) + jnp.dot(s, mid, preferred_element_type=F32)
            + jnp.dot(s, lo, preferred_element_type=F32))


def _rms(x, gain):
    return x * lax.rsqrt(jnp.mean(x * x, axis=-1, keepdims=True) + EPS) * gain


def _sigmoid(x):
    return 1.0 / (1.0 + jnp.exp(-x))


def _iota2(shape, axis):
    return lax.broadcasted_iota(jnp.int32, shape, axis)


def _head_ones():
    return (_iota2((GW, GW), 0) // HD == _iota2((GW, GW), 1) // HD).astype(F32)


def _head_sum(x, ones_bd):
    s = ones_bd.astype(BF16)
    hi = x.astype(BF16)
    lo = (x - hi.astype(F32)).astype(BF16)
    return jnp.dot(hi, s, preferred_element_type=F32) + jnp.dot(lo, s, preferred_element_type=F32)


ROW_PARTS = 2


def _row_parts(n_rows, n_parts=ROW_PARTS):
    if n_rows % (n_parts * 2 * SUBLANES):
        return [slice(0, n_rows)]
    step = n_rows // n_parts
    return [slice(i * step, (i + 1) * step) for i in range(n_parts)]


def _norm_mm_kernel(x_ref, g_ref, w_ref, o_ref):
    parts = _row_parts(x_ref.shape[0])
    xn = [_rms(x_ref[r, :], g_ref[...]) for r in parts]
    for r, xp in zip(parts, xn):
        o_ref[r, :] = _mm(xp, w_ref[...])


def norm_matmul(x, gain, w, *, tm):
    t, d = x.shape
    n = w.shape[1]
    return pl.pallas_call(
        _norm_mm_kernel,
        out_shape=jax.ShapeDtypeStruct((t, n), F32),
        grid=(t // tm,),
        in_specs=[pl.BlockSpec((tm, d), lambda i: (i, 0)),
                  pl.BlockSpec((1, d), lambda i: (0, 0)),
                  pl.BlockSpec((d, n), lambda i: (0, 0))],
        out_specs=pl.BlockSpec((tm, n), lambda i: (i, 0)),
        compiler_params=_cparams(("parallel",)),
        name="norm_matmul",
    )(x, gain, w)


def _mm_kernel(a_ref, w_ref, o_ref):
    o_ref[...] = _mm(a_ref[...], w_ref[...])


def matmul_layers(a, w, *, tm):
    t, k = a.shape
    layers, _, n = w.shape
    return pl.pallas_call(
        _mm_kernel,
        out_shape=jax.ShapeDtypeStruct((layers, t, n), F32),
        grid=(layers, t // tm),
        in_specs=[pl.BlockSpec((tm, k), lambda l, i: (i, 0)),
                  pl.BlockSpec((None, k, n), lambda l, i: (l, 0, 0))],
        out_specs=pl.BlockSpec((None, tm, n), lambda l, i: (l, i, 0)),
        compiler_params=_cparams(("parallel", "parallel")),
        name="matmul_layers",
    )(a, w)


def _mm_res_kernel(a_ref, w_ref, r_ref, o_ref):
    o_ref[...] = r_ref[...] + _mm(a_ref[...], w_ref[...])


def matmul_residual(a, w, res, *, tm):
    t, k = a.shape
    n = w.shape[1]
    return pl.pallas_call(
        _mm_res_kernel,
        out_shape=jax.ShapeDtypeStruct((t, n), F32),
        grid=(t // tm,),
        in_specs=[pl.BlockSpec((tm, k), lambda i: (i, 0)),
                  pl.BlockSpec((k, n), lambda i: (0, 0)),
                  pl.BlockSpec((tm, n), lambda i: (i, 0))],
        out_specs=pl.BlockSpec((tm, n), lambda i: (i, 0)),
        compiler_params=_cparams(("parallel",)),
        name="matmul_residual",
    )(a, w, res)


def _mix_out_q_kernel(a0_ref, a1_ref, a2_ref, a3_ref, w_ref, r_ref, g_ref, wq_ref, x_ref, q_ref):
    parts = _row_parts(r_ref.shape[0])
    a = [jnp.concatenate([a0_ref[r, :], a1_ref[r, :], a2_ref[r, :], a3_ref[r, :]], axis=-1) for r in parts]
    x = [r_ref[r, :] + _mm(ap, w_ref[...]) for r, ap in zip(parts, a)]
    xn = [_rms(xp, g_ref[...]) for xp in x]
    for r, xp, xnp in zip(parts, x, xn):
        x_ref[r, :] = xp
        q_ref[r, :] = _mm(xnp, wq_ref[...]).astype(BF16)


def mix_out_q(parts, w, res, gain, w_q, *, tm):
    t, d = res.shape
    part = pl.BlockSpec((tm, GW), lambda i: (i, 0))
    mat = pl.BlockSpec((d, d), lambda i: (0, 0))
    tile = pl.BlockSpec((tm, d), lambda i: (i, 0))
    return pl.pallas_call(
        _mix_out_q_kernel,
        out_shape=(jax.ShapeDtypeStruct((t, d), F32), jax.ShapeDtypeStruct((t, d), BF16)),
        grid=(t // tm,),
        in_specs=[part, part, part, part, mat, tile, pl.BlockSpec((1, d), lambda i: (0, 0)), mat],
        out_specs=(tile, tile),
        compiler_params=_cparams(("parallel",)),
        name="mix_out_q",
    )(*parts, w, res, gain, w_q)


def _ffn_kernel(x_ref, g_ref, wu_ref, wd_ref, gf_ref, o_ref, *, final_norm):
    x = x_ref[...]
    h = jnp.dot(_rms(x, g_ref[...]).astype(BF16), wu_ref[...], preferred_element_type=F32)
    h = jnp.square(jnp.maximum(h, 0.0))
    y = x + jnp.dot(h.astype(BF16), wd_ref[...], preferred_element_type=F32)
    if final_norm:
        y = _rms(y, gf_ref[...])
    o_ref[...] = y


def ffn(x, gain, w_up, w_down, gain_final, *, tm, final_norm):
    t, d = x.shape
    ff = w_up.shape[1]
    resident = lambda shape: pl.BlockSpec(shape, lambda i: (0, 0), pipeline_mode=pl.Buffered(1))
    return pl.pallas_call(
        functools.partial(_ffn_kernel, final_norm=final_norm),
        out_shape=jax.ShapeDtypeStruct((t, d), F32),
        grid=(t // tm,),
        in_specs=[pl.BlockSpec((tm, d), lambda i: (i, 0)),
                  pl.BlockSpec((1, d), lambda i: (0, 0)),
                  resident((d, ff)), resident((ff, d)),
                  pl.BlockSpec((1, d), lambda i: (0, 0))],
        out_specs=pl.BlockSpec((tm, d), lambda i: (i, 0)),
        compiler_params=_cparams(("parallel",)),
        name="ffn",
    )(x, gain, w_up, w_down, gain_final)


def _attn_out_kernel(q_ref, k_ref, v_ref, wo_ref, r_ref, x_ref):
    sls = [slice(h * MEM_HD, (h + 1) * MEM_HD) for h in range(MEM_HEADS)]
    s = [_mm_nt(q_ref[:, sl], k_ref[:, sl]) * (MEM_HD ** -0.5) for sl in sls]
    p = [jnp.exp(sh - jnp.max(sh, axis=-1, keepdims=True)) for sh in s]
    p = [ph * (1.0 / jnp.sum(ph, axis=-1, keepdims=True)) for ph in p]
    heads = [_mm(ph, v_ref[:, sl]) for ph, sl in zip(p, sls)]
    x_ref[...] = r_ref[...] + _mm(jnp.concatenate(heads, axis=-1), wo_ref[...])


def cross_attention_out(q, mem_k, mem_v, layer, w_o, res, *, rows):
    t, d = res.shape
    bsz = mem_k.shape[1] // MEM_LEN
    lt = t // bsz // rows
    tile = pl.BlockSpec((rows, d), lambda b, l: (b * lt + l, 0))
    mem = pl.BlockSpec((None, MEM_LEN, d), lambda b, l: (layer, b, 0))
    return pl.pallas_call(
        _attn_out_kernel,
        out_shape=jax.ShapeDtypeStruct((t, d), F32),
        grid=(bsz, lt),
        in_specs=[tile, mem, mem, pl.BlockSpec((d, d), lambda b, l: (0, 0)), tile],
        out_specs=tile,
        compiler_params=_cparams(("parallel", "arbitrary")),
        name="cross_attention_out",
    )(q, mem_k, mem_v, w_o, res)


KV_RING = 3


def _attn_cache_kernel(q_ref, k_hbm, v_hbm, o_ref, k_buf, v_buf, sem, *, nb, rows, layer, n_steps):
    step = pl.program_id(0)

    def copies(st, slot):
        return (pltpu.make_async_copy(k_hbm.at[layer, pl.ds(st * nb, nb)], k_buf.at[slot], sem.at[0, slot]),
                pltpu.make_async_copy(v_hbm.at[layer, pl.ds(st * nb, nb)], v_buf.at[slot], sem.at[1, slot]))

    @pl.when(step == 0)
    def _():
        for st in range(min(KV_RING - 1, n_steps)):
            for cp in copies(st, st):
                cp.start()

    @pl.when(step + KV_RING - 1 < n_steps)
    def _():
        for cp in copies(step + KV_RING - 1, (step + KV_RING - 1) % KV_RING):
            cp.start()

    slot = step % KV_RING
    for cp in copies(step, slot):
        cp.wait()
    k_ref = k_buf.at[slot]
    v_ref = v_buf.at[slot]
    nr = MEM_HEADS * rows
    same_head = _iota2((nr, MEM_HEADS * MEM_LEN), 1) % MEM_HEADS == _iota2((nr, MEM_HEADS * MEM_LEN), 0) // rows
    q_all = q_ref[...].astype(F32)
    rng = range(nb)
    qs = [jnp.concatenate([q_all[b * rows:(b + 1) * rows, h * MEM_HD:(h + 1) * MEM_HD] for h in range(MEM_HEADS)],
                          axis=0) for b in rng]
    s = [_mm_nt(qs[b], k_ref[b].reshape(MEM_HEADS * MEM_LEN, MEM_HD)) * (MEM_HD ** -0.5) for b in rng]
    s = [jnp.where(same_head, sb, -1e30) for sb in s]
    p = [jnp.exp(sb - jnp.max(sb, axis=-1, keepdims=True)) for sb in s]
    p = [pb * (1.0 / jnp.sum(pb, axis=-1, keepdims=True)) for pb in p]
    o = [_mm(p[b], v_ref[b].reshape(MEM_HEADS * MEM_LEN, MEM_HD)) for b in rng]
    for b in rng:
        for h in range(MEM_HEADS):
            o_ref[b * rows:(b + 1) * rows, h * MEM_HD:(h + 1) * MEM_HD] = o[b][h * rows:(h + 1) * rows]


def cross_attention_cache(q, cache_k, cache_v, layer, *, nb, rows):
    t = q.shape[0]
    bsz = cache_k.shape[1]
    n_steps = bsz // nb
    hbm = pl.BlockSpec(memory_space=pl.ANY)
    ring = pltpu.VMEM((KV_RING, nb, MEM_LEN, MEM_HEADS, MEM_HD), F32)
    return pl.pallas_call(
        functools.partial(_attn_cache_kernel, nb=nb, rows=rows, layer=layer, n_steps=n_steps),
        out_shape=jax.ShapeDtypeStruct((t, D_MODEL), F32),
        grid=(n_steps,),
        in_specs=[pl.BlockSpec((nb * rows, D_MODEL), lambda b: (b, 0)), hbm, hbm],
        out_specs=pl.BlockSpec((nb * rows, D_MODEL), lambda b: (b, 0)),
        scratch_shapes=[ring, ring, pltpu.SemaphoreType.DMA((2, KV_RING))],
        compiler_params=_cparams(("arbitrary",)),
        name="cross_attention_cache",
    )(q, cache_k, cache_v)


def _gelu_tanh(x):
    return 0.5 * x * (1.0 + jnp.tanh(math.sqrt(2.0 / math.pi) * (x + 0.044715 * (x * x * x))))


def _s5_kernel(u_ref, h0_ref, lam_ref, bblk_ref, cblk_ref, d_ref, wglu_ref, bglu_ref, gain_ref,
               y_ref, hfin_ref, scr_ref, tm_ref, *, steps, nb):
    @pl.when(pl.program_id(1) == 0)
    def _():
        scr_ref[0:nb, :] = h0_ref[...]

    def to_time_major(b, carry):
        for j in range(GW // LANES):
            tm_ref[j, pl.ds(b, steps, stride=nb), :] = u_ref[b, :, j * LANES:(j + 1) * LANES]
        return carry

    lax.fori_loop(0, nb, to_time_major, 0)
    u = jnp.concatenate([tm_ref[j] for j in range(GW // LANES)], axis=1)
    rows = _row_parts(steps * nb, 4)
    for r in rows:
        scr_ref[nb + r.start:nb + r.stop, :] = _mm(u[r], bblk_ref[...])
    lam_re = jnp.broadcast_to(lam_ref[0:1, :], (nb, S5_W))
    lam_im = jnp.broadcast_to(lam_ref[1:2, :], (nb, S5_W))

    h_re, h_im = scr_ref[0:nb, 0:S5_W], scr_ref[0:nb, S5_W:]
    for t in range(steps):
        c0 = (t + 1) * nb
        h_re, h_im = (scr_ref[c0:c0 + nb, 0:S5_W] + lam_re * h_re - lam_im * h_im,
                      scr_ref[c0:c0 + nb, S5_W:] + lam_re * h_im + lam_im * h_re)
        scr_ref[c0:c0 + nb, 0:S5_W] = h_re
        scr_ref[c0:c0 + nb, S5_W:] = h_im
    h_last = scr_ref[steps * nb:, :]
    hfin_ref[...] = h_last
    y = [_mm(scr_ref[nb + r.start:nb + r.stop, :], cblk_ref[...]) + d_ref[...] * u[r] for r in rows]
    y = [_gelu_tanh(yp) for yp in y]
    z = [_mm(yp, wglu_ref[...]) for yp in y]
    y = [_rms(yp * _sigmoid(zp + bglu_ref[...]), gain_ref[...]) for yp, zp in zip(y, z)]
    for r, yp in zip(rows, y):
        for j in range(GW // LANES):
            tm_ref[j, r, :] = yp[:, j * LANES:(j + 1) * LANES]

    def to_batch_major(b, carry):
        for j in range(GW // LANES):
            y_ref[b, :, j * LANES:(j + 1) * LANES] = tm_ref[j, pl.ds(b, steps, stride=nb), :]
        return carry

    lax.fori_loop(0, nb, to_batch_major, 0)
    scr_ref[0:nb, :] = h_last


def s5_mixer(proj, h0, lam, bblk, cblk, dvec, wglu, bglu, gain, *, steps, nb):
    bsz, seq, _ = proj.shape
    const = lambda shape: pl.BlockSpec(shape, lambda b, t: (0, 0))
    return pl.pallas_call(
        functools.partial(_s5_kernel, steps=steps, nb=nb),
        out_shape=(jax.ShapeDtypeStruct((bsz, seq, GW), F32),
                   jax.ShapeDtypeStruct((bsz, 2 * S5_W), F32)),
        grid=(bsz // nb, seq // steps),
        in_specs=[pl.BlockSpec((nb, steps, GW), lambda b, t: (b, t, 0)),
                  pl.BlockSpec((nb, 2 * S5_W), lambda b, t: (b, 0)),
                  const((2, S5_W)), const((GW, 2 * S5_W)), const((2 * S5_W, GW)), const((1, GW)),
                  const((GW, GW)), const((1, GW)), const((1, GW))],
        out_specs=(pl.BlockSpec((nb, steps, GW), lambda b, t: (b, t, 0)),
                   pl.BlockSpec((nb, 2 * S5_W), lambda b, t: (b, 0))),
        scratch_shapes=[pltpu.VMEM(((steps + 1) * nb, 2 * S5_W), F32),
                        pltpu.VMEM((GW // LANES, steps * nb, LANES), F32)],
        compiler_params=_cparams(("parallel", "arbitrary")),
        name="s5_mixer",
    )(proj, h0, lam, bblk, cblk, dvec, wglu, bglu, gain)


def s5_params(lam_re, lam_im, b_re, b_im, c_re, c_im, log_step):
    step = jnp.exp(log_step)[:, None]
    mag = jnp.exp(lam_re * step)
    lbar_re = mag * jnp.cos(lam_im * step)
    lbar_im = mag * jnp.sin(lam_im * step)
    den = lam_re * lam_re + lam_im * lam_im
    f_re = ((lbar_re - 1.0) * lam_re + lbar_im * lam_im) / den
    f_im = (lbar_im * lam_re - (lbar_re - 1.0) * lam_im) / den
    bbar_re = f_re[..., None] * b_re - f_im[..., None] * b_im
    bbar_im = f_re[..., None] * b_im + f_im[..., None] * b_re
    eye_g = jnp.eye(S5_NG, dtype=F32)

    def in_blk(m):
        return jnp.einsum('gph,gk->ghkp', m, eye_g).reshape(GW, S5_W)

    def out_blk(m):
        return jnp.einsum('ghp,gk->gpkh', m, eye_g).reshape(S5_W, GW)

    bblk = jnp.concatenate([in_blk(bbar_re), in_blk(bbar_im)], axis=1)
    cblk = jnp.concatenate([out_blk(c_re), -out_blk(c_im)], axis=0)
    lam2 = jnp.stack([lbar_re.reshape(S5_W), lbar_im.reshape(S5_W)])
    return lam2, bblk.astype(BF16), cblk.astype(BF16)


def s5_apply(proj, h0_re, h0_im, p, l, *, steps, nb):
    bsz = proj.shape[0]
    lam2, bblk, cblk = s5_params(p['s5_lam_re'][l], p['s5_lam_im'][l], p['s5_b_re'][l], p['s5_b_im'][l],
                                 p['s5_c_re'][l], p['s5_c_im'][l], p['s5_log_step'][l])
    h0 = jnp.concatenate([h0_re.reshape(bsz, S5_W), h0_im.reshape(bsz, S5_W)], axis=1)
    y, h = s5_mixer(proj, h0, lam2, bblk, cblk, p['s5_d'][l].reshape(1, GW),
                    p['s5_w_glu'][l].astype(BF16), p['s5_b_glu'][l].reshape(1, GW),
                    p['s5_norm'][l].reshape(1, GW), steps=steps, nb=nb)
    return y, h[:, :S5_W].reshape(bsz, S5_NG, S5_P), h[:, S5_W:].reshape(bsz, S5_NG, S5_P)


def _load_state(s0_ref, st_ref, nb, transpose):
    for b in range(nb):
        rows = []
        for h in range(NH):
            pieces = [s0_ref[b, h]]
            if h:
                pieces.insert(0, jnp.zeros((HD, h * HD), F32))
            if h < NH - 1:
                pieces.append(jnp.zeros((HD, (NH - 1 - h) * HD), F32))
            rows.append(jnp.concatenate(pieces, axis=1))
        st = jnp.concatenate(rows, axis=0)
        st_ref[b] = st.T if transpose else st


def _store_state(st_ref, sfin_ref, nb, transpose, own_layer):
    if own_layer is not None:
        for other in range(DEPTH):
            if other != own_layer:
                sfin_ref[other] = jnp.zeros(sfin_ref.shape[1:], F32)
        sfin_ref = sfin_ref.at[own_layer]
    for b in range(nb):
        st = st_ref[b].T if transpose else st_ref[b]
        for h in range(NH):
            sfin_ref[b, h] = st[h * HD:(h + 1) * HD, h * HD:(h + 1) * HD]


def _head_masks():
    lane_head = _iota2((1, GW), 1) // HD
    return [(lane_head == h).astype(F32) for h in range(NH)]


def _stack_heads(x, masks):
    return jnp.concatenate([x * m for m in masks], axis=0)


def _pad_rows(x, rows):
    if x.shape[0] == rows:
        return x
    return jnp.concatenate([x, jnp.zeros((rows - x.shape[0], x.shape[1]), x.dtype)], axis=0)


def _silu(x):
    return x * _sigmoid(x)


def _chunk_rows(ref, b, r0, rows, c):
    return _pad_rows(ref[b, pl.ds(r0, rows), :], c)


CHUNK_UNROLL = 4


def _chunk_loop(n_chunks, rows, body):
    unroll = math.gcd(n_chunks, CHUNK_UNROLL)

    def step(i, carry):
        for j in range(unroll):
            body(pl.multiple_of((i * unroll + j) * rows, rows))
        return carry

    lax.fori_loop(0, n_chunks // unroll, step, 0)


def _ret_kernel(q_ref, k_ref, v_ref, g_ref, cos_ref, sin_ref, s0_ref, _states_in, o_ref, sfin_ref, st_ref,
                *, nb, tb, c, c_real, own_layer):
    rows = min(tb, c)

    @pl.when(pl.program_id(1) == 0)
    def _():
        _load_state(s0_ref, st_ref, nb, transpose=False)

    masks = _head_masks()
    ones_bd = _head_ones()
    lane_head = _iota2((1, GW), 1) // HD
    log_gamma = jnp.zeros((1, GW), F32)
    for h in range(NH):
        log_gamma = jnp.where(lane_head == h, math.log(1.0 - 2.0 ** (-5.0 - h)), log_gamma)
    tt = _iota2((c, GW), 0).astype(F32)
    scale = HD ** -0.5
    g_q = jnp.exp(log_gamma * (tt + 1.0))
    g_k = jnp.exp(-log_gamma * (tt + 1.0)) * scale
    g_tail = jnp.exp(log_gamma * (c_real - 1.0 - tt)) * scale
    row_head = _iota2((GW, GW), 0) // HD
    g_chunk = jnp.zeros((GW, GW), F32)
    for h in range(NH):
        g_chunk = jnp.where(row_head == h, math.exp(math.log(1.0 - 2.0 ** (-5.0 - h)) * c_real), g_chunk)
    causal = _iota2((c, NH * c), 0) >= _iota2((c, NH * c), 1) % c
    first_half = _iota2((c, GW), 1) % HD < HD // 2

    def rope(x, cos, sin):
        swapped = jnp.where(first_half, pltpu.roll(x, GW - HD // 2, axis=1), pltpu.roll(x, HD // 2, axis=1))
        return x * cos + swapped * sin

    def chunk(r0):
        cos = _pad_rows(cos_ref[pl.ds(r0, rows), :], c)
        sin = _pad_rows(sin_ref[pl.ds(r0, rows), :], c)
        rng = range(nb)
        qt = [rope(_chunk_rows(q_ref, b, r0, rows, c), cos, sin) * g_q for b in rng]
        k = [rope(_chunk_rows(k_ref, b, r0, rows, c), cos, sin) for b in rng]
        v = [_chunk_rows(v_ref, b, r0, rows, c) for b in rng]
        st = [st_ref[b] for b in rng]
        sc = [_mm_nt(qt[b], _stack_heads(k[b] * g_k, masks)) for b in rng]
        o_in = [_mm(jnp.where(causal, sc[b], 0.0), _stack_heads(v[b], masks)) for b in rng]
        o_st = [_mm(qt[b], st[b]) for b in rng]
        d_st = [_mm_tn(k[b] * g_tail, v[b]) for b in rng]
        o = [o_in[b] + o_st[b] for b in rng]
        ms = [_head_sum(o[b] * o[b], ones_bd) for b in rng]
        for b in rng:
            st_ref[b] = st[b] * g_chunk + ones_bd * d_st[b]
            ob = o[b] * lax.rsqrt(ms[b] * (1.0 / HD) + EPS) * _silu(_chunk_rows(g_ref, b, r0, rows, c))
            o_ref[b, pl.ds(r0, rows), :] = ob[:rows]

    _chunk_loop(tb // rows, rows, chunk)

    @pl.when(pl.program_id(1) == pl.num_programs(1) - 1)
    def _():
        _store_state(st_ref, sfin_ref, nb, transpose=False, own_layer=own_layer)


def _rope_tables(pos0, seq):
    half = HD // 2
    inv = ROPE_BASE ** (-jnp.arange(half, dtype=F32) / half)
    pos = pos0 + jnp.arange(seq, dtype=jnp.int32)
    ang = pos.astype(F32)[:, None] * inv[None, :]
    cos, sin = jnp.cos(ang), jnp.sin(ang)
    return (jnp.tile(jnp.concatenate([cos, cos], axis=-1), (1, NH)),
            jnp.tile(jnp.concatenate([-sin, sin], axis=-1), (1, NH)))


def _proj_spec(nb, tb, col):
    return pl.BlockSpec((nb, tb, GW), lambda b, t: (b, t, col))


def _state_spec(nb, layer):
    return pl.BlockSpec((None, nb, NH, HD, HD), lambda b, t: (layer, b, 0, 0, 0))


_ALIASED = pl.BlockSpec(memory_space=pl.ANY)


def _collector(states, bsz, nb, layer, operand_index):
    shape = jax.ShapeDtypeStruct((DEPTH, bsz, NH, HD, HD), F32)
    if states is None:
        spec = pl.BlockSpec((DEPTH, nb, NH, HD, HD), lambda b, t: (0, b, 0, 0, 0))
        return jnp.zeros((1, 1, NH, HD, HD), F32), spec, shape, {}, layer
    return states, _state_spec(nb, layer), shape, {operand_index: 1}, None


def retention_apply(proj, s0, s0_layer, states, layer, pos0, *, tb, nb):
    bsz, seq, _ = proj.shape
    c = RET_CHUNK if tb >= RET_CHUNK else CHUNK
    cos, sin = _rope_tables(pos0, seq)
    states, st_spec, st_shape, aliases, own_layer = _collector(states, bsz, nb, layer, 7)
    return pl.pallas_call(
        functools.partial(_ret_kernel, nb=nb, tb=tb, c=c, c_real=min(tb, c), own_layer=own_layer),
        out_shape=(jax.ShapeDtypeStruct((bsz, seq, GW), F32), st_shape),
        grid=(bsz // nb, seq // tb),
        in_specs=[_proj_spec(nb, tb, 5), _proj_spec(nb, tb, 6), _proj_spec(nb, tb, 7), _proj_spec(nb, tb, 8),
                  pl.BlockSpec((tb, GW), lambda b, t: (t, 0)), pl.BlockSpec((tb, GW), lambda b, t: (t, 0)),
                  _state_spec(nb, s0_layer), _ALIASED],
        out_specs=(pl.BlockSpec((nb, tb, GW), lambda b, t: (b, t, 0)), st_spec),
        scratch_shapes=[pltpu.VMEM((nb, GW, GW), F32)],
        input_output_aliases=aliases,
        compiler_params=_cparams(("parallel", "arbitrary")),
        name="retention",
    )(proj, proj, proj, proj, cos, sin, s0, states)


def _log_sigmoid(z):
    return jnp.minimum(z, 0.0) - jnp.log(1.0 + jnp.exp(-jnp.abs(z)))


def _hgrn_kernel(q_ref, f_ref, i_ref, g_ref, lb_ref, gain_ref, s0_ref, _states_in, o_ref, sfin_ref, st_ref,
                 *, nb, tb, c, own_layer):
    rows = min(tb, c)

    @pl.when(pl.program_id(1) == 0)
    def _():
        _load_state(s0_ref, st_ref, nb, transpose=True)

    t_hi = -(-rows // SUBLANES) * SUBLANES
    ones_bd = _head_ones()
    tri = (_iota2((c, c), 0) >= _iota2((c, c), 1)).astype(F32)
    t_idx = _iota2((c, GW), 0)
    lb = lb_ref[...]
    log_lb = jnp.log(lb)
    log_1m_lb = jnp.log(1.0 - lb)
    gain = gain_ref[...]

    def chunk(r0):
        rng = range(nb)
        valid = t_idx < rows

        def gates(b):
            z = _chunk_rows(f_ref, b, r0, rows, c)
            ls_pos = _log_sigmoid(z)
            ls_neg = ls_pos - z
            b2 = log_lb + ls_neg
            log_f = jnp.maximum(ls_pos, b2) + jnp.log(1.0 + jnp.exp(-jnp.abs(ls_pos - b2)))
            return (jnp.where(valid, log_f, 0.0), jnp.where(valid, (1.0 - lb) * jnp.exp(ls_neg), 0.0),
                    ls_neg + log_1m_lb)

        log_f, key, log_key = zip(*[gates(b) for b in rng])
        q = [_silu(_chunk_rows(q_ref, b, r0, rows, c)) for b in rng]
        v = [_chunk_rows(i_ref, b, r0, rows, c) for b in rng]
        st = [st_ref[b] for b in rng]
        cum = [_mm_exact_lhs(tri, log_f[b]) for b in rng]
        last = [cum[b][c - 1:c, :] for b in rng]
        cum2 = [cum[b] * LOG2_E for b in rng]
        kd2 = [(log_key[b] - cum[b]) * LOG2_E for b in rng]

        def pair_rows(b, s):
            lo = SUBLANES * (s // SUBLANES)
            t_rows = _iota2((t_hi - lo, GW), 0) + lo
            key_decay = jnp.where(t_rows >= s, jnp.exp2(cum2[b][lo:t_hi] + kd2[b][s:s + 1, :]), 0.0)
            return key_decay * q[b][lo:t_hi]

        sc = [_mm(jnp.concatenate([pair_rows(b, s) for s in range(rows)], axis=0), ones_bd) for b in rng]
        o_st = [_mm_nt(q[b] * jnp.exp(cum[b]), st[b]) for b in rng]
        d_st = [_mm_tn(v[b], key[b] * jnp.exp(last[b] - cum[b])) for b in rng]

        def intra(b):
            tiles = [None] * (c // SUBLANES)
            off = 0
            for s in range(rows):
                for ti in range(s // SUBLANES, t_hi // SUBLANES):
                    term = sc[b][off:off + SUBLANES, :] * v[b][s:s + 1, :]
                    tiles[ti] = term if tiles[ti] is None else tiles[ti] + term
                    off += SUBLANES
            zero = jnp.zeros((SUBLANES, GW), F32)
            return jnp.concatenate([zero if t is None else t for t in tiles], axis=0)

        o = [o_st[b] + intra(b) for b in rng]
        ms = [_head_sum(o[b] * o[b], ones_bd) for b in rng]
        for b in rng:
            st_ref[b] = st[b] * jnp.exp(last[b]) + ones_bd * d_st[b]
            ob = o[b] * lax.rsqrt(ms[b] * (1.0 / HD) + EPS) * gain * _silu(_chunk_rows(g_ref, b, r0, rows, c))
            o_ref[b, pl.ds(r0, rows), :] = ob[:rows]

    _chunk_loop(tb // rows, rows, chunk)

    @pl.when(pl.program_id(1) == pl.num_programs(1) - 1)
    def _():
        _store_state(st_ref, sfin_ref, nb, transpose=True, own_layer=own_layer)


def hgrn_apply(proj, s0, s0_layer, states, layer, lb, gain, *, tb, nb):
    bsz, seq, _ = proj.shape
    row = pl.BlockSpec((1, GW), lambda b, t: (0, 0))
    states, st_spec, st_shape, aliases, own_layer = _collector(states, bsz, nb, layer, 7)
    return pl.pallas_call(
        functools.partial(_hgrn_kernel, nb=nb, tb=tb, c=CHUNK, own_layer=own_layer),
        out_shape=(jax.ShapeDtypeStruct((bsz, seq, GW), F32), st_shape),
        grid=(bsz // nb, seq // tb),
        in_specs=[_proj_spec(nb, tb, 1), _proj_spec(nb, tb, 2), _proj_spec(nb, tb, 3), _proj_spec(nb, tb, 4),
                  row, row, _state_spec(nb, s0_layer), _ALIASED],
        out_specs=(pl.BlockSpec((nb, tb, GW), lambda b, t: (b, t, 0)), st_spec),
        scratch_shapes=[pltpu.VMEM((nb, GW, GW), F32)],
        input_output_aliases=aliases,
        compiler_params=_cparams(("parallel", "arbitrary")),
        name="hgrn2",
    )(proj, proj, proj, proj, lb.reshape(1, GW), gain.reshape(1, GW), s0, states)


def _softplus(z):
    return jnp.maximum(z, 0.0) + jnp.log(1.0 + jnp.exp(-jnp.abs(z)))


def _rwkv_prepare(ins, consts):
    masks, ones_bd, bd_mask, eye_flat, tri, strict, incl = consts
    rng = range(len(ins))
    r, lw, k, v, kk, a = [[i[j] for i in ins] for j in range(6)]
    c = r[0].shape[0]
    n = NH * c

    def to_bd(flat):
        return jnp.concatenate([flat] * NH, axis=0) * bd_mask

    cum = [_mm_exact_lhs(tri, lw[i]) for i in rng]
    last = [cum[i][c - 1:c, :] for i in rng]
    p_inv = [jnp.exp(-cum[i]) for i in rng]
    p_tail = [jnp.exp(last[i] - cum[i]) for i in rng]
    ka = [kk[i] * a[i] for i in rng]
    x = [jnp.concatenate([kk[i] * jnp.exp(cum[i] - lw[i]), r[i] * jnp.exp(cum[i])], axis=0) for i in rng]
    g = [_mm_nt(x[i], jnp.concatenate([_stack_heads(ka[i] * p_inv[i], masks),
                                       _stack_heads(k[i] * p_inv[i], masks)], axis=0)) for i in rng]
    m_ak = [jnp.where(strict, g[i][:c], 0.0) for i in rng]
    n_ak = [jnp.where(incl, g[i][c:], 0.0) for i in rng]
    pw = [m_ak[i][:, :n] for i in rng]
    t_inv = [eye_flat - pw[i] for i in rng]
    pw_bd = [to_bd(pw[i]) for i in rng]
    for _ in range(int(math.log2(c)) - 1):
        pw = [_mm(pw[i], pw_bd[i]) for i in rng]
        pw_bd = [to_bd(pw[i]) for i in rng]
        t_inv = [t_inv[i] + _mm(t_inv[i], pw_bd[i]) for i in rng]
    v_stk = [_stack_heads(v[i], masks) for i in rng]
    zeros = jnp.zeros((n, GW), F32)
    mkv = [_mm(m_ak[i], jnp.concatenate([zeros, v_stk[i]], axis=0)) for i in rng]
    tmkv = [_mm(t_inv[i], _stack_heads(mkv[i], masks)) for i in rng]
    return [dict(x=x[i], t_inv=t_inv[i], tmkv=tmkv[i], n_ak=n_ak[i], v=v[i], v_stk=v_stk[i],
                 decay=jnp.exp(last[i]),
                 k_tail=jnp.concatenate([ka[i] * p_tail[i], k[i] * p_tail[i]], axis=0)) for i in rng]


def _rwkv_advance(prep, st, consts):
    masks, ones_bd = consts[0], consts[1]
    rng = range(len(prep))
    c = prep[0]['t_inv'].shape[0]
    xa = [_mm_nt(prep[i]['x'], st[i]) for i in rng]
    u = [-_mm(prep[i]['t_inv'], _stack_heads(xa[i][:c], masks)) - prep[i]['tmkv'] for i in rng]
    nuv = [_mm(prep[i]['n_ak'], jnp.concatenate([_stack_heads(u[i], masks), prep[i]['v_stk']], axis=0))
           for i in rng]
    d_uv = [_mm_tn(jnp.concatenate([u[i], prep[i]['v']], axis=0), prep[i]['k_tail']) for i in rng]
    return [(xa[i][c:] + nuv[i], st[i] * prep[i]['decay'] + ones_bd * d_uv[i]) for i in rng]


def _rwkv_kernel(x_r_ref, x_k_ref, x_v_ref, x_l_ref, sh0_ref, s0_ref, mu_ref, w0_ref, ww_ref, a0_ref,
                 wa_ref, wg_ref, kk_ref, ka_ref, rk_ref, lng_ref, lnb_ref, _states_in,
                 o_ref, sfin_ref, shfin_ref,
                 st_ref, sh_ref, r_s, lw_s, k_s, v_s, kkn_s, a_s, y_s, bonus_s, gate_s, *, nb, tb, c, own_layer):
    tbp = max(tb, c)

    @pl.when(pl.program_id(1) == 0)
    def _():
        _load_state(s0_ref, st_ref, nb, transpose=False)
        sh_ref[...] = sh0_ref[...]

    masks = _head_masks()
    ones_bd = _head_ones()
    n = NH * c
    row_t = _iota2((c, 2 * n), 0)
    col_t = _iota2((c, 2 * n), 1) % c
    bd_mask = (_iota2((n, n), 0) // c == _iota2((n, n), 1) // c).astype(F32)
    eye_flat = (_iota2((c, n), 1) % c == _iota2((c, n), 0)).astype(F32)
    consts = (masks, ones_bd, bd_mask, eye_flat, (_iota2((c, c), 0) >= _iota2((c, c), 1)).astype(F32),
              col_t < row_t, col_t <= row_t)
    first_row = _iota2((nb * tb, GW), 0) % tb == 0

    def mixed(x_ref, j):
        x = x_ref[...].reshape(nb * tb, GW)
        carried = jnp.broadcast_to(sh_ref[:, j:j + 1, :], (nb, tb, GW)).reshape(nb * tb, GW)
        prev = jnp.where(first_row, carried, pltpu.roll(x, 1, axis=0))
        sh_ref[:, j:j + 1, :] = x_ref[:, tb - 1:tb, :]
        return x + (prev - x) * mu_ref[j:j + 1, :]

    r = mixed(x_r_ref, 0)
    k = mixed(x_k_ref, 1)
    v = mixed(x_v_ref, 2)
    xl = mixed(x_l_ref, 3)
    log_w = -_softplus(-(w0_ref[...] + _mm(jnp.tanh(xl), ww_ref[...]))) - 0.5
    a = _sigmoid(a0_ref[...] + _mm(xl, wa_ref[...]))
    gate_s[...] = _mm(_sigmoid(xl), wg_ref[...]).reshape(nb, tb, GW)
    kk = k * kk_ref[...]
    kk = kk * lax.rsqrt(jnp.maximum(_head_sum(kk * kk, ones_bd), 1e-24))
    k = k * (1.0 + (a - 1.0) * ka_ref[...])
    bonus_s[...] = (_head_sum(r * k * rk_ref[...], ones_bd) * v).reshape(nb, tb, GW)
    for s, val in ((r_s, r), (lw_s, -jnp.exp(log_w)), (k_s, k), (v_s, v), (kkn_s, kk), (a_s, a)):
        s[:, 0:tb, :] = val.reshape(nb, tb, GW)
        if tbp > tb:
            s[:, tb:tbp, :] = jnp.zeros((nb, tbp - tb, GW), F32)

    n_chunks = tbp // c
    group = 8 if n_chunks % 8 == 0 else 1

    def chunks(gi, carry):
        r0 = [pl.multiple_of((gi * group + j) * c, c) for j in range(group)]
        prep = _rwkv_prepare([tuple(s[b, pl.ds(r0[j], c), :] for s in (r_s, lw_s, k_s, v_s, kkn_s, a_s))
                              for j in range(group) for b in range(nb)], consts)
        st = [st_ref[b] for b in range(nb)]
        for j in range(group):
            outs = _rwkv_advance(prep[j * nb:(j + 1) * nb], st, consts)
            st = [o[1] for o in outs]
            for b in range(nb):
                y_s[b, pl.ds(r0[j], c), :] = outs[b][0]
        for b in range(nb):
            st_ref[b] = st[b]
        return carry

    lax.fori_loop(0, n_chunks // group, chunks, 0)

    y = y_s[:, 0:tb, :].reshape(nb * tb, GW)
    mean = _head_sum(y, ones_bd) * (1.0 / HD)
    d = y - mean
    var = _head_sum(d * d, ones_bd) * (1.0 / HD)
    y = d * lax.rsqrt(var + RW_LN_EPS) * lng_ref[...] + lnb_ref[...]
    o_ref[...] = ((y.reshape(nb, tb, GW) + bonus_s[...]) * gate_s[...])
    shfin_ref[...] = sh_ref[...]

    @pl.when(pl.program_id(1) == pl.num_programs(1) - 1)
    def _():
        _store_state(st_ref, sfin_ref, nb, transpose=False, own_layer=own_layer)


def rwkv_apply(proj, s0, s0_layer, states, shift0, p, l, *, tb, nb):
    bsz, seq, _ = proj.shape
    states, st_spec, st_shape, aliases, own_layer = _collector(states, bsz, nb, l, 17)
    c = CHUNK
    tbp = max(tb, c)
    row = pl.BlockSpec((1, GW), lambda b, t: (0, 0))
    mat = pl.BlockSpec((GW, GW), lambda b, t: (0, 0))
    sh_spec = pl.BlockSpec((nb, 4, GW), lambda b, t: (b, 0, 0))
    zeros = lambda r: jnp.zeros((r, GW), F32)
    ww = jnp.concatenate([p['rw_w_w2'][l], zeros(192)], axis=0).astype(BF16)
    wa = jnp.concatenate([zeros(64), p['rw_w_a2'][l], zeros(128)], axis=0).astype(BF16)
    wg = jnp.concatenate([zeros(128), p['rw_w_g2'][l]], axis=0).astype(BF16)
    r1 = lambda name: p[name][l].reshape(1, GW)
    seq_buf = pltpu.VMEM((nb, tbp, GW), F32)
    blk_buf = pltpu.VMEM((nb, tb, GW), F32)
    y, st, sh = pl.pallas_call(
        functools.partial(_rwkv_kernel, nb=nb, tb=tb, c=c, own_layer=own_layer),
        out_shape=(jax.ShapeDtypeStruct((bsz, seq, GW), F32), st_shape,
                   jax.ShapeDtypeStruct((bsz, 4, GW), F32)),
        grid=(bsz // nb, seq // tb),
        in_specs=[_proj_spec(nb, tb, 9), _proj_spec(nb, tb, 10), _proj_spec(nb, tb, 11), _proj_spec(nb, tb, 12),
                  sh_spec, _state_spec(nb, s0_layer), pl.BlockSpec((4, GW), lambda b, t: (0, 0)),
                  row, mat, row, mat, mat, row, row, row, row, row, _ALIASED],
        out_specs=(pl.BlockSpec((nb, tb, GW), lambda b, t: (b, t, 0)), st_spec, sh_spec),
        scratch_shapes=[pltpu.VMEM((nb, GW, GW), F32), pltpu.VMEM((nb, 4, GW), F32),
                        seq_buf, seq_buf, seq_buf, seq_buf, seq_buf, seq_buf, seq_buf, blk_buf, blk_buf],
        input_output_aliases=aliases,
        compiler_params=_cparams(("parallel", "arbitrary")),
        name="rwkv7",
    )(proj, proj, proj, proj, shift0.reshape(bsz, 4, GW), s0, p['rw_mu'][l].reshape(4, GW),
      r1('rw_w0'), ww, r1('rw_a0'), wa, wg, r1('rw_k_k'), r1('rw_k_a'), r1('rw_r_k'), r1('rw_ln_g'), r1('rw_ln_b'),
      states)
    return y, st, sh.reshape(bsz, RW_PROJ)


TOKEN_TILE = 512
TIME_BLOCK = 256
RWKV_TIME_BLOCK = 128
CACHE_ATTN_SEQS = 4


def _tiles(bsz, seq):
    if seq >= TIME_BLOCK:
        return dict(tb=TIME_BLOCK, rw_tb=RWKV_TIME_BLOCK, s5_nb=bsz, mix_nb=bsz, rw_nb=bsz)
    return dict(tb=seq, rw_tb=seq, s5_nb=bsz, mix_nb=16, rw_nb=32)


def _trunk_layer(x, bsz, seq, pos0, attend, st, mats, p, wb, l, lb, final_norm):
    cfg = _tiles(bsz, seq)
    tm = TOKEN_TILE
    row = lambda name: p[name][l].reshape(1, -1)
    proj = norm_matmul(x, row('norm_mix'), wb['w_in'][l], tm=tm)
    proj = proj.reshape(bsz, seq, IN_WIDTH)
    y_s5, s5_re, s5_im = s5_apply(proj, st['s5_re'], st['s5_im'], p, l, steps=cfg['tb'], nb=cfg['s5_nb'])
    y_hg, hg_s = hgrn_apply(proj, st['hgrn'], st['layer'], mats['hgrn'], l, lb, p['hg_norm'][l],
                            tb=cfg['tb'], nb=cfg['mix_nb'])
    y_rt, rt_s = retention_apply(proj, st['ret'], st['layer'], mats['ret'], l, pos0,
                                 tb=cfg['tb'], nb=cfg['mix_nb'])
    y_rw, rw_s, shift = rwkv_apply(proj, st['rwkv'], st['layer'], mats['rwkv'], st['shift'], p, l,
                                   tb=cfg['rw_tb'], nb=cfg['rw_nb'])
    parts = [y.reshape(bsz * seq, GW) for y in (y_s5, y_hg, y_rt, y_rw)]
    x, q = mix_out_q(parts, wb['w_out'][l], x, row('norm_mem'), wb['mem_w_q'][l], tm=tm)
    x = attend(q, wb['mem_w_o'][l], x)
    x = ffn(x, row('norm_ffn'), wb['ffn_w_up'][l], wb['ffn_w_down'][l], p['norm_final'].reshape(1, -1),
            tm=tm, final_norm=final_norm)
    return x, (s5_re, s5_im, shift), dict(hgrn=hg_s, ret=rt_s, rwkv=rw_s)


def kernel(x_prompt, x_sample, mem_prompt, state_s5_re, state_s5_im, state_hgrn, state_ret, state_rwkv,
           state_rwkv_shift, cache_mem_k, cache_mem_v, norm_mix, w_in, w_out, s5_lam_re, s5_lam_im,
           s5_b_re, s5_b_im, s5_c_re, s5_c_im, s5_d, s5_log_step, s5_w_glu, s5_b_glu, s5_norm,
           hg_lb_logits, hg_norm, rw_mu, rw_w0, rw_w_w2, rw_a0, rw_w_a2, rw_w_g2, rw_k_k, rw_k_a, rw_r_k,
           rw_ln_g, rw_ln_b, norm_mem, mem_w_q, mem_w_k, mem_w_v, mem_w_o, norm_ffn, ffn_w_up, ffn_w_down,
           norm_final):
    p = dict(norm_mix=norm_mix, s5_lam_re=s5_lam_re, s5_lam_im=s5_lam_im, s5_b_re=s5_b_re, s5_b_im=s5_b_im,
             s5_c_re=s5_c_re, s5_c_im=s5_c_im, s5_d=s5_d, s5_log_step=s5_log_step, s5_w_glu=s5_w_glu,
             s5_b_glu=s5_b_glu, s5_norm=s5_norm, hg_norm=hg_norm, rw_mu=rw_mu, rw_w0=rw_w0, rw_w_w2=rw_w_w2,
             rw_a0=rw_a0, rw_w_a2=rw_w_a2, rw_w_g2=rw_w_g2, rw_k_k=rw_k_k, rw_k_a=rw_k_a, rw_r_k=rw_r_k,
             rw_ln_g=rw_ln_g, rw_ln_b=rw_ln_b, norm_mem=norm_mem, norm_ffn=norm_ffn, norm_final=norm_final)
    wb = {name: w.astype(BF16) for name, w in dict(
        w_in=w_in, w_out=w_out, mem_w_q=mem_w_q, mem_w_k=mem_w_k, mem_w_v=mem_w_v, mem_w_o=mem_w_o,
        ffn_w_up=ffn_w_up, ffn_w_down=ffn_w_down).items()}
    lb_all = jnp.cumsum(jax.nn.softmax(hg_lb_logits.astype(F32), axis=0), axis=0)
    lb_all = lb_all - lb_all[0:1]

    bp, lp, _ = x_prompt.shape
    bs, ls, _ = x_sample.shape
    yp = x_prompt.reshape(bp * lp, D_MODEL)
    ys = x_sample.reshape(bs * ls, D_MODEL)
    mem2d = mem_prompt.reshape(bp * MEM_LEN, D_MODEL)
    mat_zero = jnp.zeros((1, bp, NH, HD, HD), F32)
    p_small, s_small = [], []
    p_mats = dict(hgrn=None, ret=None, rwkv=None)
    s_mats = dict(hgrn=None, ret=None, rwkv=None)
    mk = matmul_layers(mem2d, wb['mem_w_k'], tm=TOKEN_TILE)
    mv = matmul_layers(mem2d, wb['mem_w_v'], tm=TOKEN_TILE)
    for l in range(DEPTH):
        final = l == DEPTH - 1
        def attend_p(q, w_o, x, l=l):
            return cross_attention_out(q, mk, mv, l, w_o, x, rows=TOKEN_TILE)

        zero_state = dict(s5_re=jnp.zeros((bp, S5_NG, S5_P), F32), s5_im=jnp.zeros((bp, S5_NG, S5_P), F32),
                          shift=jnp.zeros((bp, RW_PROJ), F32), hgrn=mat_zero, ret=mat_zero, rwkv=mat_zero, layer=0)
        yp, small, p_mats = _trunk_layer(yp, bp, lp, 0, attend_p, zero_state, p_mats, p, wb, l, lb_all[l], final)
        p_small.append(small)
        sst = dict(s5_re=state_s5_re[l], s5_im=state_s5_im[l], shift=state_rwkv_shift[l],
                   hgrn=state_hgrn, ret=state_ret, rwkv=state_rwkv, layer=l)
        def attend_s(q, w_o, x, l=l):
            o = cross_attention_cache(q, cache_mem_k, cache_mem_v, l, nb=CACHE_ATTN_SEQS, rows=ls)
            return matmul_residual(o, w_o, x, tm=TOKEN_TILE)

        ys, small, s_mats = _trunk_layer(ys, bs, ls, PAST_LEN, attend_s, sst, s_mats, p, wb, l, lb_all[l], final)
        s_small.append(small)
    stack = lambda states, i: jnp.stack([s[i] for s in states])
    return (yp.reshape(bp, lp, D_MODEL), ys.reshape(bs, ls, D_MODEL),
            stack(p_small, 0), stack(p_small, 1), p_mats['hgrn'], p_mats['ret'], p_mats['rwkv'], stack(p_small, 2),
            mk.reshape(DEPTH, bp, MEM_LEN, MEM_HEADS, MEM_HD), mv.reshape(DEPTH, bp, MEM_LEN, MEM_HEADS, MEM_HD),
            stack(s_small, 0), stack(s_small, 1), s_mats['hgrn'], s_mats['ret'], s_mats['rwkv'], stack(s_small, 2))
```
